```python
import math
import jax
import jax.numpy as jnp
from jax import lax
import numpy as np

D_MODEL = 4096
BATCH = 32
SEQ = 256
DEPTH = 1
DEC_BATCH = 2
DEC_SEQ = 1024
PAST_LEN = 512

GRID_W = 64
LRU_WIDTH = D_MODEL // 2
LRU_BLOCKS = 16
LRU_BLOCK_W = LRU_WIDTH // LRU_BLOCKS
CONV_W = 4
CONV_LEFT = 2
LRU_C = 8.0
RET_HEADS = 16
RET_WIDTH = D_MODEL - LRU_WIDTH
RET_DH = RET_WIDTH // RET_HEADS
CHUNK = 128
ROPE_BASE = 10000.0
N_GROUPS = 4
EXPERTS_PER_GROUP = 8
N_EXPERTS = N_GROUPS * EXPERTS_PER_GROUP
TOP_K = 2
D_EXPERT = D_MODEL // 4
MOE_BLOCK = 128
LN_EPS = 1e-6
GN_EPS = 1e-5
IN_COLS = 2 * LRU_WIDTH + 4 * RET_WIDTH

kernel_name = "hybrid_rglru_retention_hmoe_diffusion_step"

F32 = jnp.float32


def layer_norm(x):
    xf = x.astype(F32)
    mu = jnp.mean(xf, -1, keepdims=True)
    var = jnp.mean(jnp.square(xf - mu), -1, keepdims=True)
    return (xf - mu) * lax.rsqrt(var + LN_EPS)


def layer_norm_affine(x, g, b):
    return (layer_norm(x) * g.astype(F32) + b.astype(F32)).astype(x.dtype)


def modulate(x, shift, scale):
    return (layer_norm(x) * (1.0 + scale.astype(F32)) + shift.astype(F32)).astype(x.dtype)


def dwconv(x, w, b):
    L = x.shape[1]
    xp = jnp.pad(x, ((0, 0), (CONV_LEFT, CONV_W - 1 - CONV_LEFT), (0, 0)))
    y = b
    for j in range(CONV_W):
        y = y + xp[:, j:j + L] * w[j]
    return y


def rglru(x, wa, ba, wx, bx, lam, h0, reverse):
    B, L, W = x.shape
    xf = x.astype(F32)
    xb = xf.reshape(B, L, LRU_BLOCKS, LRU_BLOCK_W)
    r = jax.nn.sigmoid(jnp.einsum('blnc,ncd->blnd', xb, wa.astype(F32)).reshape(B, L, W) + ba.astype(F32))
    i = jax.nn.sigmoid(jnp.einsum('blnc,ncd->blnd', xb, wx.astype(F32)).reshape(B, L, W) + bx.astype(F32))
    log_a = -LRU_C * r * jax.nn.softplus(-lam.astype(F32))
    a = jnp.exp(log_a)
    u = jnp.sqrt(-jnp.expm1(2.0 * log_a)) * (i * xf)

    def step(h, au):
        a_t, u_t = au
        h = a_t * h + u_t
        return h, h

    h_last, hs = lax.scan(step, h0.astype(F32), (jnp.swapaxes(a, 0, 1), jnp.swapaxes(u, 0, 1)), reverse=reverse)
    return jnp.swapaxes(hs, 0, 1), h_last


def retention_dir(q, k, v, log_g, s0, reverse):
    B, L, H, Dh = q.shape
    if reverse:
        q, k, v = jnp.flip(q, 1), jnp.flip(k, 1), jnp.flip(v, 1)
    n = L // CHUNK
    qc = q.reshape(B, n, CHUNK, H, Dh)
    kc = k.reshape(B, n, CHUNK, H, Dh)
    vc = v.reshape(B, n, CHUNK, H, Dh)
    idx = jnp.arange(CHUNK, dtype=F32)
    diff = idx[:, None] - idx[None, :]
    decay_mask = jnp.where(diff >= 0, jnp.exp(jnp.maximum(diff, 0.0)[None] * log_g[:, None, None]), 0.0)
    xi = jnp.exp((idx + 1.0)[None, :] * log_g[:, None])
    zeta = jnp.exp((CHUNK - 1.0 - idx)[None, :] * log_g[:, None])
    g_chunk = jnp.exp(CHUNK * log_g)
    scores = jnp.einsum('bnqhd,bnkhd->bnhqk', qc, kc) * decay_mask
    inner = jnp.einsum('bnhqk,bnkhe->bnqhe', scores, vc)
    kv = jnp.einsum('bnkhd,bnkhe->nbhde', kc * zeta.T[:, :, None], vc)

    def step(s, kv_n):
        return g_chunk[None, :, None, None] * s + kv_n, s

    s_last, s_prev = lax.scan(step, s0.astype(F32), kv)
    cross = jnp.einsum('bnqhd,nbhde->bnqhe', qc * xi.T[:, :, None], s_prev)
    out = (inner + cross).reshape(B, L, H, Dh)
    if reverse:
        out = jnp.flip(out, 1)
    return out, s_last


def rope_2d(x):
    L = x.shape[1]
    rows = L // GRID_W
    row = jnp.repeat(jnp.arange(rows), GRID_W)
    col = jnp.tile(jnp.arange(GRID_W), rows)
    nf = RET_DH // 4
    freqs = ROPE_BASE ** (-jnp.arange(nf, dtype=F32) / nf)

    def rot(xh, pos):
        ang = pos.astype(F32)[:, None] * freqs[None, :]
        cos = jnp.cos(ang)[None, :, None, :]
        sin = jnp.sin(ang)[None, :, None, :]
        x1, x2 = jnp.split(xh, 2, axis=-1)
        return jnp.concatenate([x1 * cos - x2 * sin, x1 * sin + x2 * cos], -1)

    xr, xc = jnp.split(x, 2, axis=-1)
    return jnp.concatenate([rot(xr, row), rot(xc, col)], -1)


def token_mixers(h, p, lru_h0, ret_s0, grid_pos):
    B, L, _ = h.shape
    proj = h @ p['w_in']
    x_lru = proj[..., :LRU_WIDTH]
    g_lru = proj[..., LRU_WIDTH:2 * LRU_WIDTH]
    q, k, v, g_ret = jnp.split(proj[..., 2 * LRU_WIDTH:], 4, axis=-1)
    x_lru = dwconv(x_lru, p['conv_w'], p['conv_b'])
    y_f, h_f = rglru(x_lru, p['lru_wa'][0], p['lru_ba'][0], p['lru_wx'][0], p['lru_bx'][0], p['lru_lam'][0], lru_h0[:, 0], False)
    y_b, h_b = rglru(x_lru, p['lru_wa'][1], p['lru_ba'][1], p['lru_wx'][1], p['lru_bx'][1], p['lru_lam'][1], lru_h0[:, 1], True)
    y_lru = jax.nn.gelu(g_lru.astype(F32)) * (y_f + y_b)
    shp = (B, L, RET_HEADS, RET_DH)
    q = q.astype(F32).reshape(shp)
    k = k.astype(F32).reshape(shp) * (RET_DH ** -0.5)
    v = v.astype(F32).reshape(shp)
    if grid_pos:
        q, k = rope_2d(q), rope_2d(k)
    log_g = -jax.nn.softplus(-p['ret_decay'].astype(F32))
    o_f, s_f = retention_dir(q, k, v, log_g[0], ret_s0[:, 0], False)
    o_b, s_b = retention_dir(q, k, v, log_g[1], ret_s0[:, 1], True)
    o = o_f + o_b
    mu = jnp.mean(o, -1, keepdims=True)
    var = jnp.mean(jnp.square(o - mu), -1, keepdims=True)
    o = ((o - mu) * lax.rsqrt(var + GN_EPS)).reshape(B, L, RET_WIDTH)
    y_ret = jax.nn.silu(g_ret.astype(F32)) * o
    mixed = jnp.concatenate([y_lru, y_ret], -1).astype(h.dtype) @ p['w_out']
    return mixed, jnp.stack([h_f, h_b], 1), jnp.stack([s_f, s_b], 1)


def moe_ffn(h, p):
    T = h.shape[0]
    tok = jnp.arange(T)
    lg = (h @ p['router_g'] + p['router_g_b']).astype(F32)
    pg = jax.nn.softmax(lg, -1)
    g_sel = jnp.argmax(lg, -1)
    p_sel = pg[tok, g_sel]
    le = (jnp.einsum('td,gde->tge', h, p['router_e']) + p['router_e_b']).astype(F32)
    le_sel = le[tok, g_sel]
    top_v, top_i = lax.top_k(le_sel, TOP_K)
    weight = p_sel[:, None] * jax.nn.softmax(top_v, -1)
    expert = g_sel[:, None] * EXPERTS_PER_GROUP + top_i
    A = T * TOP_K
    P = -(-(A + N_EXPERTS * (MOE_BLOCK - 1)) // MOE_BLOCK) * MOE_BLOCK
    n_blk = P // MOE_BLOCK
    flat_e = expert.reshape(-1).astype(jnp.int32)
    flat_tok = jnp.repeat(tok, TOP_K).astype(jnp.int32)
    flat_w = weight.reshape(-1)
    order = jnp.argsort(flat_e)
    se, stok, sw = flat_e[order], flat_tok[order], flat_w[order]
    counts = jnp.bincount(flat_e, length=N_EXPERTS)
    starts = jnp.cumsum(counts) - counts
    padded = ((counts + MOE_BLOCK - 1) // MOE_BLOCK) * MOE_BLOCK
    pends = jnp.cumsum(padded)
    pstarts = pends - padded
    dest = pstarts[se] + (jnp.arange(A) - starts[se])
    row_tok = jnp.zeros((P,), jnp.int32).at[dest].set(stok)
    row_w = jnp.zeros((P,), F32).at[dest].set(sw)
    blk_exp = jnp.minimum(jnp.searchsorted(pends, jnp.arange(n_blk) * MOE_BLOCK, side='right'), N_EXPERTS - 1)
    w_gate, w_up, w_down = p['w_gate'], p['w_up'], p['w_down']

    def blk_fn(args):
        e, tok_b, w_b = args
        xb = h[tok_b]
        a = jax.nn.silu(xb @ w_gate[e]) * (xb @ w_up[e])
        return (a @ w_down[e]) * w_b[:, None].astype(h.dtype)

    out = lax.map(blk_fn, (blk_exp, row_tok.reshape(n_blk, MOE_BLOCK), row_w.reshape(n_blk, MOE_BLOCK)))
    return jnp.zeros_like(h).at[row_tok].add(out.reshape(P, -1))


def trunk_layer(x, mod, p, lru_h0, ret_s0, grid_pos, alpha):
    shift1, scale1, gate1, shift2, scale2, gate2 = jnp.split(mod, 6, axis=-1)
    h = modulate(x, shift1, scale1)
    m, lru_s, ret_s = token_mixers(h, p, lru_h0, ret_s0, grid_pos)
    x = layer_norm_affine(alpha * x + gate1 * m, p['ln1_g'], p['ln1_b'])
    h = modulate(x, shift2, scale2)
    f = moe_ffn(h.reshape(-1, D_MODEL), p).reshape(x.shape)
    x = layer_norm_affine(alpha * x + gate2 * f, p['ln2_g'], p['ln2_b'])
    return x, lru_s, ret_s


def setup_inputs(seed: int = 0) -> dict:
    key = jax.random.key(seed)
    ks = iter(jax.random.split(key, 32))

    def nrm(shape, scale):
        return jax.random.normal(next(ks), shape, F32) * scale

    beta = (8.0 * DEPTH) ** -0.25
    d_inv = D_MODEL ** -0.5
    u = jax.random.uniform(next(ks), (DEPTH, 2, LRU_WIDTH), F32, 0.9, 0.999)
    a0 = u ** (1.0 / LRU_C)
    lru_lam = jnp.log(a0) - jnp.log1p(-a0)
    gam = 1.0 - jnp.exp2(-(5.0 + jnp.arange(RET_HEADS, dtype=F32)))
    ret_decay = (jnp.log(gam) - jnp.log1p(-gam))[None, None, :] + nrm((DEPTH, 2, RET_HEADS), 0.05)
    return {
        'x_prompt': nrm((BATCH, SEQ, D_MODEL), 1.0),
        'x_sample': nrm((DEC_BATCH, DEC_SEQ, D_MODEL), 1.0),
        'state_lru': nrm((DEC_BATCH, DEPTH, 2, LRU_WIDTH), 0.5),
        'state_ret': nrm((DEC_BATCH, DEPTH, 2, RET_HEADS, RET_DH, RET_DH), 1.0),
        'c': nrm((DEC_BATCH, D_MODEL), 1.0),
        'c_ctx': nrm((D_MODEL,), 1.0),
        'w_mod': nrm((DEPTH, D_MODEL, 6 * D_MODEL), 0.5 * d_inv),
        'b_mod': nrm((DEPTH, 6 * D_MODEL), 0.02),
        'w_in': nrm((DEPTH, D_MODEL, IN_COLS), d_inv),
        'conv_w': nrm((DEPTH, CONV_W, LRU_WIDTH), 0.5),
        'conv_b': nrm((DEPTH, LRU_WIDTH), 0.02),
        'lru_wa': nrm((DEPTH, 2, LRU_BLOCKS, LRU_BLOCK_W, LRU_BLOCK_W), LRU_BLOCK_W ** -0.5),
        'lru_ba': nrm((DEPTH, 2, LRU_WIDTH), 0.02),
        'lru_wx': nrm((DEPTH, 2, LRU_BLOCKS, LRU_BLOCK_W, LRU_BLOCK_W), LRU_BLOCK_W ** -0.5),
        'lru_bx': nrm((DEPTH, 2, LRU_WIDTH), 0.02),
        'lru_lam': lru_lam,
        'ret_decay': ret_decay,
        'w_out': nrm((DEPTH, D_MODEL, D_MODEL), d_inv * beta),
        'ln1_g': 1.0 + nrm((DEPTH, D_MODEL), 0.02),
        'ln1_b': nrm((DEPTH, D_MODEL), 0.02),
        'router_g': nrm((DEPTH, D_MODEL, N_GROUPS), d_inv),
        'router_g_b': nrm((DEPTH, N_GROUPS), 0.01),
        'router_e': nrm((DEPTH, N_GROUPS, D_MODEL, EXPERTS_PER_GROUP), d_inv),
        'router_e_b': nrm((DEPTH, N_GROUPS, EXPERTS_PER_GROUP), 0.01),
        'w_gate': nrm((DEPTH, N_EXPERTS, D_MODEL, D_EXPERT), d_inv),
        'w_up': nrm((DEPTH, N_EXPERTS, D_MODEL, D_EXPERT), d_inv),
        'w_down': nrm((DEPTH, N_EXPERTS, D_EXPERT, D_MODEL), (D_EXPERT ** -0.5) * beta),
        'ln2_g': 1.0 + nrm((DEPTH, D_MODEL), 0.02),
        'ln2_b': nrm((DEPTH, D_MODEL), 0.02),
    }


def reference(x_prompt, x_sample, state_lru, state_ret, c, c_ctx, w_mod, b_mod, w_in, conv_w, conv_b,
              lru_wa, lru_ba, lru_wx, lru_bx, lru_lam, ret_decay, w_out, ln1_g, ln1_b,
              router_g, router_g_b, router_e, router_e_b, w_gate, w_up, w_down, ln2_g, ln2_b):
    alpha = (2.0 * DEPTH) ** 0.25
    B = x_prompt.shape[0]
    y_p, y_s = x_prompt, x_sample
    lru_states, ret_states = [], []
    for l in range(DEPTH):
        p = {
            'w_in': w_in[l], 'conv_w': conv_w[l], 'conv_b': conv_b[l],
            'lru_wa': lru_wa[l], 'lru_ba': lru_ba[l], 'lru_wx': lru_wx[l], 'lru_bx': lru_bx[l],
            'lru_lam': lru_lam[l], 'ret_decay': ret_decay[l], 'w_out': w_out[l],
            'ln1_g': ln1_g[l], 'ln1_b': ln1_b[l],
            'router_g': router_g[l], 'router_g_b': router_g_b[l],
            'router_e': router_e[l], 'router_e_b': router_e_b[l],
            'w_gate': w_gate[l], 'w_up': w_up[l], 'w_down': w_down[l],
            'ln2_g': ln2_g[l], 'ln2_b': ln2_b[l],
        }
        mod_ctx = (jax.nn.silu(c_ctx) @ w_mod[l] + b_mod[l])[None, None, :]
        mod_lat = (jax.nn.silu(c) @ w_mod[l] + b_mod[l])[:, None, :]
        zero_lru = jnp.zeros((B, 2, LRU_WIDTH), F32)
        zero_ret = jnp.zeros((B, 2, RET_HEADS, RET_DH, RET_DH), F32)
        y_p, s_lru, s_ret = trunk_layer(y_p, mod_ctx, p, zero_lru, zero_ret, False, alpha)
        lru_states.append(s_lru)
        ret_states.append(s_ret)
        y_s, _, _ = trunk_layer(y_s, mod_lat, p, state_lru[:, l], state_ret[:, l], True, alpha)
    new_state_lru = jnp.stack(lru_states, 1).astype(x_prompt.dtype)
    new_state_ret = jnp.stack(ret_states, 1).astype(x_prompt.dtype)
    return (y_p, y_s, new_state_lru, new_state_ret)
```

```python
import functools

import jax
import jax.numpy as jnp
from jax import lax
from jax.experimental import pallas as pl
from jax.experimental.pallas import tpu as pltpu

F32 = jnp.float32
BF16 = jnp.bfloat16

D_MODEL = 4096
LRU_WIDTH = 2048
LRU_BLOCKS = 16
LANES = 128
RET_HEADS = 16
RET_DH = 128
IN_COLS = 12288
GRID_W = 64
ROPE_BASE = 10000.0
LRU_C = 8.0
N_GROUPS = 4
EXPERTS_PER_GROUP = 8
N_EXPERTS = 32
D_EXPERT = 1024
LN_EPS = 1e-6
GN_EPS = 1e-5
ALPHA = 2.0 ** 0.25

VMEM_LIMIT = 56 * 1024 * 1024
SCAN_PAD = 8

TM_PROJ = 512
TN_IN = 1024
TN_OUT = 512
TM_OUT = 256
TM_MOE = 512
F_CHUNK = 256
TN_DOWN = 1024
GATHER_ROWS = 256
TM_FIN = 256


def _params(sem):
    return pltpu.CompilerParams(dimension_semantics=sem, vmem_limit_bytes=VMEM_LIMIT)


def _softplus(x):
    return jnp.maximum(x, 0.0) + jnp.log1p(jnp.exp(-jnp.abs(x)))


def _ln(x):
    mu = jnp.mean(x, -1, keepdims=True)
    xc = x - mu
    var = jnp.mean(xc * xc, -1, keepdims=True)
    return xc * lax.rsqrt(var + LN_EPS)


def _mod_kernel(cond_ref, w_ref, b_ref, o_ref):
    c = cond_ref[...]
    s = (c * jax.nn.sigmoid(c)).astype(BF16)
    o_ref[...] = jnp.dot(s, w_ref[...].astype(BF16), preferred_element_type=F32) + b_ref[...]


def _modulation(cond, w_mod, b_mod):
    tn = 512
    n = w_mod.shape[1]
    return pl.pallas_call(
        _mod_kernel,
        grid=(n // tn,),
        in_specs=[pl.BlockSpec((8, D_MODEL), lambda j: (0, 0)),
                  pl.BlockSpec((D_MODEL, tn), lambda j: (0, j)),
                  pl.BlockSpec((1, tn), lambda j: (0, j))],
        out_specs=pl.BlockSpec((8, tn), lambda j: (0, j)),
        out_shape=jax.ShapeDtypeStruct((8, n), F32),
        compiler_params=_params(("arbitrary",)),
        name="modulation",
    )(cond, w_mod, b_mod)


def _inproj_kernel(xp_ref, xs_ref, shift_ref, scale_ref, w_ref, o_ref, h_scr, *, n_p_tiles, tiles_per_seq):
    i = pl.program_id(0)
    j = pl.program_id(1)

    def fill(x_ref, row):
        h = _ln(x_ref[...]) * (1.0 + scale_ref[pl.ds(row, 1), :]) + shift_ref[pl.ds(row, 1), :]
        h_scr[...] = h.astype(BF16)

    @pl.when(jnp.logical_and(j == 0, i < n_p_tiles))
    def _():
        fill(xp_ref, 0)

    @pl.when(jnp.logical_and(j == 0, i >= n_p_tiles))
    def _():
        fill(xs_ref, 1 + (i - n_p_tiles) // tiles_per_seq)

    o_ref[...] = jnp.dot(h_scr[...], w_ref[...], preferred_element_type=F32)


def _in_projection(xp, xs, shift, scale, w_in_bf16, dec_seq):
    tp, ts = xp.shape[0], xs.shape[0]
    tm, tn = TM_PROJ, TN_IN
    n_p, n_s = tp // tm, ts // tm
    kern = functools.partial(_inproj_kernel, n_p_tiles=n_p, tiles_per_seq=dec_seq // tm)
    return pl.pallas_call(
        kern,
        grid=(n_p + n_s, IN_COLS // tn),
        in_specs=[pl.BlockSpec((tm, D_MODEL), lambda i, j: (jnp.minimum(i, n_p - 1), 0),
                               pipeline_mode=pl.Buffered(1)),
                  pl.BlockSpec((tm, D_MODEL), lambda i, j: (jnp.maximum(i - n_p, 0), 0),
                               pipeline_mode=pl.Buffered(1)),
                  pl.BlockSpec((8, D_MODEL), lambda i, j: (0, 0)),
                  pl.BlockSpec((8, D_MODEL), lambda i, j: (0, 0)),
                  pl.BlockSpec((D_MODEL, tn), lambda i, j: (0, j))],
        out_specs=pl.BlockSpec((tm, tn), lambda i, j: (i, j)),
        out_shape=jax.ShapeDtypeStruct((tp + ts, IN_COLS), F32),
        scratch_shapes=[pltpu.VMEM((tm, D_MODEL), BF16)],
        compiler_params=_params(("arbitrary", "arbitrary")),
        name="in_projection",
    )(xp, xs, shift, scale, w_in_bf16)


def _lru_kernel(x_ref, g_ref, cw_ref, cb_ref, wg_ref, pb_ref, h0_ref, y_ref, st_ref,
                xc_scr, af_scr, uf_scr, ab_scr, ub_scr, *, bg, seq, pitch):
    rows = lax.broadcasted_iota(jnp.int32, (seq, LANES), 0)
    w = cw_ref[...]
    for b in range(bg):
        x = x_ref[b]
        xm2 = jnp.where(rows >= 2, pltpu.roll(x, 2, 0), 0.0)
        xm1 = jnp.where(rows >= 1, pltpu.roll(x, 1, 0), 0.0)
        xp1 = jnp.where(rows < seq - 1, pltpu.roll(x, seq - 1, 0), 0.0)
        xc_scr[pl.ds(b * seq, seq), :] = (cb_ref[...] + xm2 * w[0:1] + xm1 * w[1:2] + x * w[2:3] + xp1 * w[3:4])

    xc = xc_scr[...]
    gates = jnp.dot(xc.astype(BF16), wg_ref[0], preferred_element_type=F32)
    pb = pb_ref[...]
    for d, (a_scr, u_scr) in enumerate(((af_scr, uf_scr), (ab_scr, ub_scr))):
        r = jax.nn.sigmoid(gates[:, (2 * d) * LANES:(2 * d + 1) * LANES] + pb[3 * d:3 * d + 1])
        ig = jax.nn.sigmoid(gates[:, (2 * d + 1) * LANES:(2 * d + 2) * LANES] + pb[3 * d + 1:3 * d + 2])
        log_a = (-LRU_C) * r * _softplus(-pb[3 * d + 2:3 * d + 3])
        a = jnp.exp(log_a)
        u = jnp.sqrt(-jnp.tanh(log_a) * (1.0 + a * a)) * (ig * xc)
        for b in range(bg):
            a_scr[pl.ds(b * pitch, seq), :] = a[b * seq:(b + 1) * seq]
            u_scr[pl.ds(b * pitch, seq), :] = u[b * seq:(b + 1) * seq]

    def step(t, carry):
        hf, hb = carry
        tb = seq - 1 - t
        hf = af_scr[pl.ds(t, bg, stride=pitch), :] * hf + uf_scr[pl.ds(t, bg, stride=pitch), :]
        uf_scr[pl.ds(t, bg, stride=pitch), :] = hf
        hb = ab_scr[pl.ds(tb, bg, stride=pitch), :] * hb + ub_scr[pl.ds(tb, bg, stride=pitch), :]
        ub_scr[pl.ds(tb, bg, stride=pitch), :] = hb
        return hf, hb

    hf, hb = lax.fori_loop(0, seq, step, (h0_ref[0], h0_ref[1]), unroll=8)
    st_ref[0] = hf
    st_ref[1] = hb
    for b in range(bg):
        hs = uf_scr[pl.ds(b * pitch, seq), :] + ub_scr[pl.ds(b * pitch, seq), :]
        y_ref[b] = (jax.nn.gelu(g_ref[b]) * hs).astype(BF16)


def _lru_mixer(proj3, seq0, nseq, bg, conv_w, conv_b, w_gates, pb, h0):
    seq = proj3.shape[1]
    pitch = seq + SCAN_PAD
    off = seq0 // bg
    kern = functools.partial(_lru_kernel, bg=bg, seq=seq, pitch=pitch)
    return pl.pallas_call(
        kern,
        grid=(nseq // bg, LRU_BLOCKS),
        in_specs=[pl.BlockSpec((bg, seq, LANES), lambda b, c: (b + off, 0, c)),
                  pl.BlockSpec((bg, seq, LANES), lambda b, c: (b + off, 0, LRU_BLOCKS + c)),
                  pl.BlockSpec((4, LANES), lambda b, c: (0, c)),
                  pl.BlockSpec((1, LANES), lambda b, c: (0, c)),
                  pl.BlockSpec((1, LANES, 4 * LANES), lambda b, c: (c, 0, 0)),
                  pl.BlockSpec((6, LANES), lambda b, c: (0, c)),
                  pl.BlockSpec((2, bg, LANES), lambda b, c: (0, b, c))],
        out_specs=[pl.BlockSpec((bg, seq, LANES), lambda b, c: (b, 0, c)),
                   pl.BlockSpec((2, bg, LANES), lambda b, c: (0, b, c))],
        out_shape=[jax.ShapeDtypeStruct((nseq, seq, LRU_WIDTH), BF16),
                   jax.ShapeDtypeStruct((2, nseq, LRU_WIDTH), F32)],
        scratch_shapes=[pltpu.VMEM((bg * seq, LANES), F32)] + [pltpu.VMEM((bg * pitch, LANES), F32)] * 4,
        compiler_params=_params(("arbitrary", "arbitrary")),
        name="lru_mixer",
    )(proj3, proj3, conv_w, conv_b, w_gates, pb, h0)


def _rope(x, cos, sin_signed, first_half):
    partner = jnp.where(first_half, pltpu.roll(x, LANES - 32, 1), pltpu.roll(x, 32, 1))
    return x * cos + partner * sin_signed


def _ret_kernel(*refs, bg, seq, rope, has_state, emit_state, qb):
    refs = list(refs)
    q_ref, k_ref, v_ref, g_ref, dec_ref = refs[:5]
    pos = 5
    if rope:
        cos_ref, sin_ref = refs[pos:pos + 2]
        pos += 2
    if has_state:
        s0_ref = refs[pos]
        pos += 1
    y_ref = refs[pos]
    pos += 1
    if emit_state:
        st_ref = refs[pos]
        pos += 1
    mask_scr = refs[pos]

    log_g = -_softplus(-dec_ref[0])
    lgf, lgb = log_g[0:1], log_g[1:2]
    reps = seq // LANES
    lgf_row = jnp.concatenate([lgf] * reps, axis=1)
    lgb_row = jnp.concatenate([lgb] * reps, axis=1)

    @pl.when(pl.program_id(1) == 0)
    def _():
        for blk in range(seq // qb):
            ti = lax.broadcasted_iota(jnp.int32, (qb, seq), 0) + blk * qb
            si = lax.broadcasted_iota(jnp.int32, (qb, seq), 1)
            dist = (ti - si).astype(F32)
            e = jnp.where(dist >= 0, dist * lgf_row, (-dist) * lgb_row)
            mask_scr[pl.ds(blk * qb, qb), :] = jnp.where(dist == 0, 2.0, jnp.exp(e))

    lane = lax.broadcasted_iota(jnp.int32, (seq, LANES), 1)
    first_half = (lane % 64) < 32
    trow = lax.broadcasted_iota(jnp.int32, (seq, LANES), 0).astype(F32)
    for b in range(bg):
        q = q_ref[b]
        k = k_ref[b] * (RET_DH ** -0.5)
        v16 = v_ref[b].astype(BF16)
        if rope:
            q = _rope(q, cos_ref[...], sin_ref[...], first_half)
            k = _rope(k, cos_ref[...], sin_ref[...], first_half)
        q16 = q.astype(BF16)
        k16 = k.astype(BF16)
        if has_state:
            qf16 = (q * jnp.exp((trow + 1.0) * lgf)).astype(BF16)
            qb16 = (q * jnp.exp((float(seq) - trow) * lgb)).astype(BF16)
            s0f = s0_ref[b, 0].astype(BF16)
            s0b = s0_ref[b, 1].astype(BF16)
        for blk in range(seq // qb):
            sl = slice(blk * qb, (blk + 1) * qb)
            s = lax.dot_general(q16[sl], k16, (((1,), (1,)), ((), ())), preferred_element_type=F32)
            p = (s * mask_scr[pl.ds(blk * qb, qb), :]).astype(BF16)
            o = jnp.dot(p, v16, preferred_element_type=F32)
            if has_state:
                o = o + jnp.dot(qf16[sl], s0f, preferred_element_type=F32)
                o = o + jnp.dot(qb16[sl], s0b, preferred_element_type=F32)
            mu = jnp.mean(o, -1, keepdims=True)
            oc = o - mu
            var = jnp.mean(oc * oc, -1, keepdims=True)
            on = oc * lax.rsqrt(var + GN_EPS)
            gt = g_ref[b, pl.ds(blk * qb, qb), :]
            y_ref[b, pl.ds(blk * qb, qb), :] = (gt * jax.nn.sigmoid(gt) * on).astype(BF16)
        if emit_state:
            kf16 = (k * jnp.exp((float(seq - 1) - trow) * lgf)).astype(BF16)
            kb16 = (k * jnp.exp(trow * lgb)).astype(BF16)
            sf = lax.dot_general(kf16, v16, (((0,), (0,)), ((), ())), preferred_element_type=F32)
            sb = lax.dot_general(kb16, v16, (((0,), (0,)), ((), ())), preferred_element_type=F32)
            if has_state:
                sf = sf + jnp.exp(float(seq) * lgf) * s0_ref[b, 0]
                sb = sb + jnp.exp(float(seq) * lgb) * s0_ref[b, 1]
            st_ref[b, 0] = sf
            st_ref[b, 1] = sb


def _ret_mixer(proj3, seq0, nseq, bg, decay, rope_tabs=None, s0=None, emit_state=False):
    seq = proj3.shape[1]
    off = seq0 // bg
    qb = min(seq, 256)
    rope = rope_tabs is not None
    has_state = s0 is not None
    kern = functools.partial(_ret_kernel, bg=bg, seq=seq, rope=rope, has_state=has_state,
                             emit_state=emit_state, qb=qb)

    def col(base):
        return pl.BlockSpec((bg, seq, LANES), lambda h, b: (b + off, 0, base + h))

    st_spec = pl.BlockSpec((bg, None, 2, None, RET_DH, RET_DH), lambda h, b: (b, 0, 0, h, 0, 0))
    in_specs = [col(2 * LRU_BLOCKS), col(2 * LRU_BLOCKS + RET_HEADS), col(2 * LRU_BLOCKS + 2 * RET_HEADS),
                col(2 * LRU_BLOCKS + 3 * RET_HEADS), pl.BlockSpec((1, 2, LANES), lambda h, b: (h, 0, 0))]
    args = [proj3, proj3, proj3, proj3, decay]
    if rope:
        in_specs += [pl.BlockSpec((seq, LANES), lambda h, b: (0, 0))] * 2
        args += list(rope_tabs)
    if has_state:
        in_specs.append(st_spec)
        args.append(s0)
    out_specs = [pl.BlockSpec((bg, seq, LANES), lambda h, b: (b, 0, h))]
    out_shape = [jax.ShapeDtypeStruct((nseq, seq, RET_HEADS * RET_DH), BF16)]
    if emit_state:
        out_specs.append(st_spec)
        out_shape.append(jax.ShapeDtypeStruct((nseq, 1, 2, RET_HEADS, RET_DH, RET_DH), F32))
    return pl.pallas_call(
        kern,
        grid=(RET_HEADS, nseq // bg),
        in_specs=in_specs,
        out_specs=out_specs,
        out_shape=out_shape,
        scratch_shapes=[pltpu.VMEM((seq, seq), F32)],
        compiler_params=_params(("arbitrary", "arbitrary")),
        name="ret_mixer",
    )(*args)


def _rope_tables(seq):
    nf = RET_DH // 4
    freqs = ROPE_BASE ** (-jnp.arange(nf, dtype=F32) / nf)
    t = jnp.arange(seq)
    row = (t // GRID_W).astype(F32)[:, None] * freqs[None, :]
    colp = (t % GRID_W).astype(F32)[:, None] * freqs[None, :]
    cos = jnp.concatenate([jnp.cos(row), jnp.cos(row), jnp.cos(colp), jnp.cos(colp)], -1)
    sin = jnp.concatenate([-jnp.sin(row), jnp.sin(row), -jnp.sin(colp), jnp.sin(colp)], -1)
    return cos, sin


def _outproj_kernel(xp_ref, xs_ref, ylp_ref, yrp_ref, yls_ref, yrs_ref, wa_ref, wb_ref, gate_ref,
                    lng_ref, lnb_ref, shift_ref, scale_ref, wr_ref, br_ref,
                    z_ref, h2_ref, lg_ref, z_scr, *, n_p_tiles, tiles_per_seq, nj, tn):
    i = pl.program_id(0)
    j = pl.program_id(1)
    is_p = i < n_p_tiles
    row = jnp.where(is_p, 0, 1 + (i - n_p_tiles) // tiles_per_seq)

    def mix(x_ref, yl_ref, yr_ref):
        m = jnp.dot(yl_ref[...], wa_ref[...], preferred_element_type=F32)
        m = m + jnp.dot(yr_ref[...], wb_ref[...], preferred_element_type=F32)
        z = ALPHA * x_ref[...] + gate_ref[pl.ds(row, 1), :] * m
        z_ref[...] = z
        z_scr[j] = z

    @pl.when(is_p)
    def _():
        mix(xp_ref, ylp_ref, yrp_ref)

    @pl.when(jnp.logical_not(is_p))
    def _():
        mix(xs_ref, yls_ref, yrs_ref)

    @pl.when(j == nj - 1)
    def _():
        tm = z_scr.shape[1]
        inv_d = 1.0 / D_MODEL
        s1 = jnp.zeros((tm, 1), F32)
        for c in range(nj):
            s1 = s1 + jnp.sum(z_scr[c], -1, keepdims=True)
        mu = s1 * inv_d
        s2 = jnp.zeros((tm, 1), F32)
        for c in range(nj):
            zc = z_scr[c] - mu
            s2 = s2 + jnp.sum(zc * zc, -1, keepdims=True)
        rstd = lax.rsqrt(s2 * inv_d + LN_EPS)
        t1 = jnp.zeros((tm, 1), F32)
        for c in range(nj):
            cs = slice(c * tn, (c + 1) * tn)
            x1 = (z_scr[c] - mu) * rstd * lng_ref[:, cs] + lnb_ref[:, cs]
            z_scr[c] = x1
            t1 = t1 + jnp.sum(x1, -1, keepdims=True)
        mu2 = t1 * inv_d
        t2 = jnp.zeros((tm, 1), F32)
        for c in range(nj):
            xc = z_scr[c] - mu2
            t2 = t2 + jnp.sum(xc * xc, -1, keepdims=True)
        rstd2 = lax.rsqrt(t2 * inv_d + LN_EPS)
        logits = jnp.zeros((tm, LANES), F32) + br_ref[...]
        for c in range(nj):
            cs = slice(c * tn, (c + 1) * tn)
            h2 = (z_scr[c] - mu2) * rstd2 * (1.0 + scale_ref[pl.ds(row, 1), cs]) + shift_ref[pl.ds(row, 1), cs]
            h2_ref[:, cs] = h2
            logits = logits + jnp.dot(h2.astype(BF16), wr_ref[cs, :], preferred_element_type=F32)
        lg_ref[...] = logits


def _out_projection(xp, xs, ylp, yrp, yls, yrs, w_out_bf16, gate1, ln_g, ln_b, shift2, scale2, w_router, b_router,
                    dec_seq):
    tp, ts = xp.shape[0], xs.shape[0]
    tm, tn = TM_OUT, TN_OUT
    n_p, n_s = tp // tm, ts // tm
    nj = D_MODEL // tn
    half = LRU_WIDTH
    kern = functools.partial(_outproj_kernel, n_p_tiles=n_p, tiles_per_seq=dec_seq // tm, nj=nj, tn=tn)
    p_idx = lambda i, j: (jnp.minimum(i, n_p - 1), 0)
    s_idx = lambda i, j: (jnp.maximum(i - n_p, 0), 0)
    full = lambda i, j: (0, 0)
    return pl.pallas_call(
        kern,
        grid=(n_p + n_s, nj),
        in_specs=[pl.BlockSpec((tm, tn), lambda i, j: (jnp.minimum(i, n_p - 1), j)),
                  pl.BlockSpec((tm, tn), lambda i, j: (jnp.maximum(i - n_p, 0), j)),
                  pl.BlockSpec((tm, half), p_idx), pl.BlockSpec((tm, half), p_idx),
                  pl.BlockSpec((tm, half), s_idx), pl.BlockSpec((tm, half), s_idx),
                  pl.BlockSpec((half, tn), lambda i, j: (0, j)),
                  pl.BlockSpec((half, tn), lambda i, j: (1, j)),
                  pl.BlockSpec((8, tn), lambda i, j: (0, j)),
                  pl.BlockSpec((1, D_MODEL), full), pl.BlockSpec((1, D_MODEL), full),
                  pl.BlockSpec((8, D_MODEL), full), pl.BlockSpec((8, D_MODEL), full),
                  pl.BlockSpec((D_MODEL, LANES), full), pl.BlockSpec((1, LANES), full)],
        out_specs=[pl.BlockSpec((tm, tn), lambda i, j: (i, j)),
                   pl.BlockSpec((tm, D_MODEL), lambda i, j: (i, 0)),
                   pl.BlockSpec((tm, LANES), lambda i, j: (i, 0))],
        out_shape=[jax.ShapeDtypeStruct((tp + ts, D_MODEL), F32),
                   jax.ShapeDtypeStruct((tp + ts, D_MODEL), F32),
                   jax.ShapeDtypeStruct((tp + ts, LANES), F32)],
        scratch_shapes=[pltpu.VMEM((nj, tm, tn), F32)],
        compiler_params=_params(("arbitrary", "arbitrary")),
        name="out_projection",
    )(xp, xs, ylp, yrp, yls, yrs, w_out_bf16, w_out_bf16, gate1, ln_g, ln_b, shift2, scale2, w_router, b_router)


def _route(logits):
    lg = logits[:, :N_GROUPS]
    le = logits[:, N_GROUPS:N_GROUPS + N_EXPERTS].reshape(-1, N_GROUPS, EXPERTS_PER_GROUP)
    pg = jax.nn.softmax(lg, -1)
    g_sel = jnp.argmax(lg, -1)
    p_sel = jnp.take_along_axis(pg, g_sel[:, None], 1)[:, 0]
    le_sel = jnp.take_along_axis(le, g_sel[:, None, None], 1)[:, 0]
    top_v, top_i = lax.top_k(le_sel, 2)
    weight = p_sel[:, None] * jax.nn.softmax(top_v, -1)
    expert = (g_sel[:, None] * EXPERTS_PER_GROUP + top_i).astype(jnp.int32)
    return expert, weight


def _dispatch_plan(expert, n_blocks):
    flat_e = expert.reshape(-1)
    n_assign = flat_e.shape[0]
    onehot = (flat_e[:, None] == jnp.arange(N_EXPERTS, dtype=jnp.int32)[None, :]).astype(jnp.int32)
    csum = jnp.cumsum(onehot, 0)
    counts = csum[-1]
    rank = jnp.sum(onehot * (csum - 1), 1)
    padded = ((counts + TM_MOE - 1) // TM_MOE) * TM_MOE
    pends = jnp.cumsum(padded)
    pstarts = pends - padded
    dest = (jnp.sum(onehot * pstarts[None, :], 1) + rank).astype(jnp.int32)
    tok = (jnp.arange(n_assign, dtype=jnp.int32) // 2)
    row_tok = jnp.zeros((n_blocks * TM_MOE,), jnp.int32).at[dest].set(tok)
    blk_start = jnp.arange(n_blocks, dtype=jnp.int32) * TM_MOE
    blk_exp_raw = jnp.searchsorted(pends, blk_start, side="right").astype(jnp.int32)
    used = blk_start < pends[-1]
    last_exp = jnp.max(jnp.where(counts > 0, jnp.arange(N_EXPERTS, dtype=jnp.int32), 0))
    blk_exp = jnp.where(used, jnp.minimum(blk_exp_raw, N_EXPERTS - 1), last_exp)
    e_start = pstarts[blk_exp]
    blk_rows = jnp.where(used, jnp.clip(counts[blk_exp] - (blk_start - e_start), 0, TM_MOE), 0).astype(jnp.int32)
    return dest, row_tok, blk_exp, blk_rows


def _expert_kernel(exp_ref, rows_ref, tok_ref, h_hbm, wg_ref, wu_ref, wd_ref, y_ref, stage, xs16, a_scr, sem,
                   *, nf):
    i = pl.program_id(0)
    t = pl.program_id(1)
    n = rows_ref[i]
    half = stage.shape[0]

    def row_copy(src_row, dst_row):
        return pltpu.make_async_copy(h_hbm.at[pl.ds(src_row, 1)], stage.at[pl.ds(dst_row, 1)], sem)

    @pl.when(jnp.logical_and(i == 0, t == 0))
    def _():
        stage[...] = jnp.zeros_like(stage)
        xs16[...] = jnp.zeros_like(xs16)

    @pl.when(jnp.logical_and(t == 0, n > 0))
    def _():
        for part in range(TM_MOE // half):
            cnt = jnp.clip(n - part * half, 0, half)
            base = i * TM_MOE + part * half

            def issue(r, c):
                row_copy(tok_ref[base + r], r).start()
                return c

            lax.fori_loop(0, cnt, issue, 0)

            def drain(r, c):
                row_copy(0, r).wait()
                return c

            lax.fori_loop(0, cnt, drain, 0)
            xs16[part * half:(part + 1) * half, :] = stage[...].astype(BF16)

    @pl.when(jnp.logical_and(t < nf, n > 0))
    def _():
        x = xs16[...]
        g = jnp.dot(x, wg_ref[...].astype(BF16), preferred_element_type=F32)
        u = jnp.dot(x, wu_ref[...].astype(BF16), preferred_element_type=F32)
        a_scr[t] = (g * jax.nn.sigmoid(g) * u).astype(BF16)

    @pl.when(jnp.logical_and(t >= nf, n > 0))
    def _():
        acc = jnp.dot(a_scr[0], wd_ref[0:F_CHUNK, :].astype(BF16), preferred_element_type=F32)
        for f in range(1, nf):
            acc = acc + jnp.dot(a_scr[f], wd_ref[f * F_CHUNK:(f + 1) * F_CHUNK, :].astype(BF16),
                                preferred_element_type=F32)
        y_ref[...] = acc

    @pl.when(jnp.logical_and(t >= nf, n == 0))
    def _():
        y_ref[...] = jnp.zeros_like(y_ref)


def _experts(h2, blk_exp, blk_rows, row_tok, w_gate, w_up, w_down):
    n_blocks = blk_exp.shape[0]
    nf = D_EXPERT // F_CHUNK
    nd = D_MODEL // TN_DOWN

    def up_idx(i, t, e, r, tok):
        return (e[i], 0, jnp.where(r[i] > 0, jnp.minimum(t, nf - 1), nf - 1))

    def down_idx(i, t, e, r, tok):
        return (e[i], 0, jnp.where(r[i] > 0, jnp.maximum(t - nf, 0), nd - 1))

    grid_spec = pltpu.PrefetchScalarGridSpec(
        num_scalar_prefetch=3,
        grid=(n_blocks, nf + nd),
        in_specs=[pl.BlockSpec(memory_space=pl.ANY),
                  pl.BlockSpec((None, D_MODEL, F_CHUNK), up_idx),
                  pl.BlockSpec((None, D_MODEL, F_CHUNK), up_idx),
                  pl.BlockSpec((None, D_EXPERT, TN_DOWN), down_idx)],
        out_specs=pl.BlockSpec((TM_MOE, TN_DOWN), lambda i, t, e, r, tok: (i, jnp.maximum(t - nf, 0))),
        scratch_shapes=[pltpu.VMEM((GATHER_ROWS, D_MODEL), F32),
                        pltpu.VMEM((TM_MOE, D_MODEL), BF16),
                        pltpu.VMEM((nf, TM_MOE, F_CHUNK), BF16),
                        pltpu.SemaphoreType.DMA(())],
    )
    return pl.pallas_call(
        functools.partial(_expert_kernel, nf=nf),
        grid_spec=grid_spec,
        out_shape=jax.ShapeDtypeStruct((n_blocks * TM_MOE, D_MODEL), F32),
        compiler_params=_params(("arbitrary", "arbitrary")),
        name="experts",
    )(blk_exp, blk_rows, row_tok, h2, w_gate, w_up, w_down)


def _final_kernel(dest_ref, y_hbm, z_ref, wt_ref, gate_ref, g1_ref, b1_ref, g2_ref, b2_ref,
                  op_ref, os_ref, ybuf, sem, *, n_p_tiles, tiles_per_seq):
    i = pl.program_id(0)
    tm = z_ref.shape[0]
    base = i * tm * 2

    def row_copy(src_row, k, r):
        return pltpu.make_async_copy(y_hbm.at[pl.ds(src_row, 1)], ybuf.at[k, pl.ds(r, 1)], sem)

    def issue(r, c):
        row_copy(dest_ref[base + 2 * r], 0, r).start()
        row_copy(dest_ref[base + 2 * r + 1], 1, r).start()
        return c

    lax.fori_loop(0, tm, issue, 0)

    def drain(r, c):
        row_copy(0, 0, r).wait()
        row_copy(0, 1, r).wait()
        return c

    lax.fori_loop(0, tm, drain, 0)

    is_p = i < n_p_tiles
    row = jnp.where(is_p, 0, 1 + (i - n_p_tiles) // tiles_per_seq)
    wt = wt_ref[...]
    f = wt[:, 0:1] * ybuf[0] + wt[:, 1:2] * ybuf[1]
    x1 = _ln(z_ref[...]) * g1_ref[...] + b1_ref[...]
    out = _ln(ALPHA * x1 + gate_ref[pl.ds(row, 1), :] * f) * g2_ref[...] + b2_ref[...]

    @pl.when(is_p)
    def _():
        op_ref[...] = out

    @pl.when(jnp.logical_not(is_p))
    def _():
        os_ref[...] = out


def _combine(dest, y_rows, z, weight, gate2, g1, b1, g2, b2, n_prompt, dec_seq):
    t = z.shape[0]
    tm = TM_FIN
    n_p = n_prompt // tm
    n_s = (t - n_prompt) // tm
    full = lambda i, d: (0, 0)
    grid_spec = pltpu.PrefetchScalarGridSpec(
        num_scalar_prefetch=1,
        grid=(n_p + n_s,),
        in_specs=[pl.BlockSpec(memory_space=pl.ANY),
                  pl.BlockSpec((tm, D_MODEL), lambda i, d: (i, 0)),
                  pl.BlockSpec((tm, 2), lambda i, d: (i, 0)),
                  pl.BlockSpec((8, D_MODEL), full),
                  pl.BlockSpec((1, D_MODEL), full), pl.BlockSpec((1, D_MODEL), full),
                  pl.BlockSpec((1, D_MODEL), full), pl.BlockSpec((1, D_MODEL), full)],
        out_specs=[pl.BlockSpec((tm, D_MODEL), lambda i, d: (jnp.minimum(i, n_p - 1), 0)),
                   pl.BlockSpec((tm, D_MODEL), lambda i, d: (jnp.maximum(i - n_p, 0), 0))],
        scratch_shapes=[pltpu.VMEM((2, tm, D_MODEL), F32), pltpu.SemaphoreType.DMA(())],
    )
    return pl.pallas_call(
        functools.partial(_final_kernel, n_p_tiles=n_p, tiles_per_seq=dec_seq // tm),
        grid_spec=grid_spec,
        out_shape=[jax.ShapeDtypeStruct((n_prompt, D_MODEL), F32),
                   jax.ShapeDtypeStruct((t - n_prompt, D_MODEL), F32)],
        compiler_params=_params(("arbitrary",)),
        name="combine",
    )(dest, y_rows, z, weight, gate2, g1, b1, g2, b2)


def kernel(x_prompt, x_sample, state_lru, state_ret, c, c_ctx, w_mod, b_mod, w_in, conv_w, conv_b, lru_wa, lru_ba,
           lru_wx, lru_bx, lru_lam, ret_decay, w_out, ln1_g, ln1_b, router_g, router_g_b, router_e, router_e_b,
           w_gate, w_up, w_down, ln2_g, ln2_b):
    assert w_in.shape[0] == 1, "single trunk layer"
    nb, seq, d = x_prompt.shape
    nbs, dec_seq, _ = x_sample.shape
    tp, ts = nb * seq, nbs * dec_seq
    assert tp % dec_seq == 0 and d == D_MODEL

    cond = jnp.zeros((8, d), F32).at[0].set(c_ctx).at[1:1 + nbs].set(c)
    mod = _modulation(cond, w_mod[0], b_mod[0][None, :])
    shift1, scale1, gate1, shift2, scale2, gate2 = [mod[:, k * d:(k + 1) * d] for k in range(6)]

    xp = x_prompt.reshape(tp, d)
    xs = x_sample.reshape(ts, d)
    proj = _in_projection(xp, xs, shift1, scale1, w_in[0].astype(BF16), dec_seq)

    w_gates = jnp.concatenate([lru_wa[0, 0], lru_wx[0, 0], lru_wa[0, 1], lru_wx[0, 1]], -1).astype(BF16)
    pb = jnp.stack([lru_ba[0, 0], lru_bx[0, 0], lru_lam[0, 0], lru_ba[0, 1], lru_bx[0, 1], lru_lam[0, 1]], 0)
    proj_p = proj.reshape((tp + ts) // seq, seq, IN_COLS)
    proj_s = proj.reshape((tp + ts) // dec_seq, dec_seq, IN_COLS)
    ylp, st_lru = _lru_mixer(proj_p, 0, nb, 8, conv_w[0], conv_b[0][None, :], w_gates, pb,
                             jnp.zeros((2, nb, LRU_WIDTH), F32))
    yls, _ = _lru_mixer(proj_s, tp // dec_seq, nbs, nbs, conv_w[0], conv_b[0][None, :], w_gates, pb,
                        jnp.swapaxes(state_lru[:, 0], 0, 1))

    decay = jnp.broadcast_to(ret_decay[0].T[:, :, None], (RET_HEADS, 2, LANES))
    yrp, st_ret = _ret_mixer(proj_p, 0, nb, 8, decay, emit_state=True)
    (yrs,) = _ret_mixer(proj_s, tp // dec_seq, nbs, 1, decay, rope_tabs=_rope_tables(dec_seq), s0=state_ret)

    w_router = jnp.concatenate(
        [router_g[0], jnp.transpose(router_e[0], (1, 0, 2)).reshape(d, N_EXPERTS),
         jnp.zeros((d, LANES - N_GROUPS - N_EXPERTS), F32)], -1).astype(BF16)
    b_router = jnp.concatenate([router_g_b[0], router_e_b[0].reshape(-1),
                                jnp.zeros((LANES - N_GROUPS - N_EXPERTS,), F32)])[None, :]
    z, h2, logits = _out_projection(
        xp, xs, ylp.reshape(tp, -1), yrp.reshape(tp, -1), yls.reshape(ts, -1), yrs.reshape(ts, -1),
        w_out[0].astype(BF16), gate1, ln1_g, ln1_b, shift2, scale2, w_router, b_router, dec_seq)

    expert, weight = _route(logits)
    n_assign = 2 * (tp + ts)
    n_blocks = -(-(n_assign + N_EXPERTS * (TM_MOE - 1)) // TM_MOE)
    dest, row_tok, blk_exp, blk_rows = _dispatch_plan(expert, n_blocks)
    y_rows = _experts(h2, blk_exp, blk_rows, row_tok, w_gate[0], w_up[0], w_down[0])
    y_p, y_s = _combine(dest, y_rows, z, weight, gate2, ln1_g, ln1_b, ln2_g, ln2_b, tp, dec_seq)

    new_state_lru = jnp.swapaxes(st_lru, 0, 1)[:, None]
    return (y_p.reshape(nb, seq, d), y_s.reshape(nbs, dec_seq, d), new_state_lru, st_ret)
```

```python
import functools

import jax
import jax.numpy as jnp
from jax import lax
from jax.experimental import pallas as pl
from jax.experimental.pallas import tpu as pltpu

F32 = jnp.float32
BF16 = jnp.bfloat16

D_MODEL = 4096
LRU_WIDTH = 2048
LRU_BLOCKS = 16
LANES = 128
RET_HEADS = 16
RET_DH = 128
IN_COLS = 12288
GRID_W = 64
ROPE_BASE = 10000.0
LRU_C = 8.0
N_GROUPS = 4
EXPERTS_PER_GROUP = 8
N_EXPERTS = 32
D_EXPERT = 1024
LN_EPS = 1e-6
GN_EPS = 1e-5
ALPHA = 2.0 ** 0.25

VMEM_LIMIT = 56 * 1024 * 1024
SCAN_PAD = 8

TM_PROJ = 512
TN_IN = 1024
TN_OUT = 512
TM_OUT = 512
SUB_ROWS = 256
SUPER = 4
F_CHUNK = 256
TN_DOWN = 512
DISPATCH_ROWS = 1024
TM_FIN = 256
HALF_D = D_MODEL // 2


def _params(sem, vmem_limit=VMEM_LIMIT):
    return pltpu.CompilerParams(dimension_semantics=sem, vmem_limit_bytes=vmem_limit)


def _pack_bf16_pair(lo, hi):
    lo_bits = pltpu.bitcast(lo.astype(BF16).astype(F32), jnp.uint32) >> 16
    hi_bits = pltpu.bitcast(hi.astype(BF16).astype(F32), jnp.uint32) & jnp.uint32(0xFFFF0000)
    return lo_bits | hi_bits


def _unpack_bf16_pair(words):
    lo = pltpu.bitcast(words << 16, F32).astype(BF16)
    hi = pltpu.bitcast(words & jnp.uint32(0xFFFF0000), F32).astype(BF16)
    return jnp.concatenate([lo, hi], axis=1)


def _softplus(x):
    return jnp.maximum(x, 0.0) + jnp.log1p(jnp.exp(-jnp.abs(x)))


def _ln(x):
    mu = jnp.mean(x, -1, keepdims=True)
    xc = x - mu
    var = jnp.mean(xc * xc, -1, keepdims=True)
    return xc * lax.rsqrt(var + LN_EPS)


def _mod_kernel(cond_ref, w_ref, b_ref, o_ref):
    c = cond_ref[...]
    s = (c * jax.nn.sigmoid(c)).astype(BF16)
    o_ref[...] = jnp.dot(s, w_ref[...].astype(BF16), preferred_element_type=F32) + b_ref[...]


def _modulation(cond, w_mod, b_mod):
    tn = 512
    n = w_mod.shape[1]
    return pl.pallas_call(
        _mod_kernel,
        grid=(n // tn,),
        in_specs=[pl.BlockSpec((8, D_MODEL), lambda j: (0, 0)),
                  pl.BlockSpec((D_MODEL, tn), lambda j: (0, j)),
                  pl.BlockSpec((1, tn), lambda j: (0, j))],
        out_specs=pl.BlockSpec((8, tn), lambda j: (0, j)),
        out_shape=jax.ShapeDtypeStruct((8, n), F32),
        compiler_params=_params(("arbitrary",)),
        name="modulation",
    )(cond, w_mod, b_mod)


def _inproj_kernel(xp_ref, xs_ref, shift_ref, scale_ref, w_ref, o_ref, h_scr, *, n_p_tiles, tiles_per_seq):
    i = pl.program_id(0)
    j = pl.program_id(1)

    def fill(x_ref, row):
        h = _ln(x_ref[...]) * (1.0 + scale_ref[pl.ds(row, 1), :]) + shift_ref[pl.ds(row, 1), :]
        h_scr[...] = h.astype(BF16)

    @pl.when(jnp.logical_and(j == 0, i < n_p_tiles))
    def _():
        fill(xp_ref, 0)

    @pl.when(jnp.logical_and(j == 0, i >= n_p_tiles))
    def _():
        fill(xs_ref, 1 + (i - n_p_tiles) // tiles_per_seq)

    o_ref[...] = jnp.dot(h_scr[...], w_ref[...], preferred_element_type=F32)


def _in_projection(xp, xs, shift, scale, w_in_bf16, dec_seq):
    tp, ts = xp.shape[0], xs.shape[0]
    tm, tn = TM_PROJ, TN_IN
    n_p, n_s = tp // tm, ts // tm
    kern = functools.partial(_inproj_kernel, n_p_tiles=n_p, tiles_per_seq=dec_seq // tm)
    return pl.pallas_call(
        kern,
        grid=(n_p + n_s, IN_COLS // tn),
        in_specs=[pl.BlockSpec((tm, D_MODEL), lambda i, j: (jnp.minimum(i, n_p - 1), 0),
                               pipeline_mode=pl.Buffered(1)),
                  pl.BlockSpec((tm, D_MODEL), lambda i, j: (jnp.maximum(i - n_p, 0), 0),
                               pipeline_mode=pl.Buffered(1)),
                  pl.BlockSpec((8, D_MODEL), lambda i, j: (0, 0)),
                  pl.BlockSpec((8, D_MODEL), lambda i, j: (0, 0)),
                  pl.BlockSpec((D_MODEL, tn), lambda i, j: (0, j))],
        out_specs=pl.BlockSpec((tm, tn), lambda i, j: (i, j)),
        out_shape=jax.ShapeDtypeStruct((tp + ts, IN_COLS), F32),
        scratch_shapes=[pltpu.VMEM((tm, D_MODEL), BF16)],
        compiler_params=_params(("arbitrary", "arbitrary")),
        name="in_projection",
    )(xp, xs, shift, scale, w_in_bf16)


def _lru_kernel(x_ref, g_ref, cw_ref, cb_ref, wg_ref, pb_ref, h0_ref, y_ref, st_ref,
                xc_scr, af_scr, uf_scr, ab_scr, ub_scr, *, bg, seq, pitch):
    rows = lax.broadcasted_iota(jnp.int32, (seq, LANES), 0)
    w = cw_ref[...]
    for b in range(bg):
        x = x_ref[b]
        xm2 = jnp.where(rows >= 2, pltpu.roll(x, 2, 0), 0.0)
        xm1 = jnp.where(rows >= 1, pltpu.roll(x, 1, 0), 0.0)
        xp1 = jnp.where(rows < seq - 1, pltpu.roll(x, seq - 1, 0), 0.0)
        xc_scr[pl.ds(b * seq, seq), :] = (cb_ref[...] + xm2 * w[0:1] + xm1 * w[1:2] + x * w[2:3] + xp1 * w[3:4])

    xc = xc_scr[...]
    gates = jnp.dot(xc.astype(BF16), wg_ref[0], preferred_element_type=F32)
    pb = pb_ref[...]
    for d, (a_scr, u_scr) in enumerate(((af_scr, uf_scr), (ab_scr, ub_scr))):
        r = jax.nn.sigmoid(gates[:, (2 * d) * LANES:(2 * d + 1) * LANES] + pb[3 * d:3 * d + 1])
        ig = jax.nn.sigmoid(gates[:, (2 * d + 1) * LANES:(2 * d + 2) * LANES] + pb[3 * d + 1:3 * d + 2])
        log_a = (-LRU_C) * r * _softplus(-pb[3 * d + 2:3 * d + 3])
        a = jnp.exp(log_a)
        u = jnp.sqrt(-jnp.tanh(log_a) * (1.0 + a * a)) * (ig * xc)
        for b in range(bg):
            a_scr[pl.ds(b * pitch, seq), :] = a[b * seq:(b + 1) * seq]
            u_scr[pl.ds(b * pitch, seq), :] = u[b * seq:(b + 1) * seq]

    def step(t, carry):
        hf, hb = carry
        tb = seq - 1 - t
        hf = af_scr[pl.ds(t, bg, stride=pitch), :] * hf + uf_scr[pl.ds(t, bg, stride=pitch), :]
        uf_scr[pl.ds(t, bg, stride=pitch), :] = hf
        hb = ab_scr[pl.ds(tb, bg, stride=pitch), :] * hb + ub_scr[pl.ds(tb, bg, stride=pitch), :]
        ub_scr[pl.ds(tb, bg, stride=pitch), :] = hb
        return hf, hb

    hf, hb = lax.fori_loop(0, seq, step, (h0_ref[0], h0_ref[1]), unroll=8)
    st_ref[0] = hf
    st_ref[1] = hb
    for b in range(bg):
        hs = uf_scr[pl.ds(b * pitch, seq), :] + ub_scr[pl.ds(b * pitch, seq), :]
        y_ref[b] = (jax.nn.gelu(g_ref[b]) * hs).astype(BF16)


def _lru_mixer(proj3, seq0, nseq, bg, conv_w, conv_b, w_gates, pb, h0):
    seq = proj3.shape[1]
    pitch = seq + SCAN_PAD
    off = seq0 // bg
    kern = functools.partial(_lru_kernel, bg=bg, seq=seq, pitch=pitch)
    return pl.pallas_call(
        kern,
        grid=(nseq // bg, LRU_BLOCKS),
        in_specs=[pl.BlockSpec((bg, seq, LANES), lambda b, c: (b + off, 0, c)),
                  pl.BlockSpec((bg, seq, LANES), lambda b, c: (b + off, 0, LRU_BLOCKS + c)),
                  pl.BlockSpec((4, LANES), lambda b, c: (0, c)),
                  pl.BlockSpec((1, LANES), lambda b, c: (0, c)),
                  pl.BlockSpec((1, LANES, 4 * LANES), lambda b, c: (c, 0, 0)),
                  pl.BlockSpec((6, LANES), lambda b, c: (0, c)),
                  pl.BlockSpec((2, bg, LANES), lambda b, c: (0, b, c))],
        out_specs=[pl.BlockSpec((bg, seq, LANES), lambda b, c: (b, 0, c)),
                   pl.BlockSpec((2, bg, LANES), lambda b, c: (0, b, c))],
        out_shape=[jax.ShapeDtypeStruct((nseq, seq, LRU_WIDTH), BF16),
                   jax.ShapeDtypeStruct((2, nseq, LRU_WIDTH), F32)],
        scratch_shapes=[pltpu.VMEM((bg * seq, LANES), F32)] + [pltpu.VMEM((bg * pitch, LANES), F32)] * 4,
        compiler_params=_params(("arbitrary", "arbitrary")),
        name="lru_mixer",
    )(proj3, proj3, conv_w, conv_b, w_gates, pb, h0)


def _rope(x, cos, sin_signed, first_half):
    partner = jnp.where(first_half, pltpu.roll(x, LANES - 32, 1), pltpu.roll(x, 32, 1))
    return x * cos + partner * sin_signed


def _ret_kernel(*refs, bg, seq, rope, has_state, emit_state, qb):
    refs = list(refs)
    q_ref, k_ref, v_ref, g_ref, dec_ref = refs[:5]
    pos = 5
    if rope:
        cos_ref, sin_ref = refs[pos:pos + 2]
        pos += 2
    if has_state:
        s0_ref = refs[pos]
        pos += 1
    y_ref = refs[pos]
    pos += 1
    if emit_state:
        st_ref = refs[pos]
        pos += 1
    mask_scr = refs[pos]

    log_g = -_softplus(-dec_ref[0])
    lgf, lgb = log_g[0:1], log_g[1:2]
    reps = seq // LANES
    lgf_row = jnp.concatenate([lgf] * reps, axis=1)
    lgb_row = jnp.concatenate([lgb] * reps, axis=1)

    @pl.when(pl.program_id(1) == 0)
    def _():
        for blk in range(seq // qb):
            ti = lax.broadcasted_iota(jnp.int32, (qb, seq), 0) + blk * qb
            si = lax.broadcasted_iota(jnp.int32, (qb, seq), 1)
            dist = (ti - si).astype(F32)
            e = jnp.where(dist >= 0, dist * lgf_row, (-dist) * lgb_row)
            mask_scr[pl.ds(blk * qb, qb), :] = jnp.where(dist == 0, 2.0, jnp.exp(e))

    lane = lax.broadcasted_iota(jnp.int32, (seq, LANES), 1)
    first_half = (lane % 64) < 32
    trow = lax.broadcasted_iota(jnp.int32, (seq, LANES), 0).astype(F32)
    for b in range(bg):
        q = q_ref[b]
        k = k_ref[b] * (RET_DH ** -0.5)
        v16 = v_ref[b].astype(BF16)
        if rope:
            q = _rope(q, cos_ref[...], sin_ref[...], first_half)
            k = _rope(k, cos_ref[...], sin_ref[...], first_half)
        q16 = q.astype(BF16)
        k16 = k.astype(BF16)
        if has_state:
            qf16 = (q * jnp.exp((trow + 1.0) * lgf)).astype(BF16)
            qb16 = (q * jnp.exp((float(seq) - trow) * lgb)).astype(BF16)
            s0f = s0_ref[b, 0].astype(BF16)
            s0b = s0_ref[b, 1].astype(BF16)
        for blk in range(seq // qb):
            sl = slice(blk * qb, (blk + 1) * qb)
            s = lax.dot_general(q16[sl], k16, (((1,), (1,)), ((), ())), preferred_element_type=F32)
            p = (s * mask_scr[pl.ds(blk * qb, qb), :]).astype(BF16)
            o = jnp.dot(p, v16, preferred_element_type=F32)
            if has_state:
                o = o + jnp.dot(qf16[sl], s0f, preferred_element_type=F32)
                o = o + jnp.dot(qb16[sl], s0b, preferred_element_type=F32)
            mu = jnp.mean(o, -1, keepdims=True)
            oc = o - mu
            var = jnp.mean(oc * oc, -1, keepdims=True)
            on = oc * lax.rsqrt(var + GN_EPS)
            gt = g_ref[b, pl.ds(blk * qb, qb), :]
            y_ref[b, pl.ds(blk * qb, qb), :] = (gt * jax.nn.sigmoid(gt) * on).astype(BF16)
        if emit_state:
            kf16 = (k * jnp.exp((float(seq - 1) - trow) * lgf)).astype(BF16)
            kb16 = (k * jnp.exp(trow * lgb)).astype(BF16)
            sf = lax.dot_general(kf16, v16, (((0,), (0,)), ((), ())), preferred_element_type=F32)
            sb = lax.dot_general(kb16, v16, (((0,), (0,)), ((), ())), preferred_element_type=F32)
            if has_state:
                sf = sf + jnp.exp(float(seq) * lgf) * s0_ref[b, 0]
                sb = sb + jnp.exp(float(seq) * lgb) * s0_ref[b, 1]
            st_ref[b, 0] = sf
            st_ref[b, 1] = sb


def _ret_mixer(proj3, seq0, nseq, bg, decay, rope_tabs=None, s0=None, emit_state=False):
    seq = proj3.shape[1]
    off = seq0 // bg
    qb = min(seq, 256)
    rope = rope_tabs is not None
    has_state = s0 is not None
    kern = functools.partial(_ret_kernel, bg=bg, seq=seq, rope=rope, has_state=has_state,
                             emit_state=emit_state, qb=qb)

    def col(base):
        return pl.BlockSpec((bg, seq, LANES), lambda h, b: (b + off, 0, base + h))

    st_spec = pl.BlockSpec((bg, None, 2, None, RET_DH, RET_DH), lambda h, b: (b, 0, 0, h, 0, 0))
    in_specs = [col(2 * LRU_BLOCKS), col(2 * LRU_BLOCKS + RET_HEADS), col(2 * LRU_BLOCKS + 2 * RET_HEADS),
                col(2 * LRU_BLOCKS + 3 * RET_HEADS), pl.BlockSpec((1, 2, LANES), lambda h, b: (h, 0, 0))]
    args = [proj3, proj3, proj3, proj3, decay]
    if rope:
        in_specs += [pl.BlockSpec((seq, LANES), lambda h, b: (0, 0))] * 2
        args += list(rope_tabs)
    if has_state:
        in_specs.append(st_spec)
        args.append(s0)
    out_specs = [pl.BlockSpec((bg, seq, LANES), lambda h, b: (b, 0, h))]
    out_shape = [jax.ShapeDtypeStruct((nseq, seq, RET_HEADS * RET_DH), BF16)]
    if emit_state:
        out_specs.append(st_spec)
        out_shape.append(jax.ShapeDtypeStruct((nseq, 1, 2, RET_HEADS, RET_DH, RET_DH), F32))
    return pl.pallas_call(
        kern,
        grid=(RET_HEADS, nseq // bg),
        in_specs=in_specs,
        out_specs=out_specs,
        out_shape=out_shape,
        scratch_shapes=[pltpu.VMEM((seq, seq), F32)],
        compiler_params=_params(("arbitrary", "arbitrary")),
        name="ret_mixer",
    )(*args)


def _rope_tables(seq):
    nf = RET_DH // 4
    freqs = ROPE_BASE ** (-jnp.arange(nf, dtype=F32) / nf)
    t = jnp.arange(seq)
    row = (t // GRID_W).astype(F32)[:, None] * freqs[None, :]
    colp = (t % GRID_W).astype(F32)[:, None] * freqs[None, :]
    cos = jnp.concatenate([jnp.cos(row), jnp.cos(row), jnp.cos(colp), jnp.cos(colp)], -1)
    sin = jnp.concatenate([-jnp.sin(row), jnp.sin(row), -jnp.sin(colp), jnp.sin(colp)], -1)
    return cos, sin


def _outproj_kernel(xp_ref, xs_ref, ylp_ref, yrp_ref, yls_ref, yrs_ref, wa_ref, wb_ref, gate_ref,
                    lng_ref, lnb_ref, shift_ref, scale_ref, wr_ref, br_ref,
                    z_ref, h2_ref, lg_ref, z_scr, *, n_p_tiles, tiles_per_seq, nj, tn):
    i = pl.program_id(0)
    j = pl.program_id(1)
    is_p = i < n_p_tiles
    row = jnp.where(is_p, 0, 1 + (i - n_p_tiles) // tiles_per_seq)

    def mix(x_ref, yl_ref, yr_ref):
        m = jnp.dot(yl_ref[...], wa_ref[...], preferred_element_type=F32)
        m = m + jnp.dot(yr_ref[...], wb_ref[...], preferred_element_type=F32)
        z = ALPHA * x_ref[...] + gate_ref[pl.ds(row, 1), :] * m
        z_ref[...] = z
        z_scr[j] = z

    @pl.when(is_p)
    def _():
        mix(xp_ref, ylp_ref, yrp_ref)

    @pl.when(jnp.logical_not(is_p))
    def _():
        mix(xs_ref, yls_ref, yrs_ref)

    @pl.when(j == nj - 1)
    def _():
        tm = z_scr.shape[1]
        inv_d = 1.0 / D_MODEL
        s1 = jnp.zeros((tm, 1), F32)
        for c in range(nj):
            s1 = s1 + jnp.sum(z_scr[c], -1, keepdims=True)
        mu = s1 * inv_d
        s2 = jnp.zeros((tm, 1), F32)
        for c in range(nj):
            zc = z_scr[c] - mu
            s2 = s2 + jnp.sum(zc * zc, -1, keepdims=True)
        rstd = lax.rsqrt(s2 * inv_d + LN_EPS)
        t1 = jnp.zeros((tm, 1), F32)
        for c in range(nj):
            cs = slice(c * tn, (c + 1) * tn)
            x1 = (z_scr[c] - mu) * rstd * lng_ref[:, cs] + lnb_ref[:, cs]
            z_scr[c] = x1
            t1 = t1 + jnp.sum(x1, -1, keepdims=True)
        mu2 = t1 * inv_d
        t2 = jnp.zeros((tm, 1), F32)
        for c in range(nj):
            xc = z_scr[c] - mu2
            t2 = t2 + jnp.sum(xc * xc, -1, keepdims=True)
        rstd2 = lax.rsqrt(t2 * inv_d + LN_EPS)
        logits = jnp.zeros((tm, LANES), F32) + br_ref[...]

        def h2_chunk(c):
            cs = slice(c * tn, (c + 1) * tn)
            return (z_scr[c] - mu2) * rstd2 * (1.0 + scale_ref[pl.ds(row, 1), cs]) + shift_ref[pl.ds(row, 1), cs]

        for c in range(nj // 2):
            lo, hi = h2_chunk(c), h2_chunk(c + nj // 2)
            h2_ref[:, c * tn:(c + 1) * tn] = _pack_bf16_pair(lo, hi)
            logits = logits + jnp.dot(lo.astype(BF16), wr_ref[c * tn:(c + 1) * tn, :], preferred_element_type=F32)
            logits = logits + jnp.dot(hi.astype(BF16), wr_ref[HALF_D + c * tn:HALF_D + (c + 1) * tn, :],
                                      preferred_element_type=F32)
        lg_ref[...] = logits


def _out_projection(xp, xs, ylp, yrp, yls, yrs, w_out_bf16, gate1, ln_g, ln_b, shift2, scale2, w_router, b_router,
                    dec_seq):
    tp, ts = xp.shape[0], xs.shape[0]
    tm, tn = TM_OUT, TN_OUT
    n_p, n_s = tp // tm, ts // tm
    nj = D_MODEL // tn
    half = LRU_WIDTH
    kern = functools.partial(_outproj_kernel, n_p_tiles=n_p, tiles_per_seq=dec_seq // tm, nj=nj, tn=tn)
    p_idx = lambda i, j: (jnp.minimum(i, n_p - 1), 0)
    s_idx = lambda i, j: (jnp.maximum(i - n_p, 0), 0)
    full = lambda i, j: (0, 0)
    return pl.pallas_call(
        kern,
        grid=(n_p + n_s, nj),
        in_specs=[pl.BlockSpec((tm, tn), lambda i, j: (jnp.minimum(i, n_p - 1), j)),
                  pl.BlockSpec((tm, tn), lambda i, j: (jnp.maximum(i - n_p, 0), j)),
                  pl.BlockSpec((tm, half), p_idx), pl.BlockSpec((tm, half), p_idx),
                  pl.BlockSpec((tm, half), s_idx), pl.BlockSpec((tm, half), s_idx),
                  pl.BlockSpec((half, tn), lambda i, j: (0, j)),
                  pl.BlockSpec((half, tn), lambda i, j: (1, j)),
                  pl.BlockSpec((8, tn), lambda i, j: (0, j)),
                  pl.BlockSpec((1, D_MODEL), full), pl.BlockSpec((1, D_MODEL), full),
                  pl.BlockSpec((8, D_MODEL), full), pl.BlockSpec((8, D_MODEL), full),
                  pl.BlockSpec((D_MODEL, LANES), full), pl.BlockSpec((1, LANES), full)],
        out_specs=[pl.BlockSpec((tm, tn), lambda i, j: (i, j)),
                   pl.BlockSpec((tm, HALF_D), lambda i, j: (i, 0)),
                   pl.BlockSpec((tm, LANES), lambda i, j: (i, 0))],
        out_shape=[jax.ShapeDtypeStruct((tp + ts, D_MODEL), F32),
                   jax.ShapeDtypeStruct((tp + ts, HALF_D), jnp.uint32),
                   jax.ShapeDtypeStruct((tp + ts, LANES), F32)],
        scratch_shapes=[pltpu.VMEM((nj, tm, tn), F32)],
        compiler_params=_params(("arbitrary", "arbitrary")),
        name="out_projection",
    )(xp, xs, ylp, yrp, yls, yrs, w_out_bf16, w_out_bf16, gate1, ln_g, ln_b, shift2, scale2, w_router, b_router)


def _route(logits):
    lg = logits[:, :N_GROUPS]
    le = logits[:, N_GROUPS:N_GROUPS + N_EXPERTS].reshape(-1, N_GROUPS, EXPERTS_PER_GROUP)
    pg = jax.nn.softmax(lg, -1)
    g_sel = jnp.argmax(lg, -1)
    p_sel = jnp.take_along_axis(pg, g_sel[:, None], 1)[:, 0]
    le_sel = jnp.take_along_axis(le, g_sel[:, None, None], 1)[:, 0]
    top_v, top_i = lax.top_k(le_sel, 2)
    weight = p_sel[:, None] * jax.nn.softmax(top_v, -1)
    expert = (g_sel[:, None] * EXPERTS_PER_GROUP + top_i).astype(jnp.int32)
    return expert, weight


def _dispatch_plan(expert, n_sub, n_super):
    i32 = jnp.int32
    flat_e = expert.reshape(-1)
    n_assign = flat_e.shape[0]
    ids = jnp.arange(N_EXPERTS, dtype=i32)
    onehot = (flat_e[:, None] == ids[None, :]).astype(i32)
    csum = jnp.cumsum(onehot, 0)
    counts = csum[-1]
    rank = jnp.sum(onehot * (csum - 1), 1)
    nb = (counts + SUB_ROWS - 1) // SUB_ROWS
    sub_end = jnp.cumsum(nb)
    sub_start = sub_end - nb
    dest = (jnp.sum(onehot * (sub_start * SUB_ROWS)[None, :], 1) + rank).astype(i32)
    tok = jnp.arange(n_assign, dtype=i32) // 2
    row_tok = jnp.zeros((n_sub * SUB_ROWS,), i32).at[dest].set(tok)

    nsup = (nb + SUPER - 1) // SUPER
    sup_end = jnp.cumsum(nsup)
    sup_start = sup_end - nsup
    n_used = sup_end[-1]
    s = jnp.arange(n_super, dtype=i32)
    used = s < n_used
    last_exp = jnp.max(jnp.where(counts > 0, ids, 0))
    e_s = jnp.where(used, jnp.minimum(jnp.searchsorted(sup_end, s, side="right").astype(i32), N_EXPERTS - 1), last_exp)
    local = s - sup_start[e_s]
    first_sub = sub_start[e_s] + SUPER * local
    n_comp = jnp.where(used, jnp.clip(nb[e_s] - SUPER * local, 0, SUPER), 0).astype(i32)
    zero_first = sub_end[-1] + SUPER * (s - n_used)
    n_zero = jnp.where(used, 0, jnp.clip(n_sub - zero_first, 0, SUPER)).astype(i32)
    out_sub = jnp.where(used, first_sub, jnp.minimum(zero_first, n_sub - 1)).astype(i32)
    k = jnp.arange(SUPER, dtype=i32)
    x_sub_used = first_sub[:, None] + jnp.minimum(k[None, :], jnp.maximum(n_comp - 1, 0)[:, None])
    x_sub_last = x_sub_used[jnp.maximum(n_used - 1, 0)]
    x_sub = jnp.where(used[:, None], x_sub_used, x_sub_last[None, :]).astype(i32).reshape(-1)
    return dest, row_tok, e_s.astype(i32), n_comp, n_zero, out_sub, x_sub


def _dispatch_kernel(tok_ref, h_hbm, xs_hbm, sem):
    base = pl.program_id(0) * DISPATCH_ROWS

    def row_copy(src_row, dst_row):
        return pltpu.make_async_copy(h_hbm.at[pl.ds(src_row, 1)], xs_hbm.at[pl.ds(dst_row, 1)], sem)

    def issue(r, c):
        row_copy(tok_ref[base + r], base + r).start()
        return c

    lax.fori_loop(0, DISPATCH_ROWS, issue, 0, unroll=8)

    def drain(r, c):
        row_copy(0, base + r).wait()
        return c

    lax.fori_loop(0, DISPATCH_ROWS, drain, 0, unroll=8)


def _dispatch(row_tok, h2_packed):
    n_rows = row_tok.shape[0]
    grid_spec = pltpu.PrefetchScalarGridSpec(
        num_scalar_prefetch=1,
        grid=(n_rows // DISPATCH_ROWS,),
        in_specs=[pl.BlockSpec(memory_space=pl.ANY)],
        out_specs=pl.BlockSpec(memory_space=pl.ANY),
        scratch_shapes=[pltpu.SemaphoreType.DMA(())],
    )
    return pl.pallas_call(
        _dispatch_kernel,
        grid_spec=grid_spec,
        out_shape=jax.ShapeDtypeStruct((n_rows, HALF_D), jnp.uint32),
        compiler_params=_params(("arbitrary",)),
        name="dispatch",
    )(row_tok, h2_packed)


def _expert_kernel(exp_ref, nc_ref, nz_ref, osub_ref, xsub_ref, x0_ref, x1_ref, x2_ref, x3_ref,
                   wg_ref, wu_ref, wd_ref, y_hbm, a_scr, ytile, sem, *, nf, nd):
    s = pl.program_id(0)
    t = pl.program_id(1)
    n_comp = nc_ref[s]
    n_out = n_comp + nz_ref[s]
    x_refs = (x0_ref, x1_ref, x2_ref, x3_ref)

    @pl.when(jnp.logical_and(t < nf, n_comp > 0))
    def _():
        wg16 = wg_ref[...].astype(BF16)
        wu16 = wu_ref[...].astype(BF16)
        for k in range(SUPER):
            @pl.when(k < n_comp)
            def _():
                x = _unpack_bf16_pair(x_refs[k][...])
                g = jnp.dot(x, wg16, preferred_element_type=F32)
                u = jnp.dot(x, wu16, preferred_element_type=F32)
                a_scr[t, k * SUB_ROWS:(k + 1) * SUB_ROWS, :] = (g * jax.nn.sigmoid(g) * u).astype(BF16)

    @pl.when(t >= nf)
    def _():
        j = t - nf
        slot = j % 2

        @pl.when(n_comp > 0)
        def _():
            wd16 = [wd_ref[f * F_CHUNK:(f + 1) * F_CHUNK, :].astype(BF16) for f in range(nf)]
            for k in range(SUPER):
                @pl.when(k < n_comp)
                def _():
                    rows = slice(k * SUB_ROWS, (k + 1) * SUB_ROWS)
                    acc = jnp.dot(a_scr[0, rows, :], wd16[0], preferred_element_type=F32)
                    for f in range(1, nf):
                        acc = acc + jnp.dot(a_scr[f, rows, :], wd16[f], preferred_element_type=F32)
                    ytile[slot, rows, :] = acc

        @pl.when(n_comp == 0)
        def _():
            ytile[slot] = jnp.zeros(ytile.shape[1:], F32)

        def out_copy(sl, k, jj):
            dst_rows = pl.ds(pl.multiple_of((osub_ref[s] + k) * SUB_ROWS, SUB_ROWS), SUB_ROWS)
            dst_cols = pl.ds(pl.multiple_of(jj * TN_DOWN, TN_DOWN), TN_DOWN)
            return pltpu.make_async_copy(ytile.at[sl, pl.ds(k * SUB_ROWS, SUB_ROWS), :],
                                         y_hbm.at[dst_rows, dst_cols], sem)

        for k in range(SUPER):
            @pl.when(jnp.logical_and(j > 0, k < n_out))
            def _():
                out_copy(1 - slot, k, j - 1).wait()
        for k in range(SUPER):
            @pl.when(k < n_out)
            def _():
                out_copy(slot, k, j).start()
        for k in range(SUPER):
            @pl.when(jnp.logical_and(j == nd - 1, k < n_out))
            def _():
                out_copy(slot, k, j).wait()


def _experts(xs, sup_exp, n_comp, n_zero, out_sub, x_sub, w_gate, w_up, w_down):
    n_super = sup_exp.shape[0]
    n_rows = xs.shape[0]
    nf = D_EXPERT // F_CHUNK
    nd = D_MODEL // TN_DOWN

    def up_idx(s, t, e, nc, nz, osub, xsub):
        return (e[s], 0, jnp.where(nc[s] > 0, jnp.minimum(t, nf - 1), nf - 1))

    def down_idx(s, t, e, nc, nz, osub, xsub):
        return (e[s], 0, jnp.where(nc[s] > 0, jnp.maximum(t - nf, 0), nd - 1))

    def x_spec(k):
        return pl.BlockSpec((SUB_ROWS, HALF_D), lambda s, t, e, nc, nz, osub, xsub: (xsub[s * SUPER + k], 0))

    grid_spec = pltpu.PrefetchScalarGridSpec(
        num_scalar_prefetch=5,
        grid=(n_super, nf + nd),
        in_specs=[x_spec(0), x_spec(1), x_spec(2), x_spec(3),
                  pl.BlockSpec((None, D_MODEL, F_CHUNK), up_idx),
                  pl.BlockSpec((None, D_MODEL, F_CHUNK), up_idx),
                  pl.BlockSpec((None, D_EXPERT, TN_DOWN), down_idx)],
        out_specs=pl.BlockSpec(memory_space=pl.ANY),
        scratch_shapes=[pltpu.VMEM((nf, SUPER * SUB_ROWS, F_CHUNK), BF16),
                        pltpu.VMEM((2, SUPER * SUB_ROWS, TN_DOWN), F32),
                        pltpu.SemaphoreType.DMA(())],
    )
    return pl.pallas_call(
        functools.partial(_expert_kernel, nf=nf, nd=nd),
        grid_spec=grid_spec,
        out_shape=jax.ShapeDtypeStruct((n_rows, D_MODEL), F32),
        compiler_params=_params(("arbitrary", "arbitrary"), 60 * 1024 * 1024),
        name="experts",
    )(sup_exp, n_comp, n_zero, out_sub, x_sub, xs, xs, xs, xs, w_gate, w_up, w_down)


def _final_kernel(dest_ref, y_hbm, z_ref, wt_ref, gate_ref, g1_ref, b1_ref, g2_ref, b2_ref,
                  op_ref, os_ref, ybuf, sem, *, n_p_tiles, tiles_per_seq):
    i = pl.program_id(0)
    tm = z_ref.shape[0]
    base = i * tm * 2

    def row_copy(src_row, k, r):
        return pltpu.make_async_copy(y_hbm.at[pl.ds(src_row, 1)], ybuf.at[k, pl.ds(r, 1)], sem)

    def issue(r, c):
        row_copy(dest_ref[base + 2 * r], 0, r).start()
        row_copy(dest_ref[base + 2 * r + 1], 1, r).start()
        return c

    lax.fori_loop(0, tm, issue, 0)

    def drain(r, c):
        row_copy(0, 0, r).wait()
        row_copy(0, 1, r).wait()
        return c

    lax.fori_loop(0, tm, drain, 0)

    is_p = i < n_p_tiles
    row = jnp.where(is_p, 0, 1 + (i - n_p_tiles) // tiles_per_seq)
    wt = wt_ref[...]
    f = wt[:, 0:1] * ybuf[0] + wt[:, 1:2] * ybuf[1]
    x1 = _ln(z_ref[...]) * g1_ref[...] + b1_ref[...]
    out = _ln(ALPHA * x1 + gate_ref[pl.ds(row, 1), :] * f) * g2_ref[...] + b2_ref[...]

    @pl.when(is_p)
    def _():
        op_ref[...] = out

    @pl.when(jnp.logical_not(is_p))
    def _():
        os_ref[...] = out


def _combine(dest, y_rows, z, weight, gate2, g1, b1, g2, b2, n_prompt, dec_seq):
    t = z.shape[0]
    tm = TM_FIN
    n_p = n_prompt // tm
    n_s = (t - n_prompt) // tm
    full = lambda i, d: (0, 0)
    grid_spec = pltpu.PrefetchScalarGridSpec(
        num_scalar_prefetch=1,
        grid=(n_p + n_s,),
        in_specs=[pl.BlockSpec(memory_space=pl.ANY),
                  pl.BlockSpec((tm, D_MODEL), lambda i, d: (i, 0)),
                  pl.BlockSpec((tm, 2), lambda i, d: (i, 0)),
                  pl.BlockSpec((8, D_MODEL), full),
                  pl.BlockSpec((1, D_MODEL), full), pl.BlockSpec((1, D_MODEL), full),
                  pl.BlockSpec((1, D_MODEL), full), pl.BlockSpec((1, D_MODEL), full)],
        out_specs=[pl.BlockSpec((tm, D_MODEL), lambda i, d: (jnp.minimum(i, n_p - 1), 0)),
                   pl.BlockSpec((tm, D_MODEL), lambda i, d: (jnp.maximum(i - n_p, 0), 0))],
        scratch_shapes=[pltpu.VMEM((2, tm, D_MODEL), F32), pltpu.SemaphoreType.DMA(())],
    )
    return pl.pallas_call(
        functools.partial(_final_kernel, n_p_tiles=n_p, tiles_per_seq=dec_seq // tm),
        grid_spec=grid_spec,
        out_shape=[jax.ShapeDtypeStruct((n_prompt, D_MODEL), F32),
                   jax.ShapeDtypeStruct((t - n_prompt, D_MODEL), F32)],
        compiler_params=_params(("arbitrary",)),
        name="combine",
    )(dest, y_rows, z, weight, gate2, g1, b1, g2, b2)


def kernel(x_prompt, x_sample, state_lru, state_ret, c, c_ctx, w_mod, b_mod, w_in, conv_w, conv_b, lru_wa, lru_ba,
           lru_wx, lru_bx, lru_lam, ret_decay, w_out, ln1_g, ln1_b, router_g, router_g_b, router_e, router_e_b,
           w_gate, w_up, w_down, ln2_g, ln2_b):
    assert w_in.shape[0] == 1, "single trunk layer"
    nb, seq, d = x_prompt.shape
    nbs, dec_seq, _ = x_sample.shape
    tp, ts = nb * seq, nbs * dec_seq
    assert tp % dec_seq == 0 and d == D_MODEL

    cond = jnp.zeros((8, d), F32).at[0].set(c_ctx).at[1:1 + nbs].set(c)
    mod = _modulation(cond, w_mod[0], b_mod[0][None, :])
    shift1, scale1, gate1, shift2, scale2, gate2 = [mod[:, k * d:(k + 1) * d] for k in range(6)]

    xp = x_prompt.reshape(tp, d)
    xs = x_sample.reshape(ts, d)
    proj = _in_projection(xp, xs, shift1, scale1, w_in[0].astype(BF16), dec_seq)

    w_gates = jnp.concatenate([lru_wa[0, 0], lru_wx[0, 0], lru_wa[0, 1], lru_wx[0, 1]], -1).astype(BF16)
    pb = jnp.stack([lru_ba[0, 0], lru_bx[0, 0], lru_lam[0, 0], lru_ba[0, 1], lru_bx[0, 1], lru_lam[0, 1]], 0)
    proj_p = proj.reshape((tp + ts) // seq, seq, IN_COLS)
    proj_s = proj.reshape((tp + ts) // dec_seq, dec_seq, IN_COLS)
    ylp, st_lru = _lru_mixer(proj_p, 0, nb, 8, conv_w[0], conv_b[0][None, :], w_gates, pb,
                             jnp.zeros((2, nb, LRU_WIDTH), F32))
    yls, _ = _lru_mixer(proj_s, tp // dec_seq, nbs, nbs, conv_w[0], conv_b[0][None, :], w_gates, pb,
                        jnp.swapaxes(state_lru[:, 0], 0, 1))

    decay = jnp.broadcast_to(ret_decay[0].T[:, :, None], (RET_HEADS, 2, LANES))
    yrp, st_ret = _ret_mixer(proj_p, 0, nb, 8, decay, emit_state=True)
    (yrs,) = _ret_mixer(proj_s, tp // dec_seq, nbs, 1, decay, rope_tabs=_rope_tables(dec_seq), s0=state_ret)

    w_router = jnp.concatenate(
        [router_g[0], jnp.transpose(router_e[0], (1, 0, 2)).reshape(d, N_EXPERTS),
         jnp.zeros((d, LANES - N_GROUPS - N_EXPERTS), F32)], -1).astype(BF16)
    b_router = jnp.concatenate([router_g_b[0], router_e_b[0].reshape(-1),
                                jnp.zeros((LANES - N_GROUPS - N_EXPERTS,), F32)])[None, :]
    z, h2, logits = _out_projection(
        xp, xs, ylp.reshape(tp, -1), yrp.reshape(tp, -1), yls.reshape(ts, -1), yrs.reshape(ts, -1),
        w_out[0].astype(BF16), gate1, ln1_g, ln1_b, shift2, scale2, w_router, b_router, dec_seq)

    expert, weight = _route(logits)
    n_assign = 2 * (tp + ts)
    n_sub = -(-(n_assign + N_EXPERTS * (SUB_ROWS - 1)) // SUB_ROWS)
    n_sub = -(-n_sub * SUB_ROWS // DISPATCH_ROWS) * DISPATCH_ROWS // SUB_ROWS
    n_super = N_EXPERTS + -(-n_assign // (SUB_ROWS * SUPER))
    assert SUPER * n_super >= n_sub + (SUPER - 1) * N_EXPERTS
    dest, row_tok, sup_exp, n_comp, n_zero, out_sub, x_sub = _dispatch_plan(expert, n_sub, n_super)
    xs_rows = _dispatch(row_tok, h2)
    y_rows = _experts(xs_rows, sup_exp, n_comp, n_zero, out_sub, x_sub, w_gate[0], w_up[0], w_down[0])
    y_p, y_s = _combine(dest, y_rows, z, weight, gate2, ln1_g, ln1_b, ln2_g, ln2_b, tp, dec_seq)

    new_state_lru = jnp.swapaxes(st_lru, 0, 1)[:, None]
    return (y_p.reshape(nb, seq, d), y_s.reshape(nbs, dec_seq, d), new_state_lru, st_ret)
```

```python
import functools

import jax
import jax.numpy as jnp
from jax import lax
from jax.experimental import pallas as pl
from jax.experimental.pallas import tpu as pltpu

F32 = jnp.float32
BF16 = jnp.bfloat16

D_MODEL = 4096
LRU_WIDTH = 2048
LRU_BLOCKS = 16
LANES = 128
RET_HEADS = 16
RET_DH = 128
IN_COLS = 12288
GRID_W = 64
ROPE_BASE = 10000.0
LRU_C = 8.0
N_GROUPS = 4
EXPERTS_PER_GROUP = 8
N_EXPERTS = 32
D_EXPERT = 1024
LN_EPS = 1e-6
GN_EPS = 1e-5
ALPHA = 2.0 ** 0.25

VMEM_LIMIT = 56 * 1024 * 1024
SCAN_PAD = 8

TM_PROJ = 512
TN_IN = 1024
TN_OUT = 512
TM_OUT = 512
SUB_ROWS = 256
SUPER = 4
F_CHUNK = 256
TN_DOWN = 1024
DISPATCH_ROWS = 1024
TM_FIN = 256
HALF_D = D_MODEL // 2


def _params(sem, vmem_limit=VMEM_LIMIT):
    return pltpu.CompilerParams(dimension_semantics=sem, vmem_limit_bytes=vmem_limit)


def _pack_bf16_pair(lo, hi):
    lo_bits = pltpu.bitcast(lo.astype(BF16).astype(F32), jnp.uint32) >> 16
    hi_bits = pltpu.bitcast(hi.astype(BF16).astype(F32), jnp.uint32) & jnp.uint32(0xFFFF0000)
    return lo_bits | hi_bits


def _unpack_bf16_pair(words):
    lo = pltpu.bitcast(words << 16, F32).astype(BF16)
    hi = pltpu.bitcast(words & jnp.uint32(0xFFFF0000), F32).astype(BF16)
    return jnp.concatenate([lo, hi], axis=1)


def _sigmoid(x):
    return 0.5 * jnp.tanh(0.5 * x) + 0.5


def _softplus(x):
    return jnp.maximum(x, 0.0) + jnp.log1p(jnp.exp(-jnp.abs(x)))


def _ln(x):
    mu = jnp.mean(x, -1, keepdims=True)
    xc = x - mu
    var = jnp.mean(xc * xc, -1, keepdims=True)
    return xc * lax.rsqrt(var + LN_EPS)


def _mod_kernel(cond_ref, w_ref, b_ref, o_ref):
    c = cond_ref[...]
    s = (c * _sigmoid(c)).astype(BF16)
    o_ref[...] = jnp.dot(s, w_ref[...].astype(BF16), preferred_element_type=F32) + b_ref[...]


def _modulation(cond, w_mod, b_mod):
    tn = 512
    n = w_mod.shape[1]
    return pl.pallas_call(
        _mod_kernel,
        grid=(n // tn,),
        in_specs=[pl.BlockSpec((8, D_MODEL), lambda j: (0, 0)),
                  pl.BlockSpec((D_MODEL, tn), lambda j: (0, j)),
                  pl.BlockSpec((1, tn), lambda j: (0, j))],
        out_specs=pl.BlockSpec((8, tn), lambda j: (0, j)),
        out_shape=jax.ShapeDtypeStruct((8, n), F32),
        compiler_params=_params(("arbitrary",)),
        name="modulation",
    )(cond, w_mod, b_mod)


def _inproj_kernel(xp_ref, xs_ref, shift_ref, scale_ref, w_ref, o_ref, h_scr, *, n_p_tiles, tiles_per_seq):
    i = pl.program_id(0)
    j = pl.program_id(1)

    def fill(x_ref, row):
        h = _ln(x_ref[...]) * (1.0 + scale_ref[pl.ds(row, 1), :]) + shift_ref[pl.ds(row, 1), :]
        h_scr[...] = h.astype(BF16)

    @pl.when(jnp.logical_and(j == 0, i < n_p_tiles))
    def _():
        fill(xp_ref, 0)

    @pl.when(jnp.logical_and(j == 0, i >= n_p_tiles))
    def _():
        fill(xs_ref, 1 + (i - n_p_tiles) // tiles_per_seq)

    o_ref[...] = jnp.dot(h_scr[...], w_ref[...], preferred_element_type=F32)


def _in_projection(xp, xs, shift, scale, w_in_bf16, dec_seq):
    tp, ts = xp.shape[0], xs.shape[0]
    tm, tn = TM_PROJ, TN_IN
    n_p, n_s = tp // tm, ts // tm
    kern = functools.partial(_inproj_kernel, n_p_tiles=n_p, tiles_per_seq=dec_seq // tm)
    return pl.pallas_call(
        kern,
        grid=(n_p + n_s, IN_COLS // tn),
        in_specs=[pl.BlockSpec((tm, D_MODEL), lambda i, j: (jnp.minimum(i, n_p - 1), 0),
                               pipeline_mode=pl.Buffered(1)),
                  pl.BlockSpec((tm, D_MODEL), lambda i, j: (jnp.maximum(i - n_p, 0), 0),
                               pipeline_mode=pl.Buffered(1)),
                  pl.BlockSpec((8, D_MODEL), lambda i, j: (0, 0)),
                  pl.BlockSpec((8, D_MODEL), lambda i, j: (0, 0)),
                  pl.BlockSpec((D_MODEL, tn), lambda i, j: (0, j))],
        out_specs=pl.BlockSpec((tm, tn), lambda i, j: (i, j)),
        out_shape=jax.ShapeDtypeStruct((tp + ts, IN_COLS), F32),
        scratch_shapes=[pltpu.VMEM((tm, D_MODEL), BF16)],
        compiler_params=_params(("arbitrary", "arbitrary")),
        name="in_projection",
    )(xp, xs, shift, scale, w_in_bf16)


def _lru_kernel(x_ref, g_ref, cw_ref, cb_ref, wg_ref, pb_ref, h0_ref, y_ref, st_ref,
                xc_scr, af_scr, uf_scr, ab_scr, ub_scr, *, bg, seq, pitch):
    rows = lax.broadcasted_iota(jnp.int32, (seq, LANES), 0)
    w = cw_ref[...]
    for b in range(bg):
        x = x_ref[b]
        xm2 = jnp.where(rows >= 2, pltpu.roll(x, 2, 0), 0.0)
        xm1 = jnp.where(rows >= 1, pltpu.roll(x, 1, 0), 0.0)
        xp1 = jnp.where(rows < seq - 1, pltpu.roll(x, seq - 1, 0), 0.0)
        xc_scr[pl.ds(b * seq, seq), :] = (cb_ref[...] + xm2 * w[0:1] + xm1 * w[1:2] + x * w[2:3] + xp1 * w[3:4])

    xc = xc_scr[...]
    gates = jnp.dot(xc.astype(BF16), wg_ref[0], preferred_element_type=F32)
    pb = pb_ref[...]
    for d, (a_scr, u_scr) in enumerate(((af_scr, uf_scr), (ab_scr, ub_scr))):
        r = _sigmoid(gates[:, (2 * d) * LANES:(2 * d + 1) * LANES] + pb[3 * d:3 * d + 1])
        ig = _sigmoid(gates[:, (2 * d + 1) * LANES:(2 * d + 2) * LANES] + pb[3 * d + 1:3 * d + 2])
        log_a = (-LRU_C) * r * _softplus(-pb[3 * d + 2:3 * d + 3])
        a = jnp.exp(log_a)
        u = jnp.sqrt(-jnp.tanh(log_a) * (1.0 + a * a)) * (ig * xc)
        for b in range(bg):
            a_scr[pl.ds(b * pitch, seq), :] = a[b * seq:(b + 1) * seq]
            u_scr[pl.ds(b * pitch, seq), :] = u[b * seq:(b + 1) * seq]

    def step(t, carry):
        hf, hb = carry
        tb = seq - 1 - t
        hf = af_scr[pl.ds(t, bg, stride=pitch), :] * hf + uf_scr[pl.ds(t, bg, stride=pitch), :]
        uf_scr[pl.ds(t, bg, stride=pitch), :] = hf
        hb = ab_scr[pl.ds(tb, bg, stride=pitch), :] * hb + ub_scr[pl.ds(tb, bg, stride=pitch), :]
        ub_scr[pl.ds(tb, bg, stride=pitch), :] = hb
        return hf, hb

    hf, hb = lax.fori_loop(0, seq, step, (h0_ref[0], h0_ref[1]), unroll=8)
    st_ref[0] = hf
    st_ref[1] = hb
    for b in range(bg):
        hs = uf_scr[pl.ds(b * pitch, seq), :] + ub_scr[pl.ds(b * pitch, seq), :]
        y_ref[b] = (jax.nn.gelu(g_ref[b]) * hs).astype(BF16)


def _lru_mixer(proj3, seq0, nseq, bg, conv_w, conv_b, w_gates, pb, h0):
    seq = proj3.shape[1]
    pitch = seq + SCAN_PAD
    off = seq0 // bg
    kern = functools.partial(_lru_kernel, bg=bg, seq=seq, pitch=pitch)
    return pl.pallas_call(
        kern,
        grid=(nseq // bg, LRU_BLOCKS),
        in_specs=[pl.BlockSpec((bg, seq, LANES), lambda b, c: (b + off, 0, c)),
                  pl.BlockSpec((bg, seq, LANES), lambda b, c: (b + off, 0, LRU_BLOCKS + c)),
                  pl.BlockSpec((4, LANES), lambda b, c: (0, c)),
                  pl.BlockSpec((1, LANES), lambda b, c: (0, c)),
                  pl.BlockSpec((1, LANES, 4 * LANES), lambda b, c: (c, 0, 0)),
                  pl.BlockSpec((6, LANES), lambda b, c: (0, c)),
                  pl.BlockSpec((2, bg, LANES), lambda b, c: (0, b, c))],
        out_specs=[pl.BlockSpec((bg, seq, LANES), lambda b, c: (b, 0, c)),
                   pl.BlockSpec((2, bg, LANES), lambda b, c: (0, b, c))],
        out_shape=[jax.ShapeDtypeStruct((nseq, seq, LRU_WIDTH), BF16),
                   jax.ShapeDtypeStruct((2, nseq, LRU_WIDTH), F32)],
        scratch_shapes=[pltpu.VMEM((bg * seq, LANES), F32)] + [pltpu.VMEM((bg * pitch, LANES), F32)] * 4,
        compiler_params=_params(("arbitrary", "arbitrary")),
        name="lru_mixer",
    )(proj3, proj3, conv_w, conv_b, w_gates, pb, h0)


def _rope(x, cos, sin_signed, first_half):
    partner = jnp.where(first_half, pltpu.roll(x, LANES - 32, 1), pltpu.roll(x, 32, 1))
    return x * cos + partner * sin_signed


def _ret_kernel(*refs, bg, seq, rope, has_state, emit_state, qb):
    refs = list(refs)
    q_ref, k_ref, v_ref, g_ref, dec_ref = refs[:5]
    pos = 5
    if rope:
        cos_ref, sin_ref = refs[pos:pos + 2]
        pos += 2
    if has_state:
        s0_ref = refs[pos]
        pos += 1
    y_ref = refs[pos]
    pos += 1
    if emit_state:
        st_ref = refs[pos]
        pos += 1
    mask_scr = refs[pos]

    log_g = -_softplus(-dec_ref[0])
    lgf, lgb = log_g[0:1], log_g[1:2]
    reps = seq // LANES
    lgf_row = jnp.concatenate([lgf] * reps, axis=1)
    lgb_row = jnp.concatenate([lgb] * reps, axis=1)

    @pl.when(pl.program_id(1) == 0)
    def _():
        for blk in range(seq // qb):
            ti = lax.broadcasted_iota(jnp.int32, (qb, seq), 0) + blk * qb
            si = lax.broadcasted_iota(jnp.int32, (qb, seq), 1)
            dist = (ti - si).astype(F32)
            e = jnp.where(dist >= 0, dist * lgf_row, (-dist) * lgb_row)
            mask_scr[pl.ds(blk * qb, qb), :] = jnp.where(dist == 0, 2.0, jnp.exp(e))

    lane = lax.broadcasted_iota(jnp.int32, (seq, LANES), 1)
    first_half = (lane % 64) < 32
    trow = lax.broadcasted_iota(jnp.int32, (seq, LANES), 0).astype(F32)
    for b in range(bg):
        q = q_ref[b]
        k = k_ref[b] * (RET_DH ** -0.5)
        v16 = v_ref[b].astype(BF16)
        if rope:
            q = _rope(q, cos_ref[...], sin_ref[...], first_half)
            k = _rope(k, cos_ref[...], sin_ref[...], first_half)
        q16 = q.astype(BF16)
        k16 = k.astype(BF16)
        if has_state:
            qf16 = (q * jnp.exp((trow + 1.0) * lgf)).astype(BF16)
            qb16 = (q * jnp.exp((float(seq) - trow) * lgb)).astype(BF16)
            s0f = s0_ref[b, 0].astype(BF16)
            s0b = s0_ref[b, 1].astype(BF16)
        for blk in range(seq // qb):
            sl = slice(blk * qb, (blk + 1) * qb)
            s = lax.dot_general(q16[sl], k16, (((1,), (1,)), ((), ())), preferred_element_type=F32)
            p = (s * mask_scr[pl.ds(blk * qb, qb), :]).astype(BF16)
            o = jnp.dot(p, v16, preferred_element_type=F32)
            if has_state:
                o = o + jnp.dot(qf16[sl], s0f, preferred_element_type=F32)
                o = o + jnp.dot(qb16[sl], s0b, preferred_element_type=F32)
            mu = jnp.mean(o, -1, keepdims=True)
            oc = o - mu
            var = jnp.mean(oc * oc, -1, keepdims=True)
            on = oc * lax.rsqrt(var + GN_EPS)
            gt = g_ref[b, pl.ds(blk * qb, qb), :]
            y_ref[b, pl.ds(blk * qb, qb), :] = (gt * _sigmoid(gt) * on).astype(BF16)
        if emit_state:
            kf16 = (k * jnp.exp((float(seq - 1) - trow) * lgf)).astype(BF16)
            kb16 = (k * jnp.exp(trow * lgb)).astype(BF16)
            sf = lax.dot_general(kf16, v16, (((0,), (0,)), ((), ())), preferred_element_type=F32)
            sb = lax.dot_general(kb16, v16, (((0,), (0,)), ((), ())), preferred_element_type=F32)
            if has_state:
                sf = sf + jnp.exp(float(seq) * lgf) * s0_ref[b, 0]
                sb = sb + jnp.exp(float(seq) * lgb) * s0_ref[b, 1]
            st_ref[b, 0] = sf
            st_ref[b, 1] = sb


def _ret_mixer(proj3, seq0, nseq, bg, decay, rope_tabs=None, s0=None, emit_state=False):
    seq = proj3.shape[1]
    off = seq0 // bg
    qb = min(seq, 256)
    rope = rope_tabs is not None
    has_state = s0 is not None
    kern = functools.partial(_ret_kernel, bg=bg, seq=seq, rope=rope, has_state=has_state,
                             emit_state=emit_state, qb=qb)

    def col(base):
        return pl.BlockSpec((bg, seq, LANES), lambda h, b: (b + off, 0, base + h))

    st_spec = pl.BlockSpec((bg, None, 2, None, RET_DH, RET_DH), lambda h, b: (b, 0, 0, h, 0, 0))
    in_specs = [col(2 * LRU_BLOCKS), col(2 * LRU_BLOCKS + RET_HEADS), col(2 * LRU_BLOCKS + 2 * RET_HEADS),
                col(2 * LRU_BLOCKS + 3 * RET_HEADS), pl.BlockSpec((1, 2, LANES), lambda h, b: (h, 0, 0))]
    args = [proj3, proj3, proj3, proj3, decay]
    if rope:
        in_specs += [pl.BlockSpec((seq, LANES), lambda h, b: (0, 0))] * 2
        args += list(rope_tabs)
    if has_state:
        in_specs.append(st_spec)
        args.append(s0)
    out_specs = [pl.BlockSpec((bg, seq, LANES), lambda h, b: (b, 0, h))]
    out_shape = [jax.ShapeDtypeStruct((nseq, seq, RET_HEADS * RET_DH), BF16)]
    if emit_state:
        out_specs.append(st_spec)
        out_shape.append(jax.ShapeDtypeStruct((nseq, 1, 2, RET_HEADS, RET_DH, RET_DH), F32))
    return pl.pallas_call(
        kern,
        grid=(RET_HEADS, nseq // bg),
        in_specs=in_specs,
        out_specs=out_specs,
        out_shape=out_shape,
        scratch_shapes=[pltpu.VMEM((seq, seq), F32)],
        compiler_params=_params(("arbitrary", "arbitrary")),
        name="ret_mixer",
    )(*args)


def _rope_tables(seq):
    nf = RET_DH // 4
    freqs = ROPE_BASE ** (-jnp.arange(nf, dtype=F32) / nf)
    t = jnp.arange(seq)
    row = (t // GRID_W).astype(F32)[:, None] * freqs[None, :]
    colp = (t % GRID_W).astype(F32)[:, None] * freqs[None, :]
    cos = jnp.concatenate([jnp.cos(row), jnp.cos(row), jnp.cos(colp), jnp.cos(colp)], -1)
    sin = jnp.concatenate([-jnp.sin(row), jnp.sin(row), -jnp.sin(colp), jnp.sin(colp)], -1)
    return cos, sin


def _outproj_kernel(xp_ref, xs_ref, ylp_ref, yrp_ref, yls_ref, yrs_ref, wa_ref, wb_ref, gate_ref,
                    lng_ref, lnb_ref, shift_ref, scale_ref, wr_ref, br_ref,
                    z_ref, h2_ref, lg_ref, z_scr, *, n_p_tiles, tiles_per_seq, nj, tn):
    i = pl.program_id(0)
    j = pl.program_id(1)
    is_p = i < n_p_tiles
    row = jnp.where(is_p, 0, 1 + (i - n_p_tiles) // tiles_per_seq)

    def mix(x_ref, yl_ref, yr_ref):
        m = jnp.dot(yl_ref[...], wa_ref[...], preferred_element_type=F32)
        m = m + jnp.dot(yr_ref[...], wb_ref[...], preferred_element_type=F32)
        z = ALPHA * x_ref[...] + gate_ref[pl.ds(row, 1), :] * m
        z_ref[...] = z
        z_scr[j] = z

    @pl.when(is_p)
    def _():
        mix(xp_ref, ylp_ref, yrp_ref)

    @pl.when(jnp.logical_not(is_p))
    def _():
        mix(xs_ref, yls_ref, yrs_ref)

    @pl.when(j == nj - 1)
    def _():
        tm = z_scr.shape[1]
        inv_d = 1.0 / D_MODEL
        s1 = jnp.zeros((tm, 1), F32)
        for c in range(nj):
            s1 = s1 + jnp.sum(z_scr[c], -1, keepdims=True)
        mu = s1 * inv_d
        s2 = jnp.zeros((tm, 1), F32)
        for c in range(nj):
            zc = z_scr[c] - mu
            s2 = s2 + jnp.sum(zc * zc, -1, keepdims=True)
        rstd = lax.rsqrt(s2 * inv_d + LN_EPS)
        t1 = jnp.zeros((tm, 1), F32)
        for c in range(nj):
            cs = slice(c * tn, (c + 1) * tn)
            x1 = (z_scr[c] - mu) * rstd * lng_ref[:, cs] + lnb_ref[:, cs]
            z_scr[c] = x1
            t1 = t1 + jnp.sum(x1, -1, keepdims=True)
        mu2 = t1 * inv_d
        t2 = jnp.zeros((tm, 1), F32)
        for c in range(nj):
            xc = z_scr[c] - mu2
            t2 = t2 + jnp.sum(xc * xc, -1, keepdims=True)
        rstd2 = lax.rsqrt(t2 * inv_d + LN_EPS)
        logits = jnp.zeros((tm, LANES), F32) + br_ref[...]

        def h2_chunk(c):
            cs = slice(c * tn, (c + 1) * tn)
            return (z_scr[c] - mu2) * rstd2 * (1.0 + scale_ref[pl.ds(row, 1), cs]) + shift_ref[pl.ds(row, 1), cs]

        for c in range(nj // 2):
            lo, hi = h2_chunk(c), h2_chunk(c + nj // 2)
            h2_ref[:, c * tn:(c + 1) * tn] = _pack_bf16_pair(lo, hi)
            logits = logits + jnp.dot(lo.astype(BF16), wr_ref[c * tn:(c + 1) * tn, :], preferred_element_type=F32)
            logits = logits + jnp.dot(hi.astype(BF16), wr_ref[HALF_D + c * tn:HALF_D + (c + 1) * tn, :],
                                      preferred_element_type=F32)
        lg_ref[...] = logits


def _out_projection(xp, xs, ylp, yrp, yls, yrs, w_out_bf16, gate1, ln_g, ln_b, shift2, scale2, w_router, b_router,
                    dec_seq):
    tp, ts = xp.shape[0], xs.shape[0]
    tm, tn = TM_OUT, TN_OUT
    n_p, n_s = tp // tm, ts // tm
    nj = D_MODEL // tn
    half = LRU_WIDTH
    kern = functools.partial(_outproj_kernel, n_p_tiles=n_p, tiles_per_seq=dec_seq // tm, nj=nj, tn=tn)
    p_idx = lambda i, j: (jnp.minimum(i, n_p - 1), 0)
    s_idx = lambda i, j: (jnp.maximum(i - n_p, 0), 0)
    full = lambda i, j: (0, 0)
    return pl.pallas_call(
        kern,
        grid=(n_p + n_s, nj),
        in_specs=[pl.BlockSpec((tm, tn), lambda i, j: (jnp.minimum(i, n_p - 1), j)),
                  pl.BlockSpec((tm, tn), lambda i, j: (jnp.maximum(i - n_p, 0), j)),
                  pl.BlockSpec((tm, half), p_idx), pl.BlockSpec((tm, half), p_idx),
                  pl.BlockSpec((tm, half), s_idx), pl.BlockSpec((tm, half), s_idx),
                  pl.BlockSpec((half, tn), lambda i, j: (0, j)),
                  pl.BlockSpec((half, tn), lambda i, j: (1, j)),
                  pl.BlockSpec((8, tn), lambda i, j: (0, j)),
                  pl.BlockSpec((1, D_MODEL), full), pl.BlockSpec((1, D_MODEL), full),
                  pl.BlockSpec((8, D_MODEL), full), pl.BlockSpec((8, D_MODEL), full),
                  pl.BlockSpec((D_MODEL, LANES), full), pl.BlockSpec((1, LANES), full)],
        out_specs=[pl.BlockSpec((tm, tn), lambda i, j: (i, j)),
                   pl.BlockSpec((tm, HALF_D), lambda i, j: (i, 0)),
                   pl.BlockSpec((tm, LANES), lambda i, j: (i, 0))],
        out_shape=[jax.ShapeDtypeStruct((tp + ts, D_MODEL), F32),
                   jax.ShapeDtypeStruct((tp + ts, HALF_D), jnp.uint32),
                   jax.ShapeDtypeStruct((tp + ts, LANES), F32)],
        scratch_shapes=[pltpu.VMEM((nj, tm, tn), F32)],
        compiler_params=_params(("arbitrary", "arbitrary")),
        name="out_projection",
    )(xp, xs, ylp, yrp, yls, yrs, w_out_bf16, w_out_bf16, gate1, ln_g, ln_b, shift2, scale2, w_router, b_router)


def _route(logits):
    lg = logits[:, :N_GROUPS]
    le = logits[:, N_GROUPS:N_GROUPS + N_EXPERTS].reshape(-1, N_GROUPS, EXPERTS_PER_GROUP)
    pg = jax.nn.softmax(lg, -1)
    g_sel = jnp.argmax(lg, -1)
    p_sel = jnp.take_along_axis(pg, g_sel[:, None], 1)[:, 0]
    le_sel = jnp.take_along_axis(le, g_sel[:, None, None], 1)[:, 0]
    top_v, top_i = lax.top_k(le_sel, 2)
    weight = p_sel[:, None] * jax.nn.softmax(top_v, -1)
    expert = (g_sel[:, None] * EXPERTS_PER_GROUP + top_i).astype(jnp.int32)
    return expert, weight


def _dispatch_plan(expert, n_sub, n_super):
    i32 = jnp.int32
    flat_e = expert.reshape(-1)
    n_assign = flat_e.shape[0]
    ids = jnp.arange(N_EXPERTS, dtype=i32)
    onehot = (flat_e[:, None] == ids[None, :]).astype(i32)
    csum = jnp.cumsum(onehot, 0)
    counts = csum[-1]
    rank = jnp.sum(onehot * (csum - 1), 1)
    nb = (counts + SUB_ROWS - 1) // SUB_ROWS
    sub_end = jnp.cumsum(nb)
    sub_start = sub_end - nb
    dest = (jnp.sum(onehot * (sub_start * SUB_ROWS)[None, :], 1) + rank).astype(i32)
    tok = jnp.arange(n_assign, dtype=i32) // 2
    row_tok = jnp.zeros((n_sub * SUB_ROWS,), i32).at[dest].set(tok)

    nsup = (nb + SUPER - 1) // SUPER
    sup_end = jnp.cumsum(nsup)
    sup_start = sup_end - nsup
    n_used = sup_end[-1]
    s = jnp.arange(n_super, dtype=i32)
    used = s < n_used
    last_exp = jnp.max(jnp.where(counts > 0, ids, 0))
    e_s = jnp.where(used, jnp.minimum(jnp.searchsorted(sup_end, s, side="right").astype(i32), N_EXPERTS - 1), last_exp)
    local = s - sup_start[e_s]
    first_sub = sub_start[e_s] + SUPER * local
    n_comp = jnp.where(used, jnp.clip(nb[e_s] - SUPER * local, 0, SUPER), 0).astype(i32)
    zero_first = sub_end[-1] + SUPER * (s - n_used)
    n_zero = jnp.where(used, 0, jnp.clip(n_sub - zero_first, 0, SUPER)).astype(i32)
    out_sub = jnp.where(used, first_sub, jnp.minimum(zero_first, n_sub - 1)).astype(i32)
    k = jnp.arange(SUPER, dtype=i32)
    x_sub_used = first_sub[:, None] + jnp.minimum(k[None, :], jnp.maximum(n_comp - 1, 0)[:, None])
    x_sub_last = x_sub_used[jnp.maximum(n_used - 1, 0)]
    x_sub = jnp.where(used[:, None], x_sub_used, x_sub_last[None, :]).astype(i32).reshape(-1)
    return dest, row_tok, e_s.astype(i32), n_comp, n_zero, out_sub, x_sub


def _dispatch_kernel(tok_ref, h_hbm, xs_ref, sem):
    base = pl.program_id(0) * DISPATCH_ROWS

    def row_copy(src_row, dst_row):
        return pltpu.make_async_copy(h_hbm.at[pl.ds(src_row, 1)], xs_ref.at[pl.ds(dst_row, 1)], sem)

    def issue(r, c):
        row_copy(tok_ref[base + r], r).start()
        return c

    lax.fori_loop(0, DISPATCH_ROWS, issue, 0, unroll=8)

    def drain(r, c):
        row_copy(0, r).wait()
        return c

    lax.fori_loop(0, DISPATCH_ROWS, drain, 0, unroll=8)


def _dispatch(row_tok, h2_packed):
    n_rows = row_tok.shape[0]
    grid_spec = pltpu.PrefetchScalarGridSpec(
        num_scalar_prefetch=1,
        grid=(n_rows // DISPATCH_ROWS,),
        in_specs=[pl.BlockSpec(memory_space=pl.ANY)],
        out_specs=pl.BlockSpec((DISPATCH_ROWS, HALF_D), lambda i, tok: (i, 0)),
        scratch_shapes=[pltpu.SemaphoreType.DMA(())],
    )
    return pl.pallas_call(
        _dispatch_kernel,
        grid_spec=grid_spec,
        out_shape=jax.ShapeDtypeStruct((n_rows, HALF_D), jnp.uint32),
        compiler_params=_params(("arbitrary",)),
        name="dispatch",
    )(row_tok, h2_packed)


def _expert_kernel(exp_ref, nc_ref, nz_ref, osub_ref, xsub_ref, x0_ref, x1_ref, x2_ref, x3_ref,
                   wg_ref, wu_ref, wd_ref, y_hbm, a_scr, ytile, sem, *, nf, nd):
    s = pl.program_id(0)
    t = pl.program_id(1)
    n_comp = nc_ref[s]
    n_out = n_comp + nz_ref[s]
    x_refs = (x0_ref, x1_ref, x2_ref, x3_ref)

    @pl.when(jnp.logical_and(t < nf, n_comp > 0))
    def _():
        wg16 = wg_ref[...].astype(BF16)
        wu16 = wu_ref[...].astype(BF16)

        def up(k):
            x = _unpack_bf16_pair(x_refs[k][...])
            g = jnp.dot(x, wg16, preferred_element_type=F32)
            u = jnp.dot(x, wu16, preferred_element_type=F32)
            a_scr[t, k * SUB_ROWS:(k + 1) * SUB_ROWS, :] = (g * _sigmoid(g) * u).astype(BF16)

        up(0)
        for k in range(1, SUPER):
            pl.when(k < n_comp)(functools.partial(up, k))

    @pl.when(t >= nf)
    def _():
        j = t - nf
        slot = j % 2

        @pl.when(n_comp > 0)
        def _():
            wd16 = [wd_ref[f * F_CHUNK:(f + 1) * F_CHUNK, :].astype(BF16) for f in range(nf)]

            def down(k):
                rows = slice(k * SUB_ROWS, (k + 1) * SUB_ROWS)
                acc = jnp.dot(a_scr[0, rows, :], wd16[0], preferred_element_type=F32)
                for f in range(1, nf):
                    acc = acc + jnp.dot(a_scr[f, rows, :], wd16[f], preferred_element_type=F32)
                ytile[slot, rows, :] = acc

            down(0)
            for k in range(1, SUPER):
                pl.when(k < n_comp)(functools.partial(down, k))

        @pl.when(n_comp == 0)
        def _():
            ytile[slot] = jnp.zeros(ytile.shape[1:], F32)

        def out_copy(sl, k, jj):
            dst_rows = pl.ds(pl.multiple_of((osub_ref[s] + k) * SUB_ROWS, SUB_ROWS), SUB_ROWS)
            dst_cols = pl.ds(pl.multiple_of(jj * TN_DOWN, TN_DOWN), TN_DOWN)
            return pltpu.make_async_copy(ytile.at[sl, pl.ds(k * SUB_ROWS, SUB_ROWS), :],
                                         y_hbm.at[dst_rows, dst_cols], sem)

        for k in range(SUPER):
            @pl.when(jnp.logical_and(j > 0, k < n_out))
            def _():
                out_copy(1 - slot, k, j - 1).wait()
        for k in range(SUPER):
            @pl.when(k < n_out)
            def _():
                out_copy(slot, k, j).start()
        for k in range(SUPER):
            @pl.when(jnp.logical_and(j == nd - 1, k < n_out))
            def _():
                out_copy(slot, k, j).wait()


def _experts(xs, sup_exp, n_comp, n_zero, out_sub, x_sub, w_gate, w_up, w_down):
    n_super = sup_exp.shape[0]
    n_rows = xs.shape[0]
    nf = D_EXPERT // F_CHUNK
    nd = D_MODEL // TN_DOWN

    def up_idx(s, t, e, nc, nz, osub, xsub):
        return (e[s], 0, jnp.where(nc[s] > 0, jnp.minimum(t, nf - 1), nf - 1))

    def down_idx(s, t, e, nc, nz, osub, xsub):
        return (e[s], 0, jnp.where(nc[s] > 0, jnp.maximum(t - nf, 0), nd - 1))

    def x_spec(k):
        return pl.BlockSpec((SUB_ROWS, HALF_D), lambda s, t, e, nc, nz, osub, xsub: (xsub[s * SUPER + k], 0))

    grid_spec = pltpu.PrefetchScalarGridSpec(
        num_scalar_prefetch=5,
        grid=(n_super, nf + nd),
        in_specs=[x_spec(0), x_spec(1), x_spec(2), x_spec(3),
                  pl.BlockSpec((None, D_MODEL, F_CHUNK), up_idx),
                  pl.BlockSpec((None, D_MODEL, F_CHUNK), up_idx),
                  pl.BlockSpec((None, D_EXPERT, TN_DOWN), down_idx)],
        out_specs=pl.BlockSpec(memory_space=pl.ANY),
        scratch_shapes=[pltpu.VMEM((nf, SUPER * SUB_ROWS, F_CHUNK), BF16),
                        pltpu.VMEM((2, SUPER * SUB_ROWS, TN_DOWN), F32),
                        pltpu.SemaphoreType.DMA(())],
    )
    return pl.pallas_call(
        functools.partial(_expert_kernel, nf=nf, nd=nd),
        grid_spec=grid_spec,
        out_shape=jax.ShapeDtypeStruct((n_rows, D_MODEL), F32),
        compiler_params=_params(("arbitrary", "arbitrary"), 60 * 1024 * 1024),
        name="experts",
    )(sup_exp, n_comp, n_zero, out_sub, x_sub, xs, xs, xs, xs, w_gate, w_up, w_down)


def _final_kernel(dest_ref, y_hbm, z_ref, wt_ref, gate_ref, g1_ref, b1_ref, g2_ref, b2_ref,
                  op_ref, os_ref, ybuf, sem, *, n_p_tiles, tiles_per_seq):
    i = pl.program_id(0)
    n_tiles = pl.num_programs(0)
    tm = z_ref.shape[0]
    slot = i % 2

    def row_copy(src_row, sl, k, r):
        return pltpu.make_async_copy(y_hbm.at[pl.ds(src_row, 1)], ybuf.at[sl, k, pl.ds(r, 1)], sem.at[sl])

    def gather(tile, sl):
        base = tile * tm * 2

        def issue(r, c):
            row_copy(dest_ref[base + 2 * r], sl, 0, r).start()
            row_copy(dest_ref[base + 2 * r + 1], sl, 1, r).start()
            return c

        lax.fori_loop(0, tm, issue, 0, unroll=4)

    @pl.when(i == 0)
    def _():
        gather(0, 0)

    @pl.when(i + 1 < n_tiles)
    def _():
        gather(i + 1, 1 - slot)

    def drain(r, c):
        row_copy(0, slot, 0, r).wait()
        row_copy(0, slot, 1, r).wait()
        return c

    lax.fori_loop(0, tm, drain, 0, unroll=4)

    is_p = i < n_p_tiles
    row = jnp.where(is_p, 0, 1 + (i - n_p_tiles) // tiles_per_seq)
    wt = wt_ref[...]
    f = wt[:, 0:1] * ybuf[slot, 0] + wt[:, 1:2] * ybuf[slot, 1]
    x1 = _ln(z_ref[...]) * g1_ref[...] + b1_ref[...]
    out = _ln(ALPHA * x1 + gate_ref[pl.ds(row, 1), :] * f) * g2_ref[...] + b2_ref[...]

    @pl.when(is_p)
    def _():
        op_ref[...] = out

    @pl.when(jnp.logical_not(is_p))
    def _():
        os_ref[...] = out


def _combine(dest, y_rows, z, weight, gate2, g1, b1, g2, b2, n_prompt, dec_seq):
    t = z.shape[0]
    tm = TM_FIN
    n_p = n_prompt // tm
    n_s = (t - n_prompt) // tm
    full = lambda i, d: (0, 0)
    grid_spec = pltpu.PrefetchScalarGridSpec(
        num_scalar_prefetch=1,
        grid=(n_p + n_s,),
        in_specs=[pl.BlockSpec(memory_space=pl.ANY),
                  pl.BlockSpec((tm, D_MODEL), lambda i, d: (i, 0)),
                  pl.BlockSpec((tm, 2), lambda i, d: (i, 0)),
                  pl.BlockSpec((8, D_MODEL), full),
                  pl.BlockSpec((1, D_MODEL), full), pl.BlockSpec((1, D_MODEL), full),
                  pl.BlockSpec((1, D_MODEL), full), pl.BlockSpec((1, D_MODEL), full)],
        out_specs=[pl.BlockSpec((tm, D_MODEL), lambda i, d: (jnp.minimum(i, n_p - 1), 0)),
                   pl.BlockSpec((tm, D_MODEL), lambda i, d: (jnp.maximum(i - n_p, 0), 0))],
        scratch_shapes=[pltpu.VMEM((2, 2, tm, D_MODEL), F32), pltpu.SemaphoreType.DMA((2,))],
    )
    return pl.pallas_call(
        functools.partial(_final_kernel, n_p_tiles=n_p, tiles_per_seq=dec_seq // tm),
        grid_spec=grid_spec,
        out_shape=[jax.ShapeDtypeStruct((n_prompt, D_MODEL), F32),
                   jax.ShapeDtypeStruct((t - n_prompt, D_MODEL), F32)],
        compiler_params=_params(("arbitrary",)),
        name="combine",
    )(dest, y_rows, z, weight, gate2, g1, b1, g2, b2)


def kernel(x_prompt, x_sample, state_lru, state_ret, c, c_ctx, w_mod, b_mod, w_in, conv_w, conv_b, lru_wa, lru_ba,
           lru_wx, lru_bx, lru_lam, ret_decay, w_out, ln1_g, ln1_b, router_g, router_g_b, router_e, router_e_b,
           w_gate, w_up, w_down, ln2_g, ln2_b):
    assert w_in.shape[0] == 1, "single trunk layer"
    nb, seq, d = x_prompt.shape
    nbs, dec_seq, _ = x_sample.shape
    tp, ts = nb * seq, nbs * dec_seq
    assert tp % dec_seq == 0 and d == D_MODEL

    cond = jnp.zeros((8, d), F32).at[0].set(c_ctx).at[1:1 + nbs].set(c)
    mod = _modulation(cond, w_mod[0], b_mod[0][None, :])
    shift1, scale1, gate1, shift2, scale2, gate2 = [mod[:, k * d:(k + 1) * d] for k in range(6)]

    xp = x_prompt.reshape(tp, d)
    xs = x_sample.reshape(ts, d)
    proj = _in_projection(xp, xs, shift1, scale1, w_in[0].astype(BF16), dec_seq)

    w_gates = jnp.concatenate([lru_wa[0, 0], lru_wx[0, 0], lru_wa[0, 1], lru_wx[0, 1]], -1).astype(BF16)
    pb = jnp.stack([lru_ba[0, 0], lru_bx[0, 0], lru_lam[0, 0], lru_ba[0, 1], lru_bx[0, 1], lru_lam[0, 1]], 0)
    proj_p = proj.reshape((tp + ts) // seq, seq, IN_COLS)
    proj_s = proj.reshape((tp + ts) // dec_seq, dec_seq, IN_COLS)
    ylp, st_lru = _lru_mixer(proj_p, 0, nb, 16, conv_w[0], conv_b[0][None, :], w_gates, pb,
                             jnp.zeros((2, nb, LRU_WIDTH), F32))
    yls, _ = _lru_mixer(proj_s, tp // dec_seq, nbs, nbs, conv_w[0], conv_b[0][None, :], w_gates, pb,
                        jnp.swapaxes(state_lru[:, 0], 0, 1))

    decay = jnp.broadcast_to(ret_decay[0].T[:, :, None], (RET_HEADS, 2, LANES))
    yrp, st_ret = _ret_mixer(proj_p, 0, nb, 8, decay, emit_state=True)
    (yrs,) = _ret_mixer(proj_s, tp // dec_seq, nbs, 1, decay, rope_tabs=_rope_tables(dec_seq), s0=state_ret)

    w_router = jnp.concatenate(
        [router_g[0], jnp.transpose(router_e[0], (1, 0, 2)).reshape(d, N_EXPERTS),
         jnp.zeros((d, LANES - N_GROUPS - N_EXPERTS), F32)], -1).astype(BF16)
    b_router = jnp.concatenate([router_g_b[0], router_e_b[0].reshape(-1),
                                jnp.zeros((LANES - N_GROUPS - N_EXPERTS,), F32)])[None, :]
    z, h2, logits = _out_projection(
        xp, xs, ylp.reshape(tp, -1), yrp.reshape(tp, -1), yls.reshape(ts, -1), yrs.reshape(ts, -1),
        w_out[0].astype(BF16), gate1, ln1_g, ln1_b, shift2, scale2, w_router, b_router, dec_seq)

    expert, weight = _route(logits)
    n_assign = 2 * (tp + ts)
    n_sub = -(-(n_assign + N_EXPERTS * (SUB_ROWS - 1)) // SUB_ROWS)
    n_sub = -(-n_sub * SUB_ROWS // DISPATCH_ROWS) * DISPATCH_ROWS // SUB_ROWS
    n_super = N_EXPERTS + -(-n_assign // (SUB_ROWS * SUPER))
    assert SUPER * n_super >= n_sub + (SUPER - 1) * N_EXPERTS
    dest, row_tok, sup_exp, n_comp, n_zero, out_sub, x_sub = _dispatch_plan(expert, n_sub, n_super)
    xs_rows = _dispatch(row_tok, h2)
    y_rows = _experts(xs_rows, sup_exp, n_comp, n_zero, out_sub, x_sub, w_gate[0], w_up[0], w_down[0])
    y_p, y_s = _combine(dest, y_rows, z, weight, gate2, ln1_g, ln1_b, ln2_g, ln2_b, tp, dec_seq)

    new_state_lru = jnp.swapaxes(st_lru, 0, 1)[:, None]
    return (y_p.reshape(nb, seq, d), y_s.reshape(nbs, dec_seq, d), new_state_lru, st_ret)
```

```python
import functools

import jax
import jax.numpy as jnp
import numpy as np
from jax import lax
from jax.experimental import pallas as pl
from jax.experimental.pallas import tpu as pltpu

F32 = jnp.float32
BF16 = jnp.bfloat16

D_MODEL = 4096
LRU_WIDTH = 2048
LRU_BLOCKS = 16
LANES = 128
RET_HEADS = 16
RET_DH = 128
IN_COLS = 12288
GRID_W = 64
ROPE_BASE = 10000.0
LRU_C = 8.0
N_GROUPS = 4
EXPERTS_PER_GROUP = 8
N_EXPERTS = 32
D_EXPERT = 1024
LN_EPS = 1e-6
GN_EPS = 1e-5
ALPHA = 2.0 ** 0.25

VMEM_LIMIT = 56 * 1024 * 1024
SCAN_PAD = 8

TM_PROJ = 512
TN_IN = 1024
TN_OUT = 512
TM_OUT = 512
SUB_ROWS = 256
SUPER = 4
F_CHUNK = 256
TN_DOWN = 1024
DISPATCH_ROWS = 1024
TM_FIN = 256
HALF_D = D_MODEL // 2


def _params(sem, vmem_limit=VMEM_LIMIT):
    return pltpu.CompilerParams(dimension_semantics=sem, vmem_limit_bytes=vmem_limit)


def _pack_bf16_pair(lo, hi):
    lo_bits = pltpu.bitcast(lo.astype(BF16).astype(F32), jnp.uint32) >> 16
    hi_bits = pltpu.bitcast(hi.astype(BF16).astype(F32), jnp.uint32) & jnp.uint32(0xFFFF0000)
    return lo_bits | hi_bits


def _unpack_bf16_pair(words):
    lo = pltpu.bitcast(words << 16, F32).astype(BF16)
    hi = pltpu.bitcast(words & jnp.uint32(0xFFFF0000), F32).astype(BF16)
    return jnp.concatenate([lo, hi], axis=1)


def _sigmoid(x):
    return 0.5 * jnp.tanh(0.5 * x) + 0.5


def _softplus(x):
    return jnp.maximum(x, 0.0) + jnp.log1p(jnp.exp(-jnp.abs(x)))


def _ln(x):
    mu = jnp.mean(x, -1, keepdims=True)
    xc = x - mu
    var = jnp.mean(xc * xc, -1, keepdims=True)
    return xc * lax.rsqrt(var + LN_EPS)


def _mod_kernel(cond_ref, w_ref, b_ref, o_ref):
    c = cond_ref[...]
    s = (c * _sigmoid(c)).astype(BF16)
    o_ref[...] = jnp.dot(s, w_ref[...].astype(BF16), preferred_element_type=F32) + b_ref[...]


def _modulation(cond, w_mod, b_mod):
    tn = 512
    n = w_mod.shape[1]
    return pl.pallas_call(
        _mod_kernel,
        grid=(n // tn,),
        in_specs=[pl.BlockSpec((8, D_MODEL), lambda j: (0, 0)),
                  pl.BlockSpec((D_MODEL, tn), lambda j: (0, j)),
                  pl.BlockSpec((1, tn), lambda j: (0, j))],
        out_specs=pl.BlockSpec((8, tn), lambda j: (0, j)),
        out_shape=jax.ShapeDtypeStruct((8, n), F32),
        compiler_params=_params(("arbitrary",)),
        name="modulation",
    )(cond, w_mod, b_mod)


def _inproj_kernel(xp_ref, xs_ref, shift_ref, scale_ref, w_ref, o_ref, h_scr, *, n_p_tiles, tiles_per_seq):
    i = pl.program_id(0)
    j = pl.program_id(1)

    def fill(x_ref, row):
        h = _ln(x_ref[...]) * (1.0 + scale_ref[pl.ds(row, 1), :]) + shift_ref[pl.ds(row, 1), :]
        h_scr[...] = h.astype(BF16)

    @pl.when(jnp.logical_and(j == 0, i < n_p_tiles))
    def _():
        fill(xp_ref, 0)

    @pl.when(jnp.logical_and(j == 0, i >= n_p_tiles))
    def _():
        fill(xs_ref, 1 + (i - n_p_tiles) // tiles_per_seq)

    o_ref[...] = jnp.dot(h_scr[...], w_ref[...], preferred_element_type=F32)


def _in_projection(xp, xs, shift, scale, w_in_bf16, dec_seq):
    tp, ts = xp.shape[0], xs.shape[0]
    tm, tn = TM_PROJ, TN_IN
    n_p, n_s = tp // tm, ts // tm
    kern = functools.partial(_inproj_kernel, n_p_tiles=n_p, tiles_per_seq=dec_seq // tm)
    return pl.pallas_call(
        kern,
        grid=(n_p + n_s, IN_COLS // tn),
        in_specs=[pl.BlockSpec((tm, D_MODEL), lambda i, j: (jnp.minimum(i, n_p - 1), 0),
                               pipeline_mode=pl.Buffered(1)),
                  pl.BlockSpec((tm, D_MODEL), lambda i, j: (jnp.maximum(i - n_p, 0), 0),
                               pipeline_mode=pl.Buffered(1)),
                  pl.BlockSpec((8, D_MODEL), lambda i, j: (0, 0)),
                  pl.BlockSpec((8, D_MODEL), lambda i, j: (0, 0)),
                  pl.BlockSpec((D_MODEL, tn), lambda i, j: (0, j))],
        out_specs=pl.BlockSpec((tm, tn), lambda i, j: (i, j)),
        out_shape=jax.ShapeDtypeStruct((tp + ts, IN_COLS), F32),
        scratch_shapes=[pltpu.VMEM((tm, D_MODEL), BF16)],
        compiler_params=_params(("arbitrary", "arbitrary")),
        name="in_projection",
    )(xp, xs, shift, scale, w_in_bf16)


def _lru_kernel(x_ref, g_ref, cw_ref, cb_ref, wg_ref, pb_ref, h0_ref, y_ref, st_ref,
                xc_scr, af_scr, uf_scr, ab_scr, ub_scr, *, bg, seq, pitch, ncb):
    rows = lax.broadcasted_iota(jnp.int32, (seq, LANES), 0)
    for c in range(ncb):
        lanes = slice(c * LANES, (c + 1) * LANES)
        w = cw_ref[:, lanes]
        bias = cb_ref[:, lanes]
        for b in range(bg):
            x = x_ref[b, :, lanes]
            xm2 = jnp.where(rows >= 2, pltpu.roll(x, 2, 0), 0.0)
            xm1 = jnp.where(rows >= 1, pltpu.roll(x, 1, 0), 0.0)
            xp1 = jnp.where(rows < seq - 1, pltpu.roll(x, seq - 1, 0), 0.0)
            xc_scr[pl.ds(b * seq, seq), :] = bias + xm2 * w[0:1] + xm1 * w[1:2] + x * w[2:3] + xp1 * w[3:4]

        xc = xc_scr[...]
        gates = jnp.dot(xc.astype(BF16), wg_ref[c], preferred_element_type=F32)
        pb = pb_ref[:, lanes]
        for d, (a_scr, u_scr) in enumerate(((af_scr, uf_scr), (ab_scr, ub_scr))):
            r = _sigmoid(gates[:, (2 * d) * LANES:(2 * d + 1) * LANES] + pb[3 * d:3 * d + 1])
            ig = _sigmoid(gates[:, (2 * d + 1) * LANES:(2 * d + 2) * LANES] + pb[3 * d + 1:3 * d + 2])
            log_a = (-LRU_C) * r * _softplus(-pb[3 * d + 2:3 * d + 3])
            a = jnp.exp(log_a)
            u = jnp.sqrt(-jnp.tanh(log_a) * (1.0 + a * a)) * (ig * xc)
            for b in range(bg):
                a_scr[c, pl.ds(b * pitch, seq), :] = a[b * seq:(b + 1) * seq]
                u_scr[c, pl.ds(b * pitch, seq), :] = u[b * seq:(b + 1) * seq]

    def step(t, carry):
        tb = seq - 1 - t
        out = []
        for c in range(ncb):
            hf, hb = carry[2 * c], carry[2 * c + 1]
            hf = af_scr[c, pl.ds(t, bg, stride=pitch), :] * hf + uf_scr[c, pl.ds(t, bg, stride=pitch), :]
            uf_scr[c, pl.ds(t, bg, stride=pitch), :] = hf
            hb = ab_scr[c, pl.ds(tb, bg, stride=pitch), :] * hb + ub_scr[c, pl.ds(tb, bg, stride=pitch), :]
            ub_scr[c, pl.ds(tb, bg, stride=pitch), :] = hb
            out += [hf, hb]
        return tuple(out)

    init = []
    for c in range(ncb):
        init += [h0_ref[0, :, c * LANES:(c + 1) * LANES], h0_ref[1, :, c * LANES:(c + 1) * LANES]]
    last = lax.fori_loop(0, seq, step, tuple(init), unroll=8)
    for c in range(ncb):
        lanes = slice(c * LANES, (c + 1) * LANES)
        st_ref[0, :, lanes] = last[2 * c]
        st_ref[1, :, lanes] = last[2 * c + 1]
        for b in range(bg):
            hs = uf_scr[c, pl.ds(b * pitch, seq), :] + ub_scr[c, pl.ds(b * pitch, seq), :]
            y_ref[b, :, lanes] = (jax.nn.gelu(g_ref[b, :, lanes]) * hs).astype(BF16)


def _lru_mixer(proj3, seq0, nseq, bg, ncb, conv_w, conv_b, w_gates, pb, h0):
    seq = proj3.shape[1]
    pitch = seq + SCAN_PAD
    off = seq0 // bg
    cw = ncb * LANES
    ncol = LRU_WIDTH // cw
    kern = functools.partial(_lru_kernel, bg=bg, seq=seq, pitch=pitch, ncb=ncb)
    return pl.pallas_call(
        kern,
        grid=(nseq // bg, ncol),
        in_specs=[pl.BlockSpec((bg, seq, cw), lambda b, c: (b + off, 0, c)),
                  pl.BlockSpec((bg, seq, cw), lambda b, c: (b + off, 0, ncol + c)),
                  pl.BlockSpec((4, cw), lambda b, c: (0, c)),
                  pl.BlockSpec((1, cw), lambda b, c: (0, c)),
                  pl.BlockSpec((ncb, LANES, 4 * LANES), lambda b, c: (c, 0, 0)),
                  pl.BlockSpec((6, cw), lambda b, c: (0, c)),
                  pl.BlockSpec((2, bg, cw), lambda b, c: (0, b, c))],
        out_specs=[pl.BlockSpec((bg, seq, cw), lambda b, c: (b, 0, c)),
                   pl.BlockSpec((2, bg, cw), lambda b, c: (0, b, c))],
        out_shape=[jax.ShapeDtypeStruct((nseq, seq, LRU_WIDTH), BF16),
                   jax.ShapeDtypeStruct((2, nseq, LRU_WIDTH), F32)],
        scratch_shapes=[pltpu.VMEM((bg * seq, LANES), F32)] + [pltpu.VMEM((ncb, bg * pitch, LANES), F32)] * 4,
        compiler_params=_params(("arbitrary", "arbitrary")),
        name="lru_mixer",
    )(proj3, proj3, conv_w, conv_b, w_gates, pb, h0)


def _rope(x, cos, sin_signed, first_half):
    partner = jnp.where(first_half, pltpu.roll(x, LANES - 32, 1), pltpu.roll(x, 32, 1))
    return x * cos + partner * sin_signed


def _ret_kernel(*refs, bg, seq, rope, has_state, emit_state, qb):
    refs = list(refs)
    q_ref, k_ref, v_ref, g_ref, dec_ref = refs[:5]
    pos = 5
    if rope:
        cos_ref, sin_ref = refs[pos:pos + 2]
        pos += 2
    if has_state:
        s0_ref = refs[pos]
        pos += 1
    y_ref = refs[pos]
    pos += 1
    if emit_state:
        st_ref = refs[pos]
        pos += 1
    mask_scr = refs[pos]

    log_g = -_softplus(-dec_ref[0])
    lgf, lgb = log_g[0:1], log_g[1:2]
    reps = seq // LANES
    lgf_row = jnp.concatenate([lgf] * reps, axis=1)
    lgb_row = jnp.concatenate([lgb] * reps, axis=1)

    @pl.when(pl.program_id(1) == 0)
    def _():
        for blk in range(seq // qb):
            ti = lax.broadcasted_iota(jnp.int32, (qb, seq), 0) + blk * qb
            si = lax.broadcasted_iota(jnp.int32, (qb, seq), 1)
            dist = (ti - si).astype(F32)
            e = jnp.where(dist >= 0, dist * lgf_row, (-dist) * lgb_row)
            mask_scr[pl.ds(blk * qb, qb), :] = jnp.where(dist == 0, 2.0, jnp.exp(e))

    lane = lax.broadcasted_iota(jnp.int32, (seq, LANES), 1)
    first_half = (lane % 64) < 32
    trow = lax.broadcasted_iota(jnp.int32, (seq, LANES), 0).astype(F32)
    for b in range(bg):
        q = q_ref[b]
        k = k_ref[b] * (RET_DH ** -0.5)
        v16 = v_ref[b].astype(BF16)
        if rope:
            q = _rope(q, cos_ref[...], sin_ref[...], first_half)
            k = _rope(k, cos_ref[...], sin_ref[...], first_half)
        q16 = q.astype(BF16)
        k16 = k.astype(BF16)
        if has_state:
            qf16 = (q * jnp.exp((trow + 1.0) * lgf)).astype(BF16)
            qb16 = (q * jnp.exp((float(seq) - trow) * lgb)).astype(BF16)
            s0f = s0_ref[b, 0].astype(BF16)
            s0b = s0_ref[b, 1].astype(BF16)
        for blk in range(seq // qb):
            sl = slice(blk * qb, (blk + 1) * qb)
            s = lax.dot_general(q16[sl], k16, (((1,), (1,)), ((), ())), preferred_element_type=F32)
            p = (s * mask_scr[pl.ds(blk * qb, qb), :]).astype(BF16)
            o = jnp.dot(p, v16, preferred_element_type=F32)
            if has_state:
                o = o + jnp.dot(qf16[sl], s0f, preferred_element_type=F32)
                o = o + jnp.dot(qb16[sl], s0b, preferred_element_type=F32)
            mu = jnp.mean(o, -1, keepdims=True)
            oc = o - mu
            var = jnp.mean(oc * oc, -1, keepdims=True)
            on = oc * lax.rsqrt(var + GN_EPS)
            gt = g_ref[b, pl.ds(blk * qb, qb), :]
            y_ref[b, pl.ds(blk * qb, qb), :] = (gt * _sigmoid(gt) * on).astype(BF16)
        if emit_state:
            kf16 = (k * jnp.exp((float(seq - 1) - trow) * lgf)).astype(BF16)
            kb16 = (k * jnp.exp(trow * lgb)).astype(BF16)
            sf = lax.dot_general(kf16, v16, (((0,), (0,)), ((), ())), preferred_element_type=F32)
            sb = lax.dot_general(kb16, v16, (((0,), (0,)), ((), ())), preferred_element_type=F32)
            if has_state:
                sf = sf + jnp.exp(float(seq) * lgf) * s0_ref[b, 0]
                sb = sb + jnp.exp(float(seq) * lgb) * s0_ref[b, 1]
            st_ref[b, 0] = sf
            st_ref[b, 1] = sb


def _ret_mixer(proj3, seq0, nseq, bg, decay, rope_tabs=None, s0=None, emit_state=False):
    seq = proj3.shape[1]
    off = seq0 // bg
    qb = min(seq, 256)
    rope = rope_tabs is not None
    has_state = s0 is not None
    kern = functools.partial(_ret_kernel, bg=bg, seq=seq, rope=rope, has_state=has_state,
                             emit_state=emit_state, qb=qb)

    def col(base):
        return pl.BlockSpec((bg, seq, LANES), lambda h, b: (b + off, 0, base + h))

    st_spec = pl.BlockSpec((bg, None, 2, None, RET_DH, RET_DH), lambda h, b: (b, 0, 0, h, 0, 0))
    in_specs = [col(2 * LRU_BLOCKS), col(2 * LRU_BLOCKS + RET_HEADS), col(2 * LRU_BLOCKS + 2 * RET_HEADS),
                col(2 * LRU_BLOCKS + 3 * RET_HEADS), pl.BlockSpec((1, 2, LANES), lambda h, b: (h, 0, 0))]
    args = [proj3, proj3, proj3, proj3, decay]
    if rope:
        in_specs += [pl.BlockSpec((seq, LANES), lambda h, b: (0, 0))] * 2
        args += list(rope_tabs)
    if has_state:
        in_specs.append(st_spec)
        args.append(s0)
    out_specs = [pl.BlockSpec((bg, seq, LANES), lambda h, b: (b, 0, h))]
    out_shape = [jax.ShapeDtypeStruct((nseq, seq, RET_HEADS * RET_DH), BF16)]
    if emit_state:
        out_specs.append(st_spec)
        out_shape.append(jax.ShapeDtypeStruct((nseq, 1, 2, RET_HEADS, RET_DH, RET_DH), F32))
    return pl.pallas_call(
        kern,
        grid=(RET_HEADS, nseq // bg),
        in_specs=in_specs,
        out_specs=out_specs,
        out_shape=out_shape,
        scratch_shapes=[pltpu.VMEM((seq, seq), F32)],
        compiler_params=_params(("arbitrary", "arbitrary")),
        name="ret_mixer",
    )(*args)


def _rope_tables(seq):
    nf = RET_DH // 4
    freqs = (np.float32(ROPE_BASE) ** (-np.arange(nf, dtype=np.float32) / np.float32(nf))).astype(np.float32)
    t = np.arange(seq)
    row = (t // GRID_W).astype(np.float32)[:, None] * freqs[None, :]
    colp = (t % GRID_W).astype(np.float32)[:, None] * freqs[None, :]
    cos = np.concatenate([np.cos(row), np.cos(row), np.cos(colp), np.cos(colp)], -1)
    sin = np.concatenate([-np.sin(row), np.sin(row), -np.sin(colp), np.sin(colp)], -1)
    return jnp.asarray(cos, F32), jnp.asarray(sin, F32)


def _outproj_kernel(xp_ref, xs_ref, ylp_ref, yrp_ref, yls_ref, yrs_ref, wa_ref, wb_ref, gate_ref,
                    lng_ref, lnb_ref, shift_ref, scale_ref, wr_ref, br_ref,
                    z_ref, h2_ref, lg_ref, z_scr, *, n_p_tiles, tiles_per_seq, nj, tn):
    i = pl.program_id(0)
    j = pl.program_id(1)
    is_p = i < n_p_tiles
    row = jnp.where(is_p, 0, 1 + (i - n_p_tiles) // tiles_per_seq)

    def mix(x_ref, yl_ref, yr_ref):
        m = jnp.dot(yl_ref[...], wa_ref[...], preferred_element_type=F32)
        m = m + jnp.dot(yr_ref[...], wb_ref[...], preferred_element_type=F32)
        z = ALPHA * x_ref[...] + gate_ref[pl.ds(row, 1), :] * m
        z_ref[...] = z
        z_scr[j] = z

    @pl.when(is_p)
    def _():
        mix(xp_ref, ylp_ref, yrp_ref)

    @pl.when(jnp.logical_not(is_p))
    def _():
        mix(xs_ref, yls_ref, yrs_ref)

    @pl.when(j == nj - 1)
    def _():
        tm = z_scr.shape[1]
        inv_d = 1.0 / D_MODEL
        s1 = jnp.zeros((tm, 1), F32)
        for c in range(nj):
            s1 = s1 + jnp.sum(z_scr[c], -1, keepdims=True)
        mu = s1 * inv_d
        s2 = jnp.zeros((tm, 1), F32)
        for c in range(nj):
            zc = z_scr[c] - mu
            s2 = s2 + jnp.sum(zc * zc, -1, keepdims=True)
        rstd = lax.rsqrt(s2 * inv_d + LN_EPS)
        t1 = jnp.zeros((tm, 1), F32)
        for c in range(nj):
            cs = slice(c * tn, (c + 1) * tn)
            x1 = (z_scr[c] - mu) * rstd * lng_ref[:, cs] + lnb_ref[:, cs]
            z_scr[c] = x1
            t1 = t1 + jnp.sum(x1, -1, keepdims=True)
        mu2 = t1 * inv_d
        t2 = jnp.zeros((tm, 1), F32)
        for c in range(nj):
            xc = z_scr[c] - mu2
            t2 = t2 + jnp.sum(xc * xc, -1, keepdims=True)
        rstd2 = lax.rsqrt(t2 * inv_d + LN_EPS)
        logits = jnp.zeros((tm, LANES), F32) + br_ref[...]

        def h2_chunk(c):
            cs = slice(c * tn, (c + 1) * tn)
            return (z_scr[c] - mu2) * rstd2 * (1.0 + scale_ref[pl.ds(row, 1), cs]) + shift_ref[pl.ds(row, 1), cs]

        for c in range(nj // 2):
            lo, hi = h2_chunk(c), h2_chunk(c + nj // 2)
            h2_ref[:, c * tn:(c + 1) * tn] = _pack_bf16_pair(lo, hi)
            logits = logits + jnp.dot(lo.astype(BF16), wr_ref[c * tn:(c + 1) * tn, :], preferred_element_type=F32)
            logits = logits + jnp.dot(hi.astype(BF16), wr_ref[HALF_D + c * tn:HALF_D + (c + 1) * tn, :],
                                      preferred_element_type=F32)
        lg_ref[...] = logits


def _out_projection(xp, xs, ylp, yrp, yls, yrs, w_out_bf16, gate1, ln_g, ln_b, shift2, scale2, w_router, b_router,
                    dec_seq):
    tp, ts = xp.shape[0], xs.shape[0]
    tm, tn = TM_OUT, TN_OUT
    n_p, n_s = tp // tm, ts // tm
    nj = D_MODEL // tn
    half = LRU_WIDTH
    kern = functools.partial(_outproj_kernel, n_p_tiles=n_p, tiles_per_seq=dec_seq // tm, nj=nj, tn=tn)
    p_idx = lambda i, j: (jnp.minimum(i, n_p - 1), 0)
    s_idx = lambda i, j: (jnp.maximum(i - n_p, 0), 0)
    full = lambda i, j: (0, 0)
    return pl.pallas_call(
        kern,
        grid=(n_p + n_s, nj),
        in_specs=[pl.BlockSpec((tm, tn), lambda i, j: (jnp.minimum(i, n_p - 1), j)),
                  pl.BlockSpec((tm, tn), lambda i, j: (jnp.maximum(i - n_p, 0), j)),
                  pl.BlockSpec((tm, half), p_idx), pl.BlockSpec((tm, half), p_idx),
                  pl.BlockSpec((tm, half), s_idx), pl.BlockSpec((tm, half), s_idx),
                  pl.BlockSpec((half, tn), lambda i, j: (0, j)),
                  pl.BlockSpec((half, tn), lambda i, j: (1, j)),
                  pl.BlockSpec((8, tn), lambda i, j: (0, j)),
                  pl.BlockSpec((1, D_MODEL), full), pl.BlockSpec((1, D_MODEL), full),
                  pl.BlockSpec((8, D_MODEL), full), pl.BlockSpec((8, D_MODEL), full),
                  pl.BlockSpec((D_MODEL, LANES), full), pl.BlockSpec((1, LANES), full)],
        out_specs=[pl.BlockSpec((tm, tn), lambda i, j: (i, j)),
                   pl.BlockSpec((tm, HALF_D), lambda i, j: (i, 0)),
                   pl.BlockSpec((tm, LANES), lambda i, j: (i, 0))],
        out_shape=[jax.ShapeDtypeStruct((tp + ts, D_MODEL), F32),
                   jax.ShapeDtypeStruct((tp + ts, HALF_D), jnp.uint32),
                   jax.ShapeDtypeStruct((tp + ts, LANES), F32)],
        scratch_shapes=[pltpu.VMEM((nj, tm, tn), F32)],
        compiler_params=_params(("arbitrary", "arbitrary")),
        name="out_projection",
    )(xp, xs, ylp, yrp, yls, yrs, w_out_bf16, w_out_bf16, gate1, ln_g, ln_b, shift2, scale2, w_router, b_router)


def _route(logits):
    lg = logits[:, :N_GROUPS]
    le = logits[:, N_GROUPS:N_GROUPS + N_EXPERTS].reshape(-1, N_GROUPS, EXPERTS_PER_GROUP)
    pg = jax.nn.softmax(lg, -1)
    g_sel = jnp.argmax(lg, -1)
    p_sel = jnp.take_along_axis(pg, g_sel[:, None], 1)[:, 0]
    le_sel = jnp.take_along_axis(le, g_sel[:, None, None], 1)[:, 0]
    top_v, top_i = lax.top_k(le_sel, 2)
    weight = p_sel[:, None] * jax.nn.softmax(top_v, -1)
    expert = (g_sel[:, None] * EXPERTS_PER_GROUP + top_i).astype(jnp.int32)
    return expert, weight


def _dispatch_plan(expert, n_sub, n_super):
    i32 = jnp.int32
    flat_e = expert.reshape(-1)
    n_assign = flat_e.shape[0]
    ids = jnp.arange(N_EXPERTS, dtype=i32)
    onehot = (flat_e[:, None] == ids[None, :]).astype(i32)
    csum = jnp.cumsum(onehot, 0)
    counts = csum[-1]
    rank = jnp.sum(onehot * (csum - 1), 1)
    nb = (counts + SUB_ROWS - 1) // SUB_ROWS
    sub_end = jnp.cumsum(nb)
    sub_start = sub_end - nb
    dest = (jnp.sum(onehot * (sub_start * SUB_ROWS)[None, :], 1) + rank).astype(i32)
    tok = jnp.arange(n_assign, dtype=i32) // 2
    n_tok = n_assign // 2
    row_tok = (jnp.arange(n_sub * SUB_ROWS, dtype=i32) % n_tok).at[dest].set(tok)

    nsup = (nb + SUPER - 1) // SUPER
    sup_end = jnp.cumsum(nsup)
    sup_start = sup_end - nsup
    n_used = sup_end[-1]
    s = jnp.arange(n_super, dtype=i32)
    used = s < n_used
    last_exp = jnp.max(jnp.where(counts > 0, ids, 0))
    e_s = jnp.where(used, jnp.minimum(jnp.searchsorted(sup_end, s, side="right").astype(i32), N_EXPERTS - 1), last_exp)
    local = s - sup_start[e_s]
    first_sub = sub_start[e_s] + SUPER * local
    n_comp = jnp.where(used, jnp.clip(nb[e_s] - SUPER * local, 0, SUPER), 0).astype(i32)
    zero_first = sub_end[-1] + SUPER * (s - n_used)
    n_zero = jnp.where(used, 0, jnp.clip(n_sub - zero_first, 0, SUPER)).astype(i32)
    out_sub = jnp.where(used, first_sub, jnp.minimum(zero_first, n_sub - 1)).astype(i32)
    k = jnp.arange(SUPER, dtype=i32)
    x_sub_used = first_sub[:, None] + jnp.minimum(k[None, :], jnp.maximum(n_comp - 1, 0)[:, None])
    x_sub_last = x_sub_used[jnp.maximum(n_used - 1, 0)]
    x_sub = jnp.where(used[:, None], x_sub_used, x_sub_last[None, :]).astype(i32).reshape(-1)
    return dest, row_tok, e_s.astype(i32), n_comp, n_zero, out_sub, x_sub


def _dispatch_kernel(tok_ref, h_hbm, xs_ref, sem):
    base = pl.program_id(0) * DISPATCH_ROWS

    def row_copy(src_row, dst_row):
        return pltpu.make_async_copy(h_hbm.at[pl.ds(src_row, 1)], xs_ref.at[pl.ds(dst_row, 1)], sem)

    def issue(p, c):
        row_copy(tok_ref[base + 2 * p], 2 * p).start(priority=0)
        row_copy(tok_ref[base + 2 * p + 1], 2 * p + 1).start(priority=1)
        return c

    lax.fori_loop(0, DISPATCH_ROWS // 2, issue, 0, unroll=4)

    def drain(r, c):
        row_copy(0, r).wait()
        return c

    lax.fori_loop(0, DISPATCH_ROWS, drain, 0, unroll=8)


def _dispatch(row_tok, h2_packed):
    n_rows = row_tok.shape[0]
    grid_spec = pltpu.PrefetchScalarGridSpec(
        num_scalar_prefetch=1,
        grid=(n_rows // DISPATCH_ROWS,),
        in_specs=[pl.BlockSpec(memory_space=pl.ANY)],
        out_specs=pl.BlockSpec((DISPATCH_ROWS, HALF_D), lambda i, tok: (i, 0)),
        scratch_shapes=[pltpu.SemaphoreType.DMA(())],
    )
    return pl.pallas_call(
        _dispatch_kernel,
        grid_spec=grid_spec,
        out_shape=jax.ShapeDtypeStruct((n_rows, HALF_D), jnp.uint32),
        compiler_params=_params(("arbitrary",)),
        name="dispatch",
    )(row_tok, h2_packed)


def _expert_kernel(exp_ref, nc_ref, nz_ref, osub_ref, xsub_ref, x0_ref, x1_ref, x2_ref, x3_ref,
                   wg_ref, wu_ref, wd_ref, y_hbm, a_scr, ytile, sem, *, nf, nd):
    s = pl.program_id(0)
    t = pl.program_id(1)
    n_comp = nc_ref[s]
    n_out = n_comp + nz_ref[s]
    x_refs = (x0_ref, x1_ref, x2_ref, x3_ref)

    @pl.when(jnp.logical_and(t < nf, n_comp > 0))
    def _():
        wg16 = wg_ref[...].astype(BF16)
        wu16 = wu_ref[...].astype(BF16)

        def up(k):
            x = _unpack_bf16_pair(x_refs[k][...])
            g = jnp.dot(x, wg16, preferred_element_type=F32)
            u = jnp.dot(x, wu16, preferred_element_type=F32)
            a_scr[t, k * SUB_ROWS:(k + 1) * SUB_ROWS, :] = (g * _sigmoid(g) * u).astype(BF16)

        up(0)
        for k in range(1, SUPER):
            pl.when(k < n_comp)(functools.partial(up, k))

    @pl.when(t >= nf)
    def _():
        j = t - nf
        slot = j % 2

        @pl.when(n_comp > 0)
        def _():
            wd16 = [wd_ref[f * F_CHUNK:(f + 1) * F_CHUNK, :].astype(BF16) for f in range(nf)]

            def down(k):
                rows = slice(k * SUB_ROWS, (k + 1) * SUB_ROWS)
                acc = jnp.dot(a_scr[0, rows, :], wd16[0], preferred_element_type=F32)
                for f in range(1, nf):
                    acc = acc + jnp.dot(a_scr[f, rows, :], wd16[f], preferred_element_type=F32)
                ytile[slot, rows, :] = acc

            down(0)
            for k in range(1, SUPER):
                pl.when(k < n_comp)(functools.partial(down, k))

        @pl.when(n_comp == 0)
        def _():
            ytile[slot] = jnp.zeros(ytile.shape[1:], F32)

        def out_copy(sl, k, jj):
            dst_rows = pl.ds(pl.multiple_of((osub_ref[s] + k) * SUB_ROWS, SUB_ROWS), SUB_ROWS)
            dst_cols = pl.ds(pl.multiple_of(jj * TN_DOWN, TN_DOWN), TN_DOWN)
            return pltpu.make_async_copy(ytile.at[sl, pl.ds(k * SUB_ROWS, SUB_ROWS), :],
                                         y_hbm.at[dst_rows, dst_cols], sem)

        for k in range(SUPER):
            @pl.when(jnp.logical_and(j > 0, k < n_out))
            def _():
                out_copy(1 - slot, k, j - 1).wait()
        for k in range(SUPER):
            @pl.when(k < n_out)
            def _():
                out_copy(slot, k, j).start()
        for k in range(SUPER):
            @pl.when(jnp.logical_and(j == nd - 1, k < n_out))
            def _():
                out_copy(slot, k, j).wait()


def _experts(xs, sup_exp, n_comp, n_zero, out_sub, x_sub, w_gate, w_up, w_down):
    n_super = sup_exp.shape[0]
    n_rows = xs.shape[0]
    nf = D_EXPERT // F_CHUNK
    nd = D_MODEL // TN_DOWN

    def up_idx(s, t, e, nc, nz, osub, xsub):
        return (e[s], 0, jnp.where(nc[s] > 0, jnp.minimum(t, nf - 1), nf - 1))

    def down_idx(s, t, e, nc, nz, osub, xsub):
        return (e[s], 0, jnp.where(nc[s] > 0, jnp.maximum(t - nf, 0), nd - 1))

    def x_spec(k):
        return pl.BlockSpec((SUB_ROWS, HALF_D), lambda s, t, e, nc, nz, osub, xsub: (xsub[s * SUPER + k], 0))

    grid_spec = pltpu.PrefetchScalarGridSpec(
        num_scalar_prefetch=5,
        grid=(n_super, nf + nd),
        in_specs=[x_spec(0), x_spec(1), x_spec(2), x_spec(3),
                  pl.BlockSpec((None, D_MODEL, F_CHUNK), up_idx),
                  pl.BlockSpec((None, D_MODEL, F_CHUNK), up_idx),
                  pl.BlockSpec((None, D_EXPERT, TN_DOWN), down_idx)],
        out_specs=pl.BlockSpec(memory_space=pl.ANY),
        scratch_shapes=[pltpu.VMEM((nf, SUPER * SUB_ROWS, F_CHUNK), BF16),
                        pltpu.VMEM((2, SUPER * SUB_ROWS, TN_DOWN), F32),
                        pltpu.SemaphoreType.DMA(())],
    )
    return pl.pallas_call(
        functools.partial(_expert_kernel, nf=nf, nd=nd),
        grid_spec=grid_spec,
        out_shape=jax.ShapeDtypeStruct((n_rows, D_MODEL), F32),
        compiler_params=_params(("arbitrary", "arbitrary"), 60 * 1024 * 1024),
        name="experts",
    )(sup_exp, n_comp, n_zero, out_sub, x_sub, xs, xs, xs, xs, w_gate, w_up, w_down)


def _final_kernel(dest_ref, y_hbm, z_ref, wt_ref, gate_ref, g1_ref, b1_ref, g2_ref, b2_ref,
                  op_ref, os_ref, ybuf, sem, *, n_p_tiles, tiles_per_seq):
    i = pl.program_id(0)
    n_tiles = pl.num_programs(0)
    tm = z_ref.shape[0]
    slot = i % 2

    def row_copy(src_row, sl, k, r):
        return pltpu.make_async_copy(y_hbm.at[pl.ds(src_row, 1)], ybuf.at[sl, k, pl.ds(r, 1)], sem.at[sl])

    def gather(tile, sl):
        base = tile * tm * 2

        def issue(r, c):
            row_copy(dest_ref[base + 2 * r], sl, 0, r).start(priority=0)
            row_copy(dest_ref[base + 2 * r + 1], sl, 1, r).start(priority=1)
            return c

        lax.fori_loop(0, tm, issue, 0, unroll=4)

    @pl.when(i == 0)
    def _():
        gather(0, 0)

    @pl.when(i + 1 < n_tiles)
    def _():
        gather(i + 1, 1 - slot)

    def drain(r, c):
        row_copy(0, slot, 0, r).wait()
        row_copy(0, slot, 1, r).wait()
        return c

    lax.fori_loop(0, tm, drain, 0, unroll=4)

    is_p = i < n_p_tiles
    row = jnp.where(is_p, 0, 1 + (i - n_p_tiles) // tiles_per_seq)
    wt = wt_ref[...]
    f = wt[:, 0:1] * ybuf[slot, 0] + wt[:, 1:2] * ybuf[slot, 1]
    x1 = _ln(z_ref[...]) * g1_ref[...] + b1_ref[...]
    out = _ln(ALPHA * x1 + gate_ref[pl.ds(row, 1), :] * f) * g2_ref[...] + b2_ref[...]

    @pl.when(is_p)
    def _():
        op_ref[...] = out

    @pl.when(jnp.logical_not(is_p))
    def _():
        os_ref[...] = out


def _combine(dest, y_rows, z, weight, gate2, g1, b1, g2, b2, n_prompt, dec_seq):
    t = z.shape[0]
    tm = TM_FIN
    n_p = n_prompt // tm
    n_s = (t - n_prompt) // tm
    full = lambda i, d: (0, 0)
    grid_spec = pltpu.PrefetchScalarGridSpec(
        num_scalar_prefetch=1,
        grid=(n_p + n_s,),
        in_specs=[pl.BlockSpec(memory_space=pl.ANY),
                  pl.BlockSpec((tm, D_MODEL), lambda i, d: (i, 0)),
                  pl.BlockSpec((tm, 2), lambda i, d: (i, 0)),
                  pl.BlockSpec((8, D_MODEL), full),
                  pl.BlockSpec((1, D_MODEL), full), pl.BlockSpec((1, D_MODEL), full),
                  pl.BlockSpec((1, D_MODEL), full), pl.BlockSpec((1, D_MODEL), full)],
        out_specs=[pl.BlockSpec((tm, D_MODEL), lambda i, d: (jnp.minimum(i, n_p - 1), 0)),
                   pl.BlockSpec((tm, D_MODEL), lambda i, d: (jnp.maximum(i - n_p, 0), 0))],
        scratch_shapes=[pltpu.VMEM((2, 2, tm, D_MODEL), F32), pltpu.SemaphoreType.DMA((2,))],
    )
    return pl.pallas_call(
        functools.partial(_final_kernel, n_p_tiles=n_p, tiles_per_seq=dec_seq // tm),
        grid_spec=grid_spec,
        out_shape=[jax.ShapeDtypeStruct((n_prompt, D_MODEL), F32),
                   jax.ShapeDtypeStruct((t - n_prompt, D_MODEL), F32)],
        compiler_params=_params(("arbitrary",)),
        name="combine",
    )(dest, y_rows, z, weight, gate2, g1, b1, g2, b2)


def kernel(x_prompt, x_sample, state_lru, state_ret, c, c_ctx, w_mod, b_mod, w_in, conv_w, conv_b, lru_wa, lru_ba,
           lru_wx, lru_bx, lru_lam, ret_decay, w_out, ln1_g, ln1_b, router_g, router_g_b, router_e, router_e_b,
           w_gate, w_up, w_down, ln2_g, ln2_b):
    assert w_in.shape[0] == 1, "single trunk layer"
    nb, seq, d = x_prompt.shape
    nbs, dec_seq, _ = x_sample.shape
    tp, ts = nb * seq, nbs * dec_seq
    assert tp % dec_seq == 0 and d == D_MODEL

    cond = jnp.zeros((8, d), F32).at[0].set(c_ctx).at[1:1 + nbs].set(c)
    mod = _modulation(cond, w_mod[0], b_mod[0][None, :])
    shift1, scale1, gate1, shift2, scale2, gate2 = [mod[:, k * d:(k + 1) * d] for k in range(6)]

    xp = x_prompt.reshape(tp, d)
    xs = x_sample.reshape(ts, d)
    proj = _in_projection(xp, xs, shift1, scale1, w_in[0].astype(BF16), dec_seq)

    w_gates = jnp.concatenate([lru_wa[0, 0], lru_wx[0, 0], lru_wa[0, 1], lru_wx[0, 1]], -1).astype(BF16)
    pb = jnp.stack([lru_ba[0, 0], lru_bx[0, 0], lru_lam[0, 0], lru_ba[0, 1], lru_bx[0, 1], lru_lam[0, 1]], 0)
    proj_p = proj.reshape((tp + ts) // seq, seq, IN_COLS)
    proj_s = proj.reshape((tp + ts) // dec_seq, dec_seq, IN_COLS)
    ylp, st_lru = _lru_mixer(proj_p, 0, nb, 16, 1, conv_w[0], conv_b[0][None, :], w_gates, pb,
                             jnp.zeros((2, nb, LRU_WIDTH), F32))
    yls, _ = _lru_mixer(proj_s, tp // dec_seq, nbs, nbs, 4, conv_w[0], conv_b[0][None, :], w_gates, pb,
                        jnp.swapaxes(state_lru[:, 0], 0, 1))

    decay = jnp.broadcast_to(ret_decay[0].T[:, :, None], (RET_HEADS, 2, LANES))
    yrp, st_ret = _ret_mixer(proj_p, 0, nb, 8, decay, emit_state=True)
    (yrs,) = _ret_mixer(proj_s, tp // dec_seq, nbs, 1, decay, rope_tabs=_rope_tables(dec_seq), s0=state_ret)

    w_router = jnp.concatenate(
        [router_g[0], jnp.transpose(router_e[0], (1, 0, 2)).reshape(d, N_EXPERTS),
         jnp.zeros((d, LANES - N_GROUPS - N_EXPERTS), F32)], -1).astype(BF16)
    b_router = jnp.concatenate([router_g_b[0], router_e_b[0].reshape(-1),
                                jnp.zeros((LANES - N_GROUPS - N_EXPERTS,), F32)])[None, :]
    z, h2, logits = _out_projection(
        xp, xs, ylp.reshape(tp, -1), yrp.reshape(tp, -1), yls.reshape(ts, -1), yrs.reshape(ts, -1),
        w_out[0].astype(BF16), gate1, ln1_g, ln1_b, shift2, scale2, w_router, b_router, dec_seq)

    expert, weight = _route(logits)
    n_assign = 2 * (tp + ts)
    n_sub = -(-(n_assign + N_EXPERTS * (SUB_ROWS - 1)) // SUB_ROWS)
    n_sub = -(-n_sub * SUB_ROWS // DISPATCH_ROWS) * DISPATCH_ROWS // SUB_ROWS
    n_super = N_EXPERTS + -(-n_assign // (SUB_ROWS * SUPER))
    assert SUPER * n_super >= n_sub + (SUPER - 1) * N_EXPERTS
    dest, row_tok, sup_exp, n_comp, n_zero, out_sub, x_sub = _dispatch_plan(expert, n_sub, n_super)
    xs_rows = _dispatch(row_tok, h2)
    y_rows = _experts(xs_rows, sup_exp, n_comp, n_zero, out_sub, x_sub, w_gate[0], w_up[0], w_down[0])
    y_p, y_s = _combine(dest, y_rows, z, weight, gate2, ln1_g, ln1_b, ln2_g, ln2_b, tp, dec_seq)

    new_state_lru = jnp.swapaxes(st_lru, 0, 1)[:, None]
    return (y_p.reshape(nb, seq, d), y_s.reshape(nbs, dec_seq, d), new_state_lru, st_ret)
```

```python
import functools

import jax
import jax.numpy as jnp
import numpy as np
from jax import lax
from jax.experimental import pallas as pl
from jax.experimental.pallas import tpu as pltpu

F32 = jnp.float32
BF16 = jnp.bfloat16

D_MODEL = 4096
LRU_WIDTH = 2048
LRU_BLOCKS = 16
LANES = 128
RET_HEADS = 16
RET_DH = 128
IN_COLS = 12288
GRID_W = 64
ROPE_BASE = 10000.0
LRU_C = 8.0
N_GROUPS = 4
EXPERTS_PER_GROUP = 8
N_EXPERTS = 32
D_EXPERT = 1024
LN_EPS = 1e-6
GN_EPS = 1e-5
ALPHA = 2.0 ** 0.25

VMEM_LIMIT = 56 * 1024 * 1024
SCAN_PAD = 8

TM_LN = 512
TM_PROJ = 1024
TN_IN = 1024
TN_OUT = 512
TM_OUT = 512
SUB_ROWS = 256
SUPER = 4
F_CHUNK = 256
TN_DOWN = 1024
DISPATCH_ROWS = 1024
TM_FIN = 256
HALF_D = D_MODEL // 2


def _params(sem, vmem_limit=VMEM_LIMIT):
    return pltpu.CompilerParams(dimension_semantics=sem, vmem_limit_bytes=vmem_limit)


def _pack_bf16_pair(lo, hi):
    lo_bits = pltpu.bitcast(lo.astype(BF16).astype(F32), jnp.uint32) >> 16
    hi_bits = pltpu.bitcast(hi.astype(BF16).astype(F32), jnp.uint32) & jnp.uint32(0xFFFF0000)
    return lo_bits | hi_bits


def _unpack_bf16_pair(words):
    lo = pltpu.bitcast(words << 16, F32).astype(BF16)
    hi = pltpu.bitcast(words & jnp.uint32(0xFFFF0000), F32).astype(BF16)
    return jnp.concatenate([lo, hi], axis=1)


def _sigmoid(x):
    return 0.5 * jnp.tanh(0.5 * x) + 0.5


def _softplus(x):
    return jnp.maximum(x, 0.0) + jnp.log1p(jnp.exp(-jnp.abs(x)))


def _ln(x):
    mu = jnp.mean(x, -1, keepdims=True)
    xc = x - mu
    var = jnp.mean(xc * xc, -1, keepdims=True)
    return xc * lax.rsqrt(var + LN_EPS)


def _mod_kernel(cond_ref, w_ref, b_ref, o_ref):
    c = cond_ref[...]
    s = (c * _sigmoid(c)).astype(BF16)
    o_ref[...] = jnp.dot(s, w_ref[...].astype(BF16), preferred_element_type=F32) + b_ref[...]


def _modulation(cond, w_mod, b_mod):
    tn = 512
    n = w_mod.shape[1]
    return pl.pallas_call(
        _mod_kernel,
        grid=(n // tn,),
        in_specs=[pl.BlockSpec((8, D_MODEL), lambda j: (0, 0)),
                  pl.BlockSpec((D_MODEL, tn), lambda j: (0, j)),
                  pl.BlockSpec((1, tn), lambda j: (0, j))],
        out_specs=pl.BlockSpec((8, tn), lambda j: (0, j)),
        out_shape=jax.ShapeDtypeStruct((8, n), F32),
        compiler_params=_params(("arbitrary",)),
        name="modulation",
    )(cond, w_mod, b_mod)


def _ln_mod_kernel(xp_ref, xs_ref, shift_ref, scale_ref, h_ref, *, n_p_tiles, tiles_per_seq):
    i = pl.program_id(0)

    def fill(x_ref, row):
        h = _ln(x_ref[...]) * (1.0 + scale_ref[pl.ds(row, 1), :]) + shift_ref[pl.ds(row, 1), :]
        h_ref[...] = h.astype(BF16)

    @pl.when(i < n_p_tiles)
    def _():
        fill(xp_ref, 0)

    @pl.when(i >= n_p_tiles)
    def _():
        fill(xs_ref, 1 + (i - n_p_tiles) // tiles_per_seq)


def _ln_modulate(xp, xs, shift, scale, dec_seq):
    tp, ts = xp.shape[0], xs.shape[0]
    tm = TM_LN
    n_p, n_s = tp // tm, ts // tm
    kern = functools.partial(_ln_mod_kernel, n_p_tiles=n_p, tiles_per_seq=dec_seq // tm)
    return pl.pallas_call(
        kern,
        grid=(n_p + n_s,),
        in_specs=[pl.BlockSpec((tm, D_MODEL), lambda i: (jnp.minimum(i, n_p - 1), 0)),
                  pl.BlockSpec((tm, D_MODEL), lambda i: (jnp.maximum(i - n_p, 0), 0)),
                  pl.BlockSpec((8, D_MODEL), lambda i: (0, 0)),
                  pl.BlockSpec((8, D_MODEL), lambda i: (0, 0))],
        out_specs=pl.BlockSpec((tm, D_MODEL), lambda i: (i, 0)),
        out_shape=jax.ShapeDtypeStruct((tp + ts, D_MODEL), BF16),
        compiler_params=_params(("arbitrary",)),
        name="ln_modulate",
    )(xp, xs, shift, scale)


def _matmul_kernel(h_ref, w_ref, o_ref):
    o_ref[...] = jnp.dot(h_ref[...], w_ref[...], preferred_element_type=F32)


def _in_projection(h, w_in_bf16):
    t = h.shape[0]
    tm, tn = TM_PROJ, TN_IN
    return pl.pallas_call(
        _matmul_kernel,
        grid=(t // tm, IN_COLS // tn),
        in_specs=[pl.BlockSpec((tm, D_MODEL), lambda i, j: (i, 0)),
                  pl.BlockSpec((D_MODEL, tn), lambda i, j: (0, j))],
        out_specs=pl.BlockSpec((tm, tn), lambda i, j: (i, j)),
        out_shape=jax.ShapeDtypeStruct((t, IN_COLS), F32),
        compiler_params=_params(("arbitrary", "arbitrary")),
        name="in_projection",
    )(h, w_in_bf16)


def _lru_kernel(x_ref, g_ref, cw_ref, cb_ref, wg_ref, pb_ref, h0_ref, y_ref, st_ref,
                xc_scr, af_scr, uf_scr, ab_scr, ub_scr, *, bg, seq, pitch, ncb):
    rows = lax.broadcasted_iota(jnp.int32, (seq, LANES), 0)
    for c in range(ncb):
        lanes = slice(c * LANES, (c + 1) * LANES)
        w = cw_ref[:, lanes]
        bias = cb_ref[:, lanes]
        for b in range(bg):
            x = x_ref[b, :, lanes]
            xm2 = jnp.where(rows >= 2, pltpu.roll(x, 2, 0), 0.0)
            xm1 = jnp.where(rows >= 1, pltpu.roll(x, 1, 0), 0.0)
            xp1 = jnp.where(rows < seq - 1, pltpu.roll(x, seq - 1, 0), 0.0)
            xc_scr[pl.ds(b * seq, seq), :] = bias + xm2 * w[0:1] + xm1 * w[1:2] + x * w[2:3] + xp1 * w[3:4]

        xc = xc_scr[...]
        gates = jnp.dot(xc.astype(BF16), wg_ref[c], preferred_element_type=F32)
        pb = pb_ref[:, lanes]
        half_xc = 0.5 * xc
        for d, (a_scr, u_scr) in enumerate(((af_scr, uf_scr), (ab_scr, ub_scr))):
            t_r = jnp.tanh(gates[:, (2 * d) * LANES:(2 * d + 1) * LANES] + pb[3 * d:3 * d + 1])
            t_i = jnp.tanh(gates[:, (2 * d + 1) * LANES:(2 * d + 2) * LANES] + pb[3 * d + 1:3 * d + 2])
            c4 = (0.5 * LRU_C) * _softplus(-pb[3 * d + 2:3 * d + 3])
            neg_log_a = c4 * t_r + c4
            a = jnp.exp(-neg_log_a)
            y = jnp.tanh(neg_log_a) * (1.0 + a * a)
            root = jnp.where(y > 0.0, y * lax.rsqrt(y), 0.0)
            u = root * (half_xc * (t_i + 1.0))
            for b in range(bg):
                a_scr[c, pl.ds(b * pitch, seq), :] = a[b * seq:(b + 1) * seq]
                u_scr[c, pl.ds(b * pitch, seq), :] = u[b * seq:(b + 1) * seq]

    def step(t, carry):
        tb = seq - 1 - t
        out = []
        for c in range(ncb):
            hf, hb = carry[2 * c], carry[2 * c + 1]
            hf = af_scr[c, pl.ds(t, bg, stride=pitch), :] * hf + uf_scr[c, pl.ds(t, bg, stride=pitch), :]
            uf_scr[c, pl.ds(t, bg, stride=pitch), :] = hf
            hb = ab_scr[c, pl.ds(tb, bg, stride=pitch), :] * hb + ub_scr[c, pl.ds(tb, bg, stride=pitch), :]
            ub_scr[c, pl.ds(tb, bg, stride=pitch), :] = hb
            out += [hf, hb]
        return tuple(out)

    init = []
    for c in range(ncb):
        init += [h0_ref[0, :, c * LANES:(c + 1) * LANES], h0_ref[1, :, c * LANES:(c + 1) * LANES]]
    last = lax.fori_loop(0, seq, step, tuple(init), unroll=8)
    for c in range(ncb):
        lanes = slice(c * LANES, (c + 1) * LANES)
        st_ref[0, :, lanes] = last[2 * c]
        st_ref[1, :, lanes] = last[2 * c + 1]
        for b in range(bg):
            hs = uf_scr[c, pl.ds(b * pitch, seq), :] + ub_scr[c, pl.ds(b * pitch, seq), :]
            y_ref[b, :, lanes] = (jax.nn.gelu(g_ref[b, :, lanes]) * hs).astype(BF16)


def _lru_mixer(proj3, seq0, nseq, bg, ncb, conv_w, conv_b, w_gates, pb, h0):
    seq = proj3.shape[1]
    pitch = seq + SCAN_PAD
    off = seq0 // bg
    cw = ncb * LANES
    ncol = LRU_WIDTH // cw
    kern = functools.partial(_lru_kernel, bg=bg, seq=seq, pitch=pitch, ncb=ncb)
    return pl.pallas_call(
        kern,
        grid=(nseq // bg, ncol),
        in_specs=[pl.BlockSpec((bg, seq, cw), lambda b, c: (b + off, 0, c)),
                  pl.BlockSpec((bg, seq, cw), lambda b, c: (b + off, 0, ncol + c)),
                  pl.BlockSpec((4, cw), lambda b, c: (0, c)),
                  pl.BlockSpec((1, cw), lambda b, c: (0, c)),
                  pl.BlockSpec((ncb, LANES, 4 * LANES), lambda b, c: (c, 0, 0)),
                  pl.BlockSpec((6, cw), lambda b, c: (0, c)),
                  pl.BlockSpec((2, bg, cw), lambda b, c: (0, b, c))],
        out_specs=[pl.BlockSpec((bg, seq, cw), lambda b, c: (b, 0, c)),
                   pl.BlockSpec((2, bg, cw), lambda b, c: (0, b, c))],
        out_shape=[jax.ShapeDtypeStruct((nseq, seq, LRU_WIDTH), BF16),
                   jax.ShapeDtypeStruct((2, nseq, LRU_WIDTH), F32)],
        scratch_shapes=[pltpu.VMEM((bg * seq, LANES), F32)] + [pltpu.VMEM((ncb, bg * pitch, LANES), F32)] * 4,
        compiler_params=_params(("arbitrary", "arbitrary")),
        name="lru_mixer",
    )(proj3, proj3, conv_w, conv_b, w_gates, pb, h0)


def _rope(x, cos, sin_signed, first_half):
    partner = jnp.where(first_half, pltpu.roll(x, LANES - 32, 1), pltpu.roll(x, 32, 1))
    return x * cos + partner * sin_signed


def _ret_kernel(*refs, bg, seq, rope, has_state, emit_state, qb):
    refs = list(refs)
    q_ref, k_ref, v_ref, g_ref, dec_ref = refs[:5]
    pos = 5
    if rope:
        cos_ref, sin_ref = refs[pos:pos + 2]
        pos += 2
    if has_state:
        s0_ref = refs[pos]
        pos += 1
    y_ref = refs[pos]
    pos += 1
    if emit_state:
        st_ref = refs[pos]
        pos += 1
    mask_scr = refs[pos]

    log_g = -_softplus(-dec_ref[0])
    lgf, lgb = log_g[0:1], log_g[1:2]
    reps = seq // LANES
    lgf_row = jnp.concatenate([lgf] * reps, axis=1)
    lgb_row = jnp.concatenate([lgb] * reps, axis=1)

    @pl.when(pl.program_id(1) == 0)
    def _():
        for blk in range(seq // qb):
            ti = lax.broadcasted_iota(jnp.int32, (qb, seq), 0) + blk * qb
            si = lax.broadcasted_iota(jnp.int32, (qb, seq), 1)
            dist = (ti - si).astype(F32)
            e = jnp.where(dist >= 0, dist * lgf_row, (-dist) * lgb_row)
            mask_scr[pl.ds(blk * qb, qb), :] = jnp.where(dist == 0, 2.0, jnp.exp(e))

    lane = lax.broadcasted_iota(jnp.int32, (seq, LANES), 1)
    first_half = (lane % 64) < 32
    trow = lax.broadcasted_iota(jnp.int32, (seq, LANES), 0).astype(F32)
    for b in range(bg):
        q = q_ref[b]
        k = k_ref[b] * (RET_DH ** -0.5)
        v16 = v_ref[b].astype(BF16)
        if rope:
            q = _rope(q, cos_ref[...], sin_ref[...], first_half)
            k = _rope(k, cos_ref[...], sin_ref[...], first_half)
        q16 = q.astype(BF16)
        k16 = k.astype(BF16)
        if has_state:
            qf16 = (q * jnp.exp((trow + 1.0) * lgf)).astype(BF16)
            qb16 = (q * jnp.exp((float(seq) - trow) * lgb)).astype(BF16)
            s0f = s0_ref[b, 0].astype(BF16)
            s0b = s0_ref[b, 1].astype(BF16)
        for blk in range(seq // qb):
            sl = slice(blk * qb, (blk + 1) * qb)
            s = lax.dot_general(q16[sl], k16, (((1,), (1,)), ((), ())), preferred_element_type=F32)
            p = (s * mask_scr[pl.ds(blk * qb, qb), :]).astype(BF16)
            o = jnp.dot(p, v16, preferred_element_type=F32)
            if has_state:
                o = o + jnp.dot(qf16[sl], s0f, preferred_element_type=F32)
                o = o + jnp.dot(qb16[sl], s0b, preferred_element_type=F32)
            mu = jnp.mean(o, -1, keepdims=True)
            oc = o - mu
            var = jnp.mean(oc * oc, -1, keepdims=True)
            on = oc * lax.rsqrt(var + GN_EPS)
            gt = g_ref[b, pl.ds(blk * qb, qb), :]
            y_ref[b, pl.ds(blk * qb, qb), :] = (gt * _sigmoid(gt) * on).astype(BF16)
        if emit_state:
            kf16 = (k * jnp.exp((float(seq - 1) - trow) * lgf)).astype(BF16)
            kb16 = (k * jnp.exp(trow * lgb)).astype(BF16)
            sf = lax.dot_general(kf16, v16, (((0,), (0,)), ((), ())), preferred_element_type=F32)
            sb = lax.dot_general(kb16, v16, (((0,), (0,)), ((), ())), preferred_element_type=F32)
            if has_state:
                sf = sf + jnp.exp(float(seq) * lgf) * s0_ref[b, 0]
                sb = sb + jnp.exp(float(seq) * lgb) * s0_ref[b, 1]
            st_ref[b, 0] = sf
            st_ref[b, 1] = sb


def _ret_mixer(proj3, seq0, nseq, bg, decay, rope_tabs=None, s0=None, emit_state=False):
    seq = proj3.shape[1]
    off = seq0 // bg
    qb = min(seq, 256)
    rope = rope_tabs is not None
    has_state = s0 is not None
    kern = functools.partial(_ret_kernel, bg=bg, seq=seq, rope=rope, has_state=has_state,
                             emit_state=emit_state, qb=qb)

    def col(base):
        return pl.BlockSpec((bg, seq, LANES), lambda h, b: (b + off, 0, base + h))

    st_spec = pl.BlockSpec((bg, None, 2, None, RET_DH, RET_DH), lambda h, b: (b, 0, 0, h, 0, 0))
    in_specs = [col(2 * LRU_BLOCKS), col(2 * LRU_BLOCKS + RET_HEADS), col(2 * LRU_BLOCKS + 2 * RET_HEADS),
                col(2 * LRU_BLOCKS + 3 * RET_HEADS), pl.BlockSpec((1, 2, LANES), lambda h, b: (h, 0, 0))]
    args = [proj3, proj3, proj3, proj3, decay]
    if rope:
        in_specs += [pl.BlockSpec((seq, LANES), lambda h, b: (0, 0))] * 2
        args += list(rope_tabs)
    if has_state:
        in_specs.append(st_spec)
        args.append(s0)
    out_specs = [pl.BlockSpec((bg, seq, LANES), lambda h, b: (b, 0, h))]
    out_shape = [jax.ShapeDtypeStruct((nseq, seq, RET_HEADS * RET_DH), BF16)]
    if emit_state:
        out_specs.append(st_spec)
        out_shape.append(jax.ShapeDtypeStruct((nseq, 1, 2, RET_HEADS, RET_DH, RET_DH), F32))
    return pl.pallas_call(
        kern,
        grid=(RET_HEADS, nseq // bg),
        in_specs=in_specs,
        out_specs=out_specs,
        out_shape=out_shape,
        scratch_shapes=[pltpu.VMEM((seq, seq), F32)],
        compiler_params=_params(("arbitrary", "arbitrary")),
        name="ret_mixer",
    )(*args)


def _rope_tables(seq):
    nf = RET_DH // 4
    freqs = (np.float32(ROPE_BASE) ** (-np.arange(nf, dtype=np.float32) / np.float32(nf))).astype(np.float32)
    t = np.arange(seq)
    row = (t // GRID_W).astype(np.float32)[:, None] * freqs[None, :]
    colp = (t % GRID_W).astype(np.float32)[:, None] * freqs[None, :]
    cos = np.concatenate([np.cos(row), np.cos(row), np.cos(colp), np.cos(colp)], -1)
    sin = np.concatenate([-np.sin(row), np.sin(row), -np.sin(colp), np.sin(colp)], -1)
    return jnp.asarray(cos, F32), jnp.asarray(sin, F32)


def _outproj_kernel(xp_ref, xs_ref, ylp_ref, yrp_ref, yls_ref, yrs_ref, wa_ref, wb_ref, gate_ref,
                    lng_ref, lnb_ref, shift_ref, scale_ref, wr_ref, br_ref,
                    z_ref, h2_ref, lg_ref, z_scr, *, n_p_tiles, tiles_per_seq, nj, tn):
    i = pl.program_id(0)
    j = pl.program_id(1)
    is_p = i < n_p_tiles
    row = jnp.where(is_p, 0, 1 + (i - n_p_tiles) // tiles_per_seq)

    def mix(x_ref, yl_ref, yr_ref):
        m = jnp.dot(yl_ref[...], wa_ref[...], preferred_element_type=F32)
        m = m + jnp.dot(yr_ref[...], wb_ref[...], preferred_element_type=F32)
        z = ALPHA * x_ref[...] + gate_ref[pl.ds(row, 1), :] * m
        z_ref[...] = z
        z_scr[j] = z

    @pl.when(is_p)
    def _():
        mix(xp_ref, ylp_ref, yrp_ref)

    @pl.when(jnp.logical_not(is_p))
    def _():
        mix(xs_ref, yls_ref, yrs_ref)

    @pl.when(j == nj - 1)
    def _():
        tm = z_scr.shape[1]
        inv_d = 1.0 / D_MODEL
        s1 = jnp.zeros((tm, 1), F32)
        for c in range(nj):
            s1 = s1 + jnp.sum(z_scr[c], -1, keepdims=True)
        mu = s1 * inv_d
        s2 = jnp.zeros((tm, 1), F32)
        for c in range(nj):
            zc = z_scr[c] - mu
            s2 = s2 + jnp.sum(zc * zc, -1, keepdims=True)
        rstd = lax.rsqrt(s2 * inv_d + LN_EPS)
        t1 = jnp.zeros((tm, 1), F32)
        for c in range(nj):
            cs = slice(c * tn, (c + 1) * tn)
            x1 = (z_scr[c] - mu) * rstd * lng_ref[:, cs] + lnb_ref[:, cs]
            z_scr[c] = x1
            t1 = t1 + jnp.sum(x1, -1, keepdims=True)
        mu2 = t1 * inv_d
        t2 = jnp.zeros((tm, 1), F32)
        for c in range(nj):
            xc = z_scr[c] - mu2
            t2 = t2 + jnp.sum(xc * xc, -1, keepdims=True)
        rstd2 = lax.rsqrt(t2 * inv_d + LN_EPS)
        logits = jnp.zeros((tm, LANES), F32) + br_ref[...]

        def h2_chunk(c):
            cs = slice(c * tn, (c + 1) * tn)
            return (z_scr[c] - mu2) * rstd2 * (1.0 + scale_ref[pl.ds(row, 1), cs]) + shift_ref[pl.ds(row, 1), cs]

        for c in range(nj // 2):
            lo, hi = h2_chunk(c), h2_chunk(c + nj // 2)
            h2_ref[:, c * tn:(c + 1) * tn] = _pack_bf16_pair(lo, hi)
            logits = logits + jnp.dot(lo.astype(BF16), wr_ref[c * tn:(c + 1) * tn, :], preferred_element_type=F32)
            logits = logits + jnp.dot(hi.astype(BF16), wr_ref[HALF_D + c * tn:HALF_D + (c + 1) * tn, :],
                                      preferred_element_type=F32)
        lg_ref[...] = logits


def _out_projection(xp, xs, ylp, yrp, yls, yrs, w_out_bf16, gate1, ln_g, ln_b, shift2, scale2, w_router, b_router,
                    dec_seq):
    tp, ts = xp.shape[0], xs.shape[0]
    tm, tn = TM_OUT, TN_OUT
    n_p, n_s = tp // tm, ts // tm
    nj = D_MODEL // tn
    half = LRU_WIDTH
    kern = functools.partial(_outproj_kernel, n_p_tiles=n_p, tiles_per_seq=dec_seq // tm, nj=nj, tn=tn)
    p_idx = lambda i, j: (jnp.minimum(i, n_p - 1), 0)
    s_idx = lambda i, j: (jnp.maximum(i - n_p, 0), 0)
    full = lambda i, j: (0, 0)
    return pl.pallas_call(
        kern,
        grid=(n_p + n_s, nj),
        in_specs=[pl.BlockSpec((tm, tn), lambda i, j: (jnp.minimum(i, n_p - 1), j)),
                  pl.BlockSpec((tm, tn), lambda i, j: (jnp.maximum(i - n_p, 0), j)),
                  pl.BlockSpec((tm, half), p_idx), pl.BlockSpec((tm, half), p_idx),
                  pl.BlockSpec((tm, half), s_idx), pl.BlockSpec((tm, half), s_idx),
                  pl.BlockSpec((half, tn), lambda i, j: (0, j)),
                  pl.BlockSpec((half, tn), lambda i, j: (1, j)),
                  pl.BlockSpec((8, tn), lambda i, j: (0, j)),
                  pl.BlockSpec((1, D_MODEL), full), pl.BlockSpec((1, D_MODEL), full),
                  pl.BlockSpec((8, D_MODEL), full), pl.BlockSpec((8, D_MODEL), full),
                  pl.BlockSpec((D_MODEL, LANES), full), pl.BlockSpec((1, LANES), full)],
        out_specs=[pl.BlockSpec((tm, tn), lambda i, j: (i, j)),
                   pl.BlockSpec((tm, HALF_D), lambda i, j: (i, 0)),
                   pl.BlockSpec((tm, LANES), lambda i, j: (i, 0))],
        out_shape=[jax.ShapeDtypeStruct((tp + ts, D_MODEL), F32),
                   jax.ShapeDtypeStruct((tp + ts, HALF_D), jnp.uint32),
                   jax.ShapeDtypeStruct((tp + ts, LANES), F32)],
        scratch_shapes=[pltpu.VMEM((nj, tm, tn), F32)],
        compiler_params=_params(("arbitrary", "arbitrary")),
        name="out_projection",
    )(xp, xs, ylp, yrp, yls, yrs, w_out_bf16, w_out_bf16, gate1, ln_g, ln_b, shift2, scale2, w_router, b_router)


def _route(logits):
    lg = logits[:, :N_GROUPS]
    le = logits[:, N_GROUPS:N_GROUPS + N_EXPERTS].reshape(-1, N_GROUPS, EXPERTS_PER_GROUP)
    pg = jax.nn.softmax(lg, -1)
    g_sel = jnp.argmax(lg, -1)
    p_sel = jnp.take_along_axis(pg, g_sel[:, None], 1)[:, 0]
    le_sel = jnp.take_along_axis(le, g_sel[:, None, None], 1)[:, 0]
    top_v, top_i = lax.top_k(le_sel, 2)
    weight = p_sel[:, None] * jax.nn.softmax(top_v, -1)
    expert = (g_sel[:, None] * EXPERTS_PER_GROUP + top_i).astype(jnp.int32)
    return expert, weight


def _dispatch_plan(expert, n_sub, n_super):
    i32 = jnp.int32
    flat_e = expert.reshape(-1)
    n_assign = flat_e.shape[0]
    ids = jnp.arange(N_EXPERTS, dtype=i32)
    onehot = (flat_e[:, None] == ids[None, :]).astype(i32)
    csum = jnp.cumsum(onehot, 0)
    counts = csum[-1]
    rank = jnp.sum(onehot * (csum - 1), 1)
    nb = (counts + SUB_ROWS - 1) // SUB_ROWS
    sub_end = jnp.cumsum(nb)
    sub_start = sub_end - nb
    dest = (jnp.sum(onehot * (sub_start * SUB_ROWS)[None, :], 1) + rank).astype(i32)
    tok = jnp.arange(n_assign, dtype=i32) // 2
    n_tok = n_assign // 2
    row_tok = (jnp.arange(n_sub * SUB_ROWS, dtype=i32) % n_tok).at[dest].set(tok)

    nsup = (nb + SUPER - 1) // SUPER
    sup_end = jnp.cumsum(nsup)
    sup_start = sup_end - nsup
    n_used = sup_end[-1]
    s = jnp.arange(n_super, dtype=i32)
    used = s < n_used
    last_exp = jnp.max(jnp.where(counts > 0, ids, 0))
    e_s = jnp.where(used, jnp.minimum(jnp.searchsorted(sup_end, s, side="right").astype(i32), N_EXPERTS - 1), last_exp)
    local = s - sup_start[e_s]
    first_sub = sub_start[e_s] + SUPER * local
    n_comp = jnp.where(used, jnp.clip(nb[e_s] - SUPER * local, 0, SUPER), 0).astype(i32)
    zero_first = sub_end[-1] + SUPER * (s - n_used)
    n_zero = jnp.where(used, 0, jnp.clip(n_sub - zero_first, 0, SUPER)).astype(i32)
    out_sub = jnp.where(used, first_sub, jnp.minimum(zero_first, n_sub - 1)).astype(i32)
    k = jnp.arange(SUPER, dtype=i32)
    x_sub_used = first_sub[:, None] + jnp.minimum(k[None, :], jnp.maximum(n_comp - 1, 0)[:, None])
    x_sub_last = x_sub_used[jnp.maximum(n_used - 1, 0)]
    x_sub = jnp.where(used[:, None], x_sub_used, x_sub_last[None, :]).astype(i32).reshape(-1)
    return dest, row_tok, e_s.astype(i32), n_comp, n_zero, out_sub, x_sub


def _dispatch_kernel(tok_ref, h_hbm, xs_ref, sem):
    base = pl.program_id(0) * DISPATCH_ROWS

    def row_copy(src_row, dst_row):
        return pltpu.make_async_copy(h_hbm.at[pl.ds(src_row, 1)], xs_ref.at[pl.ds(dst_row, 1)], sem)

    def issue(p, c):
        row_copy(tok_ref[base + 2 * p], 2 * p).start(priority=0)
        row_copy(tok_ref[base + 2 * p + 1], 2 * p + 1).start(priority=1)
        return c

    lax.fori_loop(0, DISPATCH_ROWS // 2, issue, 0, unroll=4)

    def drain(r, c):
        row_copy(0, r).wait()
        return c

    lax.fori_loop(0, DISPATCH_ROWS, drain, 0, unroll=8)


def _dispatch(row_tok, h2_packed):
    n_rows = row_tok.shape[0]
    grid_spec = pltpu.PrefetchScalarGridSpec(
        num_scalar_prefetch=1,
        grid=(n_rows // DISPATCH_ROWS,),
        in_specs=[pl.BlockSpec(memory_space=pl.ANY)],
        out_specs=pl.BlockSpec((DISPATCH_ROWS, HALF_D), lambda i, tok: (i, 0)),
        scratch_shapes=[pltpu.SemaphoreType.DMA(())],
    )
    return pl.pallas_call(
        _dispatch_kernel,
        grid_spec=grid_spec,
        out_shape=jax.ShapeDtypeStruct((n_rows, HALF_D), jnp.uint32),
        compiler_params=_params(("arbitrary",)),
        name="dispatch",
    )(row_tok, h2_packed)


def _expert_kernel(exp_ref, nc_ref, nz_ref, osub_ref, xsub_ref, x0_ref, x1_ref, x2_ref, x3_ref,
                   wg_ref, wu_ref, wd_ref, y_hbm, a_scr, ytile, sem, *, nf, nd):
    s = pl.program_id(0)
    t = pl.program_id(1)
    n_comp = nc_ref[s]
    n_out = n_comp + nz_ref[s]
    x_refs = (x0_ref, x1_ref, x2_ref, x3_ref)

    @pl.when(jnp.logical_and(t < nf, n_comp > 0))
    def _():
        wg16 = wg_ref[...].astype(BF16)
        wu16 = wu_ref[...].astype(BF16)

        def up(k):
            x = _unpack_bf16_pair(x_refs[k][...])
            g = jnp.dot(x, wg16, preferred_element_type=F32)
            u = jnp.dot(x, wu16, preferred_element_type=F32)
            a_scr[t, k * SUB_ROWS:(k + 1) * SUB_ROWS, :] = (g * _sigmoid(g) * u).astype(BF16)

        up(0)
        for k in range(1, SUPER):
            pl.when(k < n_comp)(functools.partial(up, k))

    @pl.when(t >= nf)
    def _():
        j = t - nf
        slot = j % 2

        @pl.when(n_comp > 0)
        def _():
            wd16 = wd_ref[...].astype(BF16)

            def down(k):
                rows = slice(k * SUB_ROWS, (k + 1) * SUB_ROWS)
                a = jnp.concatenate([a_scr[f, rows, :] for f in range(nf)], axis=1)
                ytile[slot, rows, :] = jnp.dot(a, wd16, preferred_element_type=F32)

            down(0)
            for k in range(1, SUPER):
                pl.when(k < n_comp)(functools.partial(down, k))

        @pl.when(n_comp == 0)
        def _():
            ytile[slot] = jnp.zeros(ytile.shape[1:], F32)

        def out_copy(sl, k, jj):
            dst_rows = pl.ds(pl.multiple_of((osub_ref[s] + k) * SUB_ROWS, SUB_ROWS), SUB_ROWS)
            dst_cols = pl.ds(pl.multiple_of(jj * TN_DOWN, TN_DOWN), TN_DOWN)
            return pltpu.make_async_copy(ytile.at[sl, pl.ds(k * SUB_ROWS, SUB_ROWS), :],
                                         y_hbm.at[dst_rows, dst_cols], sem)

        for k in range(SUPER):
            @pl.when(jnp.logical_and(j > 0, k < n_out))
            def _():
                out_copy(1 - slot, k, j - 1).wait()
        for k in range(SUPER):
            @pl.when(k < n_out)
            def _():
                out_copy(slot, k, j).start()
        for k in range(SUPER):
            @pl.when(jnp.logical_and(j == nd - 1, k < n_out))
            def _():
                out_copy(slot, k, j).wait()


def _experts(xs, sup_exp, n_comp, n_zero, out_sub, x_sub, w_gate, w_up, w_down):
    n_super = sup_exp.shape[0]
    n_rows = xs.shape[0]
    nf = D_EXPERT // F_CHUNK
    nd = D_MODEL // TN_DOWN

    def up_idx(s, t, e, nc, nz, osub, xsub):
        return (e[s], 0, jnp.where(nc[s] > 0, jnp.minimum(t, nf - 1), nf - 1))

    def down_idx(s, t, e, nc, nz, osub, xsub):
        return (e[s], 0, jnp.where(nc[s] > 0, jnp.maximum(t - nf, 0), nd - 1))

    def x_spec(k):
        return pl.BlockSpec((SUB_ROWS, HALF_D), lambda s, t, e, nc, nz, osub, xsub: (xsub[s * SUPER + k], 0))

    grid_spec = pltpu.PrefetchScalarGridSpec(
        num_scalar_prefetch=5,
        grid=(n_super, nf + nd),
        in_specs=[x_spec(0), x_spec(1), x_spec(2), x_spec(3),
                  pl.BlockSpec((None, D_MODEL, F_CHUNK), up_idx),
                  pl.BlockSpec((None, D_MODEL, F_CHUNK), up_idx),
                  pl.BlockSpec((None, D_EXPERT, TN_DOWN), down_idx)],
        out_specs=pl.BlockSpec(memory_space=pl.ANY),
        scratch_shapes=[pltpu.VMEM((nf, SUPER * SUB_ROWS, F_CHUNK), BF16),
                        pltpu.VMEM((2, SUPER * SUB_ROWS, TN_DOWN), F32),
                        pltpu.SemaphoreType.DMA(())],
    )
    return pl.pallas_call(
        functools.partial(_expert_kernel, nf=nf, nd=nd),
        grid_spec=grid_spec,
        out_shape=jax.ShapeDtypeStruct((n_rows, D_MODEL), F32),
        compiler_params=_params(("arbitrary", "arbitrary"), 60 * 1024 * 1024),
        name="experts",
    )(sup_exp, n_comp, n_zero, out_sub, x_sub, xs, xs, xs, xs, w_gate, w_up, w_down)


def _final_kernel(dest_ref, y_hbm, z_ref, wt_ref, gate_ref, g1_ref, b1_ref, g2_ref, b2_ref,
                  op_ref, os_ref, ybuf, sem, *, n_p_tiles, tiles_per_seq):
    i = pl.program_id(0)
    n_tiles = pl.num_programs(0)
    tm = z_ref.shape[0]
    slot = i % 2

    def row_copy(src_row, sl, k, r):
        return pltpu.make_async_copy(y_hbm.at[pl.ds(src_row, 1)], ybuf.at[sl, k, pl.ds(r, 1)], sem.at[sl])

    def gather(tile, sl):
        base = tile * tm * 2

        def issue(r, c):
            row_copy(dest_ref[base + 2 * r], sl, 0, r).start(priority=0)
            row_copy(dest_ref[base + 2 * r + 1], sl, 1, r).start(priority=1)
            return c

        lax.fori_loop(0, tm, issue, 0, unroll=4)

    @pl.when(i == 0)
    def _():
        gather(0, 0)

    @pl.when(i + 1 < n_tiles)
    def _():
        gather(i + 1, 1 - slot)

    def drain(r, c):
        row_copy(0, slot, 0, r).wait()
        row_copy(0, slot, 1, r).wait()
        return c

    lax.fori_loop(0, tm, drain, 0, unroll=4)

    is_p = i < n_p_tiles
    row = jnp.where(is_p, 0, 1 + (i - n_p_tiles) // tiles_per_seq)
    wt = wt_ref[...]
    f = wt[:, 0:1] * ybuf[slot, 0] + wt[:, 1:2] * ybuf[slot, 1]
    x1 = _ln(z_ref[...]) * g1_ref[...] + b1_ref[...]
    out = _ln(ALPHA * x1 + gate_ref[pl.ds(row, 1), :] * f) * g2_ref[...] + b2_ref[...]

    @pl.when(is_p)
    def _():
        op_ref[...] = out

    @pl.when(jnp.logical_not(is_p))
    def _():
        os_ref[...] = out


def _combine(dest, y_rows, z, weight, gate2, g1, b1, g2, b2, n_prompt, dec_seq):
    t = z.shape[0]
    tm = TM_FIN
    n_p = n_prompt // tm
    n_s = (t - n_prompt) // tm
    full = lambda i, d: (0, 0)
    grid_spec = pltpu.PrefetchScalarGridSpec(
        num_scalar_prefetch=1,
        grid=(n_p + n_s,),
        in_specs=[pl.BlockSpec(memory_space=pl.ANY),
                  pl.BlockSpec((tm, D_MODEL), lambda i, d: (i, 0)),
                  pl.BlockSpec((tm, 2), lambda i, d: (i, 0)),
                  pl.BlockSpec((8, D_MODEL), full),
                  pl.BlockSpec((1, D_MODEL), full), pl.BlockSpec((1, D_MODEL), full),
                  pl.BlockSpec((1, D_MODEL), full), pl.BlockSpec((1, D_MODEL), full)],
        out_specs=[pl.BlockSpec((tm, D_MODEL), lambda i, d: (jnp.minimum(i, n_p - 1), 0)),
                   pl.BlockSpec((tm, D_MODEL), lambda i, d: (jnp.maximum(i - n_p, 0), 0))],
        scratch_shapes=[pltpu.VMEM((2, 2, tm, D_MODEL), F32), pltpu.SemaphoreType.DMA((2,))],
    )
    return pl.pallas_call(
        functools.partial(_final_kernel, n_p_tiles=n_p, tiles_per_seq=dec_seq // tm),
        grid_spec=grid_spec,
        out_shape=[jax.ShapeDtypeStruct((n_prompt, D_MODEL), F32),
                   jax.ShapeDtypeStruct((t - n_prompt, D_MODEL), F32)],
        compiler_params=_params(("arbitrary",)),
        name="combine",
    )(dest, y_rows, z, weight, gate2, g1, b1, g2, b2)


def kernel(x_prompt, x_sample, state_lru, state_ret, c, c_ctx, w_mod, b_mod, w_in, conv_w, conv_b, lru_wa, lru_ba,
           lru_wx, lru_bx, lru_lam, ret_decay, w_out, ln1_g, ln1_b, router_g, router_g_b, router_e, router_e_b,
           w_gate, w_up, w_down, ln2_g, ln2_b):
    assert w_in.shape[0] == 1, "single trunk layer"
    nb, seq, d = x_prompt.shape
    nbs, dec_seq, _ = x_sample.shape
    tp, ts = nb * seq, nbs * dec_seq
    assert tp % dec_seq == 0 and d == D_MODEL

    cond = jnp.zeros((8, d), F32).at[0].set(c_ctx).at[1:1 + nbs].set(c)
    mod = _modulation(cond, w_mod[0], b_mod[0][None, :])
    shift1, scale1, gate1, shift2, scale2, gate2 = [mod[:, k * d:(k + 1) * d] for k in range(6)]

    xp = x_prompt.reshape(tp, d)
    xs = x_sample.reshape(ts, d)
    proj = _in_projection(_ln_modulate(xp, xs, shift1, scale1, dec_seq), w_in[0].astype(BF16))

    w_gates = (0.5 * jnp.concatenate([lru_wa[0, 0], lru_wx[0, 0], lru_wa[0, 1], lru_wx[0, 1]], -1)).astype(BF16)
    pb = jnp.stack([0.5 * lru_ba[0, 0], 0.5 * lru_bx[0, 0], lru_lam[0, 0],
                    0.5 * lru_ba[0, 1], 0.5 * lru_bx[0, 1], lru_lam[0, 1]], 0)
    proj_p = proj.reshape((tp + ts) // seq, seq, IN_COLS)
    proj_s = proj.reshape((tp + ts) // dec_seq, dec_seq, IN_COLS)
    ylp, st_lru = _lru_mixer(proj_p, 0, nb, 16, 1, conv_w[0], conv_b[0][None, :], w_gates, pb,
                             jnp.zeros((2, nb, LRU_WIDTH), F32))
    yls, _ = _lru_mixer(proj_s, tp // dec_seq, nbs, nbs, 4, conv_w[0], conv_b[0][None, :], w_gates, pb,
                        jnp.swapaxes(state_lru[:, 0], 0, 1))

    decay = jnp.broadcast_to(ret_decay[0].T[:, :, None], (RET_HEADS, 2, LANES))
    yrp, st_ret = _ret_mixer(proj_p, 0, nb, 8, decay, emit_state=True)
    (yrs,) = _ret_mixer(proj_s, tp // dec_seq, nbs, 1, decay, rope_tabs=_rope_tables(dec_seq), s0=state_ret)

    w_router = jnp.concatenate(
        [router_g[0], jnp.transpose(router_e[0], (1, 0, 2)).reshape(d, N_EXPERTS),
         jnp.zeros((d, LANES - N_GROUPS - N_EXPERTS), F32)], -1).astype(BF16)
    b_router = jnp.concatenate([router_g_b[0], router_e_b[0].reshape(-1),
                                jnp.zeros((LANES - N_GROUPS - N_EXPERTS,), F32)])[None, :]
    z, h2, logits = _out_projection(
        xp, xs, ylp.reshape(tp, -1), yrp.reshape(tp, -1), yls.reshape(ts, -1), yrs.reshape(ts, -1),
        w_out[0].astype(BF16), gate1, ln1_g, ln1_b, shift2, scale2, w_router, b_router, dec_seq)

    expert, weight = _route(logits)
    n_assign = 2 * (tp + ts)
    n_sub = -(-(n_assign + N_EXPERTS * (SUB_ROWS - 1)) // SUB_ROWS)
    n_sub = -(-n_sub * SUB_ROWS // DISPATCH_ROWS) * DISPATCH_ROWS // SUB_ROWS
    n_super = N_EXPERTS + -(-n_assign // (SUB_ROWS * SUPER))
    assert SUPER * n_super >= n_sub + (SUPER - 1) * N_EXPERTS
    dest, row_tok, sup_exp, n_comp, n_zero, out_sub, x_sub = _dispatch_plan(expert, n_sub, n_super)
    xs_rows = _dispatch(row_tok, h2)
    y_rows = _experts(xs_rows, sup_exp, n_comp, n_zero, out_sub, x_sub, w_gate[0], w_up[0], w_down[0])
    y_p, y_s = _combine(dest, y_rows, z, weight, gate2, ln1_g, ln1_b, ln2_g, ln2_b, tp, dec_seq)

    new_state_lru = jnp.swapaxes(st_lru, 0, 1)[:, None]
    return (y_p.reshape(nb, seq, d), y_s.reshape(nbs, dec_seq, d), new_state_lru, st_ret)
```

```python
import functools

import jax
import jax.numpy as jnp
import numpy as np
from jax import lax
from jax.experimental import pallas as pl
from jax.experimental.pallas import tpu as pltpu

F32 = jnp.float32
BF16 = jnp.bfloat16

D_MODEL = 4096
LRU_WIDTH = 2048
LRU_BLOCKS = 16
LANES = 128
RET_HEADS = 16
RET_DH = 128
IN_COLS = 12288
GRID_W = 64
ROPE_BASE = 10000.0
LRU_C = 8.0
N_GROUPS = 4
EXPERTS_PER_GROUP = 8
N_EXPERTS = 32
D_EXPERT = 1024
LN_EPS = 1e-6
GN_EPS = 1e-5
ALPHA = 2.0 ** 0.25

VMEM_LIMIT = 56 * 1024 * 1024
SCAN_PAD = 8

TM_LN = 512
TM_PROJ = 1024
TN_IN = 1024
TN_OUT = 512
TM_OUT = 512
SUB_ROWS = 256
SUPER = 4
F_CHUNK = 256
TN_DOWN = 1024
DISPATCH_ROWS = 1024
TM_FIN = 256
HALF_D = D_MODEL // 2


def _params(sem, vmem_limit=VMEM_LIMIT):
    return pltpu.CompilerParams(dimension_semantics=sem, vmem_limit_bytes=vmem_limit)


def _pack_bf16_pair(lo, hi):
    lo_bits = pltpu.bitcast(lo.astype(BF16).astype(F32), jnp.uint32) >> 16
    hi_bits = pltpu.bitcast(hi.astype(BF16).astype(F32), jnp.uint32) & jnp.uint32(0xFFFF0000)
    return lo_bits | hi_bits


def _unpack_bf16_pair(words):
    lo = pltpu.bitcast(words << 16, F32).astype(BF16)
    hi = pltpu.bitcast(words & jnp.uint32(0xFFFF0000), F32).astype(BF16)
    return jnp.concatenate([lo, hi], axis=1)


def _sigmoid(x):
    return 0.5 * jnp.tanh(0.5 * x) + 0.5


def _softplus(x):
    return jnp.maximum(x, 0.0) + jnp.log1p(jnp.exp(-jnp.abs(x)))


def _ln(x):
    mu = jnp.mean(x, -1, keepdims=True)
    xc = x - mu
    var = jnp.mean(xc * xc, -1, keepdims=True)
    return xc * lax.rsqrt(var + LN_EPS)


def _mod_kernel(cond_ref, w_ref, b_ref, o_ref):
    c = cond_ref[...]
    s = (c * _sigmoid(c)).astype(BF16)
    o_ref[...] = jnp.dot(s, w_ref[...].astype(BF16), preferred_element_type=F32) + b_ref[...]


def _modulation(cond, w_mod, b_mod):
    tn = 512
    n = w_mod.shape[1]
    return pl.pallas_call(
        _mod_kernel,
        grid=(n // tn,),
        in_specs=[pl.BlockSpec((8, D_MODEL), lambda j: (0, 0)),
                  pl.BlockSpec((D_MODEL, tn), lambda j: (0, j)),
                  pl.BlockSpec((1, tn), lambda j: (0, j))],
        out_specs=pl.BlockSpec((8, tn), lambda j: (0, j)),
        out_shape=jax.ShapeDtypeStruct((8, n), F32),
        compiler_params=_params(("arbitrary",)),
        name="modulation",
    )(cond, w_mod, b_mod)


def _ln_mod_kernel(xp_ref, xs_ref, shift_ref, scale_ref, h_ref, *, n_p_tiles, tiles_per_seq):
    i = pl.program_id(0)

    def fill(x_ref, row):
        h = _ln(x_ref[...]) * (1.0 + scale_ref[pl.ds(row, 1), :]) + shift_ref[pl.ds(row, 1), :]
        h_ref[...] = h.astype(BF16)

    @pl.when(i < n_p_tiles)
    def _():
        fill(xp_ref, 0)

    @pl.when(i >= n_p_tiles)
    def _():
        fill(xs_ref, 1 + (i - n_p_tiles) // tiles_per_seq)


def _ln_modulate(xp, xs, shift, scale, dec_seq):
    tp, ts = xp.shape[0], xs.shape[0]
    tm = TM_LN
    n_p, n_s = tp // tm, ts // tm
    kern = functools.partial(_ln_mod_kernel, n_p_tiles=n_p, tiles_per_seq=dec_seq // tm)
    return pl.pallas_call(
        kern,
        grid=(n_p + n_s,),
        in_specs=[pl.BlockSpec((tm, D_MODEL), lambda i: (jnp.minimum(i, n_p - 1), 0)),
                  pl.BlockSpec((tm, D_MODEL), lambda i: (jnp.maximum(i - n_p, 0), 0)),
                  pl.BlockSpec((8, D_MODEL), lambda i: (0, 0)),
                  pl.BlockSpec((8, D_MODEL), lambda i: (0, 0))],
        out_specs=pl.BlockSpec((tm, D_MODEL), lambda i: (i, 0)),
        out_shape=jax.ShapeDtypeStruct((tp + ts, D_MODEL), BF16),
        compiler_params=_params(("arbitrary",)),
        name="ln_modulate",
    )(xp, xs, shift, scale)


def _matmul_kernel(h_ref, w_ref, o_ref):
    o_ref[...] = jnp.dot(h_ref[...], w_ref[...], preferred_element_type=F32).astype(o_ref.dtype)


def _in_projection(h, w_in_bf16):
    t = h.shape[0]
    tm, tn = TM_PROJ, TN_IN
    return pl.pallas_call(
        _matmul_kernel,
        grid=(t // tm, IN_COLS // tn),
        in_specs=[pl.BlockSpec((tm, D_MODEL), lambda i, j: (i, 0)),
                  pl.BlockSpec((D_MODEL, tn), lambda i, j: (0, j))],
        out_specs=pl.BlockSpec((tm, tn), lambda i, j: (i, j)),
        out_shape=jax.ShapeDtypeStruct((t, IN_COLS), BF16),
        compiler_params=_params(("arbitrary", "arbitrary")),
        name="in_projection",
    )(h, w_in_bf16)


def _lru_kernel(x_ref, g_ref, cw_ref, cb_ref, wg_ref, pb_ref, h0_ref, y_ref, st_ref,
                xc_scr, af_scr, uf_scr, ab_scr, ub_scr, *, bg, seq, pitch, ncb):
    rows = lax.broadcasted_iota(jnp.int32, (seq, LANES), 0)
    for c in range(ncb):
        lanes = slice(c * LANES, (c + 1) * LANES)
        w = cw_ref[:, lanes]
        bias = cb_ref[:, lanes]
        for b in range(bg):
            x = x_ref[b, :, lanes].astype(F32)
            xm2 = jnp.where(rows >= 2, pltpu.roll(x, 2, 0), 0.0)
            xm1 = jnp.where(rows >= 1, pltpu.roll(x, 1, 0), 0.0)
            xp1 = jnp.where(rows < seq - 1, pltpu.roll(x, seq - 1, 0), 0.0)
            xc_scr[pl.ds(b * seq, seq), :] = bias + xm2 * w[0:1] + xm1 * w[1:2] + x * w[2:3] + xp1 * w[3:4]

        xc = xc_scr[...]
        gates = jnp.dot(xc.astype(BF16), wg_ref[c], preferred_element_type=F32)
        pb = pb_ref[:, lanes]
        half_xc = 0.5 * xc
        for d, (a_scr, u_scr) in enumerate(((af_scr, uf_scr), (ab_scr, ub_scr))):
            t_r = jnp.tanh(gates[:, (2 * d) * LANES:(2 * d + 1) * LANES] + pb[3 * d:3 * d + 1])
            t_i = jnp.tanh(gates[:, (2 * d + 1) * LANES:(2 * d + 2) * LANES] + pb[3 * d + 1:3 * d + 2])
            c4 = (0.5 * LRU_C) * _softplus(-pb[3 * d + 2:3 * d + 3])
            neg_log_a = c4 * t_r + c4
            a = jnp.exp(-neg_log_a)
            y = jnp.tanh(neg_log_a) * (1.0 + a * a)
            root = jnp.where(y > 0.0, y * lax.rsqrt(y), 0.0)
            u = root * (half_xc * (t_i + 1.0))
            for b in range(bg):
                a_scr[c, pl.ds(b * pitch, seq), :] = a[b * seq:(b + 1) * seq]
                u_scr[c, pl.ds(b * pitch, seq), :] = u[b * seq:(b + 1) * seq]

    def step(t, carry):
        tb = seq - 1 - t
        out = []
        for c in range(ncb):
            hf, hb = carry[2 * c], carry[2 * c + 1]
            hf = af_scr[c, pl.ds(t, bg, stride=pitch), :] * hf + uf_scr[c, pl.ds(t, bg, stride=pitch), :]
            uf_scr[c, pl.ds(t, bg, stride=pitch), :] = hf
            hb = ab_scr[c, pl.ds(tb, bg, stride=pitch), :] * hb + ub_scr[c, pl.ds(tb, bg, stride=pitch), :]
            ub_scr[c, pl.ds(tb, bg, stride=pitch), :] = hb
            out += [hf, hb]
        return tuple(out)

    init = []
    for c in range(ncb):
        init += [h0_ref[0, :, c * LANES:(c + 1) * LANES], h0_ref[1, :, c * LANES:(c + 1) * LANES]]
    last = lax.fori_loop(0, seq, step, tuple(init), unroll=8)
    for c in range(ncb):
        lanes = slice(c * LANES, (c + 1) * LANES)
        st_ref[0, :, lanes] = last[2 * c]
        st_ref[1, :, lanes] = last[2 * c + 1]
        for b in range(bg):
            hs = uf_scr[c, pl.ds(b * pitch, seq), :] + ub_scr[c, pl.ds(b * pitch, seq), :]
            y_ref[b, :, lanes] = (jax.nn.gelu(g_ref[b, :, lanes].astype(F32)) * hs).astype(BF16)


def _lru_mixer(proj3, seq0, nseq, bg, ncb, conv_w, conv_b, w_gates, pb, h0):
    seq = proj3.shape[1]
    pitch = seq + SCAN_PAD
    off = seq0 // bg
    cw = ncb * LANES
    ncol = LRU_WIDTH // cw
    kern = functools.partial(_lru_kernel, bg=bg, seq=seq, pitch=pitch, ncb=ncb)
    return pl.pallas_call(
        kern,
        grid=(nseq // bg, ncol),
        in_specs=[pl.BlockSpec((bg, seq, cw), lambda b, c: (b + off, 0, c)),
                  pl.BlockSpec((bg, seq, cw), lambda b, c: (b + off, 0, ncol + c)),
                  pl.BlockSpec((4, cw), lambda b, c: (0, c)),
                  pl.BlockSpec((1, cw), lambda b, c: (0, c)),
                  pl.BlockSpec((ncb, LANES, 4 * LANES), lambda b, c: (c, 0, 0)),
                  pl.BlockSpec((6, cw), lambda b, c: (0, c)),
                  pl.BlockSpec((2, bg, cw), lambda b, c: (0, b, c))],
        out_specs=[pl.BlockSpec((bg, seq, cw), lambda b, c: (b, 0, c)),
                   pl.BlockSpec((2, bg, cw), lambda b, c: (0, b, c))],
        out_shape=[jax.ShapeDtypeStruct((nseq, seq, LRU_WIDTH), BF16),
                   jax.ShapeDtypeStruct((2, nseq, LRU_WIDTH), F32)],
        scratch_shapes=[pltpu.VMEM((bg * seq, LANES), F32)] + [pltpu.VMEM((ncb, bg * pitch, LANES), F32)] * 4,
        compiler_params=_params(("arbitrary", "arbitrary")),
        name="lru_mixer",
    )(proj3, proj3, conv_w, conv_b, w_gates, pb, h0)


def _rope(x, cos, sin_signed, first_half):
    partner = jnp.where(first_half, pltpu.roll(x, LANES - 32, 1), pltpu.roll(x, 32, 1))
    return x * cos + partner * sin_signed


def _ret_kernel(*refs, bg, seq, rope, has_state, emit_state, qb):
    refs = list(refs)
    q_ref, k_ref, v_ref, g_ref, dec_ref = refs[:5]
    pos = 5
    if rope:
        cos_ref, sin_ref = refs[pos:pos + 2]
        pos += 2
    if has_state:
        s0_ref = refs[pos]
        pos += 1
    y_ref = refs[pos]
    pos += 1
    if emit_state:
        st_ref = refs[pos]
        pos += 1
    mask_scr = refs[pos]

    log_g = -_softplus(-dec_ref[0])
    lgf, lgb = log_g[0:1], log_g[1:2]
    reps = seq // LANES
    lgf_row = jnp.concatenate([lgf] * reps, axis=1)
    lgb_row = jnp.concatenate([lgb] * reps, axis=1)

    @pl.when(pl.program_id(1) == 0)
    def _():
        for blk in range(seq // qb):
            ti = lax.broadcasted_iota(jnp.int32, (qb, seq), 0) + blk * qb
            si = lax.broadcasted_iota(jnp.int32, (qb, seq), 1)
            dist = (ti - si).astype(F32)
            e = jnp.where(dist >= 0, dist * lgf_row, (-dist) * lgb_row)
            mask_scr[pl.ds(blk * qb, qb), :] = jnp.where(dist == 0, 2.0, jnp.exp(e))

    lane = lax.broadcasted_iota(jnp.int32, (seq, LANES), 1)
    first_half = (lane % 64) < 32
    trow = lax.broadcasted_iota(jnp.int32, (seq, LANES), 0).astype(F32)
    for b in range(bg):
        q = q_ref[b].astype(F32)
        k = k_ref[b].astype(F32) * (RET_DH ** -0.5)
        v16 = v_ref[b]
        if rope:
            q = _rope(q, cos_ref[...], sin_ref[...], first_half)
            k = _rope(k, cos_ref[...], sin_ref[...], first_half)
        q16 = q.astype(BF16)
        k16 = k.astype(BF16)
        if has_state:
            qf16 = (q * jnp.exp((trow + 1.0) * lgf)).astype(BF16)
            qb16 = (q * jnp.exp((float(seq) - trow) * lgb)).astype(BF16)
            s0f = s0_ref[b, 0].astype(BF16)
            s0b = s0_ref[b, 1].astype(BF16)
        for blk in range(seq // qb):
            sl = slice(blk * qb, (blk + 1) * qb)
            s = lax.dot_general(q16[sl], k16, (((1,), (1,)), ((), ())), preferred_element_type=F32)
            p = (s * mask_scr[pl.ds(blk * qb, qb), :]).astype(BF16)
            o = jnp.dot(p, v16, preferred_element_type=F32)
            if has_state:
                o = o + jnp.dot(qf16[sl], s0f, preferred_element_type=F32)
                o = o + jnp.dot(qb16[sl], s0b, preferred_element_type=F32)
            mu = jnp.mean(o, -1, keepdims=True)
            oc = o - mu
            var = jnp.mean(oc * oc, -1, keepdims=True)
            on = oc * lax.rsqrt(var + GN_EPS)
            gt = g_ref[b, pl.ds(blk * qb, qb), :].astype(F32)
            y_ref[b, pl.ds(blk * qb, qb), :] = (gt * _sigmoid(gt) * on).astype(BF16)
        if emit_state:
            kf16 = (k * jnp.exp((float(seq - 1) - trow) * lgf)).astype(BF16)
            kb16 = (k * jnp.exp(trow * lgb)).astype(BF16)
            sf = lax.dot_general(kf16, v16, (((0,), (0,)), ((), ())), preferred_element_type=F32)
            sb = lax.dot_general(kb16, v16, (((0,), (0,)), ((), ())), preferred_element_type=F32)
            if has_state:
                sf = sf + jnp.exp(float(seq) * lgf) * s0_ref[b, 0]
                sb = sb + jnp.exp(float(seq) * lgb) * s0_ref[b, 1]
            st_ref[b, 0] = sf
            st_ref[b, 1] = sb


def _ret_mixer(proj3, seq0, nseq, bg, decay, rope_tabs=None, s0=None, emit_state=False):
    seq = proj3.shape[1]
    off = seq0 // bg
    qb = min(seq, 256)
    rope = rope_tabs is not None
    has_state = s0 is not None
    kern = functools.partial(_ret_kernel, bg=bg, seq=seq, rope=rope, has_state=has_state,
                             emit_state=emit_state, qb=qb)

    def col(base):
        return pl.BlockSpec((bg, seq, LANES), lambda h, b: (b + off, 0, base + h))

    st_spec = pl.BlockSpec((bg, None, 2, None, RET_DH, RET_DH), lambda h, b: (b, 0, 0, h, 0, 0))
    in_specs = [col(2 * LRU_BLOCKS), col(2 * LRU_BLOCKS + RET_HEADS), col(2 * LRU_BLOCKS + 2 * RET_HEADS),
                col(2 * LRU_BLOCKS + 3 * RET_HEADS), pl.BlockSpec((1, 2, LANES), lambda h, b: (h, 0, 0))]
    args = [proj3, proj3, proj3, proj3, decay]
    if rope:
        in_specs += [pl.BlockSpec((seq, LANES), lambda h, b: (0, 0))] * 2
        args += list(rope_tabs)
    if has_state:
        in_specs.append(st_spec)
        args.append(s0)
    out_specs = [pl.BlockSpec((bg, seq, LANES), lambda h, b: (b, 0, h))]
    out_shape = [jax.ShapeDtypeStruct((nseq, seq, RET_HEADS * RET_DH), BF16)]
    if emit_state:
        out_specs.append(st_spec)
        out_shape.append(jax.ShapeDtypeStruct((nseq, 1, 2, RET_HEADS, RET_DH, RET_DH), F32))
    return pl.pallas_call(
        kern,
        grid=(RET_HEADS, nseq // bg),
        in_specs=in_specs,
        out_specs=out_specs,
        out_shape=out_shape,
        scratch_shapes=[pltpu.VMEM((seq, seq), F32)],
        compiler_params=_params(("arbitrary", "arbitrary")),
        name="ret_mixer",
    )(*args)


def _rope_tables(seq):
    nf = RET_DH // 4
    freqs = (np.float32(ROPE_BASE) ** (-np.arange(nf, dtype=np.float32) / np.float32(nf))).astype(np.float32)
    t = np.arange(seq)
    row = (t // GRID_W).astype(np.float32)[:, None] * freqs[None, :]
    colp = (t % GRID_W).astype(np.float32)[:, None] * freqs[None, :]
    cos = np.concatenate([np.cos(row), np.cos(row), np.cos(colp), np.cos(colp)], -1)
    sin = np.concatenate([-np.sin(row), np.sin(row), -np.sin(colp), np.sin(colp)], -1)
    return jnp.asarray(cos, F32), jnp.asarray(sin, F32)


def _outproj_kernel(xp_ref, xs_ref, ylp_ref, yrp_ref, yls_ref, yrs_ref, wa_ref, wb_ref, gate_ref,
                    lng_ref, lnb_ref, shift_ref, scale_ref, wr_ref, br_ref,
                    z_ref, h2_ref, lg_ref, z_scr, *, n_p_tiles, tiles_per_seq, nj, tn):
    i = pl.program_id(0)
    j = pl.program_id(1)
    is_p = i < n_p_tiles
    row = jnp.where(is_p, 0, 1 + (i - n_p_tiles) // tiles_per_seq)

    def mix(x_ref, yl_ref, yr_ref):
        m = jnp.dot(yl_ref[...], wa_ref[...], preferred_element_type=F32)
        m = m + jnp.dot(yr_ref[...], wb_ref[...], preferred_element_type=F32)
        z = ALPHA * x_ref[...] + gate_ref[pl.ds(row, 1), :] * m
        z_ref[...] = z
        z_scr[j] = z

    @pl.when(is_p)
    def _():
        mix(xp_ref, ylp_ref, yrp_ref)

    @pl.when(jnp.logical_not(is_p))
    def _():
        mix(xs_ref, yls_ref, yrs_ref)

    @pl.when(j == nj - 1)
    def _():
        tm = z_scr.shape[1]
        inv_d = 1.0 / D_MODEL
        s1 = jnp.zeros((tm, 1), F32)
        for c in range(nj):
            s1 = s1 + jnp.sum(z_scr[c], -1, keepdims=True)
        mu = s1 * inv_d
        s2 = jnp.zeros((tm, 1), F32)
        for c in range(nj):
            zc = z_scr[c] - mu
            s2 = s2 + jnp.sum(zc * zc, -1, keepdims=True)
        rstd = lax.rsqrt(s2 * inv_d + LN_EPS)
        t1 = jnp.zeros((tm, 1), F32)
        for c in range(nj):
            cs = slice(c * tn, (c + 1) * tn)
            x1 = (z_scr[c] - mu) * rstd * lng_ref[:, cs] + lnb_ref[:, cs]
            z_scr[c] = x1
            t1 = t1 + jnp.sum(x1, -1, keepdims=True)
        mu2 = t1 * inv_d
        t2 = jnp.zeros((tm, 1), F32)
        for c in range(nj):
            xc = z_scr[c] - mu2
            t2 = t2 + jnp.sum(xc * xc, -1, keepdims=True)
        rstd2 = lax.rsqrt(t2 * inv_d + LN_EPS)
        logits = jnp.zeros((tm, LANES), F32) + br_ref[...]

        def h2_chunk(c):
            cs = slice(c * tn, (c + 1) * tn)
            return (z_scr[c] - mu2) * rstd2 * (1.0 + scale_ref[pl.ds(row, 1), cs]) + shift_ref[pl.ds(row, 1), cs]

        for c in range(nj // 2):
            lo, hi = h2_chunk(c), h2_chunk(c + nj // 2)
            h2_ref[:, c * tn:(c + 1) * tn] = _pack_bf16_pair(lo, hi)
            logits = logits + jnp.dot(lo.astype(BF16), wr_ref[c * tn:(c + 1) * tn, :], preferred_element_type=F32)
            logits = logits + jnp.dot(hi.astype(BF16), wr_ref[HALF_D + c * tn:HALF_D + (c + 1) * tn, :],
                                      preferred_element_type=F32)
        lg_ref[...] = logits


def _out_projection(xp, xs, ylp, yrp, yls, yrs, w_out_bf16, gate1, ln_g, ln_b, shift2, scale2, w_router, b_router,
                    dec_seq):
    tp, ts = xp.shape[0], xs.shape[0]
    tm, tn = TM_OUT, TN_OUT
    n_p, n_s = tp // tm, ts // tm
    nj = D_MODEL // tn
    half = LRU_WIDTH
    kern = functools.partial(_outproj_kernel, n_p_tiles=n_p, tiles_per_seq=dec_seq // tm, nj=nj, tn=tn)
    p_idx = lambda i, j: (jnp.minimum(i, n_p - 1), 0)
    s_idx = lambda i, j: (jnp.maximum(i - n_p, 0), 0)
    full = lambda i, j: (0, 0)
    return pl.pallas_call(
        kern,
        grid=(n_p + n_s, nj),
        in_specs=[pl.BlockSpec((tm, tn), lambda i, j: (jnp.minimum(i, n_p - 1), j)),
                  pl.BlockSpec((tm, tn), lambda i, j: (jnp.maximum(i - n_p, 0), j)),
                  pl.BlockSpec((tm, half), p_idx), pl.BlockSpec((tm, half), p_idx),
                  pl.BlockSpec((tm, half), s_idx), pl.BlockSpec((tm, half), s_idx),
                  pl.BlockSpec((half, tn), lambda i, j: (0, j)),
                  pl.BlockSpec((half, tn), lambda i, j: (1, j)),
                  pl.BlockSpec((8, tn), lambda i, j: (0, j)),
                  pl.BlockSpec((1, D_MODEL), full), pl.BlockSpec((1, D_MODEL), full),
                  pl.BlockSpec((8, D_MODEL), full), pl.BlockSpec((8, D_MODEL), full),
                  pl.BlockSpec((D_MODEL, LANES), full), pl.BlockSpec((1, LANES), full)],
        out_specs=[pl.BlockSpec((tm, tn), lambda i, j: (i, j)),
                   pl.BlockSpec((tm, HALF_D), lambda i, j: (i, 0)),
                   pl.BlockSpec((tm, LANES), lambda i, j: (i, 0))],
        out_shape=[jax.ShapeDtypeStruct((tp + ts, D_MODEL), F32),
                   jax.ShapeDtypeStruct((tp + ts, HALF_D), jnp.uint32),
                   jax.ShapeDtypeStruct((tp + ts, LANES), F32)],
        scratch_shapes=[pltpu.VMEM((nj, tm, tn), F32)],
        compiler_params=_params(("arbitrary", "arbitrary")),
        name="out_projection",
    )(xp, xs, ylp, yrp, yls, yrs, w_out_bf16, w_out_bf16, gate1, ln_g, ln_b, shift2, scale2, w_router, b_router)


def _route(logits):
    lg = logits[:, :N_GROUPS]
    le = logits[:, N_GROUPS:N_GROUPS + N_EXPERTS].reshape(-1, N_GROUPS, EXPERTS_PER_GROUP)
    pg = jax.nn.softmax(lg, -1)
    g_sel = jnp.argmax(lg, -1)
    p_sel = jnp.take_along_axis(pg, g_sel[:, None], 1)[:, 0]
    le_sel = jnp.take_along_axis(le, g_sel[:, None, None], 1)[:, 0]
    top_v, top_i = lax.top_k(le_sel, 2)
    weight = p_sel[:, None] * jax.nn.softmax(top_v, -1)
    expert = (g_sel[:, None] * EXPERTS_PER_GROUP + top_i).astype(jnp.int32)
    return expert, weight


def _dispatch_plan(expert, n_sub, n_super):
    i32 = jnp.int32
    flat_e = expert.reshape(-1)
    n_assign = flat_e.shape[0]
    ids = jnp.arange(N_EXPERTS, dtype=i32)
    onehot = (flat_e[:, None] == ids[None, :]).astype(i32)
    csum = jnp.cumsum(onehot, 0)
    counts = csum[-1]
    rank = jnp.sum(onehot * (csum - 1), 1)
    nb = (counts + SUB_ROWS - 1) // SUB_ROWS
    sub_end = jnp.cumsum(nb)
    sub_start = sub_end - nb
    dest = (jnp.sum(onehot * (sub_start * SUB_ROWS)[None, :], 1) + rank).astype(i32)
    tok = jnp.arange(n_assign, dtype=i32) // 2
    n_tok = n_assign // 2
    row_tok = (jnp.arange(n_sub * SUB_ROWS, dtype=i32) % n_tok).at[dest].set(tok)

    nsup = (nb + SUPER - 1) // SUPER
    sup_end = jnp.cumsum(nsup)
    sup_start = sup_end - nsup
    n_used = sup_end[-1]
    s = jnp.arange(n_super, dtype=i32)
    used = s < n_used
    last_exp = jnp.max(jnp.where(counts > 0, ids, 0))
    e_s = jnp.where(used, jnp.minimum(jnp.searchsorted(sup_end, s, side="right").astype(i32), N_EXPERTS - 1), last_exp)
    local = s - sup_start[e_s]
    first_sub = sub_start[e_s] + SUPER * local
    n_comp = jnp.where(used, jnp.clip(nb[e_s] - SUPER * local, 0, SUPER), 0).astype(i32)
    zero_first = sub_end[-1] + SUPER * (s - n_used)
    n_zero = jnp.where(used, 0, jnp.clip(n_sub - zero_first, 0, SUPER)).astype(i32)
    out_sub = jnp.where(used, first_sub, jnp.minimum(zero_first, n_sub - 1)).astype(i32)
    k = jnp.arange(SUPER, dtype=i32)
    x_sub_used = first_sub[:, None] + jnp.minimum(k[None, :], jnp.maximum(n_comp - 1, 0)[:, None])
    x_sub_last = x_sub_used[jnp.maximum(n_used - 1, 0)]
    x_sub = jnp.where(used[:, None], x_sub_used, x_sub_last[None, :]).astype(i32).reshape(-1)
    return dest, row_tok, e_s.astype(i32), n_comp, n_zero, out_sub, x_sub


def _dispatch_kernel(tok_ref, h_hbm, xs_ref, sem):
    base = pl.program_id(0) * DISPATCH_ROWS

    def row_copy(src_row, dst_row):
        return pltpu.make_async_copy(h_hbm.at[pl.ds(src_row, 1)], xs_ref.at[pl.ds(dst_row, 1)], sem)

    def issue(p, c):
        row_copy(tok_ref[base + 2 * p], 2 * p).start(priority=0)
        row_copy(tok_ref[base + 2 * p + 1], 2 * p + 1).start(priority=1)
        return c

    lax.fori_loop(0, DISPATCH_ROWS // 2, issue, 0, unroll=4)

    def drain(r, c):
        row_copy(0, r).wait()
        return c

    lax.fori_loop(0, DISPATCH_ROWS, drain, 0, unroll=8)


def _dispatch(row_tok, h2_packed):
    n_rows = row_tok.shape[0]
    grid_spec = pltpu.PrefetchScalarGridSpec(
        num_scalar_prefetch=1,
        grid=(n_rows // DISPATCH_ROWS,),
        in_specs=[pl.BlockSpec(memory_space=pl.ANY)],
        out_specs=pl.BlockSpec((DISPATCH_ROWS, HALF_D), lambda i, tok: (i, 0)),
        scratch_shapes=[pltpu.SemaphoreType.DMA(())],
    )
    return pl.pallas_call(
        _dispatch_kernel,
        grid_spec=grid_spec,
        out_shape=jax.ShapeDtypeStruct((n_rows, HALF_D), jnp.uint32),
        compiler_params=_params(("arbitrary",)),
        name="dispatch",
    )(row_tok, h2_packed)


def _expert_kernel(exp_ref, nc_ref, nz_ref, osub_ref, xsub_ref, x0_ref, x1_ref, x2_ref, x3_ref,
                   wg_ref, wu_ref, wd_ref, y_hbm, a_scr, ytile, sem, *, nf, nd):
    s = pl.program_id(0)
    t = pl.program_id(1)
    n_comp = nc_ref[s]
    n_out = n_comp + nz_ref[s]
    x_refs = (x0_ref, x1_ref, x2_ref, x3_ref)

    @pl.when(jnp.logical_and(t < nf, n_comp > 0))
    def _():
        wg16 = wg_ref[...].astype(BF16)
        wu16 = wu_ref[...].astype(BF16)

        def up(k):
            x = _unpack_bf16_pair(x_refs[k][...])
            g = jnp.dot(x, wg16, preferred_element_type=F32)
            u = jnp.dot(x, wu16, preferred_element_type=F32)
            a_scr[t, k * SUB_ROWS:(k + 1) * SUB_ROWS, :] = (g * _sigmoid(g) * u).astype(BF16)

        up(0)
        for k in range(1, SUPER):
            pl.when(k < n_comp)(functools.partial(up, k))

    @pl.when(t >= nf)
    def _():
        j = t - nf
        slot = j % 2

        @pl.when(n_comp > 0)
        def _():
            wd16 = wd_ref[...].astype(BF16)

            def down(k):
                rows = slice(k * SUB_ROWS, (k + 1) * SUB_ROWS)
                a = jnp.concatenate([a_scr[f, rows, :] for f in range(nf)], axis=1)
                acc = jnp.dot(a, wd16, preferred_element_type=F32)
                ytile[slot, rows, :] = _pack_bf16_pair(acc[:, :TN_DOWN // 2], acc[:, TN_DOWN // 2:])

            down(0)
            for k in range(1, SUPER):
                pl.when(k < n_comp)(functools.partial(down, k))

        @pl.when(n_comp == 0)
        def _():
            ytile[slot] = jnp.zeros(ytile.shape[1:], jnp.uint32)

        def out_copy(sl, k, jj):
            dst_rows = pl.ds(pl.multiple_of((osub_ref[s] + k) * SUB_ROWS, SUB_ROWS), SUB_ROWS)
            dst_cols = pl.ds(pl.multiple_of(jj * (TN_DOWN // 2), TN_DOWN // 2), TN_DOWN // 2)
            return pltpu.make_async_copy(ytile.at[sl, pl.ds(k * SUB_ROWS, SUB_ROWS), :],
                                         y_hbm.at[dst_rows, dst_cols], sem)

        for k in range(SUPER):
            @pl.when(jnp.logical_and(j > 0, k < n_out))
            def _():
                out_copy(1 - slot, k, j - 1).wait()
        for k in range(SUPER):
            @pl.when(k < n_out)
            def _():
                out_copy(slot, k, j).start()
        for k in range(SUPER):
            @pl.when(jnp.logical_and(j == nd - 1, k < n_out))
            def _():
                out_copy(slot, k, j).wait()


def _experts(xs, sup_exp, n_comp, n_zero, out_sub, x_sub, w_gate, w_up, w_down):
    n_super = sup_exp.shape[0]
    n_rows = xs.shape[0]
    nf = D_EXPERT // F_CHUNK
    nd = D_MODEL // TN_DOWN

    def up_idx(s, t, e, nc, nz, osub, xsub):
        return (e[s], 0, jnp.where(nc[s] > 0, jnp.minimum(t, nf - 1), nf - 1))

    def down_idx(s, t, e, nc, nz, osub, xsub):
        return (e[s], 0, jnp.where(nc[s] > 0, jnp.maximum(t - nf, 0), nd - 1))

    def x_spec(k):
        return pl.BlockSpec((SUB_ROWS, HALF_D), lambda s, t, e, nc, nz, osub, xsub: (xsub[s * SUPER + k], 0))

    grid_spec = pltpu.PrefetchScalarGridSpec(
        num_scalar_prefetch=5,
        grid=(n_super, nf + nd),
        in_specs=[x_spec(0), x_spec(1), x_spec(2), x_spec(3),
                  pl.BlockSpec((None, D_MODEL, F_CHUNK), up_idx),
                  pl.BlockSpec((None, D_MODEL, F_CHUNK), up_idx),
                  pl.BlockSpec((None, D_EXPERT, TN_DOWN), down_idx)],
        out_specs=pl.BlockSpec(memory_space=pl.ANY),
        scratch_shapes=[pltpu.VMEM((nf, SUPER * SUB_ROWS, F_CHUNK), BF16),
                        pltpu.VMEM((2, SUPER * SUB_ROWS, TN_DOWN // 2), jnp.uint32),
                        pltpu.SemaphoreType.DMA(())],
    )
    return pl.pallas_call(
        functools.partial(_expert_kernel, nf=nf, nd=nd),
        grid_spec=grid_spec,
        out_shape=jax.ShapeDtypeStruct((n_rows, HALF_D), jnp.uint32),
        compiler_params=_params(("arbitrary", "arbitrary"), 60 * 1024 * 1024),
        name="experts",
    )(sup_exp, n_comp, n_zero, out_sub, x_sub, xs, xs, xs, xs, w_gate, w_up, w_down)


def _final_kernel(dest_ref, y_hbm, z_ref, wt_ref, gate_ref, g1_ref, b1_ref, g2_ref, b2_ref,
                  op_ref, os_ref, ybuf, sem, *, n_p_tiles, tiles_per_seq):
    i = pl.program_id(0)
    n_tiles = pl.num_programs(0)
    tm = z_ref.shape[0]
    slot = i % 2

    def row_copy(src_row, sl, k, r):
        return pltpu.make_async_copy(y_hbm.at[pl.ds(src_row, 1)], ybuf.at[sl, k, pl.ds(r, 1)], sem.at[sl])

    def gather(tile, sl):
        base = tile * tm * 2

        def issue(r, c):
            row_copy(dest_ref[base + 2 * r], sl, 0, r).start(priority=0)
            row_copy(dest_ref[base + 2 * r + 1], sl, 1, r).start(priority=1)
            return c

        lax.fori_loop(0, tm, issue, 0, unroll=4)

    @pl.when(i == 0)
    def _():
        gather(0, 0)

    @pl.when(i + 1 < n_tiles)
    def _():
        gather(i + 1, 1 - slot)

    def drain(r, c):
        row_copy(0, slot, 0, r).wait()
        row_copy(0, slot, 1, r).wait()
        return c

    lax.fori_loop(0, tm, drain, 0, unroll=4)

    is_p = i < n_p_tiles
    row = jnp.where(is_p, 0, 1 + (i - n_p_tiles) // tiles_per_seq)
    wt = wt_ref[...]
    w0, w1 = wt[:, 0:1], wt[:, 1:2]
    half_tile = TN_DOWN // 2
    pieces = []
    for j in range(D_MODEL // TN_DOWN):
        y0 = ybuf[slot, 0, :, j * half_tile:(j + 1) * half_tile]
        y1 = ybuf[slot, 1, :, j * half_tile:(j + 1) * half_tile]
        pieces.append(w0 * pltpu.bitcast(y0 << 16, F32) + w1 * pltpu.bitcast(y1 << 16, F32))
        pieces.append(w0 * pltpu.bitcast(y0 & jnp.uint32(0xFFFF0000), F32)
                      + w1 * pltpu.bitcast(y1 & jnp.uint32(0xFFFF0000), F32))
    f = jnp.concatenate(pieces, axis=1)
    x1 = _ln(z_ref[...]) * g1_ref[...] + b1_ref[...]
    out = _ln(ALPHA * x1 + gate_ref[pl.ds(row, 1), :] * f) * g2_ref[...] + b2_ref[...]

    @pl.when(is_p)
    def _():
        op_ref[...] = out

    @pl.when(jnp.logical_not(is_p))
    def _():
        os_ref[...] = out


def _combine(dest, y_rows, z, weight, gate2, g1, b1, g2, b2, n_prompt, dec_seq):
    t = z.shape[0]
    tm = TM_FIN
    n_p = n_prompt // tm
    n_s = (t - n_prompt) // tm
    full = lambda i, d: (0, 0)
    grid_spec = pltpu.PrefetchScalarGridSpec(
        num_scalar_prefetch=1,
        grid=(n_p + n_s,),
        in_specs=[pl.BlockSpec(memory_space=pl.ANY),
                  pl.BlockSpec((tm, D_MODEL), lambda i, d: (i, 0)),
                  pl.BlockSpec((tm, 2), lambda i, d: (i, 0)),
                  pl.BlockSpec((8, D_MODEL), full),
                  pl.BlockSpec((1, D_MODEL), full), pl.BlockSpec((1, D_MODEL), full),
                  pl.BlockSpec((1, D_MODEL), full), pl.BlockSpec((1, D_MODEL), full)],
        out_specs=[pl.BlockSpec((tm, D_MODEL), lambda i, d: (jnp.minimum(i, n_p - 1), 0)),
                   pl.BlockSpec((tm, D_MODEL), lambda i, d: (jnp.maximum(i - n_p, 0), 0))],
        scratch_shapes=[pltpu.VMEM((2, 2, tm, HALF_D), jnp.uint32), pltpu.SemaphoreType.DMA((2,))],
    )
    return pl.pallas_call(
        functools.partial(_final_kernel, n_p_tiles=n_p, tiles_per_seq=dec_seq // tm),
        grid_spec=grid_spec,
        out_shape=[jax.ShapeDtypeStruct((n_prompt, D_MODEL), F32),
                   jax.ShapeDtypeStruct((t - n_prompt, D_MODEL), F32)],
        compiler_params=_params(("arbitrary",)),
        name="combine",
    )(dest, y_rows, z, weight, gate2, g1, b1, g2, b2)


def kernel(x_prompt, x_sample, state_lru, state_ret, c, c_ctx, w_mod, b_mod, w_in, conv_w, conv_b, lru_wa, lru_ba,
           lru_wx, lru_bx, lru_lam, ret_decay, w_out, ln1_g, ln1_b, router_g, router_g_b, router_e, router_e_b,
           w_gate, w_up, w_down, ln2_g, ln2_b):
    assert w_in.shape[0] == 1, "single trunk layer"
    nb, seq, d = x_prompt.shape
    nbs, dec_seq, _ = x_sample.shape
    tp, ts = nb * seq, nbs * dec_seq
    assert tp % dec_seq == 0 and d == D_MODEL

    cond = jnp.zeros((8, d), F32).at[0].set(c_ctx).at[1:1 + nbs].set(c)
    mod = _modulation(cond, w_mod[0], b_mod[0][None, :])
    shift1, scale1, gate1, shift2, scale2, gate2 = [mod[:, k * d:(k + 1) * d] for k in range(6)]

    xp = x_prompt.reshape(tp, d)
    xs = x_sample.reshape(ts, d)
    proj = _in_projection(_ln_modulate(xp, xs, shift1, scale1, dec_seq), w_in[0].astype(BF16))

    w_gates = (0.5 * jnp.concatenate([lru_wa[0, 0], lru_wx[0, 0], lru_wa[0, 1], lru_wx[0, 1]], -1)).astype(BF16)
    pb = jnp.stack([0.5 * lru_ba[0, 0], 0.5 * lru_bx[0, 0], lru_lam[0, 0],
                    0.5 * lru_ba[0, 1], 0.5 * lru_bx[0, 1], lru_lam[0, 1]], 0)
    proj_p = proj.reshape((tp + ts) // seq, seq, IN_COLS)
    proj_s = proj.reshape((tp + ts) // dec_seq, dec_seq, IN_COLS)
    ylp, st_lru = _lru_mixer(proj_p, 0, nb, 16, 1, conv_w[0], conv_b[0][None, :], w_gates, pb,
                             jnp.zeros((2, nb, LRU_WIDTH), F32))
    yls, _ = _lru_mixer(proj_s, tp // dec_seq, nbs, nbs, 4, conv_w[0], conv_b[0][None, :], w_gates, pb,
                        jnp.swapaxes(state_lru[:, 0], 0, 1))

    decay = jnp.broadcast_to(ret_decay[0].T[:, :, None], (RET_HEADS, 2, LANES))
    yrp, st_ret = _ret_mixer(proj_p, 0, nb, 8, decay, emit_state=True)
    (yrs,) = _ret_mixer(proj_s, tp // dec_seq, nbs, 1, decay, rope_tabs=_rope_tables(dec_seq), s0=state_ret)

    w_router = jnp.concatenate(
        [router_g[0], jnp.transpose(router_e[0], (1, 0, 2)).reshape(d, N_EXPERTS),
         jnp.zeros((d, LANES - N_GROUPS - N_EXPERTS), F32)], -1).astype(BF16)
    b_router = jnp.concatenate([router_g_b[0], router_e_b[0].reshape(-1),
                                jnp.zeros((LANES - N_GROUPS - N_EXPERTS,), F32)])[None, :]
    z, h2, logits = _out_projection(
        xp, xs, ylp.reshape(tp, -1), yrp.reshape(tp, -1), yls.reshape(ts, -1), yrs.reshape(ts, -1),
        w_out[0].astype(BF16), gate1, ln1_g, ln1_b, shift2, scale2, w_router, b_router, dec_seq)

    expert, weight = _route(logits)
    n_assign = 2 * (tp + ts)
    n_sub = -(-(n_assign + N_EXPERTS * (SUB_ROWS - 1)) // SUB_ROWS)
    n_sub = -(-n_sub * SUB_ROWS // DISPATCH_ROWS) * DISPATCH_ROWS // SUB_ROWS
    n_super = N_EXPERTS + -(-n_assign // (SUB_ROWS * SUPER))
    assert SUPER * n_super >= n_sub + (SUPER - 1) * N_EXPERTS
    dest, row_tok, sup_exp, n_comp, n_zero, out_sub, x_sub = _dispatch_plan(expert, n_sub, n_super)
    xs_rows = _dispatch(row_tok, h2)
    y_rows = _experts(xs_rows, sup_exp, n_comp, n_zero, out_sub, x_sub, w_gate[0], w_up[0], w_down[0])
    y_p, y_s = _combine(dest, y_rows, z, weight, gate2, ln1_g, ln1_b, ln2_g, ln2_b, tp, dec_seq)

    new_state_lru = jnp.swapaxes(st_lru, 0, 1)[:, None]
    return (y_p.reshape(nb, seq, d), y_s.reshape(nbs, dec_seq, d), new_state_lru, st_ret)
```

```python
import functools

import jax
import jax.numpy as jnp
import numpy as np
from jax import lax
from jax.experimental import pallas as pl
from jax.experimental.pallas import tpu as pltpu

F32 = jnp.float32
BF16 = jnp.bfloat16

D_MODEL = 4096
LRU_WIDTH = 2048
LRU_BLOCKS = 16
LANES = 128
RET_HEADS = 16
RET_DH = 128
IN_COLS = 12288
GRID_W = 64
ROPE_BASE = 10000.0
LRU_C = 8.0
N_GROUPS = 4
EXPERTS_PER_GROUP = 8
N_EXPERTS = 32
D_EXPERT = 1024
LN_EPS = 1e-6
GN_EPS = 1e-5
ALPHA = 2.0 ** 0.25

VMEM_LIMIT = 56 * 1024 * 1024
SCAN_PAD = 8

TM_LN = 512
TM_PROJ = 1024
TN_IN = 1024
TN_OUT = 512
TM_OUT = 512
SUB_ROWS = 256
SUPER = 4
F_CHUNK = 256
TN_DOWN = 1024
DISPATCH_TOKENS = 512
TM_FIN = 256
HALF_D = D_MODEL // 2


def _params(sem, vmem_limit=VMEM_LIMIT):
    return pltpu.CompilerParams(dimension_semantics=sem, vmem_limit_bytes=vmem_limit)


def _pack_bf16_pair(lo, hi):
    lo_bits = pltpu.bitcast(lo.astype(BF16).astype(F32), jnp.uint32) >> 16
    hi_bits = pltpu.bitcast(hi.astype(BF16).astype(F32), jnp.uint32) & jnp.uint32(0xFFFF0000)
    return lo_bits | hi_bits


def _unpack_bf16_pair(words):
    lo = pltpu.bitcast(words << 16, F32).astype(BF16)
    hi = pltpu.bitcast(words & jnp.uint32(0xFFFF0000), F32).astype(BF16)
    return jnp.concatenate([lo, hi], axis=1)


def _sigmoid(x):
    return 0.5 * jnp.tanh(0.5 * x) + 0.5


def _softplus(x):
    return jnp.maximum(x, 0.0) + jnp.log1p(jnp.exp(-jnp.abs(x)))


def _ln(x):
    mu = jnp.mean(x, -1, keepdims=True)
    xc = x - mu
    var = jnp.mean(xc * xc, -1, keepdims=True)
    return xc * lax.rsqrt(var + LN_EPS)


def _mod_kernel(cond_ref, w_ref, b_ref, o_ref):
    c = cond_ref[...]
    s = (c * _sigmoid(c)).astype(BF16)
    o_ref[...] = jnp.dot(s, w_ref[...].astype(BF16), preferred_element_type=F32) + b_ref[...]


def _modulation(cond, w_mod, b_mod):
    tn = 512
    n = w_mod.shape[1]
    return pl.pallas_call(
        _mod_kernel,
        grid=(n // tn,),
        in_specs=[pl.BlockSpec((8, D_MODEL), lambda j: (0, 0)),
                  pl.BlockSpec((D_MODEL, tn), lambda j: (0, j)),
                  pl.BlockSpec((1, tn), lambda j: (0, j))],
        out_specs=pl.BlockSpec((8, tn), lambda j: (0, j)),
        out_shape=jax.ShapeDtypeStruct((8, n), F32),
        compiler_params=_params(("arbitrary",)),
        name="modulation",
    )(cond, w_mod, b_mod)


def _ln_mod_kernel(xp_ref, xs_ref, shift_ref, scale_ref, h_ref, *, n_p_tiles, tiles_per_seq):
    i = pl.program_id(0)

    def fill(x_ref, row):
        h = _ln(x_ref[...]) * (1.0 + scale_ref[pl.ds(row, 1), :]) + shift_ref[pl.ds(row, 1), :]
        h_ref[...] = h.astype(BF16)

    @pl.when(i < n_p_tiles)
    def _():
        fill(xp_ref, 0)

    @pl.when(i >= n_p_tiles)
    def _():
        fill(xs_ref, 1 + (i - n_p_tiles) // tiles_per_seq)


def _ln_modulate(xp, xs, shift, scale, dec_seq):
    tp, ts = xp.shape[0], xs.shape[0]
    tm = TM_LN
    n_p, n_s = tp // tm, ts // tm
    kern = functools.partial(_ln_mod_kernel, n_p_tiles=n_p, tiles_per_seq=dec_seq // tm)
    return pl.pallas_call(
        kern,
        grid=(n_p + n_s,),
        in_specs=[pl.BlockSpec((tm, D_MODEL), lambda i: (jnp.minimum(i, n_p - 1), 0)),
                  pl.BlockSpec((tm, D_MODEL), lambda i: (jnp.maximum(i - n_p, 0), 0)),
                  pl.BlockSpec((8, D_MODEL), lambda i: (0, 0)),
                  pl.BlockSpec((8, D_MODEL), lambda i: (0, 0))],
        out_specs=pl.BlockSpec((tm, D_MODEL), lambda i: (i, 0)),
        out_shape=jax.ShapeDtypeStruct((tp + ts, D_MODEL), BF16),
        compiler_params=_params(("arbitrary",)),
        name="ln_modulate",
    )(xp, xs, shift, scale)


def _matmul_kernel(h_ref, w_ref, o_ref):
    o_ref[...] = jnp.dot(h_ref[...], w_ref[...], preferred_element_type=F32).astype(o_ref.dtype)


def _in_projection(h, w_in_bf16):
    t = h.shape[0]
    tm, tn = TM_PROJ, TN_IN
    return pl.pallas_call(
        _matmul_kernel,
        grid=(t // tm, IN_COLS // tn),
        in_specs=[pl.BlockSpec((tm, D_MODEL), lambda i, j: (i, 0)),
                  pl.BlockSpec((D_MODEL, tn), lambda i, j: (0, j))],
        out_specs=pl.BlockSpec((tm, tn), lambda i, j: (i, j)),
        out_shape=jax.ShapeDtypeStruct((t, IN_COLS), BF16),
        compiler_params=_params(("arbitrary", "arbitrary")),
        name="in_projection",
    )(h, w_in_bf16)


def _lru_kernel(x_ref, g_ref, cw_ref, cb_ref, wg_ref, pb_ref, h0_ref, y_ref, st_ref,
                xc_scr, af_scr, uf_scr, ab_scr, ub_scr, *, bg, seq, pitch, ncb):
    rows = lax.broadcasted_iota(jnp.int32, (seq, LANES), 0)
    for c in range(ncb):
        lanes = slice(c * LANES, (c + 1) * LANES)
        w = cw_ref[:, lanes]
        bias = cb_ref[:, lanes]
        for b in range(bg):
            x = x_ref[b, :, lanes].astype(F32)
            xm2 = jnp.where(rows >= 2, pltpu.roll(x, 2, 0), 0.0)
            xm1 = jnp.where(rows >= 1, pltpu.roll(x, 1, 0), 0.0)
            xp1 = jnp.where(rows < seq - 1, pltpu.roll(x, seq - 1, 0), 0.0)
            xc_scr[pl.ds(b * seq, seq), :] = bias + xm2 * w[0:1] + xm1 * w[1:2] + x * w[2:3] + xp1 * w[3:4]

        xc = xc_scr[...]
        gates = jnp.dot(xc.astype(BF16), wg_ref[c], preferred_element_type=F32)
        pb = pb_ref[:, lanes]
        half_xc = 0.5 * xc
        for d, (a_scr, u_scr) in enumerate(((af_scr, uf_scr), (ab_scr, ub_scr))):
            t_r = jnp.tanh(gates[:, (2 * d) * LANES:(2 * d + 1) * LANES] + pb[3 * d:3 * d + 1])
            t_i = jnp.tanh(gates[:, (2 * d + 1) * LANES:(2 * d + 2) * LANES] + pb[3 * d + 1:3 * d + 2])
            c4 = (0.5 * LRU_C) * _softplus(-pb[3 * d + 2:3 * d + 3])
            neg_log_a = c4 * t_r + c4
            a = jnp.exp(-neg_log_a)
            y = jnp.tanh(neg_log_a) * (1.0 + a * a)
            root = jnp.where(y > 0.0, y * lax.rsqrt(y), 0.0)
            u = root * (half_xc * (t_i + 1.0))
            for b in range(bg):
                a_scr[c, pl.ds(b * pitch, seq), :] = a[b * seq:(b + 1) * seq]
                u_scr[c, pl.ds(b * pitch, seq), :] = u[b * seq:(b + 1) * seq]

    def step(t, carry):
        tb = seq - 1 - t
        out = []
        for c in range(ncb):
            hf, hb = carry[2 * c], carry[2 * c + 1]
            hf = af_scr[c, pl.ds(t, bg, stride=pitch), :] * hf + uf_scr[c, pl.ds(t, bg, stride=pitch), :]
            uf_scr[c, pl.ds(t, bg, stride=pitch), :] = hf
            hb = ab_scr[c, pl.ds(tb, bg, stride=pitch), :] * hb + ub_scr[c, pl.ds(tb, bg, stride=pitch), :]
            ub_scr[c, pl.ds(tb, bg, stride=pitch), :] = hb
            out += [hf, hb]
        return tuple(out)

    init = []
    for c in range(ncb):
        init += [h0_ref[0, :, c * LANES:(c + 1) * LANES], h0_ref[1, :, c * LANES:(c + 1) * LANES]]
    last = lax.fori_loop(0, seq, step, tuple(init), unroll=8)
    for c in range(ncb):
        lanes = slice(c * LANES, (c + 1) * LANES)
        st_ref[0, :, lanes] = last[2 * c]
        st_ref[1, :, lanes] = last[2 * c + 1]
        for b in range(bg):
            hs = uf_scr[c, pl.ds(b * pitch, seq), :] + ub_scr[c, pl.ds(b * pitch, seq), :]
            y_ref[b, :, lanes] = (jax.nn.gelu(g_ref[b, :, lanes].astype(F32)) * hs).astype(BF16)


def _lru_mixer(proj3, seq0, nseq, bg, ncb, conv_w, conv_b, w_gates, pb, h0):
    seq = proj3.shape[1]
    pitch = seq + SCAN_PAD
    off = seq0 // bg
    cw = ncb * LANES
    ncol = LRU_WIDTH // cw
    kern = functools.partial(_lru_kernel, bg=bg, seq=seq, pitch=pitch, ncb=ncb)
    return pl.pallas_call(
        kern,
        grid=(nseq // bg, ncol),
        in_specs=[pl.BlockSpec((bg, seq, cw), lambda b, c: (b + off, 0, c)),
                  pl.BlockSpec((bg, seq, cw), lambda b, c: (b + off, 0, ncol + c)),
                  pl.BlockSpec((4, cw), lambda b, c: (0, c)),
                  pl.BlockSpec((1, cw), lambda b, c: (0, c)),
                  pl.BlockSpec((ncb, LANES, 4 * LANES), lambda b, c: (c, 0, 0)),
                  pl.BlockSpec((6, cw), lambda b, c: (0, c)),
                  pl.BlockSpec((2, bg, cw), lambda b, c: (0, b, c))],
        out_specs=[pl.BlockSpec((bg, seq, cw), lambda b, c: (b, 0, c)),
                   pl.BlockSpec((2, bg, cw), lambda b, c: (0, b, c))],
        out_shape=[jax.ShapeDtypeStruct((nseq, seq, LRU_WIDTH), BF16),
                   jax.ShapeDtypeStruct((2, nseq, LRU_WIDTH), F32)],
        scratch_shapes=[pltpu.VMEM((bg * seq, LANES), F32)] + [pltpu.VMEM((ncb, bg * pitch, LANES), F32)] * 4,
        compiler_params=_params(("arbitrary", "arbitrary")),
        name="lru_mixer",
    )(proj3, proj3, conv_w, conv_b, w_gates, pb, h0)


def _rope(x, cos, sin_signed, first_half):
    partner = jnp.where(first_half, pltpu.roll(x, LANES - 32, 1), pltpu.roll(x, 32, 1))
    return x * cos + partner * sin_signed


def _ret_kernel(*refs, bg, seq, rope, has_state, emit_state, qb):
    refs = list(refs)
    q_ref, k_ref, v_ref, g_ref, dec_ref = refs[:5]
    pos = 5
    if rope:
        cos_ref, sin_ref = refs[pos:pos + 2]
        pos += 2
    if has_state:
        s0_ref = refs[pos]
        pos += 1
    y_ref = refs[pos]
    pos += 1
    if emit_state:
        st_ref = refs[pos]
        pos += 1
    mask_scr = refs[pos]

    log_g = -_softplus(-dec_ref[0])
    lgf, lgb = log_g[0:1], log_g[1:2]
    reps = seq // LANES
    lgf_row = jnp.concatenate([lgf] * reps, axis=1)
    lgb_row = jnp.concatenate([lgb] * reps, axis=1)

    @pl.when(pl.program_id(1) == 0)
    def _():
        for blk in range(seq // qb):
            ti = lax.broadcasted_iota(jnp.int32, (qb, seq), 0) + blk * qb
            si = lax.broadcasted_iota(jnp.int32, (qb, seq), 1)
            dist = (ti - si).astype(F32)
            e = jnp.where(dist >= 0, dist * lgf_row, (-dist) * lgb_row)
            mask_scr[pl.ds(blk * qb, qb), :] = jnp.where(dist == 0, 2.0, jnp.exp(e))

    lane = lax.broadcasted_iota(jnp.int32, (seq, LANES), 1)
    first_half = (lane % 64) < 32
    trow = lax.broadcasted_iota(jnp.int32, (seq, LANES), 0).astype(F32)
    for b in range(bg):
        q = q_ref[b].astype(F32)
        k = k_ref[b].astype(F32) * (RET_DH ** -0.5)
        v16 = v_ref[b]
        if rope:
            q = _rope(q, cos_ref[...], sin_ref[...], first_half)
            k = _rope(k, cos_ref[...], sin_ref[...], first_half)
        q16 = q.astype(BF16)
        k16 = k.astype(BF16)
        if has_state:
            qf16 = (q * jnp.exp((trow + 1.0) * lgf)).astype(BF16)
            qb16 = (q * jnp.exp((float(seq) - trow) * lgb)).astype(BF16)
            s0f = s0_ref[b, 0].astype(BF16)
            s0b = s0_ref[b, 1].astype(BF16)
        for blk in range(seq // qb):
            sl = slice(blk * qb, (blk + 1) * qb)
            s = lax.dot_general(q16[sl], k16, (((1,), (1,)), ((), ())), preferred_element_type=F32)
            p = (s * mask_scr[pl.ds(blk * qb, qb), :]).astype(BF16)
            o = jnp.dot(p, v16, preferred_element_type=F32)
            if has_state:
                o = o + jnp.dot(qf16[sl], s0f, preferred_element_type=F32)
                o = o + jnp.dot(qb16[sl], s0b, preferred_element_type=F32)
            mu = jnp.mean(o, -1, keepdims=True)
            oc = o - mu
            var = jnp.mean(oc * oc, -1, keepdims=True)
            on = oc * lax.rsqrt(var + GN_EPS)
            gt = g_ref[b, pl.ds(blk * qb, qb), :].astype(F32)
            y_ref[b, pl.ds(blk * qb, qb), :] = (gt * _sigmoid(gt) * on).astype(BF16)
        if emit_state:
            kf16 = (k * jnp.exp((float(seq - 1) - trow) * lgf)).astype(BF16)
            kb16 = (k * jnp.exp(trow * lgb)).astype(BF16)
            sf = lax.dot_general(kf16, v16, (((0,), (0,)), ((), ())), preferred_element_type=F32)
            sb = lax.dot_general(kb16, v16, (((0,), (0,)), ((), ())), preferred_element_type=F32)
            if has_state:
                sf = sf + jnp.exp(float(seq) * lgf) * s0_ref[b, 0]
                sb = sb + jnp.exp(float(seq) * lgb) * s0_ref[b, 1]
            st_ref[b, 0] = sf
            st_ref[b, 1] = sb


def _ret_mixer(proj3, seq0, nseq, bg, decay, rope_tabs=None, s0=None, emit_state=False):
    seq = proj3.shape[1]
    off = seq0 // bg
    qb = min(seq, 256)
    rope = rope_tabs is not None
    has_state = s0 is not None
    kern = functools.partial(_ret_kernel, bg=bg, seq=seq, rope=rope, has_state=has_state,
                             emit_state=emit_state, qb=qb)

    def col(base):
        return pl.BlockSpec((bg, seq, LANES), lambda h, b: (b + off, 0, base + h))

    st_spec = pl.BlockSpec((bg, None, 2, None, RET_DH, RET_DH), lambda h, b: (b, 0, 0, h, 0, 0))
    in_specs = [col(2 * LRU_BLOCKS), col(2 * LRU_BLOCKS + RET_HEADS), col(2 * LRU_BLOCKS + 2 * RET_HEADS),
                col(2 * LRU_BLOCKS + 3 * RET_HEADS), pl.BlockSpec((1, 2, LANES), lambda h, b: (h, 0, 0))]
    args = [proj3, proj3, proj3, proj3, decay]
    if rope:
        in_specs += [pl.BlockSpec((seq, LANES), lambda h, b: (0, 0))] * 2
        args += list(rope_tabs)
    if has_state:
        in_specs.append(st_spec)
        args.append(s0)
    out_specs = [pl.BlockSpec((bg, seq, LANES), lambda h, b: (b, 0, h))]
    out_shape = [jax.ShapeDtypeStruct((nseq, seq, RET_HEADS * RET_DH), BF16)]
    if emit_state:
        out_specs.append(st_spec)
        out_shape.append(jax.ShapeDtypeStruct((nseq, 1, 2, RET_HEADS, RET_DH, RET_DH), F32))
    return pl.pallas_call(
        kern,
        grid=(RET_HEADS, nseq // bg),
        in_specs=in_specs,
        out_specs=out_specs,
        out_shape=out_shape,
        scratch_shapes=[pltpu.VMEM((seq, seq), F32)],
        compiler_params=_params(("arbitrary", "arbitrary")),
        name="ret_mixer",
    )(*args)


def _rope_tables(seq):
    nf = RET_DH // 4
    freqs = (np.float32(ROPE_BASE) ** (-np.arange(nf, dtype=np.float32) / np.float32(nf))).astype(np.float32)
    t = np.arange(seq)
    row = (t // GRID_W).astype(np.float32)[:, None] * freqs[None, :]
    colp = (t % GRID_W).astype(np.float32)[:, None] * freqs[None, :]
    cos = np.concatenate([np.cos(row), np.cos(row), np.cos(colp), np.cos(colp)], -1)
    sin = np.concatenate([-np.sin(row), np.sin(row), -np.sin(colp), np.sin(colp)], -1)
    return jnp.asarray(cos, F32), jnp.asarray(sin, F32)


def _outproj_kernel(xp_ref, xs_ref, ylp_ref, yrp_ref, yls_ref, yrs_ref, wa_ref, wb_ref, gate_ref,
                    lng_ref, lnb_ref, shift_ref, scale_ref, wr_ref, br_ref,
                    z_ref, h2_ref, lg_ref, z_scr, *, n_p_tiles, tiles_per_seq, nj, tn):
    i = pl.program_id(0)
    j = pl.program_id(1)
    is_p = i < n_p_tiles
    row = jnp.where(is_p, 0, 1 + (i - n_p_tiles) // tiles_per_seq)

    def mix(x_ref, yl_ref, yr_ref):
        m = jnp.dot(yl_ref[...], wa_ref[...], preferred_element_type=F32)
        m = m + jnp.dot(yr_ref[...], wb_ref[...], preferred_element_type=F32)
        z = ALPHA * x_ref[...] + gate_ref[pl.ds(row, 1), :] * m
        z_ref[...] = z
        z_scr[j] = z

    @pl.when(is_p)
    def _():
        mix(xp_ref, ylp_ref, yrp_ref)

    @pl.when(jnp.logical_not(is_p))
    def _():
        mix(xs_ref, yls_ref, yrs_ref)

    @pl.when(j == nj - 1)
    def _():
        tm = z_scr.shape[1]
        inv_d = 1.0 / D_MODEL
        s1 = jnp.zeros((tm, 1), F32)
        for c in range(nj):
            s1 = s1 + jnp.sum(z_scr[c], -1, keepdims=True)
        mu = s1 * inv_d
        s2 = jnp.zeros((tm, 1), F32)
        for c in range(nj):
            zc = z_scr[c] - mu
            s2 = s2 + jnp.sum(zc * zc, -1, keepdims=True)
        rstd = lax.rsqrt(s2 * inv_d + LN_EPS)
        t1 = jnp.zeros((tm, 1), F32)
        for c in range(nj):
            cs = slice(c * tn, (c + 1) * tn)
            x1 = (z_scr[c] - mu) * rstd * lng_ref[:, cs] + lnb_ref[:, cs]
            z_scr[c] = x1
            t1 = t1 + jnp.sum(x1, -1, keepdims=True)
        mu2 = t1 * inv_d
        t2 = jnp.zeros((tm, 1), F32)
        for c in range(nj):
            xc = z_scr[c] - mu2
            t2 = t2 + jnp.sum(xc * xc, -1, keepdims=True)
        rstd2 = lax.rsqrt(t2 * inv_d + LN_EPS)
        logits = jnp.zeros((tm, LANES), F32) + br_ref[...]

        def h2_chunk(c):
            cs = slice(c * tn, (c + 1) * tn)
            return (z_scr[c] - mu2) * rstd2 * (1.0 + scale_ref[pl.ds(row, 1), cs]) + shift_ref[pl.ds(row, 1), cs]

        for c in range(nj // 2):
            lo, hi = h2_chunk(c), h2_chunk(c + nj // 2)
            h2_ref[:, c * tn:(c + 1) * tn] = _pack_bf16_pair(lo, hi)
            logits = logits + jnp.dot(lo.astype(BF16), wr_ref[c * tn:(c + 1) * tn, :], preferred_element_type=F32)
            logits = logits + jnp.dot(hi.astype(BF16), wr_ref[HALF_D + c * tn:HALF_D + (c + 1) * tn, :],
                                      preferred_element_type=F32)
        lg_ref[...] = logits


def _out_projection(xp, xs, ylp, yrp, yls, yrs, w_out_bf16, gate1, ln_g, ln_b, shift2, scale2, w_router, b_router,
                    dec_seq):
    tp, ts = xp.shape[0], xs.shape[0]
    tm, tn = TM_OUT, TN_OUT
    n_p, n_s = tp // tm, ts // tm
    nj = D_MODEL // tn
    half = LRU_WIDTH
    kern = functools.partial(_outproj_kernel, n_p_tiles=n_p, tiles_per_seq=dec_seq // tm, nj=nj, tn=tn)
    p_idx = lambda i, j: (jnp.minimum(i, n_p - 1), 0)
    s_idx = lambda i, j: (jnp.maximum(i - n_p, 0), 0)
    full = lambda i, j: (0, 0)
    return pl.pallas_call(
        kern,
        grid=(n_p + n_s, nj),
        in_specs=[pl.BlockSpec((tm, tn), lambda i, j: (jnp.minimum(i, n_p - 1), j)),
                  pl.BlockSpec((tm, tn), lambda i, j: (jnp.maximum(i - n_p, 0), j)),
                  pl.BlockSpec((tm, half), p_idx), pl.BlockSpec((tm, half), p_idx),
                  pl.BlockSpec((tm, half), s_idx), pl.BlockSpec((tm, half), s_idx),
                  pl.BlockSpec((half, tn), lambda i, j: (0, j)),
                  pl.BlockSpec((half, tn), lambda i, j: (1, j)),
                  pl.BlockSpec((8, tn), lambda i, j: (0, j)),
                  pl.BlockSpec((1, D_MODEL), full), pl.BlockSpec((1, D_MODEL), full),
                  pl.BlockSpec((8, D_MODEL), full), pl.BlockSpec((8, D_MODEL), full),
                  pl.BlockSpec((D_MODEL, LANES), full), pl.BlockSpec((1, LANES), full)],
        out_specs=[pl.BlockSpec((tm, tn), lambda i, j: (i, j)),
                   pl.BlockSpec((tm, HALF_D), lambda i, j: (i, 0)),
                   pl.BlockSpec((tm, LANES), lambda i, j: (i, 0))],
        out_shape=[jax.ShapeDtypeStruct((tp + ts, D_MODEL), F32),
                   jax.ShapeDtypeStruct((tp + ts, HALF_D), jnp.uint32),
                   jax.ShapeDtypeStruct((tp + ts, LANES), F32)],
        scratch_shapes=[pltpu.VMEM((nj, tm, tn), F32)],
        compiler_params=_params(("arbitrary", "arbitrary")),
        name="out_projection",
    )(xp, xs, ylp, yrp, yls, yrs, w_out_bf16, w_out_bf16, gate1, ln_g, ln_b, shift2, scale2, w_router, b_router)


def _route(logits):
    lg = logits[:, :N_GROUPS]
    le = logits[:, N_GROUPS:N_GROUPS + N_EXPERTS].reshape(-1, N_GROUPS, EXPERTS_PER_GROUP)
    pg = jax.nn.softmax(lg, -1)
    g_sel = jnp.argmax(lg, -1)
    p_sel = jnp.take_along_axis(pg, g_sel[:, None], 1)[:, 0]
    le_sel = jnp.take_along_axis(le, g_sel[:, None, None], 1)[:, 0]
    top_v, top_i = lax.top_k(le_sel, 2)
    weight = p_sel[:, None] * jax.nn.softmax(top_v, -1)
    expert = (g_sel[:, None] * EXPERTS_PER_GROUP + top_i).astype(jnp.int32)
    return expert, weight


def _dispatch_plan(expert, n_sub, n_super):
    i32 = jnp.int32
    flat_e = expert.reshape(-1)
    n_assign = flat_e.shape[0]
    ids = jnp.arange(N_EXPERTS, dtype=i32)
    onehot = (flat_e[:, None] == ids[None, :]).astype(i32)
    csum = jnp.cumsum(onehot, 0)
    counts = csum[-1]
    rank = jnp.sum(onehot * (csum - 1), 1)
    nb = (counts + SUB_ROWS - 1) // SUB_ROWS
    sub_end = jnp.cumsum(nb)
    sub_start = sub_end - nb
    dest = (jnp.sum(onehot * (sub_start * SUB_ROWS)[None, :], 1) + rank).astype(i32)
    pad_start = (sub_start * SUB_ROWS + counts).astype(i32)
    pad_len = (nb * SUB_ROWS - counts).astype(i32)
    tail = jnp.stack([sub_end[-1], n_sub - sub_end[-1]]).astype(i32)

    nsup = (nb + SUPER - 1) // SUPER
    sup_end = jnp.cumsum(nsup)
    sup_start = sup_end - nsup
    n_used = sup_end[-1]
    s = jnp.arange(n_super, dtype=i32)
    used = s < n_used
    last_exp = jnp.max(jnp.where(counts > 0, ids, 0))
    e_s = jnp.where(used, jnp.minimum(jnp.searchsorted(sup_end, s, side="right").astype(i32), N_EXPERTS - 1), last_exp)
    local = s - sup_start[e_s]
    first_sub = sub_start[e_s] + SUPER * local
    n_comp = jnp.where(used, jnp.clip(nb[e_s] - SUPER * local, 0, SUPER), 0).astype(i32)
    zero_first = sub_end[-1] + SUPER * (s - n_used)
    n_zero = jnp.where(used, 0, jnp.clip(n_sub - zero_first, 0, SUPER)).astype(i32)
    out_sub = jnp.where(used, first_sub, jnp.minimum(zero_first, n_sub - 1)).astype(i32)
    k = jnp.arange(SUPER, dtype=i32)
    x_sub_used = first_sub[:, None] + jnp.minimum(k[None, :], jnp.maximum(n_comp - 1, 0)[:, None])
    x_sub_last = x_sub_used[jnp.maximum(n_used - 1, 0)]
    x_sub = jnp.where(used[:, None], x_sub_used, x_sub_last[None, :]).astype(i32).reshape(-1)
    return dest, (pad_start, pad_len, tail), (e_s.astype(i32), n_comp, n_zero, out_sub, x_sub)


_PAD_PIECES = (128, 64, 32, 16, 8)


def _dispatch_kernel(dest_ref, pad_start_ref, pad_len_ref, tail_ref, h_ref, xs_hbm, zeros, sem, zsem):
    i = pl.program_id(0)
    tt = h_ref.shape[0]
    base = i * tt * 2

    def row_copy(r, k):
        return pltpu.make_async_copy(h_ref.at[pl.ds(r, 1)], xs_hbm.at[pl.ds(dest_ref[base + 2 * r + k], 1)], sem)

    def issue(r, c):
        row_copy(r, 0).start(priority=0)
        row_copy(r, 1).start(priority=1)
        return c

    lax.fori_loop(0, tt, issue, 0, unroll=4)

    def zero_fill(act):
        def pad(e, c):
            start = pad_start_ref[e]
            length = pad_len_ref[e]
            head = length & 7

            def head_row(r, cc):
                act(pltpu.make_async_copy(zeros.at[pl.ds(0, 1)], xs_hbm.at[pl.ds(start + r, 1)], zsem))
                return cc

            lax.fori_loop(0, head, head_row, 0)
            pos = start + head
            for piece in _PAD_PIECES:
                on = (length & piece) != 0

                @pl.when(on)
                def _():
                    rows = pl.ds(pl.multiple_of(pos, 8), piece)
                    act(pltpu.make_async_copy(zeros.at[pl.ds(0, piece)], xs_hbm.at[rows], zsem))

                pos = pos + jnp.where(on, piece, 0)
            return c

        lax.fori_loop(0, N_EXPERTS, pad, 0)

        def tail(q, c):
            row0 = pl.multiple_of((tail_ref[0] + q) * SUB_ROWS, SUB_ROWS)
            act(pltpu.make_async_copy(zeros, xs_hbm.at[pl.ds(row0, SUB_ROWS)], zsem))
            return c

        lax.fori_loop(0, tail_ref[1], tail, 0)

    @pl.when(i == 0)
    def _():
        zeros[...] = jnp.zeros_like(zeros)
        zero_fill(lambda cp: cp.start())
        zero_fill(lambda cp: cp.wait())

    def drain(r, c):
        row_copy(r, 0).wait()
        row_copy(r, 1).wait()
        return c

    lax.fori_loop(0, tt, drain, 0, unroll=4)


def _dispatch(dest, pad_start, pad_len, tail, h2_packed, n_rows):
    t = h2_packed.shape[0]
    grid_spec = pltpu.PrefetchScalarGridSpec(
        num_scalar_prefetch=4,
        grid=(t // DISPATCH_TOKENS,),
        in_specs=[pl.BlockSpec((DISPATCH_TOKENS, HALF_D), lambda i, d, ps, plen, tl: (i, 0))],
        out_specs=pl.BlockSpec(memory_space=pl.ANY),
        scratch_shapes=[pltpu.VMEM((SUB_ROWS, HALF_D), jnp.uint32),
                        pltpu.SemaphoreType.DMA(()), pltpu.SemaphoreType.DMA(())],
    )
    return pl.pallas_call(
        _dispatch_kernel,
        grid_spec=grid_spec,
        out_shape=jax.ShapeDtypeStruct((n_rows, HALF_D), jnp.uint32),
        compiler_params=_params(("arbitrary",)),
        name="dispatch",
    )(dest, pad_start, pad_len, tail, h2_packed)


def _expert_kernel(exp_ref, nc_ref, nz_ref, osub_ref, xsub_ref, x0_ref, x1_ref, x2_ref, x3_ref,
                   wg_ref, wu_ref, wd_ref, y_hbm, a_scr, ytile, sem, *, nf, nd):
    s = pl.program_id(0)
    t = pl.program_id(1)
    n_comp = nc_ref[s]
    n_out = n_comp + nz_ref[s]
    x_refs = (x0_ref, x1_ref, x2_ref, x3_ref)

    @pl.when(jnp.logical_and(t < nf, n_comp > 0))
    def _():
        wg16 = wg_ref[...].astype(BF16)
        wu16 = wu_ref[...].astype(BF16)

        def up(k):
            x = _unpack_bf16_pair(x_refs[k][...])
            g = jnp.dot(x, wg16, preferred_element_type=F32)
            u = jnp.dot(x, wu16, preferred_element_type=F32)
            a_scr[t, k * SUB_ROWS:(k + 1) * SUB_ROWS, :] = (g * _sigmoid(g) * u).astype(BF16)

        up(0)
        for k in range(1, SUPER):
            pl.when(k < n_comp)(functools.partial(up, k))

    @pl.when(t >= nf)
    def _():
        j = t - nf
        slot = j % 2

        @pl.when(n_comp > 0)
        def _():
            wd16 = wd_ref[...].astype(BF16)

            def down(k):
                rows = slice(k * SUB_ROWS, (k + 1) * SUB_ROWS)
                a = jnp.concatenate([a_scr[f, rows, :] for f in range(nf)], axis=1)
                acc = jnp.dot(a, wd16, preferred_element_type=F32)
                ytile[slot, rows, :] = _pack_bf16_pair(acc[:, :TN_DOWN // 2], acc[:, TN_DOWN // 2:])

            down(0)
            for k in range(1, SUPER):
                pl.when(k < n_comp)(functools.partial(down, k))

        @pl.when(n_comp == 0)
        def _():
            ytile[slot] = jnp.zeros(ytile.shape[1:], jnp.uint32)

        def out_copy(sl, k, jj):
            dst_rows = pl.ds(pl.multiple_of((osub_ref[s] + k) * SUB_ROWS, SUB_ROWS), SUB_ROWS)
            dst_cols = pl.ds(pl.multiple_of(jj * (TN_DOWN // 2), TN_DOWN // 2), TN_DOWN // 2)
            return pltpu.make_async_copy(ytile.at[sl, pl.ds(k * SUB_ROWS, SUB_ROWS), :],
                                         y_hbm.at[dst_rows, dst_cols], sem)

        for k in range(SUPER):
            @pl.when(jnp.logical_and(j > 0, k < n_out))
            def _():
                out_copy(1 - slot, k, j - 1).wait()
        for k in range(SUPER):
            @pl.when(k < n_out)
            def _():
                out_copy(slot, k, j).start()
        for k in range(SUPER):
            @pl.when(jnp.logical_and(j == nd - 1, k < n_out))
            def _():
                out_copy(slot, k, j).wait()


def _experts(xs, sup_exp, n_comp, n_zero, out_sub, x_sub, w_gate, w_up, w_down):
    n_super = sup_exp.shape[0]
    n_rows = xs.shape[0]
    nf = D_EXPERT // F_CHUNK
    nd = D_MODEL // TN_DOWN

    def up_idx(s, t, e, nc, nz, osub, xsub):
        return (e[s], 0, jnp.where(nc[s] > 0, jnp.minimum(t, nf - 1), nf - 1))

    def down_idx(s, t, e, nc, nz, osub, xsub):
        return (e[s], 0, jnp.where(nc[s] > 0, jnp.maximum(t - nf, 0), nd - 1))

    def x_spec(k):
        return pl.BlockSpec((SUB_ROWS, HALF_D), lambda s, t, e, nc, nz, osub, xsub: (xsub[s * SUPER + k], 0))

    grid_spec = pltpu.PrefetchScalarGridSpec(
        num_scalar_prefetch=5,
        grid=(n_super, nf + nd),
        in_specs=[x_spec(0), x_spec(1), x_spec(2), x_spec(3),
                  pl.BlockSpec((None, D_MODEL, F_CHUNK), up_idx),
                  pl.BlockSpec((None, D_MODEL, F_CHUNK), up_idx),
                  pl.BlockSpec((None, D_EXPERT, TN_DOWN), down_idx)],
        out_specs=pl.BlockSpec(memory_space=pl.ANY),
        scratch_shapes=[pltpu.VMEM((nf, SUPER * SUB_ROWS, F_CHUNK), BF16),
                        pltpu.VMEM((2, SUPER * SUB_ROWS, TN_DOWN // 2), jnp.uint32),
                        pltpu.SemaphoreType.DMA(())],
    )
    return pl.pallas_call(
        functools.partial(_expert_kernel, nf=nf, nd=nd),
        grid_spec=grid_spec,
        out_shape=jax.ShapeDtypeStruct((n_rows, HALF_D), jnp.uint32),
        compiler_params=_params(("arbitrary", "arbitrary"), 60 * 1024 * 1024),
        name="experts",
    )(sup_exp, n_comp, n_zero, out_sub, x_sub, xs, xs, xs, xs, w_gate, w_up, w_down)


def _final_kernel(dest_ref, y_hbm, z_ref, wt_ref, gate_ref, g1_ref, b1_ref, g2_ref, b2_ref,
                  op_ref, os_ref, ybuf, sem, *, n_p_tiles, tiles_per_seq):
    i = pl.program_id(0)
    n_tiles = pl.num_programs(0)
    tm = z_ref.shape[0]
    slot = i % 2

    def row_copy(src_row, sl, k, r):
        return pltpu.make_async_copy(y_hbm.at[pl.ds(src_row, 1)], ybuf.at[sl, k, pl.ds(r, 1)], sem.at[sl])

    def gather(tile, sl):
        base = tile * tm * 2

        def issue(r, c):
            row_copy(dest_ref[base + 2 * r], sl, 0, r).start(priority=0)
            row_copy(dest_ref[base + 2 * r + 1], sl, 1, r).start(priority=1)
            return c

        lax.fori_loop(0, tm, issue, 0, unroll=4)

    @pl.when(i == 0)
    def _():
        gather(0, 0)

    @pl.when(i + 1 < n_tiles)
    def _():
        gather(i + 1, 1 - slot)

    def drain(r, c):
        row_copy(0, slot, 0, r).wait()
        row_copy(0, slot, 1, r).wait()
        return c

    lax.fori_loop(0, tm, drain, 0, unroll=4)

    is_p = i < n_p_tiles
    row = jnp.where(is_p, 0, 1 + (i - n_p_tiles) // tiles_per_seq)
    wt = wt_ref[...]
    w0, w1 = wt[:, 0:1], wt[:, 1:2]
    half_tile = TN_DOWN // 2
    pieces = []
    for j in range(D_MODEL // TN_DOWN):
        y0 = ybuf[slot, 0, :, j * half_tile:(j + 1) * half_tile]
        y1 = ybuf[slot, 1, :, j * half_tile:(j + 1) * half_tile]
        pieces.append(w0 * pltpu.bitcast(y0 << 16, F32) + w1 * pltpu.bitcast(y1 << 16, F32))
        pieces.append(w0 * pltpu.bitcast(y0 & jnp.uint32(0xFFFF0000), F32)
                      + w1 * pltpu.bitcast(y1 & jnp.uint32(0xFFFF0000), F32))
    f = jnp.concatenate(pieces, axis=1)
    x1 = _ln(z_ref[...]) * g1_ref[...] + b1_ref[...]
    out = _ln(ALPHA * x1 + gate_ref[pl.ds(row, 1), :] * f) * g2_ref[...] + b2_ref[...]

    @pl.when(is_p)
    def _():
        op_ref[...] = out

    @pl.when(jnp.logical_not(is_p))
    def _():
        os_ref[...] = out


def _combine(dest, y_rows, z, weight, gate2, g1, b1, g2, b2, n_prompt, dec_seq):
    t = z.shape[0]
    tm = TM_FIN
    n_p = n_prompt // tm
    n_s = (t - n_prompt) // tm
    full = lambda i, d: (0, 0)
    grid_spec = pltpu.PrefetchScalarGridSpec(
        num_scalar_prefetch=1,
        grid=(n_p + n_s,),
        in_specs=[pl.BlockSpec(memory_space=pl.ANY),
                  pl.BlockSpec((tm, D_MODEL), lambda i, d: (i, 0)),
                  pl.BlockSpec((tm, 2), lambda i, d: (i, 0)),
                  pl.BlockSpec((8, D_MODEL), full),
                  pl.BlockSpec((1, D_MODEL), full), pl.BlockSpec((1, D_MODEL), full),
                  pl.BlockSpec((1, D_MODEL), full), pl.BlockSpec((1, D_MODEL), full)],
        out_specs=[pl.BlockSpec((tm, D_MODEL), lambda i, d: (jnp.minimum(i, n_p - 1), 0)),
                   pl.BlockSpec((tm, D_MODEL), lambda i, d: (jnp.maximum(i - n_p, 0), 0))],
        scratch_shapes=[pltpu.VMEM((2, 2, tm, HALF_D), jnp.uint32), pltpu.SemaphoreType.DMA((2,))],
    )
    return pl.pallas_call(
        functools.partial(_final_kernel, n_p_tiles=n_p, tiles_per_seq=dec_seq // tm),
        grid_spec=grid_spec,
        out_shape=[jax.ShapeDtypeStruct((n_prompt, D_MODEL), F32),
                   jax.ShapeDtypeStruct((t - n_prompt, D_MODEL), F32)],
        compiler_params=_params(("arbitrary",)),
        name="combine",
    )(dest, y_rows, z, weight, gate2, g1, b1, g2, b2)


def kernel(x_prompt, x_sample, state_lru, state_ret, c, c_ctx, w_mod, b_mod, w_in, conv_w, conv_b, lru_wa, lru_ba,
           lru_wx, lru_bx, lru_lam, ret_decay, w_out, ln1_g, ln1_b, router_g, router_g_b, router_e, router_e_b,
           w_gate, w_up, w_down, ln2_g, ln2_b):
    assert w_in.shape[0] == 1, "single trunk layer"
    nb, seq, d = x_prompt.shape
    nbs, dec_seq, _ = x_sample.shape
    tp, ts = nb * seq, nbs * dec_seq
    assert tp % dec_seq == 0 and d == D_MODEL

    cond = jnp.zeros((8, d), F32).at[0].set(c_ctx).at[1:1 + nbs].set(c)
    mod = _modulation(cond, w_mod[0], b_mod[0][None, :])
    shift1, scale1, gate1, shift2, scale2, gate2 = [mod[:, k * d:(k + 1) * d] for k in range(6)]

    xp = x_prompt.reshape(tp, d)
    xs = x_sample.reshape(ts, d)
    proj = _in_projection(_ln_modulate(xp, xs, shift1, scale1, dec_seq), w_in[0].astype(BF16))

    w_gates = (0.5 * jnp.concatenate([lru_wa[0, 0], lru_wx[0, 0], lru_wa[0, 1], lru_wx[0, 1]], -1)).astype(BF16)
    pb = jnp.stack([0.5 * lru_ba[0, 0], 0.5 * lru_bx[0, 0], lru_lam[0, 0],
                    0.5 * lru_ba[0, 1], 0.5 * lru_bx[0, 1], lru_lam[0, 1]], 0)
    proj_p = proj.reshape((tp + ts) // seq, seq, IN_COLS)
    proj_s = proj.reshape((tp + ts) // dec_seq, dec_seq, IN_COLS)
    ylp, st_lru = _lru_mixer(proj_p, 0, nb, 16, 1, conv_w[0], conv_b[0][None, :], w_gates, pb,
                             jnp.zeros((2, nb, LRU_WIDTH), F32))
    yls, _ = _lru_mixer(proj_s, tp // dec_seq, nbs, nbs, 4, conv_w[0], conv_b[0][None, :], w_gates, pb,
                        jnp.swapaxes(state_lru[:, 0], 0, 1))

    decay = jnp.broadcast_to(ret_decay[0].T[:, :, None], (RET_HEADS, 2, LANES))
    yrp, st_ret = _ret_mixer(proj_p, 0, nb, 8, decay, emit_state=True)
    (yrs,) = _ret_mixer(proj_s, tp // dec_seq, nbs, 1, decay, rope_tabs=_rope_tables(dec_seq), s0=state_ret)

    w_router = jnp.concatenate(
        [router_g[0], jnp.transpose(router_e[0], (1, 0, 2)).reshape(d, N_EXPERTS),
         jnp.zeros((d, LANES - N_GROUPS - N_EXPERTS), F32)], -1).astype(BF16)
    b_router = jnp.concatenate([router_g_b[0], router_e_b[0].reshape(-1),
                                jnp.zeros((LANES - N_GROUPS - N_EXPERTS,), F32)])[None, :]
    z, h2, logits = _out_projection(
        xp, xs, ylp.reshape(tp, -1), yrp.reshape(tp, -1), yls.reshape(ts, -1), yrs.reshape(ts, -1),
        w_out[0].astype(BF16), gate1, ln1_g, ln1_b, shift2, scale2, w_router, b_router, dec_seq)

    expert, weight = _route(logits)
    n_assign = 2 * (tp + ts)
    n_sub = -(-(n_assign + N_EXPERTS * (SUB_ROWS - 1)) // SUB_ROWS)
    n_super = N_EXPERTS + -(-n_assign // (SUB_ROWS * SUPER))
    assert SUPER * n_super >= n_sub + (SUPER - 1) * N_EXPERTS
    dest, fill_plan, super_plan = _dispatch_plan(expert, n_sub, n_super)
    xs_rows = _dispatch(dest, *fill_plan, h2, n_sub * SUB_ROWS)
    y_rows = _experts(xs_rows, *super_plan, w_gate[0], w_up[0], w_down[0])
    y_p, y_s = _combine(dest, y_rows, z, weight, gate2, ln1_g, ln1_b, ln2_g, ln2_b, tp, dec_seq)

    new_state_lru = jnp.swapaxes(st_lru, 0, 1)[:, None]
    return (y_p.reshape(nb, seq, d), y_s.reshape(nbs, dec_seq, d), new_state_lru, st_ret)
```

```python
import functools

import jax
import jax.numpy as jnp
import numpy as np
from jax import lax
from jax.experimental import pallas as pl
from jax.experimental.pallas import tpu as pltpu

F32 = jnp.float32
BF16 = jnp.bfloat16

D_MODEL = 4096
LRU_WIDTH = 2048
LRU_BLOCKS = 16
LANES = 128
RET_HEADS = 16
RET_DH = 128
IN_COLS = 12288
GRID_W = 64
ROPE_BASE = 10000.0
LRU_C = 8.0
N_GROUPS = 4
EXPERTS_PER_GROUP = 8
N_EXPERTS = 32
D_EXPERT = 1024
LN_EPS = 1e-6
GN_EPS = 1e-5
ALPHA = 2.0 ** 0.25

VMEM_LIMIT = 56 * 1024 * 1024
SCAN_PAD = 8

TM_LN = 512
TM_PROJ = 1024
TN_IN = 1024
TN_OUT = 512
TM_OUT = 512
SUB_ROWS = 256
SUPER = 4
F_CHUNK = 256
TN_DOWN = 1024
DISPATCH_TOKENS = 512
ISSUE_GROUP = 16
TM_FIN = 256
HALF_D = D_MODEL // 2


def _params(sem, vmem_limit=VMEM_LIMIT):
    return pltpu.CompilerParams(dimension_semantics=sem, vmem_limit_bytes=vmem_limit)


def _pack_bf16_pair(lo, hi):
    lo_bits = pltpu.bitcast(lo.astype(BF16).astype(F32), jnp.uint32) >> 16
    hi_bits = pltpu.bitcast(hi.astype(BF16).astype(F32), jnp.uint32) & jnp.uint32(0xFFFF0000)
    return lo_bits | hi_bits


def _unpack_bf16_pair(words):
    lo = pltpu.bitcast(words << 16, F32).astype(BF16)
    hi = pltpu.bitcast(words & jnp.uint32(0xFFFF0000), F32).astype(BF16)
    return jnp.concatenate([lo, hi], axis=1)


def _sigmoid(x):
    return 0.5 * jnp.tanh(0.5 * x) + 0.5


def _softplus(x):
    return jnp.maximum(x, 0.0) + jnp.log1p(jnp.exp(-jnp.abs(x)))


def _ln(x):
    mu = jnp.mean(x, -1, keepdims=True)
    xc = x - mu
    var = jnp.mean(xc * xc, -1, keepdims=True)
    return xc * lax.rsqrt(var + LN_EPS)


def _mod_kernel(cond_ref, w_ref, b_ref, o_ref):
    c = cond_ref[...]
    s = (c * _sigmoid(c)).astype(BF16)
    o_ref[...] = jnp.dot(s, w_ref[...].astype(BF16), preferred_element_type=F32) + b_ref[...]


def _modulation(cond, w_mod, b_mod):
    tn = 512
    n = w_mod.shape[1]
    return pl.pallas_call(
        _mod_kernel,
        grid=(n // tn,),
        in_specs=[pl.BlockSpec((8, D_MODEL), lambda j: (0, 0)),
                  pl.BlockSpec((D_MODEL, tn), lambda j: (0, j)),
                  pl.BlockSpec((1, tn), lambda j: (0, j))],
        out_specs=pl.BlockSpec((8, tn), lambda j: (0, j)),
        out_shape=jax.ShapeDtypeStruct((8, n), F32),
        compiler_params=_params(("arbitrary",)),
        name="modulation",
    )(cond, w_mod, b_mod)


def _ln_mod_kernel(xp_ref, xs_ref, shift_ref, scale_ref, h_ref, *, n_p_tiles, tiles_per_seq):
    i = pl.program_id(0)

    def fill(x_ref, row):
        h = _ln(x_ref[...]) * (1.0 + scale_ref[pl.ds(row, 1), :]) + shift_ref[pl.ds(row, 1), :]
        h_ref[...] = h.astype(BF16)

    @pl.when(i < n_p_tiles)
    def _():
        fill(xp_ref, 0)

    @pl.when(i >= n_p_tiles)
    def _():
        fill(xs_ref, 1 + (i - n_p_tiles) // tiles_per_seq)


def _ln_modulate(xp, xs, shift, scale, dec_seq):
    tp, ts = xp.shape[0], xs.shape[0]
    tm = TM_LN
    n_p, n_s = tp // tm, ts // tm
    kern = functools.partial(_ln_mod_kernel, n_p_tiles=n_p, tiles_per_seq=dec_seq // tm)
    return pl.pallas_call(
        kern,
        grid=(n_p + n_s,),
        in_specs=[pl.BlockSpec((tm, D_MODEL), lambda i: (jnp.minimum(i, n_p - 1), 0)),
                  pl.BlockSpec((tm, D_MODEL), lambda i: (jnp.maximum(i - n_p, 0), 0)),
                  pl.BlockSpec((8, D_MODEL), lambda i: (0, 0)),
                  pl.BlockSpec((8, D_MODEL), lambda i: (0, 0))],
        out_specs=pl.BlockSpec((tm, D_MODEL), lambda i: (i, 0)),
        out_shape=jax.ShapeDtypeStruct((tp + ts, D_MODEL), BF16),
        compiler_params=_params(("arbitrary",)),
        name="ln_modulate",
    )(xp, xs, shift, scale)


def _matmul_kernel(h_ref, w_ref, o_ref):
    o_ref[...] = jnp.dot(h_ref[...], w_ref[...], preferred_element_type=F32).astype(o_ref.dtype)


def _in_projection(h, w_in_bf16):
    t = h.shape[0]
    tm, tn = TM_PROJ, TN_IN
    return pl.pallas_call(
        _matmul_kernel,
        grid=(t // tm, IN_COLS // tn),
        in_specs=[pl.BlockSpec((tm, D_MODEL), lambda i, j: (i, 0)),
                  pl.BlockSpec((D_MODEL, tn), lambda i, j: (0, j))],
        out_specs=pl.BlockSpec((tm, tn), lambda i, j: (i, j)),
        out_shape=jax.ShapeDtypeStruct((t, IN_COLS), BF16),
        compiler_params=_params(("arbitrary", "arbitrary")),
        name="in_projection",
    )(h, w_in_bf16)


def _lru_kernel(x_ref, g_ref, cw_ref, cb_ref, wg_ref, pb_ref, h0_ref, y_ref, st_ref,
                xc_scr, af_scr, uf_scr, ab_scr, ub_scr, *, bg, seq, pitch, ncb):
    rows = lax.broadcasted_iota(jnp.int32, (seq, LANES), 0)
    for c in range(ncb):
        lanes = slice(c * LANES, (c + 1) * LANES)
        w = cw_ref[:, lanes]
        bias = cb_ref[:, lanes]
        for b in range(bg):
            x = x_ref[b, :, lanes].astype(F32)
            xm2 = jnp.where(rows >= 2, pltpu.roll(x, 2, 0), 0.0)
            xm1 = jnp.where(rows >= 1, pltpu.roll(x, 1, 0), 0.0)
            xp1 = jnp.where(rows < seq - 1, pltpu.roll(x, seq - 1, 0), 0.0)
            xc_scr[pl.ds(b * seq, seq), :] = bias + xm2 * w[0:1] + xm1 * w[1:2] + x * w[2:3] + xp1 * w[3:4]

        xc = xc_scr[...]
        gates = jnp.dot(xc.astype(BF16), wg_ref[c], preferred_element_type=F32)
        pb = pb_ref[:, lanes]
        half_xc = 0.5 * xc
        for d, (a_scr, u_scr) in enumerate(((af_scr, uf_scr), (ab_scr, ub_scr))):
            t_r = jnp.tanh(gates[:, (2 * d) * LANES:(2 * d + 1) * LANES] + pb[3 * d:3 * d + 1])
            t_i = jnp.tanh(gates[:, (2 * d + 1) * LANES:(2 * d + 2) * LANES] + pb[3 * d + 1:3 * d + 2])
            c4 = (0.5 * LRU_C) * _softplus(-pb[3 * d + 2:3 * d + 3])
            neg_log_a = c4 * t_r + c4
            a = jnp.exp(-neg_log_a)
            y = jnp.tanh(neg_log_a) * (1.0 + a * a)
            root = jnp.where(y > 0.0, y * lax.rsqrt(y), 0.0)
            u = root * (half_xc * (t_i + 1.0))
            for b in range(bg):
                a_scr[c, pl.ds(b * pitch, seq), :] = a[b * seq:(b + 1) * seq]
                u_scr[c, pl.ds(b * pitch, seq), :] = u[b * seq:(b + 1) * seq]

    def step(t, carry):
        tb = seq - 1 - t
        out = []
        for c in range(ncb):
            hf, hb = carry[2 * c], carry[2 * c + 1]
            hf = af_scr[c, pl.ds(t, bg, stride=pitch), :] * hf + uf_scr[c, pl.ds(t, bg, stride=pitch), :]
            uf_scr[c, pl.ds(t, bg, stride=pitch), :] = hf
            hb = ab_scr[c, pl.ds(tb, bg, stride=pitch), :] * hb + ub_scr[c, pl.ds(tb, bg, stride=pitch), :]
            ub_scr[c, pl.ds(tb, bg, stride=pitch), :] = hb
            out += [hf, hb]
        return tuple(out)

    init = []
    for c in range(ncb):
        init += [h0_ref[0, :, c * LANES:(c + 1) * LANES], h0_ref[1, :, c * LANES:(c + 1) * LANES]]
    last = lax.fori_loop(0, seq, step, tuple(init), unroll=8)
    for c in range(ncb):
        lanes = slice(c * LANES, (c + 1) * LANES)
        st_ref[0, :, lanes] = last[2 * c]
        st_ref[1, :, lanes] = last[2 * c + 1]
        for b in range(bg):
            hs = uf_scr[c, pl.ds(b * pitch, seq), :] + ub_scr[c, pl.ds(b * pitch, seq), :]
            y_ref[b, :, lanes] = (jax.nn.gelu(g_ref[b, :, lanes].astype(F32)) * hs).astype(BF16)


def _lru_mixer(proj3, seq0, nseq, bg, ncb, conv_w, conv_b, w_gates, pb, h0):
    seq = proj3.shape[1]
    pitch = seq + SCAN_PAD
    off = seq0 // bg
    cw = ncb * LANES
    ncol = LRU_WIDTH // cw
    kern = functools.partial(_lru_kernel, bg=bg, seq=seq, pitch=pitch, ncb=ncb)
    return pl.pallas_call(
        kern,
        grid=(nseq // bg, ncol),
        in_specs=[pl.BlockSpec((bg, seq, cw), lambda b, c: (b + off, 0, c)),
                  pl.BlockSpec((bg, seq, cw), lambda b, c: (b + off, 0, ncol + c)),
                  pl.BlockSpec((4, cw), lambda b, c: (0, c)),
                  pl.BlockSpec((1, cw), lambda b, c: (0, c)),
                  pl.BlockSpec((ncb, LANES, 4 * LANES), lambda b, c: (c, 0, 0)),
                  pl.BlockSpec((6, cw), lambda b, c: (0, c)),
                  pl.BlockSpec((2, bg, cw), lambda b, c: (0, b, c))],
        out_specs=[pl.BlockSpec((bg, seq, cw), lambda b, c: (b, 0, c)),
                   pl.BlockSpec((2, bg, cw), lambda b, c: (0, b, c))],
        out_shape=[jax.ShapeDtypeStruct((nseq, seq, LRU_WIDTH), BF16),
                   jax.ShapeDtypeStruct((2, nseq, LRU_WIDTH), F32)],
        scratch_shapes=[pltpu.VMEM((bg * seq, LANES), F32)] + [pltpu.VMEM((ncb, bg * pitch, LANES), F32)] * 4,
        compiler_params=_params(("arbitrary", "arbitrary")),
        name="lru_mixer",
    )(proj3, proj3, conv_w, conv_b, w_gates, pb, h0)


def _rope(x, cos, sin_signed, first_half):
    partner = jnp.where(first_half, pltpu.roll(x, LANES - 32, 1), pltpu.roll(x, 32, 1))
    return x * cos + partner * sin_signed


def _ret_kernel(*refs, bg, seq, rope, has_state, emit_state, qb):
    refs = list(refs)
    q_ref, k_ref, v_ref, g_ref, dec_ref = refs[:5]
    pos = 5
    if rope:
        cos_ref, sin_ref = refs[pos:pos + 2]
        pos += 2
    if has_state:
        s0_ref = refs[pos]
        pos += 1
    y_ref = refs[pos]
    pos += 1
    if emit_state:
        st_ref = refs[pos]
        pos += 1
    mask_scr = refs[pos]

    log_g = -_softplus(-dec_ref[0])
    lgf, lgb = log_g[0:1], log_g[1:2]
    reps = seq // LANES
    lgf_row = jnp.concatenate([lgf] * reps, axis=1)
    lgb_row = jnp.concatenate([lgb] * reps, axis=1)

    @pl.when(pl.program_id(1) == 0)
    def _():
        for blk in range(seq // qb):
            ti = lax.broadcasted_iota(jnp.int32, (qb, seq), 0) + blk * qb
            si = lax.broadcasted_iota(jnp.int32, (qb, seq), 1)
            dist = (ti - si).astype(F32)
            e = jnp.where(dist >= 0, dist * lgf_row, (-dist) * lgb_row)
            mask_scr[pl.ds(blk * qb, qb), :] = jnp.where(dist == 0, 2.0, jnp.exp(e))

    lane = lax.broadcasted_iota(jnp.int32, (seq, LANES), 1)
    first_half = (lane % 64) < 32
    trow = lax.broadcasted_iota(jnp.int32, (seq, LANES), 0).astype(F32)
    for b in range(bg):
        q = q_ref[b].astype(F32)
        k = k_ref[b].astype(F32) * (RET_DH ** -0.5)
        v16 = v_ref[b]
        if rope:
            q = _rope(q, cos_ref[...], sin_ref[...], first_half)
            k = _rope(k, cos_ref[...], sin_ref[...], first_half)
        q16 = q.astype(BF16)
        k16 = k.astype(BF16)
        if has_state:
            qf16 = (q * jnp.exp((trow + 1.0) * lgf)).astype(BF16)
            qb16 = (q * jnp.exp((float(seq) - trow) * lgb)).astype(BF16)
            s0f = s0_ref[b, 0].astype(BF16)
            s0b = s0_ref[b, 1].astype(BF16)
        for blk in range(seq // qb):
            sl = slice(blk * qb, (blk + 1) * qb)
            s = lax.dot_general(q16[sl], k16, (((1,), (1,)), ((), ())), preferred_element_type=F32)
            p = (s * mask_scr[pl.ds(blk * qb, qb), :]).astype(BF16)
            o = jnp.dot(p, v16, preferred_element_type=F32)
            if has_state:
                o = o + jnp.dot(qf16[sl], s0f, preferred_element_type=F32)
                o = o + jnp.dot(qb16[sl], s0b, preferred_element_type=F32)
            mu = jnp.mean(o, -1, keepdims=True)
            oc = o - mu
            var = jnp.mean(oc * oc, -1, keepdims=True)
            on = oc * lax.rsqrt(var + GN_EPS)
            gt = g_ref[b, pl.ds(blk * qb, qb), :].astype(F32)
            y_ref[b, pl.ds(blk * qb, qb), :] = (gt * _sigmoid(gt) * on).astype(BF16)
        if emit_state:
            kf16 = (k * jnp.exp((float(seq - 1) - trow) * lgf)).astype(BF16)
            kb16 = (k * jnp.exp(trow * lgb)).astype(BF16)
            sf = lax.dot_general(kf16, v16, (((0,), (0,)), ((), ())), preferred_element_type=F32)
            sb = lax.dot_general(kb16, v16, (((0,), (0,)), ((), ())), preferred_element_type=F32)
            if has_state:
                sf = sf + jnp.exp(float(seq) * lgf) * s0_ref[b, 0]
                sb = sb + jnp.exp(float(seq) * lgb) * s0_ref[b, 1]
            st_ref[b, 0] = sf
            st_ref[b, 1] = sb


def _ret_mixer(proj3, seq0, nseq, bg, decay, rope_tabs=None, s0=None, emit_state=False):
    seq = proj3.shape[1]
    off = seq0 // bg
    qb = min(seq, 256)
    rope = rope_tabs is not None
    has_state = s0 is not None
    kern = functools.partial(_ret_kernel, bg=bg, seq=seq, rope=rope, has_state=has_state,
                             emit_state=emit_state, qb=qb)

    def col(base):
        return pl.BlockSpec((bg, seq, LANES), lambda h, b: (b + off, 0, base + h))

    st_spec = pl.BlockSpec((bg, None, 2, None, RET_DH, RET_DH), lambda h, b: (b, 0, 0, h, 0, 0))
    in_specs = [col(2 * LRU_BLOCKS), col(2 * LRU_BLOCKS + RET_HEADS), col(2 * LRU_BLOCKS + 2 * RET_HEADS),
                col(2 * LRU_BLOCKS + 3 * RET_HEADS), pl.BlockSpec((1, 2, LANES), lambda h, b: (h, 0, 0))]
    args = [proj3, proj3, proj3, proj3, decay]
    if rope:
        in_specs += [pl.BlockSpec((seq, LANES), lambda h, b: (0, 0))] * 2
        args += list(rope_tabs)
    if has_state:
        in_specs.append(st_spec)
        args.append(s0)
    out_specs = [pl.BlockSpec((bg, seq, LANES), lambda h, b: (b, 0, h))]
    out_shape = [jax.ShapeDtypeStruct((nseq, seq, RET_HEADS * RET_DH), BF16)]
    if emit_state:
        out_specs.append(st_spec)
        out_shape.append(jax.ShapeDtypeStruct((nseq, 1, 2, RET_HEADS, RET_DH, RET_DH), F32))
    return pl.pallas_call(
        kern,
        grid=(RET_HEADS, nseq // bg),
        in_specs=in_specs,
        out_specs=out_specs,
        out_shape=out_shape,
        scratch_shapes=[pltpu.VMEM((seq, seq), F32)],
        compiler_params=_params(("arbitrary", "arbitrary")),
        name="ret_mixer",
    )(*args)


def _rope_tables(seq):
    nf = RET_DH // 4
    freqs = (np.float32(ROPE_BASE) ** (-np.arange(nf, dtype=np.float32) / np.float32(nf))).astype(np.float32)
    t = np.arange(seq)
    row = (t // GRID_W).astype(np.float32)[:, None] * freqs[None, :]
    colp = (t % GRID_W).astype(np.float32)[:, None] * freqs[None, :]
    cos = np.concatenate([np.cos(row), np.cos(row), np.cos(colp), np.cos(colp)], -1)
    sin = np.concatenate([-np.sin(row), np.sin(row), -np.sin(colp), np.sin(colp)], -1)
    return jnp.asarray(cos, F32), jnp.asarray(sin, F32)


def _outproj_kernel(xp_ref, xs_ref, ylp_ref, yrp_ref, yls_ref, yrs_ref, wa_ref, wb_ref, gate_ref,
                    lng_ref, lnb_ref, shift_ref, scale_ref, wr_ref, br_ref,
                    z_ref, h2_ref, lg_ref, z_scr, *, n_p_tiles, tiles_per_seq, nj, tn):
    i = pl.program_id(0)
    j = pl.program_id(1)
    is_p = i < n_p_tiles
    row = jnp.where(is_p, 0, 1 + (i - n_p_tiles) // tiles_per_seq)

    def mix(x_ref, yl_ref, yr_ref):
        m = jnp.dot(yl_ref[...], wa_ref[...], preferred_element_type=F32)
        m = m + jnp.dot(yr_ref[...], wb_ref[...], preferred_element_type=F32)
        z = ALPHA * x_ref[...] + gate_ref[pl.ds(row, 1), :] * m
        z_ref[...] = z
        z_scr[j] = z

    @pl.when(is_p)
    def _():
        mix(xp_ref, ylp_ref, yrp_ref)

    @pl.when(jnp.logical_not(is_p))
    def _():
        mix(xs_ref, yls_ref, yrs_ref)

    @pl.when(j == nj - 1)
    def _():
        tm = z_scr.shape[1]
        inv_d = 1.0 / D_MODEL
        s1 = jnp.zeros((tm, 1), F32)
        for c in range(nj):
            s1 = s1 + jnp.sum(z_scr[c], -1, keepdims=True)
        mu = s1 * inv_d
        s2 = jnp.zeros((tm, 1), F32)
        for c in range(nj):
            zc = z_scr[c] - mu
            s2 = s2 + jnp.sum(zc * zc, -1, keepdims=True)
        rstd = lax.rsqrt(s2 * inv_d + LN_EPS)
        t1 = jnp.zeros((tm, 1), F32)
        for c in range(nj):
            cs = slice(c * tn, (c + 1) * tn)
            x1 = (z_scr[c] - mu) * rstd * lng_ref[:, cs] + lnb_ref[:, cs]
            z_scr[c] = x1
            t1 = t1 + jnp.sum(x1, -1, keepdims=True)
        mu2 = t1 * inv_d
        t2 = jnp.zeros((tm, 1), F32)
        for c in range(nj):
            xc = z_scr[c] - mu2
            t2 = t2 + jnp.sum(xc * xc, -1, keepdims=True)
        rstd2 = lax.rsqrt(t2 * inv_d + LN_EPS)
        logits = jnp.zeros((tm, LANES), F32) + br_ref[...]

        def h2_chunk(c):
            cs = slice(c * tn, (c + 1) * tn)
            return (z_scr[c] - mu2) * rstd2 * (1.0 + scale_ref[pl.ds(row, 1), cs]) + shift_ref[pl.ds(row, 1), cs]

        for c in range(nj // 2):
            lo, hi = h2_chunk(c), h2_chunk(c + nj // 2)
            h2_ref[:, c * tn:(c + 1) * tn] = _pack_bf16_pair(lo, hi)
            logits = logits + jnp.dot(lo.astype(BF16), wr_ref[c * tn:(c + 1) * tn, :], preferred_element_type=F32)
            logits = logits + jnp.dot(hi.astype(BF16), wr_ref[HALF_D + c * tn:HALF_D + (c + 1) * tn, :],
                                      preferred_element_type=F32)
        lg_ref[...] = logits


def _out_projection(xp, xs, ylp, yrp, yls, yrs, w_out_bf16, gate1, ln_g, ln_b, shift2, scale2, w_router, b_router,
                    dec_seq):
    tp, ts = xp.shape[0], xs.shape[0]
    tm, tn = TM_OUT, TN_OUT
    n_p, n_s = tp // tm, ts // tm
    nj = D_MODEL // tn
    half = LRU_WIDTH
    kern = functools.partial(_outproj_kernel, n_p_tiles=n_p, tiles_per_seq=dec_seq // tm, nj=nj, tn=tn)
    p_idx = lambda i, j: (jnp.minimum(i, n_p - 1), 0)
    s_idx = lambda i, j: (jnp.maximum(i - n_p, 0), 0)
    full = lambda i, j: (0, 0)
    return pl.pallas_call(
        kern,
        grid=(n_p + n_s, nj),
        in_specs=[pl.BlockSpec((tm, tn), lambda i, j: (jnp.minimum(i, n_p - 1), j)),
                  pl.BlockSpec((tm, tn), lambda i, j: (jnp.maximum(i - n_p, 0), j)),
                  pl.BlockSpec((tm, half), p_idx), pl.BlockSpec((tm, half), p_idx),
                  pl.BlockSpec((tm, half), s_idx), pl.BlockSpec((tm, half), s_idx),
                  pl.BlockSpec((half, tn), lambda i, j: (0, j)),
                  pl.BlockSpec((half, tn), lambda i, j: (1, j)),
                  pl.BlockSpec((8, tn), lambda i, j: (0, j)),
                  pl.BlockSpec((1, D_MODEL), full), pl.BlockSpec((1, D_MODEL), full),
                  pl.BlockSpec((8, D_MODEL), full), pl.BlockSpec((8, D_MODEL), full),
                  pl.BlockSpec((D_MODEL, LANES), full), pl.BlockSpec((1, LANES), full)],
        out_specs=[pl.BlockSpec((tm, tn), lambda i, j: (i, j)),
                   pl.BlockSpec((tm, HALF_D), lambda i, j: (i, 0)),
                   pl.BlockSpec((tm, LANES), lambda i, j: (i, 0))],
        out_shape=[jax.ShapeDtypeStruct((tp + ts, D_MODEL), F32),
                   jax.ShapeDtypeStruct((tp + ts, HALF_D), jnp.uint32),
                   jax.ShapeDtypeStruct((tp + ts, LANES), F32)],
        scratch_shapes=[pltpu.VMEM((nj, tm, tn), F32)],
        compiler_params=_params(("arbitrary", "arbitrary")),
        name="out_projection",
    )(xp, xs, ylp, yrp, yls, yrs, w_out_bf16, w_out_bf16, gate1, ln_g, ln_b, shift2, scale2, w_router, b_router)


def _route(logits):
    lg = logits[:, :N_GROUPS]
    le = logits[:, N_GROUPS:N_GROUPS + N_EXPERTS].reshape(-1, N_GROUPS, EXPERTS_PER_GROUP)
    pg = jax.nn.softmax(lg, -1)
    g_sel = jnp.argmax(lg, -1)
    p_sel = jnp.take_along_axis(pg, g_sel[:, None], 1)[:, 0]
    le_sel = jnp.take_along_axis(le, g_sel[:, None, None], 1)[:, 0]
    top_v, top_i = lax.top_k(le_sel, 2)
    weight = p_sel[:, None] * jax.nn.softmax(top_v, -1)
    expert = (g_sel[:, None] * EXPERTS_PER_GROUP + top_i).astype(jnp.int32)
    return expert, weight


def _dispatch_plan(expert, n_sub, n_super):
    i32 = jnp.int32
    flat_e = expert.reshape(-1)
    n_assign = flat_e.shape[0]
    ids = jnp.arange(N_EXPERTS, dtype=i32)
    onehot = (flat_e[:, None] == ids[None, :]).astype(i32)
    csum = jnp.cumsum(onehot, 0)
    counts = csum[-1]
    rank = jnp.sum(onehot * (csum - 1), 1)
    nb = (counts + SUB_ROWS - 1) // SUB_ROWS
    sub_end = jnp.cumsum(nb)
    sub_start = sub_end - nb
    dest = (jnp.sum(onehot * (sub_start * SUB_ROWS)[None, :], 1) + rank).astype(i32)
    pad_start = (sub_start * SUB_ROWS + counts).astype(i32)
    pad_len = (nb * SUB_ROWS - counts).astype(i32)
    tail = jnp.stack([sub_end[-1], n_sub - sub_end[-1]]).astype(i32)

    nsup = (nb + SUPER - 1) // SUPER
    sup_end = jnp.cumsum(nsup)
    sup_start = sup_end - nsup
    n_used = sup_end[-1]
    s = jnp.arange(n_super, dtype=i32)
    used = s < n_used
    last_exp = jnp.max(jnp.where(counts > 0, ids, 0))
    e_s = jnp.where(used, jnp.minimum(jnp.searchsorted(sup_end, s, side="right").astype(i32), N_EXPERTS - 1), last_exp)
    local = s - sup_start[e_s]
    first_sub = sub_start[e_s] + SUPER * local
    n_comp = jnp.where(used, jnp.clip(nb[e_s] - SUPER * local, 0, SUPER), 0).astype(i32)
    zero_first = sub_end[-1] + SUPER * (s - n_used)
    n_zero = jnp.where(used, 0, jnp.clip(n_sub - zero_first, 0, SUPER)).astype(i32)
    out_sub = jnp.where(used, first_sub, jnp.minimum(zero_first, n_sub - 1)).astype(i32)
    k = jnp.arange(SUPER, dtype=i32)
    x_sub_used = first_sub[:, None] + jnp.minimum(k[None, :], jnp.maximum(n_comp - 1, 0)[:, None])
    x_sub_last = x_sub_used[jnp.maximum(n_used - 1, 0)]
    x_sub = jnp.where(used[:, None], x_sub_used, x_sub_last[None, :]).astype(i32).reshape(-1)
    return dest, (pad_start, pad_len, tail), (e_s.astype(i32), n_comp, n_zero, out_sub, x_sub)


_PAD_PIECES = (128, 64, 32, 16, 8)


def _dispatch_kernel(dest_ref, pad_start_ref, pad_len_ref, tail_ref, h_ref, xs_hbm, zeros, sem, zsem):
    i = pl.program_id(0)
    tt = h_ref.shape[0]
    base = i * tt * 2

    def row_copy(r, k):
        return pltpu.make_async_copy(h_ref.at[pl.ds(r, 1)], xs_hbm.at[pl.ds(dest_ref[base + 2 * r + k], 1)], sem)

    def issue(g, c):
        r0 = pl.multiple_of(g * ISSUE_GROUP, ISSUE_GROUP)
        for k in range(ISSUE_GROUP):
            row_copy(r0 + k, 0).start(priority=0)
            row_copy(r0 + k, 1).start(priority=1)
        return c

    lax.fori_loop(0, tt // ISSUE_GROUP, issue, 0)

    def zero_fill(act):
        def pad(e, c):
            start = pad_start_ref[e]
            length = pad_len_ref[e]
            head = length & 7

            def head_row(r, cc):
                act(pltpu.make_async_copy(zeros.at[pl.ds(0, 1)], xs_hbm.at[pl.ds(start + r, 1)], zsem))
                return cc

            lax.fori_loop(0, head, head_row, 0)
            pos = start + head
            for piece in _PAD_PIECES:
                on = (length & piece) != 0

                @pl.when(on)
                def _():
                    rows = pl.ds(pl.multiple_of(pos, 8), piece)
                    act(pltpu.make_async_copy(zeros.at[pl.ds(0, piece)], xs_hbm.at[rows], zsem))

                pos = pos + jnp.where(on, piece, 0)
            return c

        lax.fori_loop(0, N_EXPERTS, pad, 0)

        def tail(q, c):
            row0 = pl.multiple_of((tail_ref[0] + q) * SUB_ROWS, SUB_ROWS)
            act(pltpu.make_async_copy(zeros, xs_hbm.at[pl.ds(row0, SUB_ROWS)], zsem))
            return c

        lax.fori_loop(0, tail_ref[1], tail, 0)

    @pl.when(i == 0)
    def _():
        zeros[...] = jnp.zeros_like(zeros)
        zero_fill(lambda cp: cp.start())
        zero_fill(lambda cp: cp.wait())

    def drain(r, c):
        row_copy(r, 0).wait()
        row_copy(r, 1).wait()
        return c

    lax.fori_loop(0, tt, drain, 0, unroll=4)


def _dispatch(dest, pad_start, pad_len, tail, h2_packed, n_rows):
    t = h2_packed.shape[0]
    grid_spec = pltpu.PrefetchScalarGridSpec(
        num_scalar_prefetch=4,
        grid=(t // DISPATCH_TOKENS,),
        in_specs=[pl.BlockSpec((DISPATCH_TOKENS, HALF_D), lambda i, d, ps, plen, tl: (i, 0))],
        out_specs=pl.BlockSpec(memory_space=pl.ANY),
        scratch_shapes=[pltpu.VMEM((SUB_ROWS, HALF_D), jnp.uint32),
                        pltpu.SemaphoreType.DMA(()), pltpu.SemaphoreType.DMA(())],
    )
    return pl.pallas_call(
        _dispatch_kernel,
        grid_spec=grid_spec,
        out_shape=jax.ShapeDtypeStruct((n_rows, HALF_D), jnp.uint32),
        compiler_params=_params(("arbitrary",)),
        name="dispatch",
    )(dest, pad_start, pad_len, tail, h2_packed)


def _expert_kernel(exp_ref, nc_ref, nz_ref, osub_ref, xsub_ref, x0_ref, x1_ref, x2_ref, x3_ref,
                   wg_ref, wu_ref, wd_ref, y_hbm, a_scr, ytile, sem, *, nf, nd):
    s = pl.program_id(0)
    t = pl.program_id(1)
    n_comp = nc_ref[s]
    n_out = n_comp + nz_ref[s]
    x_refs = (x0_ref, x1_ref, x2_ref, x3_ref)

    @pl.when(jnp.logical_and(t < nf, n_comp > 0))
    def _():
        wg16 = wg_ref[...].astype(BF16)
        wu16 = wu_ref[...].astype(BF16)

        def up(k):
            x = _unpack_bf16_pair(x_refs[k][...])
            g = jnp.dot(x, wg16, preferred_element_type=F32)
            u = jnp.dot(x, wu16, preferred_element_type=F32)
            a_scr[t, k * SUB_ROWS:(k + 1) * SUB_ROWS, :] = (g * _sigmoid(g) * u).astype(BF16)

        up(0)
        for k in range(1, SUPER):
            pl.when(k < n_comp)(functools.partial(up, k))

    @pl.when(t >= nf)
    def _():
        j = t - nf
        slot = j % 2

        @pl.when(n_comp > 0)
        def _():
            wd16 = wd_ref[...].astype(BF16)

            def down(k):
                rows = slice(k * SUB_ROWS, (k + 1) * SUB_ROWS)
                a = jnp.concatenate([a_scr[f, rows, :] for f in range(nf)], axis=1)
                acc = jnp.dot(a, wd16, preferred_element_type=F32)
                ytile[slot, rows, :] = _pack_bf16_pair(acc[:, :TN_DOWN // 2], acc[:, TN_DOWN // 2:])

            down(0)
            for k in range(1, SUPER):
                pl.when(k < n_comp)(functools.partial(down, k))

        @pl.when(n_comp == 0)
        def _():
            ytile[slot] = jnp.zeros(ytile.shape[1:], jnp.uint32)

        def out_copy(sl, k, jj):
            dst_rows = pl.ds(pl.multiple_of((osub_ref[s] + k) * SUB_ROWS, SUB_ROWS), SUB_ROWS)
            dst_cols = pl.ds(pl.multiple_of(jj * (TN_DOWN // 2), TN_DOWN // 2), TN_DOWN // 2)
            return pltpu.make_async_copy(ytile.at[sl, pl.ds(k * SUB_ROWS, SUB_ROWS), :],
                                         y_hbm.at[dst_rows, dst_cols], sem)

        for k in range(SUPER):
            @pl.when(jnp.logical_and(j > 0, k < n_out))
            def _():
                out_copy(1 - slot, k, j - 1).wait()
        for k in range(SUPER):
            @pl.when(k < n_out)
            def _():
                out_copy(slot, k, j).start()
        for k in range(SUPER):
            @pl.when(jnp.logical_and(j == nd - 1, k < n_out))
            def _():
                out_copy(slot, k, j).wait()


def _experts(xs, sup_exp, n_comp, n_zero, out_sub, x_sub, w_gate, w_up, w_down):
    n_super = sup_exp.shape[0]
    n_rows = xs.shape[0]
    nf = D_EXPERT // F_CHUNK
    nd = D_MODEL // TN_DOWN

    def nxt(s, nc):
        s2 = jnp.minimum(s + 1, n_super - 1)
        return s2, jnp.logical_and(nc[s] > 0, nc[s2] > 0)

    def up_idx(s, t, e, nc, nz, osub, xsub):
        s2, has_next = nxt(s, nc)
        ahead = jnp.logical_and(t >= nf, has_next)
        chunk = jnp.where(nc[s] > 0, jnp.minimum(t, nf - 1), nf - 1)
        return (jnp.where(ahead, e[s2], e[s]), 0, jnp.where(ahead, 0, chunk))

    def down_idx(s, t, e, nc, nz, osub, xsub):
        sp = jnp.maximum(s - 1, 0)
        behind = jnp.logical_and(jnp.logical_and(t == 0, s > 0), nc[s] > 0)
        chunk = jnp.where(nc[s] > 0, jnp.maximum(t - nf, 0), nd - 1)
        return (jnp.where(behind, e[sp], e[s]), 0, jnp.where(behind, nd - 1, chunk))

    def x_spec(k):
        def idx(s, t, e, nc, nz, osub, xsub):
            s2, has_next = nxt(s, nc)
            return (xsub[jnp.where(jnp.logical_and(t >= nf, has_next), s2, s) * SUPER + k], 0)

        return pl.BlockSpec((SUB_ROWS, HALF_D), idx)

    grid_spec = pltpu.PrefetchScalarGridSpec(
        num_scalar_prefetch=5,
        grid=(n_super, nf + nd),
        in_specs=[x_spec(0), x_spec(1), x_spec(2), x_spec(3),
                  pl.BlockSpec((None, D_MODEL, F_CHUNK), up_idx),
                  pl.BlockSpec((None, D_MODEL, F_CHUNK), up_idx),
                  pl.BlockSpec((None, D_EXPERT, TN_DOWN), down_idx)],
        out_specs=pl.BlockSpec(memory_space=pl.ANY),
        scratch_shapes=[pltpu.VMEM((nf, SUPER * SUB_ROWS, F_CHUNK), BF16),
                        pltpu.VMEM((2, SUPER * SUB_ROWS, TN_DOWN // 2), jnp.uint32),
                        pltpu.SemaphoreType.DMA(())],
    )
    return pl.pallas_call(
        functools.partial(_expert_kernel, nf=nf, nd=nd),
        grid_spec=grid_spec,
        out_shape=jax.ShapeDtypeStruct((n_rows, HALF_D), jnp.uint32),
        compiler_params=_params(("arbitrary", "arbitrary"), 60 * 1024 * 1024),
        name="experts",
    )(sup_exp, n_comp, n_zero, out_sub, x_sub, xs, xs, xs, xs, w_gate, w_up, w_down)


def _final_kernel(dest_ref, y_hbm, z_ref, wt_ref, gate_ref, g1_ref, b1_ref, g2_ref, b2_ref,
                  op_ref, os_ref, ybuf, sem, *, n_p_tiles, tiles_per_seq):
    i = pl.program_id(0)
    n_tiles = pl.num_programs(0)
    tm = z_ref.shape[0]
    slot = i % 2

    def row_copy(src_row, sl, k, r):
        return pltpu.make_async_copy(y_hbm.at[pl.ds(src_row, 1)], ybuf.at[sl, k, pl.ds(r, 1)], sem.at[sl])

    def gather(tile, sl):
        base = tile * tm * 2

        def issue(g, c):
            r0 = pl.multiple_of(g * ISSUE_GROUP, ISSUE_GROUP)
            for k in range(ISSUE_GROUP):
                row_copy(dest_ref[base + 2 * (r0 + k)], sl, 0, r0 + k).start(priority=0)
                row_copy(dest_ref[base + 2 * (r0 + k) + 1], sl, 1, r0 + k).start(priority=1)
            return c

        lax.fori_loop(0, tm // ISSUE_GROUP, issue, 0)

    @pl.when(i == 0)
    def _():
        gather(0, 0)

    @pl.when(i + 1 < n_tiles)
    def _():
        gather(i + 1, 1 - slot)

    def drain(r, c):
        row_copy(0, slot, 0, r).wait()
        row_copy(0, slot, 1, r).wait()
        return c

    lax.fori_loop(0, tm, drain, 0, unroll=4)

    is_p = i < n_p_tiles
    row = jnp.where(is_p, 0, 1 + (i - n_p_tiles) // tiles_per_seq)
    wt = wt_ref[...]
    w0, w1 = wt[:, 0:1], wt[:, 1:2]
    half_tile = TN_DOWN // 2
    pieces = []
    for j in range(D_MODEL // TN_DOWN):
        y0 = ybuf[slot, 0, :, j * half_tile:(j + 1) * half_tile]
        y1 = ybuf[slot, 1, :, j * half_tile:(j + 1) * half_tile]
        pieces.append(w0 * pltpu.bitcast(y0 << 16, F32) + w1 * pltpu.bitcast(y1 << 16, F32))
        pieces.append(w0 * pltpu.bitcast(y0 & jnp.uint32(0xFFFF0000), F32)
                      + w1 * pltpu.bitcast(y1 & jnp.uint32(0xFFFF0000), F32))
    f = jnp.concatenate(pieces, axis=1)
    x1 = _ln(z_ref[...]) * g1_ref[...] + b1_ref[...]
    out = _ln(ALPHA * x1 + gate_ref[pl.ds(row, 1), :] * f) * g2_ref[...] + b2_ref[...]

    @pl.when(is_p)
    def _():
        op_ref[...] = out

    @pl.when(jnp.logical_not(is_p))
    def _():
        os_ref[...] = out


def _combine(dest, y_rows, z, weight, gate2, g1, b1, g2, b2, n_prompt, dec_seq):
    t = z.shape[0]
    tm = TM_FIN
    n_p = n_prompt // tm
    n_s = (t - n_prompt) // tm
    full = lambda i, d: (0, 0)
    grid_spec = pltpu.PrefetchScalarGridSpec(
        num_scalar_prefetch=1,
        grid=(n_p + n_s,),
        in_specs=[pl.BlockSpec(memory_space=pl.ANY),
                  pl.BlockSpec((tm, D_MODEL), lambda i, d: (i, 0)),
                  pl.BlockSpec((tm, 2), lambda i, d: (i, 0)),
                  pl.BlockSpec((8, D_MODEL), full),
                  pl.BlockSpec((1, D_MODEL), full), pl.BlockSpec((1, D_MODEL), full),
                  pl.BlockSpec((1, D_MODEL), full), pl.BlockSpec((1, D_MODEL), full)],
        out_specs=[pl.BlockSpec((tm, D_MODEL), lambda i, d: (jnp.minimum(i, n_p - 1), 0)),
                   pl.BlockSpec((tm, D_MODEL), lambda i, d: (jnp.maximum(i - n_p, 0), 0))],
        scratch_shapes=[pltpu.VMEM((2, 2, tm, HALF_D), jnp.uint32), pltpu.SemaphoreType.DMA((2,))],
    )
    return pl.pallas_call(
        functools.partial(_final_kernel, n_p_tiles=n_p, tiles_per_seq=dec_seq // tm),
        grid_spec=grid_spec,
        out_shape=[jax.ShapeDtypeStruct((n_prompt, D_MODEL), F32),
                   jax.ShapeDtypeStruct((t - n_prompt, D_MODEL), F32)],
        compiler_params=_params(("arbitrary",)),
        name="combine",
    )(dest, y_rows, z, weight, gate2, g1, b1, g2, b2)


def kernel(x_prompt, x_sample, state_lru, state_ret, c, c_ctx, w_mod, b_mod, w_in, conv_w, conv_b, lru_wa, lru_ba,
           lru_wx, lru_bx, lru_lam, ret_decay, w_out, ln1_g, ln1_b, router_g, router_g_b, router_e, router_e_b,
           w_gate, w_up, w_down, ln2_g, ln2_b):
    assert w_in.shape[0] == 1, "single trunk layer"
    nb, seq, d = x_prompt.shape
    nbs, dec_seq, _ = x_sample.shape
    tp, ts = nb * seq, nbs * dec_seq
    assert tp % dec_seq == 0 and d == D_MODEL

    cond = jnp.zeros((8, d), F32).at[0].set(c_ctx).at[1:1 + nbs].set(c)
    mod = _modulation(cond, w_mod[0], b_mod[0][None, :])
    shift1, scale1, gate1, shift2, scale2, gate2 = [mod[:, k * d:(k + 1) * d] for k in range(6)]

    xp = x_prompt.reshape(tp, d)
    xs = x_sample.reshape(ts, d)
    proj = _in_projection(_ln_modulate(xp, xs, shift1, scale1, dec_seq), w_in[0].astype(BF16))

    w_gates = (0.5 * jnp.concatenate([lru_wa[0, 0], lru_wx[0, 0], lru_wa[0, 1], lru_wx[0, 1]], -1)).astype(BF16)
    pb = jnp.stack([0.5 * lru_ba[0, 0], 0.5 * lru_bx[0, 0], lru_lam[0, 0],
                    0.5 * lru_ba[0, 1], 0.5 * lru_bx[0, 1], lru_lam[0, 1]], 0)
    proj_p = proj.reshape((tp + ts) // seq, seq, IN_COLS)
    proj_s = proj.reshape((tp + ts) // dec_seq, dec_seq, IN_COLS)
    ylp, st_lru = _lru_mixer(proj_p, 0, nb, 16, 1, conv_w[0], conv_b[0][None, :], w_gates, pb,
                             jnp.zeros((2, nb, LRU_WIDTH), F32))
    yls, _ = _lru_mixer(proj_s, tp // dec_seq, nbs, nbs, 4, conv_w[0], conv_b[0][None, :], w_gates, pb,
                        jnp.swapaxes(state_lru[:, 0], 0, 1))

    decay = jnp.broadcast_to(ret_decay[0].T[:, :, None], (RET_HEADS, 2, LANES))
    yrp, st_ret = _ret_mixer(proj_p, 0, nb, 8, decay, emit_state=True)
    (yrs,) = _ret_mixer(proj_s, tp // dec_seq, nbs, 1, decay, rope_tabs=_rope_tables(dec_seq), s0=state_ret)

    w_router = jnp.concatenate(
        [router_g[0], jnp.transpose(router_e[0], (1, 0, 2)).reshape(d, N_EXPERTS),
         jnp.zeros((d, LANES - N_GROUPS - N_EXPERTS), F32)], -1).astype(BF16)
    b_router = jnp.concatenate([router_g_b[0], router_e_b[0].reshape(-1),
                                jnp.zeros((LANES - N_GROUPS - N_EXPERTS,), F32)])[None, :]
    z, h2, logits = _out_projection(
        xp, xs, ylp.reshape(tp, -1), yrp.reshape(tp, -1), yls.reshape(ts, -1), yrs.reshape(ts, -1),
        w_out[0].astype(BF16), gate1, ln1_g, ln1_b, shift2, scale2, w_router, b_router, dec_seq)

    expert, weight = _route(logits)
    n_assign = 2 * (tp + ts)
    n_sub = -(-(n_assign + N_EXPERTS * (SUB_ROWS - 1)) // SUB_ROWS)
    n_super = N_EXPERTS + -(-n_assign // (SUB_ROWS * SUPER))
    assert SUPER * n_super >= n_sub + (SUPER - 1) * N_EXPERTS
    dest, fill_plan, super_plan = _dispatch_plan(expert, n_sub, n_super)
    xs_rows = _dispatch(dest, *fill_plan, h2, n_sub * SUB_ROWS)
    y_rows = _experts(xs_rows, *super_plan, w_gate[0], w_up[0], w_down[0])
    y_p, y_s = _combine(dest, y_rows, z, weight, gate2, ln1_g, ln1_b, ln2_g, ln2_b, tp, dec_seq)

    new_state_lru = jnp.swapaxes(st_lru, 0, 1)[:, None]
    return (y_p.reshape(nb, seq, d), y_s.reshape(nbs, dec_seq, d), new_state_lru, st_ret)
```

```python
import functools

import jax
import jax.numpy as jnp
import numpy as np
from jax import lax
from jax.experimental import pallas as pl
from jax.experimental.pallas import tpu as pltpu

F32 = jnp.float32
BF16 = jnp.bfloat16

D_MODEL = 4096
LRU_WIDTH = 2048
LRU_BLOCKS = 16
LANES = 128
RET_HEADS = 16
RET_DH = 128
IN_COLS = 12288
GRID_W = 64
ROPE_BASE = 10000.0
LRU_C = 8.0
N_GROUPS = 4
EXPERTS_PER_GROUP = 8
N_EXPERTS = 32
D_EXPERT = 1024
LN_EPS = 1e-6
GN_EPS = 1e-5
ALPHA = 2.0 ** 0.25

VMEM_LIMIT = 56 * 1024 * 1024
SCAN_PAD = 8

TM_LN = 512
TM_PROJ = 1024
TN_IN = 1024
TN_OUT = 512
TM_OUT = 1024
TM_NORM = 512
SUB_ROWS = 256
SUPER = 4
F_CHUNK = 256
TN_DOWN = 1024
DISPATCH_TOKENS = 512
ISSUE_GROUP = 16
TM_FIN = 256
HALF_D = D_MODEL // 2


def _params(sem, vmem_limit=VMEM_LIMIT):
    return pltpu.CompilerParams(dimension_semantics=sem, vmem_limit_bytes=vmem_limit)


def _pack_bf16_pair(lo, hi):
    lo_bits = pltpu.bitcast(lo.astype(BF16).astype(F32), jnp.uint32) >> 16
    hi_bits = pltpu.bitcast(hi.astype(BF16).astype(F32), jnp.uint32) & jnp.uint32(0xFFFF0000)
    return lo_bits | hi_bits


def _unpack_bf16_pair(words):
    lo = pltpu.bitcast(words << 16, F32).astype(BF16)
    hi = pltpu.bitcast(words & jnp.uint32(0xFFFF0000), F32).astype(BF16)
    return jnp.concatenate([lo, hi], axis=1)


def _sigmoid(x):
    return 0.5 * jnp.tanh(0.5 * x) + 0.5


def _softplus(x):
    return jnp.maximum(x, 0.0) + jnp.log1p(jnp.exp(-jnp.abs(x)))


def _ln(x):
    mu = jnp.mean(x, -1, keepdims=True)
    xc = x - mu
    var = jnp.mean(xc * xc, -1, keepdims=True)
    return xc * lax.rsqrt(var + LN_EPS)


def _mod_kernel(cond_ref, w_ref, b_ref, o_ref):
    c = cond_ref[...]
    s = (c * _sigmoid(c)).astype(BF16)
    o_ref[...] = jnp.dot(s, w_ref[...].astype(BF16), preferred_element_type=F32) + b_ref[...]


def _modulation(cond, w_mod, b_mod):
    tn = 512
    n = w_mod.shape[1]
    return pl.pallas_call(
        _mod_kernel,
        grid=(n // tn,),
        in_specs=[pl.BlockSpec((8, D_MODEL), lambda j: (0, 0)),
                  pl.BlockSpec((D_MODEL, tn), lambda j: (0, j)),
                  pl.BlockSpec((1, tn), lambda j: (0, j))],
        out_specs=pl.BlockSpec((8, tn), lambda j: (0, j)),
        out_shape=jax.ShapeDtypeStruct((8, n), F32),
        compiler_params=_params(("arbitrary",)),
        name="modulation",
    )(cond, w_mod, b_mod)


def _ln_mod_kernel(xp_ref, xs_ref, shift_ref, scale_ref, h_ref, *, n_p_tiles, tiles_per_seq):
    i = pl.program_id(0)

    def fill(x_ref, row):
        h = _ln(x_ref[...]) * (1.0 + scale_ref[pl.ds(row, 1), :]) + shift_ref[pl.ds(row, 1), :]
        h_ref[...] = h.astype(BF16)

    @pl.when(i < n_p_tiles)
    def _():
        fill(xp_ref, 0)

    @pl.when(i >= n_p_tiles)
    def _():
        fill(xs_ref, 1 + (i - n_p_tiles) // tiles_per_seq)


def _ln_modulate(xp, xs, shift, scale, dec_seq):
    tp, ts = xp.shape[0], xs.shape[0]
    tm = TM_LN
    n_p, n_s = tp // tm, ts // tm
    kern = functools.partial(_ln_mod_kernel, n_p_tiles=n_p, tiles_per_seq=dec_seq // tm)
    return pl.pallas_call(
        kern,
        grid=(n_p + n_s,),
        in_specs=[pl.BlockSpec((tm, D_MODEL), lambda i: (jnp.minimum(i, n_p - 1), 0)),
                  pl.BlockSpec((tm, D_MODEL), lambda i: (jnp.maximum(i - n_p, 0), 0)),
                  pl.BlockSpec((8, D_MODEL), lambda i: (0, 0)),
                  pl.BlockSpec((8, D_MODEL), lambda i: (0, 0))],
        out_specs=pl.BlockSpec((tm, D_MODEL), lambda i: (i, 0)),
        out_shape=jax.ShapeDtypeStruct((tp + ts, D_MODEL), BF16),
        compiler_params=_params(("arbitrary",)),
        name="ln_modulate",
    )(xp, xs, shift, scale)


def _matmul_kernel(h_ref, w_ref, o_ref):
    o_ref[...] = jnp.dot(h_ref[...], w_ref[...], preferred_element_type=F32).astype(o_ref.dtype)


def _in_projection(h, w_in_bf16):
    t = h.shape[0]
    tm, tn = TM_PROJ, TN_IN
    return pl.pallas_call(
        _matmul_kernel,
        grid=(t // tm, IN_COLS // tn),
        in_specs=[pl.BlockSpec((tm, D_MODEL), lambda i, j: (i, 0)),
                  pl.BlockSpec((D_MODEL, tn), lambda i, j: (0, j))],
        out_specs=pl.BlockSpec((tm, tn), lambda i, j: (i, j)),
        out_shape=jax.ShapeDtypeStruct((t, IN_COLS), BF16),
        compiler_params=_params(("arbitrary", "arbitrary")),
        name="in_projection",
    )(h, w_in_bf16)


def _lru_kernel(x_ref, g_ref, cw_ref, cb_ref, wg_ref, pb_ref, h0_ref, y_ref, st_ref,
                xc_scr, af_scr, uf_scr, ab_scr, ub_scr, *, bg, seq, pitch, ncb):
    rows = lax.broadcasted_iota(jnp.int32, (seq, LANES), 0)
    for c in range(ncb):
        lanes = slice(c * LANES, (c + 1) * LANES)
        w = cw_ref[:, lanes]
        bias = cb_ref[:, lanes]
        for b in range(bg):
            x = x_ref[b, :, lanes].astype(F32)
            xm2 = jnp.where(rows >= 2, pltpu.roll(x, 2, 0), 0.0)
            xm1 = jnp.where(rows >= 1, pltpu.roll(x, 1, 0), 0.0)
            xp1 = jnp.where(rows < seq - 1, pltpu.roll(x, seq - 1, 0), 0.0)
            xc_scr[pl.ds(b * seq, seq), :] = bias + xm2 * w[0:1] + xm1 * w[1:2] + x * w[2:3] + xp1 * w[3:4]

        xc = xc_scr[...]
        gates = jnp.dot(xc.astype(BF16), wg_ref[c], preferred_element_type=F32)
        pb = pb_ref[:, lanes]
        half_xc = 0.5 * xc
        for d, (a_scr, u_scr) in enumerate(((af_scr, uf_scr), (ab_scr, ub_scr))):
            t_r = jnp.tanh(gates[:, (2 * d) * LANES:(2 * d + 1) * LANES] + pb[3 * d:3 * d + 1])
            t_i = jnp.tanh(gates[:, (2 * d + 1) * LANES:(2 * d + 2) * LANES] + pb[3 * d + 1:3 * d + 2])
            c4 = (0.5 * LRU_C) * _softplus(-pb[3 * d + 2:3 * d + 3])
            neg_log_a = c4 * t_r + c4
            a = jnp.exp(-neg_log_a)
            y = jnp.tanh(neg_log_a) * (1.0 + a * a)
            root = jnp.where(y > 0.0, y * lax.rsqrt(y), 0.0)
            u = root * (half_xc * (t_i + 1.0))
            for b in range(bg):
                a_scr[c, pl.ds(b * pitch, seq), :] = a[b * seq:(b + 1) * seq]
                u_scr[c, pl.ds(b * pitch, seq), :] = u[b * seq:(b + 1) * seq]

    def step(t, carry):
        tb = seq - 1 - t
        out = []
        for c in range(ncb):
            hf, hb = carry[2 * c], carry[2 * c + 1]
            hf = af_scr[c, pl.ds(t, bg, stride=pitch), :] * hf + uf_scr[c, pl.ds(t, bg, stride=pitch), :]
            uf_scr[c, pl.ds(t, bg, stride=pitch), :] = hf
            hb = ab_scr[c, pl.ds(tb, bg, stride=pitch), :] * hb + ub_scr[c, pl.ds(tb, bg, stride=pitch), :]
            ub_scr[c, pl.ds(tb, bg, stride=pitch), :] = hb
            out += [hf, hb]
        return tuple(out)

    init = []
    for c in range(ncb):
        init += [h0_ref[0, :, c * LANES:(c + 1) * LANES], h0_ref[1, :, c * LANES:(c + 1) * LANES]]
    last = lax.fori_loop(0, seq, step, tuple(init), unroll=8)
    for c in range(ncb):
        lanes = slice(c * LANES, (c + 1) * LANES)
        st_ref[0, :, lanes] = last[2 * c]
        st_ref[1, :, lanes] = last[2 * c + 1]
        for b in range(bg):
            hs = uf_scr[c, pl.ds(b * pitch, seq), :] + ub_scr[c, pl.ds(b * pitch, seq), :]
            y_ref[b, :, lanes] = (jax.nn.gelu(g_ref[b, :, lanes].astype(F32)) * hs).astype(BF16)


def _lru_mixer(proj3, seq0, nseq, bg, ncb, conv_w, conv_b, w_gates, pb, h0):
    seq = proj3.shape[1]
    pitch = seq + SCAN_PAD
    off = seq0 // bg
    cw = ncb * LANES
    ncol = LRU_WIDTH // cw
    kern = functools.partial(_lru_kernel, bg=bg, seq=seq, pitch=pitch, ncb=ncb)
    return pl.pallas_call(
        kern,
        grid=(nseq // bg, ncol),
        in_specs=[pl.BlockSpec((bg, seq, cw), lambda b, c: (b + off, 0, c)),
                  pl.BlockSpec((bg, seq, cw), lambda b, c: (b + off, 0, ncol + c)),
                  pl.BlockSpec((4, cw), lambda b, c: (0, c)),
                  pl.BlockSpec((1, cw), lambda b, c: (0, c)),
                  pl.BlockSpec((ncb, LANES, 4 * LANES), lambda b, c: (c, 0, 0)),
                  pl.BlockSpec((6, cw), lambda b, c: (0, c)),
                  pl.BlockSpec((2, bg, cw), lambda b, c: (0, b, c))],
        out_specs=[pl.BlockSpec((bg, seq, cw), lambda b, c: (b, 0, c)),
                   pl.BlockSpec((2, bg, cw), lambda b, c: (0, b, c))],
        out_shape=[jax.ShapeDtypeStruct((nseq, seq, LRU_WIDTH), BF16),
                   jax.ShapeDtypeStruct((2, nseq, LRU_WIDTH), F32)],
        scratch_shapes=[pltpu.VMEM((bg * seq, LANES), F32)] + [pltpu.VMEM((ncb, bg * pitch, LANES), F32)] * 4,
        compiler_params=_params(("arbitrary", "arbitrary")),
        name="lru_mixer",
    )(proj3, proj3, conv_w, conv_b, w_gates, pb, h0)


def _rope(x, cos, sin_signed, first_half):
    partner = jnp.where(first_half, pltpu.roll(x, LANES - 32, 1), pltpu.roll(x, 32, 1))
    return x * cos + partner * sin_signed


def _ret_kernel(*refs, bg, seq, rope, has_state, emit_state, qb):
    refs = list(refs)
    q_ref, k_ref, v_ref, g_ref, dec_ref = refs[:5]
    pos = 5
    if rope:
        cos_ref, sin_ref = refs[pos:pos + 2]
        pos += 2
    if has_state:
        s0_ref = refs[pos]
        pos += 1
    y_ref = refs[pos]
    pos += 1
    if emit_state:
        st_ref = refs[pos]
        pos += 1
    mask_scr = refs[pos]

    log_g = -_softplus(-dec_ref[0])
    lgf, lgb = log_g[0:1], log_g[1:2]
    reps = seq // LANES
    lgf_row = jnp.concatenate([lgf] * reps, axis=1)
    lgb_row = jnp.concatenate([lgb] * reps, axis=1)

    @pl.when(pl.program_id(1) == 0)
    def _():
        for blk in range(seq // qb):
            ti = lax.broadcasted_iota(jnp.int32, (qb, seq), 0) + blk * qb
            si = lax.broadcasted_iota(jnp.int32, (qb, seq), 1)
            dist = (ti - si).astype(F32)
            e = jnp.where(dist >= 0, dist * lgf_row, (-dist) * lgb_row)
            mask_scr[pl.ds(blk * qb, qb), :] = jnp.where(dist == 0, 2.0, jnp.exp(e))

    lane = lax.broadcasted_iota(jnp.int32, (seq, LANES), 1)
    first_half = (lane % 64) < 32
    trow = lax.broadcasted_iota(jnp.int32, (seq, LANES), 0).astype(F32)
    for b in range(bg):
        q = q_ref[b].astype(F32)
        k = k_ref[b].astype(F32) * (RET_DH ** -0.5)
        v16 = v_ref[b]
        if rope:
            q = _rope(q, cos_ref[...], sin_ref[...], first_half)
            k = _rope(k, cos_ref[...], sin_ref[...], first_half)
        q16 = q.astype(BF16)
        k16 = k.astype(BF16)
        if has_state:
            qf16 = (q * jnp.exp((trow + 1.0) * lgf)).astype(BF16)
            qb16 = (q * jnp.exp((float(seq) - trow) * lgb)).astype(BF16)
            s0f = s0_ref[b, 0].astype(BF16)
            s0b = s0_ref[b, 1].astype(BF16)
        for blk in range(seq // qb):
            sl = slice(blk * qb, (blk + 1) * qb)
            s = lax.dot_general(q16[sl], k16, (((1,), (1,)), ((), ())), preferred_element_type=F32)
            p = (s * mask_scr[pl.ds(blk * qb, qb), :]).astype(BF16)
            o = jnp.dot(p, v16, preferred_element_type=F32)
            if has_state:
                o = o + jnp.dot(qf16[sl], s0f, preferred_element_type=F32)
                o = o + jnp.dot(qb16[sl], s0b, preferred_element_type=F32)
            mu = jnp.mean(o, -1, keepdims=True)
            oc = o - mu
            var = jnp.mean(oc * oc, -1, keepdims=True)
            on = oc * lax.rsqrt(var + GN_EPS)
            gt = g_ref[b, pl.ds(blk * qb, qb), :].astype(F32)
            y_ref[b, pl.ds(blk * qb, qb), :] = (gt * _sigmoid(gt) * on).astype(BF16)
        if emit_state:
            kf16 = (k * jnp.exp((float(seq - 1) - trow) * lgf)).astype(BF16)
            kb16 = (k * jnp.exp(trow * lgb)).astype(BF16)
            sf = lax.dot_general(kf16, v16, (((0,), (0,)), ((), ())), preferred_element_type=F32)
            sb = lax.dot_general(kb16, v16, (((0,), (0,)), ((), ())), preferred_element_type=F32)
            if has_state:
                sf = sf + jnp.exp(float(seq) * lgf) * s0_ref[b, 0]
                sb = sb + jnp.exp(float(seq) * lgb) * s0_ref[b, 1]
            st_ref[b, 0] = sf
            st_ref[b, 1] = sb


def _ret_mixer(proj3, seq0, nseq, bg, decay, rope_tabs=None, s0=None, emit_state=False):
    seq = proj3.shape[1]
    off = seq0 // bg
    qb = min(seq, 256)
    rope = rope_tabs is not None
    has_state = s0 is not None
    kern = functools.partial(_ret_kernel, bg=bg, seq=seq, rope=rope, has_state=has_state,
                             emit_state=emit_state, qb=qb)

    def col(base):
        return pl.BlockSpec((bg, seq, LANES), lambda h, b: (b + off, 0, base + h))

    st_spec = pl.BlockSpec((bg, None, 2, None, RET_DH, RET_DH), lambda h, b: (b, 0, 0, h, 0, 0))
    in_specs = [col(2 * LRU_BLOCKS), col(2 * LRU_BLOCKS + RET_HEADS), col(2 * LRU_BLOCKS + 2 * RET_HEADS),
                col(2 * LRU_BLOCKS + 3 * RET_HEADS), pl.BlockSpec((1, 2, LANES), lambda h, b: (h, 0, 0))]
    args = [proj3, proj3, proj3, proj3, decay]
    if rope:
        in_specs += [pl.BlockSpec((seq, LANES), lambda h, b: (0, 0))] * 2
        args += list(rope_tabs)
    if has_state:
        in_specs.append(st_spec)
        args.append(s0)
    out_specs = [pl.BlockSpec((bg, seq, LANES), lambda h, b: (b, 0, h))]
    out_shape = [jax.ShapeDtypeStruct((nseq, seq, RET_HEADS * RET_DH), BF16)]
    if emit_state:
        out_specs.append(st_spec)
        out_shape.append(jax.ShapeDtypeStruct((nseq, 1, 2, RET_HEADS, RET_DH, RET_DH), F32))
    return pl.pallas_call(
        kern,
        grid=(RET_HEADS, nseq // bg),
        in_specs=in_specs,
        out_specs=out_specs,
        out_shape=out_shape,
        scratch_shapes=[pltpu.VMEM((seq, seq), F32)],
        compiler_params=_params(("arbitrary", "arbitrary")),
        name="ret_mixer",
    )(*args)


def _rope_tables(seq):
    nf = RET_DH // 4
    freqs = (np.float32(ROPE_BASE) ** (-np.arange(nf, dtype=np.float32) / np.float32(nf))).astype(np.float32)
    t = np.arange(seq)
    row = (t // GRID_W).astype(np.float32)[:, None] * freqs[None, :]
    colp = (t % GRID_W).astype(np.float32)[:, None] * freqs[None, :]
    cos = np.concatenate([np.cos(row), np.cos(row), np.cos(colp), np.cos(colp)], -1)
    sin = np.concatenate([-np.sin(row), np.sin(row), -np.sin(colp), np.sin(colp)], -1)
    return jnp.asarray(cos, F32), jnp.asarray(sin, F32)


def _outproj_kernel(xp_ref, xs_ref, ylp_ref, yrp_ref, yls_ref, yrs_ref, wa_ref, wb_ref, gate_ref, z_ref,
                    *, n_p_tiles, tiles_per_seq):
    i = pl.program_id(0)
    is_p = i < n_p_tiles
    row = jnp.where(is_p, 0, 1 + (i - n_p_tiles) // tiles_per_seq)

    def mix(x_ref, yl_ref, yr_ref):
        m = jnp.dot(yl_ref[...], wa_ref[...], preferred_element_type=F32)
        m = m + jnp.dot(yr_ref[...], wb_ref[...], preferred_element_type=F32)
        z_ref[...] = ALPHA * x_ref[...] + gate_ref[pl.ds(row, 1), :] * m

    @pl.when(is_p)
    def _():
        mix(xp_ref, ylp_ref, yrp_ref)

    @pl.when(jnp.logical_not(is_p))
    def _():
        mix(xs_ref, yls_ref, yrs_ref)


def _out_projection(xp, xs, ylp, yrp, yls, yrs, w_out_bf16, gate1, dec_seq):
    tp, ts = xp.shape[0], xs.shape[0]
    tm, tn = TM_OUT, TN_OUT
    n_p, n_s = tp // tm, ts // tm
    half = LRU_WIDTH
    kern = functools.partial(_outproj_kernel, n_p_tiles=n_p, tiles_per_seq=dec_seq // tm)
    p_idx = lambda i, j: (jnp.minimum(i, n_p - 1), 0)
    s_idx = lambda i, j: (jnp.maximum(i - n_p, 0), 0)
    once = pl.Buffered(1)
    return pl.pallas_call(
        kern,
        grid=(n_p + n_s, D_MODEL // tn),
        in_specs=[pl.BlockSpec((tm, tn), lambda i, j: (jnp.minimum(i, n_p - 1), j)),
                  pl.BlockSpec((tm, tn), lambda i, j: (jnp.maximum(i - n_p, 0), j)),
                  pl.BlockSpec((tm, half), p_idx), pl.BlockSpec((tm, half), p_idx),
                  pl.BlockSpec((tm, half), s_idx, pipeline_mode=once),
                  pl.BlockSpec((tm, half), s_idx, pipeline_mode=once),
                  pl.BlockSpec((half, tn), lambda i, j: (0, j)),
                  pl.BlockSpec((half, tn), lambda i, j: (1, j)),
                  pl.BlockSpec((8, tn), lambda i, j: (0, j))],
        out_specs=pl.BlockSpec((tm, tn), lambda i, j: (i, j)),
        out_shape=jax.ShapeDtypeStruct((tp + ts, D_MODEL), F32),
        compiler_params=_params(("arbitrary", "arbitrary")),
        name="out_projection",
    )(xp, xs, ylp, yrp, yls, yrs, w_out_bf16, w_out_bf16, gate1)


def _norm_router_kernel(z_ref, lng_ref, lnb_ref, shift_ref, scale_ref, wr_ref, br_ref, h2_ref, lg_ref, x1_scr,
                        *, n_p_tiles, tiles_per_seq, tn):
    i = pl.program_id(0)
    row = jnp.where(i < n_p_tiles, 0, 1 + (i - n_p_tiles) // tiles_per_seq)
    tm = z_ref.shape[0]
    nj = D_MODEL // tn
    inv_d = 1.0 / D_MODEL
    s1 = jnp.zeros((tm, 1), F32)
    for c in range(nj):
        s1 = s1 + jnp.sum(z_ref[:, c * tn:(c + 1) * tn], -1, keepdims=True)
    mu = s1 * inv_d
    s2 = jnp.zeros((tm, 1), F32)
    for c in range(nj):
        zc = z_ref[:, c * tn:(c + 1) * tn] - mu
        s2 = s2 + jnp.sum(zc * zc, -1, keepdims=True)
    rstd = lax.rsqrt(s2 * inv_d + LN_EPS)
    t1 = jnp.zeros((tm, 1), F32)
    for c in range(nj):
        cs = slice(c * tn, (c + 1) * tn)
        x1 = (z_ref[:, cs] - mu) * rstd * lng_ref[:, cs] + lnb_ref[:, cs]
        x1_scr[c] = x1
        t1 = t1 + jnp.sum(x1, -1, keepdims=True)
    mu2 = t1 * inv_d
    t2 = jnp.zeros((tm, 1), F32)
    for c in range(nj):
        xc = x1_scr[c] - mu2
        t2 = t2 + jnp.sum(xc * xc, -1, keepdims=True)
    rstd2 = lax.rsqrt(t2 * inv_d + LN_EPS)
    logits = jnp.zeros((tm, LANES), F32) + br_ref[...]

    def h2_chunk(c):
        cs = slice(c * tn, (c + 1) * tn)
        return (x1_scr[c] - mu2) * rstd2 * (1.0 + scale_ref[pl.ds(row, 1), cs]) + shift_ref[pl.ds(row, 1), cs]

    for c in range(nj // 2):
        lo, hi = h2_chunk(c), h2_chunk(c + nj // 2)
        h2_ref[:, c * tn:(c + 1) * tn] = _pack_bf16_pair(lo, hi)
        logits = logits + jnp.dot(lo.astype(BF16), wr_ref[c * tn:(c + 1) * tn, :], preferred_element_type=F32)
        logits = logits + jnp.dot(hi.astype(BF16), wr_ref[HALF_D + c * tn:HALF_D + (c + 1) * tn, :],
                                  preferred_element_type=F32)
    lg_ref[...] = logits


def _norm_router(z, n_prompt, ln_g, ln_b, shift2, scale2, w_router, b_router, dec_seq):
    t = z.shape[0]
    tm, tn = TM_NORM, 512
    full = lambda i: (0, 0)
    kern = functools.partial(_norm_router_kernel, n_p_tiles=n_prompt // tm, tiles_per_seq=dec_seq // tm, tn=tn)
    return pl.pallas_call(
        kern,
        grid=(t // tm,),
        in_specs=[pl.BlockSpec((tm, D_MODEL), lambda i: (i, 0)),
                  pl.BlockSpec((1, D_MODEL), full), pl.BlockSpec((1, D_MODEL), full),
                  pl.BlockSpec((8, D_MODEL), full), pl.BlockSpec((8, D_MODEL), full),
                  pl.BlockSpec((D_MODEL, LANES), full), pl.BlockSpec((1, LANES), full)],
        out_specs=[pl.BlockSpec((tm, HALF_D), lambda i: (i, 0)),
                   pl.BlockSpec((tm, LANES), lambda i: (i, 0))],
        out_shape=[jax.ShapeDtypeStruct((t, HALF_D), jnp.uint32),
                   jax.ShapeDtypeStruct((t, LANES), F32)],
        scratch_shapes=[pltpu.VMEM((D_MODEL // tn, tm, tn), F32)],
        compiler_params=_params(("arbitrary",)),
        name="norm_router",
    )(z, ln_g, ln_b, shift2, scale2, w_router, b_router)


def _route(logits):
    lg = logits[:, :N_GROUPS]
    le = logits[:, N_GROUPS:N_GROUPS + N_EXPERTS].reshape(-1, N_GROUPS, EXPERTS_PER_GROUP)
    pg = jax.nn.softmax(lg, -1)
    g_sel = jnp.argmax(lg, -1)
    p_sel = jnp.take_along_axis(pg, g_sel[:, None], 1)[:, 0]
    le_sel = jnp.take_along_axis(le, g_sel[:, None, None], 1)[:, 0]
    top_v, top_i = lax.top_k(le_sel, 2)
    weight = p_sel[:, None] * jax.nn.softmax(top_v, -1)
    expert = (g_sel[:, None] * EXPERTS_PER_GROUP + top_i).astype(jnp.int32)
    return expert, weight


def _dispatch_plan(expert, n_sub, n_super):
    i32 = jnp.int32
    flat_e = expert.reshape(-1)
    n_assign = flat_e.shape[0]
    ids = jnp.arange(N_EXPERTS, dtype=i32)
    onehot = (flat_e[:, None] == ids[None, :]).astype(i32)
    csum = jnp.cumsum(onehot, 0)
    counts = csum[-1]
    rank = jnp.sum(onehot * (csum - 1), 1)
    nb = (counts + SUB_ROWS - 1) // SUB_ROWS
    sub_end = jnp.cumsum(nb)
    sub_start = sub_end - nb
    dest = (jnp.sum(onehot * (sub_start * SUB_ROWS)[None, :], 1) + rank).astype(i32)
    pad_start = (sub_start * SUB_ROWS + counts).astype(i32)
    pad_len = (nb * SUB_ROWS - counts).astype(i32)
    tail = jnp.stack([sub_end[-1], n_sub - sub_end[-1]]).astype(i32)

    nsup = (nb + SUPER - 1) // SUPER
    sup_end = jnp.cumsum(nsup)
    sup_start = sup_end - nsup
    n_used = sup_end[-1]
    s = jnp.arange(n_super, dtype=i32)
    used = s < n_used
    last_exp = jnp.max(jnp.where(counts > 0, ids, 0))
    e_s = jnp.where(used, jnp.minimum(jnp.searchsorted(sup_end, s, side="right").astype(i32), N_EXPERTS - 1), last_exp)
    local = s - sup_start[e_s]
    first_sub = sub_start[e_s] + SUPER * local
    n_comp = jnp.where(used, jnp.clip(nb[e_s] - SUPER * local, 0, SUPER), 0).astype(i32)
    zero_first = sub_end[-1] + SUPER * (s - n_used)
    n_zero = jnp.where(used, 0, jnp.clip(n_sub - zero_first, 0, SUPER)).astype(i32)
    out_sub = jnp.where(used, first_sub, jnp.minimum(zero_first, n_sub - 1)).astype(i32)
    k = jnp.arange(SUPER, dtype=i32)
    x_sub_used = first_sub[:, None] + jnp.minimum(k[None, :], jnp.maximum(n_comp - 1, 0)[:, None])
    x_sub_last = x_sub_used[jnp.maximum(n_used - 1, 0)]
    x_sub = jnp.where(used[:, None], x_sub_used, x_sub_last[None, :]).astype(i32).reshape(-1)
    return dest, (pad_start, pad_len, tail), (e_s.astype(i32), n_comp, n_zero, out_sub, x_sub)


_PAD_PIECES = (128, 64, 32, 16, 8)


def _dispatch_kernel(dest_ref, pad_start_ref, pad_len_ref, tail_ref, h_ref, xs_hbm, zeros, sem, zsem):
    i = pl.program_id(0)
    tt = h_ref.shape[0]
    base = i * tt * 2

    def row_copy(r, k):
        return pltpu.make_async_copy(h_ref.at[pl.ds(r, 1)], xs_hbm.at[pl.ds(dest_ref[base + 2 * r + k], 1)], sem)

    def issue(g, c):
        r0 = pl.multiple_of(g * ISSUE_GROUP, ISSUE_GROUP)
        for k in range(ISSUE_GROUP):
            row_copy(r0 + k, 0).start(priority=0)
            row_copy(r0 + k, 1).start(priority=1)
        return c

    lax.fori_loop(0, tt // ISSUE_GROUP, issue, 0)

    def zero_fill(act):
        def pad(e, c):
            start = pad_start_ref[e]
            length = pad_len_ref[e]
            head = length & 7

            def head_row(r, cc):
                act(pltpu.make_async_copy(zeros.at[pl.ds(0, 1)], xs_hbm.at[pl.ds(start + r, 1)], zsem))
                return cc

            lax.fori_loop(0, head, head_row, 0)
            pos = start + head
            for piece in _PAD_PIECES:
                on = (length & piece) != 0

                @pl.when(on)
                def _():
                    rows = pl.ds(pl.multiple_of(pos, 8), piece)
                    act(pltpu.make_async_copy(zeros.at[pl.ds(0, piece)], xs_hbm.at[rows], zsem))

                pos = pos + jnp.where(on, piece, 0)
            return c

        lax.fori_loop(0, N_EXPERTS, pad, 0)

        def tail(q, c):
            row0 = pl.multiple_of((tail_ref[0] + q) * SUB_ROWS, SUB_ROWS)
            act(pltpu.make_async_copy(zeros, xs_hbm.at[pl.ds(row0, SUB_ROWS)], zsem))
            return c

        lax.fori_loop(0, tail_ref[1], tail, 0)

    @pl.when(i == 0)
    def _():
        zeros[...] = jnp.zeros_like(zeros)
        zero_fill(lambda cp: cp.start())
        zero_fill(lambda cp: cp.wait())

    def drain(r, c):
        row_copy(r, 0).wait()
        row_copy(r, 1).wait()
        return c

    lax.fori_loop(0, tt, drain, 0, unroll=4)


def _dispatch(dest, pad_start, pad_len, tail, h2_packed, n_rows):
    t = h2_packed.shape[0]
    grid_spec = pltpu.PrefetchScalarGridSpec(
        num_scalar_prefetch=4,
        grid=(t // DISPATCH_TOKENS,),
        in_specs=[pl.BlockSpec((DISPATCH_TOKENS, HALF_D), lambda i, d, ps, plen, tl: (i, 0))],
        out_specs=pl.BlockSpec(memory_space=pl.ANY),
        scratch_shapes=[pltpu.VMEM((SUB_ROWS, HALF_D), jnp.uint32),
                        pltpu.SemaphoreType.DMA(()), pltpu.SemaphoreType.DMA(())],
    )
    return pl.pallas_call(
        _dispatch_kernel,
        grid_spec=grid_spec,
        out_shape=jax.ShapeDtypeStruct((n_rows, HALF_D), jnp.uint32),
        compiler_params=_params(("arbitrary",)),
        name="dispatch",
    )(dest, pad_start, pad_len, tail, h2_packed)


def _expert_kernel(exp_ref, nc_ref, nz_ref, osub_ref, xsub_ref, x0_ref, x1_ref, x2_ref, x3_ref,
                   wg_ref, wu_ref, wd_ref, y_hbm, a_scr, ytile, sem, *, nf, nd):
    s = pl.program_id(0)
    t = pl.program_id(1)
    n_comp = nc_ref[s]
    n_out = n_comp + nz_ref[s]
    x_refs = (x0_ref, x1_ref, x2_ref, x3_ref)

    @pl.when(jnp.logical_and(t < nf, n_comp > 0))
    def _():
        wg16 = wg_ref[...].astype(BF16)
        wu16 = wu_ref[...].astype(BF16)

        def up(k):
            x = _unpack_bf16_pair(x_refs[k][...])
            g = jnp.dot(x, wg16, preferred_element_type=F32)
            u = jnp.dot(x, wu16, preferred_element_type=F32)
            a_scr[t, k * SUB_ROWS:(k + 1) * SUB_ROWS, :] = (g * _sigmoid(g) * u).astype(BF16)

        up(0)
        for k in range(1, SUPER):
            pl.when(k < n_comp)(functools.partial(up, k))

    @pl.when(t >= nf)
    def _():
        j = t - nf
        slot = j % 2

        @pl.when(n_comp > 0)
        def _():
            wd16 = wd_ref[...].astype(BF16)

            def down(k):
                rows = slice(k * SUB_ROWS, (k + 1) * SUB_ROWS)
                a = jnp.concatenate([a_scr[f, rows, :] for f in range(nf)], axis=1)
                acc = jnp.dot(a, wd16, preferred_element_type=F32)
                ytile[slot, rows, :] = _pack_bf16_pair(acc[:, :TN_DOWN // 2], acc[:, TN_DOWN // 2:])

            down(0)
            for k in range(1, SUPER):
                pl.when(k < n_comp)(functools.partial(down, k))

        @pl.when(n_comp == 0)
        def _():
            ytile[slot] = jnp.zeros(ytile.shape[1:], jnp.uint32)

        def out_copy(sl, k, jj):
            dst_rows = pl.ds(pl.multiple_of((osub_ref[s] + k) * SUB_ROWS, SUB_ROWS), SUB_ROWS)
            dst_cols = pl.ds(pl.multiple_of(jj * (TN_DOWN // 2), TN_DOWN // 2), TN_DOWN // 2)
            return pltpu.make_async_copy(ytile.at[sl, pl.ds(k * SUB_ROWS, SUB_ROWS), :],
                                         y_hbm.at[dst_rows, dst_cols], sem)

        for k in range(SUPER):
            @pl.when(jnp.logical_and(j > 0, k < n_out))
            def _():
                out_copy(1 - slot, k, j - 1).wait()
        for k in range(SUPER):
            @pl.when(k < n_out)
            def _():
                out_copy(slot, k, j).start()
        for k in range(SUPER):
            @pl.when(jnp.logical_and(j == nd - 1, k < n_out))
            def _():
                out_copy(slot, k, j).wait()


def _experts(xs, sup_exp, n_comp, n_zero, out_sub, x_sub, w_gate, w_up, w_down):
    n_super = sup_exp.shape[0]
    n_rows = xs.shape[0]
    nf = D_EXPERT // F_CHUNK
    nd = D_MODEL // TN_DOWN

    def nxt(s, nc):
        s2 = jnp.minimum(s + 1, n_super - 1)
        return s2, jnp.logical_and(nc[s] > 0, nc[s2] > 0)

    def up_idx(s, t, e, nc, nz, osub, xsub):
        s2, has_next = nxt(s, nc)
        ahead = jnp.logical_and(t >= nf, has_next)
        chunk = jnp.where(nc[s] > 0, jnp.minimum(t, nf - 1), nf - 1)
        return (jnp.where(ahead, e[s2], e[s]), 0, jnp.where(ahead, 0, chunk))

    def down_idx(s, t, e, nc, nz, osub, xsub):
        sp = jnp.maximum(s - 1, 0)
        behind = jnp.logical_and(jnp.logical_and(t == 0, s > 0), nc[s] > 0)
        chunk = jnp.where(nc[s] > 0, jnp.maximum(t - nf, 0), nd - 1)
        return (jnp.where(behind, e[sp], e[s]), 0, jnp.where(behind, nd - 1, chunk))

    def x_spec(k):
        def idx(s, t, e, nc, nz, osub, xsub):
            s2, has_next = nxt(s, nc)
            return (xsub[jnp.where(jnp.logical_and(t >= nf, has_next), s2, s) * SUPER + k], 0)

        return pl.BlockSpec((SUB_ROWS, HALF_D), idx)

    grid_spec = pltpu.PrefetchScalarGridSpec(
        num_scalar_prefetch=5,
        grid=(n_super, nf + nd),
        in_specs=[x_spec(0), x_spec(1), x_spec(2), x_spec(3),
                  pl.BlockSpec((None, D_MODEL, F_CHUNK), up_idx),
                  pl.BlockSpec((None, D_MODEL, F_CHUNK), up_idx),
                  pl.BlockSpec((None, D_EXPERT, TN_DOWN), down_idx)],
        out_specs=pl.BlockSpec(memory_space=pl.ANY),
        scratch_shapes=[pltpu.VMEM((nf, SUPER * SUB_ROWS, F_CHUNK), BF16),
                        pltpu.VMEM((2, SUPER * SUB_ROWS, TN_DOWN // 2), jnp.uint32),
                        pltpu.SemaphoreType.DMA(())],
    )
    return pl.pallas_call(
        functools.partial(_expert_kernel, nf=nf, nd=nd),
        grid_spec=grid_spec,
        out_shape=jax.ShapeDtypeStruct((n_rows, HALF_D), jnp.uint32),
        compiler_params=_params(("arbitrary", "arbitrary"), 60 * 1024 * 1024),
        name="experts",
    )(sup_exp, n_comp, n_zero, out_sub, x_sub, xs, xs, xs, xs, w_gate, w_up, w_down)


def _final_kernel(dest_ref, y_hbm, z_ref, wt_ref, gate_ref, g1_ref, b1_ref, g2_ref, b2_ref,
                  op_ref, os_ref, ybuf, sem, *, n_p_tiles, tiles_per_seq):
    i = pl.program_id(0)
    n_tiles = pl.num_programs(0)
    tm = z_ref.shape[0]
    slot = i % 2

    def row_copy(src_row, sl, k, r):
        return pltpu.make_async_copy(y_hbm.at[pl.ds(src_row, 1)], ybuf.at[sl, k, pl.ds(r, 1)], sem.at[sl])

    def gather(tile, sl):
        base = tile * tm * 2

        def issue(g, c):
            r0 = pl.multiple_of(g * ISSUE_GROUP, ISSUE_GROUP)
            for k in range(ISSUE_GROUP):
                row_copy(dest_ref[base + 2 * (r0 + k)], sl, 0, r0 + k).start(priority=0)
                row_copy(dest_ref[base + 2 * (r0 + k) + 1], sl, 1, r0 + k).start(priority=1)
            return c

        lax.fori_loop(0, tm // ISSUE_GROUP, issue, 0)

    @pl.when(i == 0)
    def _():
        gather(0, 0)

    @pl.when(i + 1 < n_tiles)
    def _():
        gather(i + 1, 1 - slot)

    def drain(r, c):
        row_copy(0, slot, 0, r).wait()
        row_copy(0, slot, 1, r).wait()
        return c

    lax.fori_loop(0, tm, drain, 0, unroll=4)

    is_p = i < n_p_tiles
    row = jnp.where(is_p, 0, 1 + (i - n_p_tiles) // tiles_per_seq)
    wt = wt_ref[...]
    w0, w1 = wt[:, 0:1], wt[:, 1:2]
    half_tile = TN_DOWN // 2
    pieces = []
    for j in range(D_MODEL // TN_DOWN):
        y0 = ybuf[slot, 0, :, j * half_tile:(j + 1) * half_tile]
        y1 = ybuf[slot, 1, :, j * half_tile:(j + 1) * half_tile]
        pieces.append(w0 * pltpu.bitcast(y0 << 16, F32) + w1 * pltpu.bitcast(y1 << 16, F32))
        pieces.append(w0 * pltpu.bitcast(y0 & jnp.uint32(0xFFFF0000), F32)
                      + w1 * pltpu.bitcast(y1 & jnp.uint32(0xFFFF0000), F32))
    f = jnp.concatenate(pieces, axis=1)
    x1 = _ln(z_ref[...]) * g1_ref[...] + b1_ref[...]
    out = _ln(ALPHA * x1 + gate_ref[pl.ds(row, 1), :] * f) * g2_ref[...] + b2_ref[...]

    @pl.when(is_p)
    def _():
        op_ref[...] = out

    @pl.when(jnp.logical_not(is_p))
    def _():
        os_ref[...] = out


def _combine(dest, y_rows, z, weight, gate2, g1, b1, g2, b2, n_prompt, dec_seq):
    t = z.shape[0]
    tm = TM_FIN
    n_p = n_prompt // tm
    n_s = (t - n_prompt) // tm
    full = lambda i, d: (0, 0)
    grid_spec = pltpu.PrefetchScalarGridSpec(
        num_scalar_prefetch=1,
        grid=(n_p + n_s,),
        in_specs=[pl.BlockSpec(memory_space=pl.ANY),
                  pl.BlockSpec((tm, D_MODEL), lambda i, d: (i, 0)),
                  pl.BlockSpec((tm, 2), lambda i, d: (i, 0)),
                  pl.BlockSpec((8, D_MODEL), full),
                  pl.BlockSpec((1, D_MODEL), full), pl.BlockSpec((1, D_MODEL), full),
                  pl.BlockSpec((1, D_MODEL), full), pl.BlockSpec((1, D_MODEL), full)],
        out_specs=[pl.BlockSpec((tm, D_MODEL), lambda i, d: (jnp.minimum(i, n_p - 1), 0)),
                   pl.BlockSpec((tm, D_MODEL), lambda i, d: (jnp.maximum(i - n_p, 0), 0))],
        scratch_shapes=[pltpu.VMEM((2, 2, tm, HALF_D), jnp.uint32), pltpu.SemaphoreType.DMA((2,))],
    )
    return pl.pallas_call(
        functools.partial(_final_kernel, n_p_tiles=n_p, tiles_per_seq=dec_seq // tm),
        grid_spec=grid_spec,
        out_shape=[jax.ShapeDtypeStruct((n_prompt, D_MODEL), F32),
                   jax.ShapeDtypeStruct((t - n_prompt, D_MODEL), F32)],
        compiler_params=_params(("arbitrary",)),
        name="combine",
    )(dest, y_rows, z, weight, gate2, g1, b1, g2, b2)


def kernel(x_prompt, x_sample, state_lru, state_ret, c, c_ctx, w_mod, b_mod, w_in, conv_w, conv_b, lru_wa, lru_ba,
           lru_wx, lru_bx, lru_lam, ret_decay, w_out, ln1_g, ln1_b, router_g, router_g_b, router_e, router_e_b,
           w_gate, w_up, w_down, ln2_g, ln2_b):
    assert w_in.shape[0] == 1, "single trunk layer"
    nb, seq, d = x_prompt.shape
    nbs, dec_seq, _ = x_sample.shape
    tp, ts = nb * seq, nbs * dec_seq
    assert tp % dec_seq == 0 and d == D_MODEL

    cond = jnp.zeros((8, d), F32).at[0].set(c_ctx).at[1:1 + nbs].set(c)
    mod = _modulation(cond, w_mod[0], b_mod[0][None, :])
    shift1, scale1, gate1, shift2, scale2, gate2 = [mod[:, k * d:(k + 1) * d] for k in range(6)]

    xp = x_prompt.reshape(tp, d)
    xs = x_sample.reshape(ts, d)
    proj = _in_projection(_ln_modulate(xp, xs, shift1, scale1, dec_seq), w_in[0].astype(BF16))

    w_gates = (0.5 * jnp.concatenate([lru_wa[0, 0], lru_wx[0, 0], lru_wa[0, 1], lru_wx[0, 1]], -1)).astype(BF16)
    pb = jnp.stack([0.5 * lru_ba[0, 0], 0.5 * lru_bx[0, 0], lru_lam[0, 0],
                    0.5 * lru_ba[0, 1], 0.5 * lru_bx[0, 1], lru_lam[0, 1]], 0)
    proj_p = proj.reshape((tp + ts) // seq, seq, IN_COLS)
    proj_s = proj.reshape((tp + ts) // dec_seq, dec_seq, IN_COLS)
    ylp, st_lru = _lru_mixer(proj_p, 0, nb, 16, 1, conv_w[0], conv_b[0][None, :], w_gates, pb,
                             jnp.zeros((2, nb, LRU_WIDTH), F32))
    yls, _ = _lru_mixer(proj_s, tp // dec_seq, nbs, nbs, 4, conv_w[0], conv_b[0][None, :], w_gates, pb,
                        jnp.swapaxes(state_lru[:, 0], 0, 1))

    decay = jnp.broadcast_to(ret_decay[0].T[:, :, None], (RET_HEADS, 2, LANES))
    yrp, st_ret = _ret_mixer(proj_p, 0, nb, 8, decay, emit_state=True)
    (yrs,) = _ret_mixer(proj_s, tp // dec_seq, nbs, 1, decay, rope_tabs=_rope_tables(dec_seq), s0=state_ret)

    w_router = jnp.concatenate(
        [router_g[0], jnp.transpose(router_e[0], (1, 0, 2)).reshape(d, N_EXPERTS),
         jnp.zeros((d, LANES - N_GROUPS - N_EXPERTS), F32)], -1).astype(BF16)
    b_router = jnp.concatenate([router_g_b[0], router_e_b[0].reshape(-1),
                                jnp.zeros((LANES - N_GROUPS - N_EXPERTS,), F32)])[None, :]
    z = _out_projection(xp, xs, ylp.reshape(tp, -1), yrp.reshape(tp, -1), yls.reshape(ts, -1), yrs.reshape(ts, -1),
                        w_out[0].astype(BF16), gate1, dec_seq)
    h2, logits = _norm_router(z, tp, ln1_g, ln1_b, shift2, scale2, w_router, b_router, dec_seq)

    expert, weight = _route(logits)
    n_assign = 2 * (tp + ts)
    n_sub = -(-(n_assign + N_EXPERTS * (SUB_ROWS - 1)) // SUB_ROWS)
    n_super = N_EXPERTS + -(-n_assign // (SUB_ROWS * SUPER))
    assert SUPER * n_super >= n_sub + (SUPER - 1) * N_EXPERTS
    dest, fill_plan, super_plan = _dispatch_plan(expert, n_sub, n_super)
    xs_rows = _dispatch(dest, *fill_plan, h2, n_sub * SUB_ROWS)
    y_rows = _experts(xs_rows, *super_plan, w_gate[0], w_up[0], w_down[0])
    y_p, y_s = _combine(dest, y_rows, z, weight, gate2, ln1_g, ln1_b, ln2_g, ln2_b, tp, dec_seq)

    new_state_lru = jnp.swapaxes(st_lru, 0, 1)[:, None]
    return (y_p.reshape(nb, seq, d), y_s.reshape(nbs, dec_seq, d), new_state_lru, st_ret)
```

```python
import functools

import jax
import jax.numpy as jnp
import numpy as np
from jax import lax
from jax.experimental import pallas as pl
from jax.experimental.pallas import tpu as pltpu

F32 = jnp.float32
BF16 = jnp.bfloat16

D_MODEL = 4096
LRU_WIDTH = 2048
LRU_BLOCKS = 16
LANES = 128
RET_HEADS = 16
RET_DH = 128
IN_COLS = 12288
GRID_W = 64
ROPE_BASE = 10000.0
LRU_C = 8.0
N_GROUPS = 4
EXPERTS_PER_GROUP = 8
N_EXPERTS = 32
D_EXPERT = 1024
LN_EPS = 1e-6
GN_EPS = 1e-5
ALPHA = 2.0 ** 0.25

VMEM_LIMIT = 56 * 1024 * 1024
SCAN_PAD = 8

TM_LN = 512
TM_PROJ = 1024
TN_IN = 1024
TN_OUT = 512
TM_OUT = 1024
TM_NORM = 512
SUB_ROWS = 256
SUPER = 4
F_CHUNK = 256
TN_DOWN = 1024
DISPATCH_TOKENS = 512
ISSUE_GROUP = 16
TM_FIN = 256
HALF_D = D_MODEL // 2


def _params(sem, vmem_limit=VMEM_LIMIT):
    return pltpu.CompilerParams(dimension_semantics=sem, vmem_limit_bytes=vmem_limit)


def _pack_bf16_pair(lo, hi):
    lo_bits = pltpu.bitcast(lo.astype(BF16).astype(F32), jnp.uint32) >> 16
    hi_bits = pltpu.bitcast(hi.astype(BF16).astype(F32), jnp.uint32) & jnp.uint32(0xFFFF0000)
    return lo_bits | hi_bits


def _unpack_bf16_pair(words):
    lo = pltpu.bitcast(words << 16, F32).astype(BF16)
    hi = pltpu.bitcast(words & jnp.uint32(0xFFFF0000), F32).astype(BF16)
    return jnp.concatenate([lo, hi], axis=1)


def _sigmoid(x):
    return 0.5 * jnp.tanh(0.5 * x) + 0.5


def _softplus(x):
    return jnp.maximum(x, 0.0) + jnp.log1p(jnp.exp(-jnp.abs(x)))


def _ln(x):
    mu = jnp.mean(x, -1, keepdims=True)
    xc = x - mu
    var = jnp.mean(xc * xc, -1, keepdims=True)
    return xc * lax.rsqrt(var + LN_EPS)


def _mod_kernel(cond_ref, w_ref, b_ref, o_ref):
    c = cond_ref[...]
    s = (c * _sigmoid(c)).astype(BF16)
    o_ref[...] = jnp.dot(s, w_ref[...].astype(BF16), preferred_element_type=F32) + b_ref[...]


def _modulation(cond, w_mod, b_mod):
    tn = 512
    n = w_mod.shape[1]
    return pl.pallas_call(
        _mod_kernel,
        grid=(n // tn,),
        in_specs=[pl.BlockSpec((8, D_MODEL), lambda j: (0, 0)),
                  pl.BlockSpec((D_MODEL, tn), lambda j: (0, j)),
                  pl.BlockSpec((1, tn), lambda j: (0, j))],
        out_specs=pl.BlockSpec((8, tn), lambda j: (0, j)),
        out_shape=jax.ShapeDtypeStruct((8, n), F32),
        compiler_params=_params(("arbitrary",)),
        name="modulation",
    )(cond, w_mod, b_mod)


def _ln_mod_kernel(xp_ref, xs_ref, shift_ref, scale_ref, h_ref, *, n_p_tiles, tiles_per_seq):
    i = pl.program_id(0)

    def fill(x_ref, row):
        h = _ln(x_ref[...]) * (1.0 + scale_ref[pl.ds(row, 1), :]) + shift_ref[pl.ds(row, 1), :]
        h_ref[...] = h.astype(BF16)

    @pl.when(i < n_p_tiles)
    def _():
        fill(xp_ref, 0)

    @pl.when(i >= n_p_tiles)
    def _():
        fill(xs_ref, 1 + (i - n_p_tiles) // tiles_per_seq)


def _ln_modulate(xp, xs, shift, scale, dec_seq):
    tp, ts = xp.shape[0], xs.shape[0]
    tm = TM_LN
    n_p, n_s = tp // tm, ts // tm
    kern = functools.partial(_ln_mod_kernel, n_p_tiles=n_p, tiles_per_seq=dec_seq // tm)
    return pl.pallas_call(
        kern,
        grid=(n_p + n_s,),
        in_specs=[pl.BlockSpec((tm, D_MODEL), lambda i: (jnp.minimum(i, n_p - 1), 0)),
                  pl.BlockSpec((tm, D_MODEL), lambda i: (jnp.maximum(i - n_p, 0), 0)),
                  pl.BlockSpec((8, D_MODEL), lambda i: (0, 0)),
                  pl.BlockSpec((8, D_MODEL), lambda i: (0, 0))],
        out_specs=pl.BlockSpec((tm, D_MODEL), lambda i: (i, 0)),
        out_shape=jax.ShapeDtypeStruct((tp + ts, D_MODEL), BF16),
        compiler_params=_params(("arbitrary",)),
        name="ln_modulate",
    )(xp, xs, shift, scale)


def _matmul_kernel(h_ref, w_ref, o_ref):
    o_ref[...] = jnp.dot(h_ref[...], w_ref[...], preferred_element_type=F32).astype(o_ref.dtype)


def _in_projection(h, w_in_bf16):
    t = h.shape[0]
    tm, tn = TM_PROJ, TN_IN
    return pl.pallas_call(
        _matmul_kernel,
        grid=(t // tm, IN_COLS // tn),
        in_specs=[pl.BlockSpec((tm, D_MODEL), lambda i, j: (i, 0)),
                  pl.BlockSpec((D_MODEL, tn), lambda i, j: (0, j))],
        out_specs=pl.BlockSpec((tm, tn), lambda i, j: (i, j)),
        out_shape=jax.ShapeDtypeStruct((t, IN_COLS), BF16),
        compiler_params=_params(("arbitrary", "arbitrary")),
        name="in_projection",
    )(h, w_in_bf16)


def _lru_kernel(x_ref, g_ref, cw_ref, cb_ref, wg_ref, pb_ref, h0_ref, y_ref, st_ref,
                xc_scr, af_scr, uf_scr, ab_scr, ub_scr, *, bg, seq, pitch, ncb):
    rows = lax.broadcasted_iota(jnp.int32, (seq, LANES), 0)
    for c in range(ncb):
        lanes = slice(c * LANES, (c + 1) * LANES)
        w = cw_ref[:, lanes]
        bias = cb_ref[:, lanes]
        for b in range(bg):
            x = x_ref[b, :, lanes].astype(F32)
            xm2 = jnp.where(rows >= 2, pltpu.roll(x, 2, 0), 0.0)
            xm1 = jnp.where(rows >= 1, pltpu.roll(x, 1, 0), 0.0)
            xp1 = jnp.where(rows < seq - 1, pltpu.roll(x, seq - 1, 0), 0.0)
            xc_scr[pl.ds(b * seq, seq), :] = bias + xm2 * w[0:1] + xm1 * w[1:2] + x * w[2:3] + xp1 * w[3:4]

        xc = xc_scr[...]
        gates = jnp.dot(xc.astype(BF16), wg_ref[c], preferred_element_type=F32)
        pb = pb_ref[:, lanes]
        half_xc = 0.5 * xc
        for d, (a_scr, u_scr) in enumerate(((af_scr, uf_scr), (ab_scr, ub_scr))):
            t_r = jnp.tanh(gates[:, (2 * d) * LANES:(2 * d + 1) * LANES] + pb[3 * d:3 * d + 1])
            t_i = jnp.tanh(gates[:, (2 * d + 1) * LANES:(2 * d + 2) * LANES] + pb[3 * d + 1:3 * d + 2])
            c4 = (0.5 * LRU_C) * _softplus(-pb[3 * d + 2:3 * d + 3])
            neg_log_a = c4 * t_r + c4
            a = jnp.exp(-neg_log_a)
            y = jnp.tanh(neg_log_a) * (1.0 + a * a)
            root = jnp.where(y > 0.0, y * lax.rsqrt(y), 0.0)
            u = root * (half_xc * (t_i + 1.0))
            for b in range(bg):
                a_scr[c, pl.ds(b * pitch, seq), :] = a[b * seq:(b + 1) * seq]
                u_scr[c, pl.ds(b * pitch, seq), :] = u[b * seq:(b + 1) * seq]

    def step(t, carry):
        tb = seq - 1 - t
        out = []
        for c in range(ncb):
            hf, hb = carry[2 * c], carry[2 * c + 1]
            hf = af_scr[c, pl.ds(t, bg, stride=pitch), :] * hf + uf_scr[c, pl.ds(t, bg, stride=pitch), :]
            uf_scr[c, pl.ds(t, bg, stride=pitch), :] = hf
            hb = ab_scr[c, pl.ds(tb, bg, stride=pitch), :] * hb + ub_scr[c, pl.ds(tb, bg, stride=pitch), :]
            ub_scr[c, pl.ds(tb, bg, stride=pitch), :] = hb
            out += [hf, hb]
        return tuple(out)

    init = []
    for c in range(ncb):
        init += [h0_ref[0, :, c * LANES:(c + 1) * LANES], h0_ref[1, :, c * LANES:(c + 1) * LANES]]
    last = lax.fori_loop(0, seq, step, tuple(init), unroll=8)
    for c in range(ncb):
        lanes = slice(c * LANES, (c + 1) * LANES)
        st_ref[0, :, lanes] = last[2 * c]
        st_ref[1, :, lanes] = last[2 * c + 1]
        for b in range(bg):
            hs = uf_scr[c, pl.ds(b * pitch, seq), :] + ub_scr[c, pl.ds(b * pitch, seq), :]
            y_ref[b, :, lanes] = (jax.nn.gelu(g_ref[b, :, lanes].astype(F32)) * hs).astype(BF16)


def _lru_mixer(proj3, seq0, nseq, bg, ncb, conv_w, conv_b, w_gates, pb, h0):
    seq = proj3.shape[1]
    pitch = seq + SCAN_PAD
    off = seq0 // bg
    cw = ncb * LANES
    ncol = LRU_WIDTH // cw
    kern = functools.partial(_lru_kernel, bg=bg, seq=seq, pitch=pitch, ncb=ncb)
    return pl.pallas_call(
        kern,
        grid=(nseq // bg, ncol),
        in_specs=[pl.BlockSpec((bg, seq, cw), lambda b, c: (b + off, 0, c)),
                  pl.BlockSpec((bg, seq, cw), lambda b, c: (b + off, 0, ncol + c)),
                  pl.BlockSpec((4, cw), lambda b, c: (0, c)),
                  pl.BlockSpec((1, cw), lambda b, c: (0, c)),
                  pl.BlockSpec((ncb, LANES, 4 * LANES), lambda b, c: (c, 0, 0)),
                  pl.BlockSpec((6, cw), lambda b, c: (0, c)),
                  pl.BlockSpec((2, bg, cw), lambda b, c: (0, b, c))],
        out_specs=[pl.BlockSpec((bg, seq, cw), lambda b, c: (b, 0, c)),
                   pl.BlockSpec((2, bg, cw), lambda b, c: (0, b, c))],
        out_shape=[jax.ShapeDtypeStruct((nseq, seq, LRU_WIDTH), BF16),
                   jax.ShapeDtypeStruct((2, nseq, LRU_WIDTH), F32)],
        scratch_shapes=[pltpu.VMEM((bg * seq, LANES), F32)] + [pltpu.VMEM((ncb, bg * pitch, LANES), F32)] * 4,
        compiler_params=_params(("arbitrary", "arbitrary")),
        name="lru_mixer",
    )(proj3, proj3, conv_w, conv_b, w_gates, pb, h0)


def _rope(x, cos, sin_signed, first_half):
    partner = jnp.where(first_half, pltpu.roll(x, LANES - 32, 1), pltpu.roll(x, 32, 1))
    return x * cos + partner * sin_signed


def _ret_kernel(*refs, bg, seq, rope, has_state, emit_state, qb):
    refs = list(refs)
    q_ref, k_ref, v_ref, g_ref, dec_ref = refs[:5]
    pos = 5
    if rope:
        cos_ref, sin_ref = refs[pos:pos + 2]
        pos += 2
    if has_state:
        s0_ref = refs[pos]
        pos += 1
    y_ref = refs[pos]
    pos += 1
    if emit_state:
        st_ref = refs[pos]
        pos += 1
    mask_scr = refs[pos]

    log_g = -_softplus(-dec_ref[0])
    lgf, lgb = log_g[0:1], log_g[1:2]
    reps = seq // LANES
    lgf_row = jnp.concatenate([lgf] * reps, axis=1)
    lgb_row = jnp.concatenate([lgb] * reps, axis=1)

    @pl.when(pl.program_id(1) == 0)
    def _():
        for blk in range(seq // qb):
            ti = lax.broadcasted_iota(jnp.int32, (qb, seq), 0) + blk * qb
            si = lax.broadcasted_iota(jnp.int32, (qb, seq), 1)
            dist = (ti - si).astype(F32)
            e = jnp.where(dist >= 0, dist * lgf_row, (-dist) * lgb_row)
            mask_scr[pl.ds(blk * qb, qb), :] = jnp.where(dist == 0, 2.0, jnp.exp(e))

    lane = lax.broadcasted_iota(jnp.int32, (seq, LANES), 1)
    first_half = (lane % 64) < 32
    trow = lax.broadcasted_iota(jnp.int32, (seq, LANES), 0).astype(F32)
    for b in range(bg):
        q = q_ref[b].astype(F32)
        k = k_ref[b].astype(F32) * (RET_DH ** -0.5)
        v16 = v_ref[b]
        if rope:
            q = _rope(q, cos_ref[...], sin_ref[...], first_half)
            k = _rope(k, cos_ref[...], sin_ref[...], first_half)
        q16 = q.astype(BF16)
        k16 = k.astype(BF16)
        if has_state:
            qf16 = (q * jnp.exp((trow + 1.0) * lgf)).astype(BF16)
            qb16 = (q * jnp.exp((float(seq) - trow) * lgb)).astype(BF16)
            s0f = s0_ref[b, 0].astype(BF16)
            s0b = s0_ref[b, 1].astype(BF16)
        for blk in range(seq // qb):
            sl = slice(blk * qb, (blk + 1) * qb)
            s = lax.dot_general(q16[sl], k16, (((1,), (1,)), ((), ())), preferred_element_type=F32)
            p = (s * mask_scr[pl.ds(blk * qb, qb), :]).astype(BF16)
            o = jnp.dot(p, v16, preferred_element_type=F32)
            if has_state:
                o = o + jnp.dot(qf16[sl], s0f, preferred_element_type=F32)
                o = o + jnp.dot(qb16[sl], s0b, preferred_element_type=F32)
            mu = jnp.mean(o, -1, keepdims=True)
            oc = o - mu
            var = jnp.mean(oc * oc, -1, keepdims=True)
            on = oc * lax.rsqrt(var + GN_EPS)
            gt = g_ref[b, pl.ds(blk * qb, qb), :].astype(F32)
            y_ref[b, pl.ds(blk * qb, qb), :] = (gt * _sigmoid(gt) * on).astype(BF16)
        if emit_state:
            kf16 = (k * jnp.exp((float(seq - 1) - trow) * lgf)).astype(BF16)
            kb16 = (k * jnp.exp(trow * lgb)).astype(BF16)
            sf = lax.dot_general(kf16, v16, (((0,), (0,)), ((), ())), preferred_element_type=F32)
            sb = lax.dot_general(kb16, v16, (((0,), (0,)), ((), ())), preferred_element_type=F32)
            if has_state:
                sf = sf + jnp.exp(float(seq) * lgf) * s0_ref[b, 0]
                sb = sb + jnp.exp(float(seq) * lgb) * s0_ref[b, 1]
            st_ref[b, 0] = sf
            st_ref[b, 1] = sb


def _ret_mixer(proj3, seq0, nseq, bg, decay, rope_tabs=None, s0=None, emit_state=False):
    seq = proj3.shape[1]
    off = seq0 // bg
    qb = min(seq, 256)
    rope = rope_tabs is not None
    has_state = s0 is not None
    kern = functools.partial(_ret_kernel, bg=bg, seq=seq, rope=rope, has_state=has_state,
                             emit_state=emit_state, qb=qb)

    def col(base):
        return pl.BlockSpec((bg, seq, LANES), lambda h, b: (b + off, 0, base + h))

    st_spec = pl.BlockSpec((bg, None, 2, None, RET_DH, RET_DH), lambda h, b: (b, 0, 0, h, 0, 0))
    in_specs = [col(2 * LRU_BLOCKS), col(2 * LRU_BLOCKS + RET_HEADS), col(2 * LRU_BLOCKS + 2 * RET_HEADS),
                col(2 * LRU_BLOCKS + 3 * RET_HEADS), pl.BlockSpec((1, 2, LANES), lambda h, b: (h, 0, 0))]
    args = [proj3, proj3, proj3, proj3, decay]
    if rope:
        in_specs += [pl.BlockSpec((seq, LANES), lambda h, b: (0, 0))] * 2
        args += list(rope_tabs)
    if has_state:
        in_specs.append(st_spec)
        args.append(s0)
    out_specs = [pl.BlockSpec((bg, seq, LANES), lambda h, b: (b, 0, h))]
    out_shape = [jax.ShapeDtypeStruct((nseq, seq, RET_HEADS * RET_DH), BF16)]
    if emit_state:
        out_specs.append(st_spec)
        out_shape.append(jax.ShapeDtypeStruct((nseq, 1, 2, RET_HEADS, RET_DH, RET_DH), F32))
    return pl.pallas_call(
        kern,
        grid=(RET_HEADS, nseq // bg),
        in_specs=in_specs,
        out_specs=out_specs,
        out_shape=out_shape,
        scratch_shapes=[pltpu.VMEM((seq, seq), F32)],
        compiler_params=_params(("arbitrary", "arbitrary")),
        name="ret_mixer",
    )(*args)


def _rope_tables(seq):
    nf = RET_DH // 4
    freqs = (np.float32(ROPE_BASE) ** (-np.arange(nf, dtype=np.float32) / np.float32(nf))).astype(np.float32)
    t = np.arange(seq)
    row = (t // GRID_W).astype(np.float32)[:, None] * freqs[None, :]
    colp = (t % GRID_W).astype(np.float32)[:, None] * freqs[None, :]
    cos = np.concatenate([np.cos(row), np.cos(row), np.cos(colp), np.cos(colp)], -1)
    sin = np.concatenate([-np.sin(row), np.sin(row), -np.sin(colp), np.sin(colp)], -1)
    return jnp.asarray(cos, F32), jnp.asarray(sin, F32)


def _outproj_kernel(xp_ref, xs_ref, ylp_ref, yrp_ref, yls_ref, yrs_ref, wa_ref, wb_ref, gate_ref, z_ref,
                    *, n_p_tiles, tiles_per_seq):
    i = pl.program_id(0)
    is_p = i < n_p_tiles
    row = jnp.where(is_p, 0, 1 + (i - n_p_tiles) // tiles_per_seq)

    def mix(x_ref, yl_ref, yr_ref):
        m = jnp.dot(yl_ref[...], wa_ref[...], preferred_element_type=F32)
        m = m + jnp.dot(yr_ref[...], wb_ref[...], preferred_element_type=F32)
        z_ref[...] = ALPHA * x_ref[...] + gate_ref[pl.ds(row, 1), :] * m

    @pl.when(is_p)
    def _():
        mix(xp_ref, ylp_ref, yrp_ref)

    @pl.when(jnp.logical_not(is_p))
    def _():
        mix(xs_ref, yls_ref, yrs_ref)


def _out_projection(xp, xs, ylp, yrp, yls, yrs, w_out_bf16, gate1, dec_seq):
    tp, ts = xp.shape[0], xs.shape[0]
    tm, tn = TM_OUT, TN_OUT
    n_p, n_s = tp // tm, ts // tm
    half = LRU_WIDTH
    kern = functools.partial(_outproj_kernel, n_p_tiles=n_p, tiles_per_seq=dec_seq // tm)
    p_idx = lambda i, j: (jnp.minimum(i, n_p - 1), 0)
    s_idx = lambda i, j: (jnp.maximum(i - n_p, 0), 0)
    once = pl.Buffered(1)
    return pl.pallas_call(
        kern,
        grid=(n_p + n_s, D_MODEL // tn),
        in_specs=[pl.BlockSpec((tm, tn), lambda i, j: (jnp.minimum(i, n_p - 1), j)),
                  pl.BlockSpec((tm, tn), lambda i, j: (jnp.maximum(i - n_p, 0), j)),
                  pl.BlockSpec((tm, half), p_idx), pl.BlockSpec((tm, half), p_idx),
                  pl.BlockSpec((tm, half), s_idx, pipeline_mode=once),
                  pl.BlockSpec((tm, half), s_idx, pipeline_mode=once),
                  pl.BlockSpec((half, tn), lambda i, j: (0, j)),
                  pl.BlockSpec((half, tn), lambda i, j: (1, j)),
                  pl.BlockSpec((8, tn), lambda i, j: (0, j))],
        out_specs=pl.BlockSpec((tm, tn), lambda i, j: (i, j)),
        out_shape=jax.ShapeDtypeStruct((tp + ts, D_MODEL), F32),
        compiler_params=_params(("arbitrary", "arbitrary")),
        name="out_projection",
    )(xp, xs, ylp, yrp, yls, yrs, w_out_bf16, w_out_bf16, gate1)


def _norm_router_kernel(z_ref, lng_ref, lnb_ref, shift_ref, scale_ref, wr_ref, br_ref, h2_ref, lg_ref, x1_scr,
                        *, n_p_tiles, tiles_per_seq, tn):
    i = pl.program_id(0)
    row = jnp.where(i < n_p_tiles, 0, 1 + (i - n_p_tiles) // tiles_per_seq)
    tm = z_ref.shape[0]
    nj = D_MODEL // tn
    inv_d = 1.0 / D_MODEL
    s1 = jnp.zeros((tm, 1), F32)
    for c in range(nj):
        s1 = s1 + jnp.sum(z_ref[:, c * tn:(c + 1) * tn], -1, keepdims=True)
    mu = s1 * inv_d
    s2 = jnp.zeros((tm, 1), F32)
    for c in range(nj):
        zc = z_ref[:, c * tn:(c + 1) * tn] - mu
        s2 = s2 + jnp.sum(zc * zc, -1, keepdims=True)
    rstd = lax.rsqrt(s2 * inv_d + LN_EPS)
    t1 = jnp.zeros((tm, 1), F32)
    for c in range(nj):
        cs = slice(c * tn, (c + 1) * tn)
        x1 = (z_ref[:, cs] - mu) * rstd * lng_ref[:, cs] + lnb_ref[:, cs]
        x1_scr[c] = x1
        t1 = t1 + jnp.sum(x1, -1, keepdims=True)
    mu2 = t1 * inv_d
    t2 = jnp.zeros((tm, 1), F32)
    for c in range(nj):
        xc = x1_scr[c] - mu2
        t2 = t2 + jnp.sum(xc * xc, -1, keepdims=True)
    rstd2 = lax.rsqrt(t2 * inv_d + LN_EPS)
    logits = jnp.zeros((tm, LANES), F32) + br_ref[...]

    def h2_chunk(c):
        cs = slice(c * tn, (c + 1) * tn)
        return (x1_scr[c] - mu2) * rstd2 * (1.0 + scale_ref[pl.ds(row, 1), cs]) + shift_ref[pl.ds(row, 1), cs]

    for c in range(nj // 2):
        lo, hi = h2_chunk(c), h2_chunk(c + nj // 2)
        h2_ref[:, c * tn:(c + 1) * tn] = _pack_bf16_pair(lo, hi)
        logits = logits + jnp.dot(lo.astype(BF16), wr_ref[c * tn:(c + 1) * tn, :], preferred_element_type=F32)
        logits = logits + jnp.dot(hi.astype(BF16), wr_ref[HALF_D + c * tn:HALF_D + (c + 1) * tn, :],
                                  preferred_element_type=F32)
    lg_ref[...] = _route_lanes(logits)


def _route_lanes(logits):
    lane = lax.broadcasted_iota(jnp.int32, logits.shape, 1)
    neg = -jnp.inf
    big = LANES

    def first_max(vals):
        top = jnp.max(vals, -1, keepdims=True)
        return top, jnp.min(jnp.where(vals == top, lane, big), -1, keepdims=True)

    is_group = lane < N_GROUPS
    g_top, g_sel = first_max(jnp.where(is_group, logits, neg))
    p_sel = 1.0 / jnp.sum(jnp.where(is_group, jnp.exp(logits - g_top), 0.0), -1, keepdims=True)
    first = N_GROUPS + EXPERTS_PER_GROUP * g_sel
    cand = jnp.where(jnp.logical_and(lane >= first, lane < first + EXPERTS_PER_GROUP), logits, neg)
    v1, i1 = first_max(cand)
    v2, i2 = first_max(jnp.where(lane == i1, neg, cand))
    w1 = 1.0 / (1.0 + jnp.exp(v2 - v1))
    out = jnp.where(lane == 0, (i1 - N_GROUPS).astype(F32), 0.0)
    out = jnp.where(lane == 1, (i2 - N_GROUPS).astype(F32), out)
    out = jnp.where(lane == 2, p_sel * w1, out)
    return jnp.where(lane == 3, p_sel * (1.0 - w1), out)


def _norm_router(z, n_prompt, ln_g, ln_b, shift2, scale2, w_router, b_router, dec_seq):
    t = z.shape[0]
    tm, tn = TM_NORM, 512
    full = lambda i: (0, 0)
    kern = functools.partial(_norm_router_kernel, n_p_tiles=n_prompt // tm, tiles_per_seq=dec_seq // tm, tn=tn)
    return pl.pallas_call(
        kern,
        grid=(t // tm,),
        in_specs=[pl.BlockSpec((tm, D_MODEL), lambda i: (i, 0)),
                  pl.BlockSpec((1, D_MODEL), full), pl.BlockSpec((1, D_MODEL), full),
                  pl.BlockSpec((8, D_MODEL), full), pl.BlockSpec((8, D_MODEL), full),
                  pl.BlockSpec((D_MODEL, LANES), full), pl.BlockSpec((1, LANES), full)],
        out_specs=[pl.BlockSpec((tm, HALF_D), lambda i: (i, 0)),
                   pl.BlockSpec((tm, LANES), lambda i: (i, 0))],
        out_shape=[jax.ShapeDtypeStruct((t, HALF_D), jnp.uint32),
                   jax.ShapeDtypeStruct((t, LANES), F32)],
        scratch_shapes=[pltpu.VMEM((D_MODEL // tn, tm, tn), F32)],
        compiler_params=_params(("arbitrary",)),
        name="norm_router",
    )(z, ln_g, ln_b, shift2, scale2, w_router, b_router)


def _dispatch_plan(expert, n_sub, n_super):
    i32 = jnp.int32
    flat_e = expert.reshape(-1)
    n_assign = flat_e.shape[0]
    ids = jnp.arange(N_EXPERTS, dtype=i32)
    onehot = (flat_e[:, None] == ids[None, :]).astype(i32)
    csum = jnp.cumsum(onehot, 0)
    counts = csum[-1]
    rank = jnp.sum(onehot * (csum - 1), 1)
    nb = (counts + SUB_ROWS - 1) // SUB_ROWS
    sub_end = jnp.cumsum(nb)
    sub_start = sub_end - nb
    dest = (jnp.sum(onehot * (sub_start * SUB_ROWS)[None, :], 1) + rank).astype(i32)
    pad_start = (sub_start * SUB_ROWS + counts).astype(i32)
    pad_len = (nb * SUB_ROWS - counts).astype(i32)
    tail = jnp.stack([sub_end[-1], n_sub - sub_end[-1]]).astype(i32)

    nsup = (nb + SUPER - 1) // SUPER
    sup_end = jnp.cumsum(nsup)
    sup_start = sup_end - nsup
    n_used = sup_end[-1]
    s = jnp.arange(n_super, dtype=i32)
    used = s < n_used
    last_exp = jnp.max(jnp.where(counts > 0, ids, 0))
    e_s = jnp.where(used, jnp.minimum(jnp.searchsorted(sup_end, s, side="right").astype(i32), N_EXPERTS - 1), last_exp)
    local = s - sup_start[e_s]
    first_sub = sub_start[e_s] + SUPER * local
    n_comp = jnp.where(used, jnp.clip(nb[e_s] - SUPER * local, 0, SUPER), 0).astype(i32)
    zero_first = sub_end[-1] + SUPER * (s - n_used)
    n_zero = jnp.where(used, 0, jnp.clip(n_sub - zero_first, 0, SUPER)).astype(i32)
    out_sub = jnp.where(used, first_sub, jnp.minimum(zero_first, n_sub - 1)).astype(i32)
    k = jnp.arange(SUPER, dtype=i32)
    x_sub_used = first_sub[:, None] + jnp.minimum(k[None, :], jnp.maximum(n_comp - 1, 0)[:, None])
    x_sub_last = x_sub_used[jnp.maximum(n_used - 1, 0)]
    x_sub = jnp.where(used[:, None], x_sub_used, x_sub_last[None, :]).astype(i32).reshape(-1)
    return dest, (pad_start, pad_len, tail), (e_s.astype(i32), n_comp, n_zero, out_sub, x_sub)


_PAD_PIECES = (128, 64, 32, 16, 8)


def _dispatch_kernel(dest_ref, pad_start_ref, pad_len_ref, tail_ref, h_ref, xs_hbm, zeros, sem, zsem):
    i = pl.program_id(0)
    tt = h_ref.shape[0]
    base = i * tt * 2

    def row_copy(r, k):
        return pltpu.make_async_copy(h_ref.at[pl.ds(r, 1)], xs_hbm.at[pl.ds(dest_ref[base + 2 * r + k], 1)], sem)

    def issue(g, c):
        r0 = pl.multiple_of(g * ISSUE_GROUP, ISSUE_GROUP)
        for k in range(ISSUE_GROUP):
            row_copy(r0 + k, 0).start(priority=0)
            row_copy(r0 + k, 1).start(priority=1)
        return c

    lax.fori_loop(0, tt // ISSUE_GROUP, issue, 0)

    def zero_fill(act):
        def pad(e, c):
            start = pad_start_ref[e]
            length = pad_len_ref[e]
            head = length & 7

            def head_row(r, cc):
                act(pltpu.make_async_copy(zeros.at[pl.ds(0, 1)], xs_hbm.at[pl.ds(start + r, 1)], zsem))
                return cc

            lax.fori_loop(0, head, head_row, 0)
            pos = start + head
            for piece in _PAD_PIECES:
                on = (length & piece) != 0

                @pl.when(on)
                def _():
                    rows = pl.ds(pl.multiple_of(pos, 8), piece)
                    act(pltpu.make_async_copy(zeros.at[pl.ds(0, piece)], xs_hbm.at[rows], zsem))

                pos = pos + jnp.where(on, piece, 0)
            return c

        lax.fori_loop(0, N_EXPERTS, pad, 0)

        def tail(q, c):
            row0 = pl.multiple_of((tail_ref[0] + q) * SUB_ROWS, SUB_ROWS)
            act(pltpu.make_async_copy(zeros, xs_hbm.at[pl.ds(row0, SUB_ROWS)], zsem))
            return c

        lax.fori_loop(0, tail_ref[1], tail, 0)

    @pl.when(i == 0)
    def _():
        zeros[...] = jnp.zeros_like(zeros)
        zero_fill(lambda cp: cp.start())
        zero_fill(lambda cp: cp.wait())

    def drain(r, c):
        row_copy(r, 0).wait()
        row_copy(r, 1).wait()
        return c

    lax.fori_loop(0, tt, drain, 0, unroll=4)


def _dispatch(dest, pad_start, pad_len, tail, h2_packed, n_rows):
    t = h2_packed.shape[0]
    grid_spec = pltpu.PrefetchScalarGridSpec(
        num_scalar_prefetch=4,
        grid=(t // DISPATCH_TOKENS,),
        in_specs=[pl.BlockSpec((DISPATCH_TOKENS, HALF_D), lambda i, d, ps, plen, tl: (i, 0))],
        out_specs=pl.BlockSpec(memory_space=pl.ANY),
        scratch_shapes=[pltpu.VMEM((SUB_ROWS, HALF_D), jnp.uint32),
                        pltpu.SemaphoreType.DMA(()), pltpu.SemaphoreType.DMA(())],
    )
    return pl.pallas_call(
        _dispatch_kernel,
        grid_spec=grid_spec,
        out_shape=jax.ShapeDtypeStruct((n_rows, HALF_D), jnp.uint32),
        compiler_params=_params(("arbitrary",)),
        name="dispatch",
    )(dest, pad_start, pad_len, tail, h2_packed)


def _expert_kernel(exp_ref, nc_ref, nz_ref, osub_ref, xsub_ref, x0_ref, x1_ref, x2_ref, x3_ref,
                   wg_ref, wu_ref, wd_ref, y_hbm, a_scr, ytile, sem, *, nf, nd):
    s = pl.program_id(0)
    t = pl.program_id(1)
    n_comp = nc_ref[s]
    n_out = n_comp + nz_ref[s]
    x_refs = (x0_ref, x1_ref, x2_ref, x3_ref)

    @pl.when(jnp.logical_and(t < nf, n_comp > 0))
    def _():
        wg16 = wg_ref[...].astype(BF16)
        wu16 = wu_ref[...].astype(BF16)

        def up(k):
            x = _unpack_bf16_pair(x_refs[k][...])
            g = jnp.dot(x, wg16, preferred_element_type=F32)
            u = jnp.dot(x, wu16, preferred_element_type=F32)
            a_scr[t, k * SUB_ROWS:(k + 1) * SUB_ROWS, :] = (g * _sigmoid(g) * u).astype(BF16)

        up(0)
        for k in range(1, SUPER):
            pl.when(k < n_comp)(functools.partial(up, k))

    @pl.when(t >= nf)
    def _():
        j = t - nf
        slot = j % 2

        @pl.when(n_comp > 0)
        def _():
            wd16 = wd_ref[...].astype(BF16)

            def down(k):
                rows = slice(k * SUB_ROWS, (k + 1) * SUB_ROWS)
                a = jnp.concatenate([a_scr[f, rows, :] for f in range(nf)], axis=1)
                acc = jnp.dot(a, wd16, preferred_element_type=F32)
                ytile[slot, rows, :] = _pack_bf16_pair(acc[:, :TN_DOWN // 2], acc[:, TN_DOWN // 2:])

            down(0)
            for k in range(1, SUPER):
                pl.when(k < n_comp)(functools.partial(down, k))

        @pl.when(n_comp == 0)
        def _():
            ytile[slot] = jnp.zeros(ytile.shape[1:], jnp.uint32)

        def out_copy(sl, k, jj):
            dst_rows = pl.ds(pl.multiple_of((osub_ref[s] + k) * SUB_ROWS, SUB_ROWS), SUB_ROWS)
            dst_cols = pl.ds(pl.multiple_of(jj * (TN_DOWN // 2), TN_DOWN // 2), TN_DOWN // 2)
            return pltpu.make_async_copy(ytile.at[sl, pl.ds(k * SUB_ROWS, SUB_ROWS), :],
                                         y_hbm.at[dst_rows, dst_cols], sem)

        for k in range(SUPER):
            @pl.when(jnp.logical_and(j > 0, k < n_out))
            def _():
                out_copy(1 - slot, k, j - 1).wait()
        for k in range(SUPER):
            @pl.when(k < n_out)
            def _():
                out_copy(slot, k, j).start()
        for k in range(SUPER):
            @pl.when(jnp.logical_and(j == nd - 1, k < n_out))
            def _():
                out_copy(slot, k, j).wait()


def _experts(xs, sup_exp, n_comp, n_zero, out_sub, x_sub, w_gate, w_up, w_down):
    n_super = sup_exp.shape[0]
    n_rows = xs.shape[0]
    nf = D_EXPERT // F_CHUNK
    nd = D_MODEL // TN_DOWN

    def nxt(s, nc):
        s2 = jnp.minimum(s + 1, n_super - 1)
        return s2, jnp.logical_and(nc[s] > 0, nc[s2] > 0)

    def up_idx(s, t, e, nc, nz, osub, xsub):
        s2, has_next = nxt(s, nc)
        ahead = jnp.logical_and(t >= nf, has_next)
        chunk = jnp.where(nc[s] > 0, jnp.minimum(t, nf - 1), nf - 1)
        return (jnp.where(ahead, e[s2], e[s]), 0, jnp.where(ahead, 0, chunk))

    def down_idx(s, t, e, nc, nz, osub, xsub):
        sp = jnp.maximum(s - 1, 0)
        behind = jnp.logical_and(jnp.logical_and(t == 0, s > 0), nc[s] > 0)
        chunk = jnp.where(nc[s] > 0, jnp.maximum(t - nf, 0), nd - 1)
        return (jnp.where(behind, e[sp], e[s]), 0, jnp.where(behind, nd - 1, chunk))

    def x_spec(k):
        def idx(s, t, e, nc, nz, osub, xsub):
            s2, has_next = nxt(s, nc)
            return (xsub[jnp.where(jnp.logical_and(t >= nf, has_next), s2, s) * SUPER + k], 0)

        return pl.BlockSpec((SUB_ROWS, HALF_D), idx)

    grid_spec = pltpu.PrefetchScalarGridSpec(
        num_scalar_prefetch=5,
        grid=(n_super, nf + nd),
        in_specs=[x_spec(0), x_spec(1), x_spec(2), x_spec(3),
                  pl.BlockSpec((None, D_MODEL, F_CHUNK), up_idx),
                  pl.BlockSpec((None, D_MODEL, F_CHUNK), up_idx),
                  pl.BlockSpec((None, D_EXPERT, TN_DOWN), down_idx)],
        out_specs=pl.BlockSpec(memory_space=pl.ANY),
        scratch_shapes=[pltpu.VMEM((nf, SUPER * SUB_ROWS, F_CHUNK), BF16),
                        pltpu.VMEM((2, SUPER * SUB_ROWS, TN_DOWN // 2), jnp.uint32),
                        pltpu.SemaphoreType.DMA(())],
    )
    return pl.pallas_call(
        functools.partial(_expert_kernel, nf=nf, nd=nd),
        grid_spec=grid_spec,
        out_shape=jax.ShapeDtypeStruct((n_rows, HALF_D), jnp.uint32),
        compiler_params=_params(("arbitrary", "arbitrary"), 60 * 1024 * 1024),
        name="experts",
    )(sup_exp, n_comp, n_zero, out_sub, x_sub, xs, xs, xs, xs, w_gate, w_up, w_down)


def _final_kernel(dest_ref, y_hbm, z_ref, wt_ref, gate_ref, g1_ref, b1_ref, g2_ref, b2_ref,
                  op_ref, os_ref, ybuf, sem, *, n_p_tiles, tiles_per_seq):
    i = pl.program_id(0)
    n_tiles = pl.num_programs(0)
    tm = z_ref.shape[0]
    slot = i % 2

    def row_copy(src_row, sl, k, r):
        return pltpu.make_async_copy(y_hbm.at[pl.ds(src_row, 1)], ybuf.at[sl, k, pl.ds(r, 1)], sem.at[sl])

    def gather(tile, sl):
        base = tile * tm * 2

        def issue(g, c):
            r0 = pl.multiple_of(g * ISSUE_GROUP, ISSUE_GROUP)
            for k in range(ISSUE_GROUP):
                row_copy(dest_ref[base + 2 * (r0 + k)], sl, 0, r0 + k).start(priority=0)
                row_copy(dest_ref[base + 2 * (r0 + k) + 1], sl, 1, r0 + k).start(priority=1)
            return c

        lax.fori_loop(0, tm // ISSUE_GROUP, issue, 0)

    @pl.when(i == 0)
    def _():
        gather(0, 0)

    @pl.when(i + 1 < n_tiles)
    def _():
        gather(i + 1, 1 - slot)

    def drain(r, c):
        row_copy(0, slot, 0, r).wait()
        row_copy(0, slot, 1, r).wait()
        return c

    lax.fori_loop(0, tm, drain, 0, unroll=4)

    is_p = i < n_p_tiles
    row = jnp.where(is_p, 0, 1 + (i - n_p_tiles) // tiles_per_seq)
    wt = wt_ref[...]
    w0, w1 = wt[:, 0:1], wt[:, 1:2]
    half_tile = TN_DOWN // 2
    pieces = []
    for j in range(D_MODEL // TN_DOWN):
        y0 = ybuf[slot, 0, :, j * half_tile:(j + 1) * half_tile]
        y1 = ybuf[slot, 1, :, j * half_tile:(j + 1) * half_tile]
        pieces.append(w0 * pltpu.bitcast(y0 << 16, F32) + w1 * pltpu.bitcast(y1 << 16, F32))
        pieces.append(w0 * pltpu.bitcast(y0 & jnp.uint32(0xFFFF0000), F32)
                      + w1 * pltpu.bitcast(y1 & jnp.uint32(0xFFFF0000), F32))
    f = jnp.concatenate(pieces, axis=1)
    x1 = _ln(z_ref[...]) * g1_ref[...] + b1_ref[...]
    out = _ln(ALPHA * x1 + gate_ref[pl.ds(row, 1), :] * f) * g2_ref[...] + b2_ref[...]

    @pl.when(is_p)
    def _():
        op_ref[...] = out

    @pl.when(jnp.logical_not(is_p))
    def _():
        os_ref[...] = out


def _combine(dest, y_rows, z, weight, gate2, g1, b1, g2, b2, n_prompt, dec_seq):
    t = z.shape[0]
    tm = TM_FIN
    n_p = n_prompt // tm
    n_s = (t - n_prompt) // tm
    full = lambda i, d: (0, 0)
    grid_spec = pltpu.PrefetchScalarGridSpec(
        num_scalar_prefetch=1,
        grid=(n_p + n_s,),
        in_specs=[pl.BlockSpec(memory_space=pl.ANY),
                  pl.BlockSpec((tm, D_MODEL), lambda i, d: (i, 0)),
                  pl.BlockSpec((tm, 2), lambda i, d: (i, 0)),
                  pl.BlockSpec((8, D_MODEL), full),
                  pl.BlockSpec((1, D_MODEL), full), pl.BlockSpec((1, D_MODEL), full),
                  pl.BlockSpec((1, D_MODEL), full), pl.BlockSpec((1, D_MODEL), full)],
        out_specs=[pl.BlockSpec((tm, D_MODEL), lambda i, d: (jnp.minimum(i, n_p - 1), 0)),
                   pl.BlockSpec((tm, D_MODEL), lambda i, d: (jnp.maximum(i - n_p, 0), 0))],
        scratch_shapes=[pltpu.VMEM((2, 2, tm, HALF_D), jnp.uint32), pltpu.SemaphoreType.DMA((2,))],
    )
    return pl.pallas_call(
        functools.partial(_final_kernel, n_p_tiles=n_p, tiles_per_seq=dec_seq // tm),
        grid_spec=grid_spec,
        out_shape=[jax.ShapeDtypeStruct((n_prompt, D_MODEL), F32),
                   jax.ShapeDtypeStruct((t - n_prompt, D_MODEL), F32)],
        compiler_params=_params(("arbitrary",)),
        name="combine",
    )(dest, y_rows, z, weight, gate2, g1, b1, g2, b2)


def kernel(x_prompt, x_sample, state_lru, state_ret, c, c_ctx, w_mod, b_mod, w_in, conv_w, conv_b, lru_wa, lru_ba,
           lru_wx, lru_bx, lru_lam, ret_decay, w_out, ln1_g, ln1_b, router_g, router_g_b, router_e, router_e_b,
           w_gate, w_up, w_down, ln2_g, ln2_b):
    assert w_in.shape[0] == 1, "single trunk layer"
    nb, seq, d = x_prompt.shape
    nbs, dec_seq, _ = x_sample.shape
    tp, ts = nb * seq, nbs * dec_seq
    assert tp % dec_seq == 0 and d == D_MODEL

    cond = jnp.zeros((8, d), F32).at[0].set(c_ctx).at[1:1 + nbs].set(c)
    mod = _modulation(cond, w_mod[0], b_mod[0][None, :])
    shift1, scale1, gate1, shift2, scale2, gate2 = [mod[:, k * d:(k + 1) * d] for k in range(6)]

    xp = x_prompt.reshape(tp, d)
    xs = x_sample.reshape(ts, d)
    proj = _in_projection(_ln_modulate(xp, xs, shift1, scale1, dec_seq), w_in[0].astype(BF16))

    w_gates = (0.5 * jnp.concatenate([lru_wa[0, 0], lru_wx[0, 0], lru_wa[0, 1], lru_wx[0, 1]], -1)).astype(BF16)
    pb = jnp.stack([0.5 * lru_ba[0, 0], 0.5 * lru_bx[0, 0], lru_lam[0, 0],
                    0.5 * lru_ba[0, 1], 0.5 * lru_bx[0, 1], lru_lam[0, 1]], 0)
    proj_p = proj.reshape((tp + ts) // seq, seq, IN_COLS)
    proj_s = proj.reshape((tp + ts) // dec_seq, dec_seq, IN_COLS)
    ylp, st_lru = _lru_mixer(proj_p, 0, nb, 16, 1, conv_w[0], conv_b[0][None, :], w_gates, pb,
                             jnp.zeros((2, nb, LRU_WIDTH), F32))
    yls, _ = _lru_mixer(proj_s, tp // dec_seq, nbs, nbs, 4, conv_w[0], conv_b[0][None, :], w_gates, pb,
                        jnp.swapaxes(state_lru[:, 0], 0, 1))

    decay = jnp.broadcast_to(ret_decay[0].T[:, :, None], (RET_HEADS, 2, LANES))
    yrp, st_ret = _ret_mixer(proj_p, 0, nb, 8, decay, emit_state=True)
    (yrs,) = _ret_mixer(proj_s, tp // dec_seq, nbs, 1, decay, rope_tabs=_rope_tables(dec_seq), s0=state_ret)

    w_router = jnp.concatenate(
        [router_g[0], jnp.transpose(router_e[0], (1, 0, 2)).reshape(d, N_EXPERTS),
         jnp.zeros((d, LANES - N_GROUPS - N_EXPERTS), F32)], -1).astype(BF16)
    b_router = jnp.concatenate([router_g_b[0], router_e_b[0].reshape(-1),
                                jnp.zeros((LANES - N_GROUPS - N_EXPERTS,), F32)])[None, :]
    z = _out_projection(xp, xs, ylp.reshape(tp, -1), yrp.reshape(tp, -1), yls.reshape(ts, -1), yrs.reshape(ts, -1),
                        w_out[0].astype(BF16), gate1, dec_seq)
    h2, route = _norm_router(z, tp, ln1_g, ln1_b, shift2, scale2, w_router, b_router, dec_seq)
    expert = route[:, 0:2].astype(jnp.int32)
    weight = route[:, 2:4]

    n_assign = 2 * (tp + ts)
    n_sub = -(-(n_assign + N_EXPERTS * (SUB_ROWS - 1)) // SUB_ROWS)
    n_super = N_EXPERTS + -(-n_assign // (SUB_ROWS * SUPER))
    assert SUPER * n_super >= n_sub + (SUPER - 1) * N_EXPERTS
    dest, fill_plan, super_plan = _dispatch_plan(expert, n_sub, n_super)
    xs_rows = _dispatch(dest, *fill_plan, h2, n_sub * SUB_ROWS)
    y_rows = _experts(xs_rows, *super_plan, w_gate[0], w_up[0], w_down[0])
    y_p, y_s = _combine(dest, y_rows, z, weight, gate2, ln1_g, ln1_b, ln2_g, ln2_b, tp, dec_seq)

    new_state_lru = jnp.swapaxes(st_lru, 0, 1)[:, None]
    return (y_p.reshape(nb, seq, d), y_s.reshape(nbs, dec_seq, d), new_state_lru, st_ret)
```

```python
import functools

import jax
import jax.numpy as jnp
import numpy as np
from jax import lax
from jax.experimental import pallas as pl
from jax.experimental.pallas import tpu as pltpu

F32 = jnp.float32
BF16 = jnp.bfloat16

D_MODEL = 4096
LRU_WIDTH = 2048
LRU_BLOCKS = 16
LANES = 128
RET_HEADS = 16
RET_DH = 128
IN_COLS = 12288
GRID_W = 64
ROPE_BASE = 10000.0
LRU_C = 8.0
N_GROUPS = 4
EXPERTS_PER_GROUP = 8
N_EXPERTS = 32
D_EXPERT = 1024
LN_EPS = 1e-6
GN_EPS = 1e-5
ALPHA = 2.0 ** 0.25

VMEM_LIMIT = 56 * 1024 * 1024
SCAN_PAD = 8

TM_LN = 512
TM_PROJ = 1024
TN_IN = 1536
TN_OUT = 512
TM_OUT = 1024
TM_NORM = 512
SUB_ROWS = 256
SUPER = 4
F_CHUNK = 256
TN_DOWN = 1024
DISPATCH_TOKENS = 512
ISSUE_GROUP = 16
TM_FIN = 256
HALF_D = D_MODEL // 2


def _params(sem, vmem_limit=VMEM_LIMIT):
    return pltpu.CompilerParams(dimension_semantics=sem, vmem_limit_bytes=vmem_limit)


def _pack_bf16_pair(lo, hi):
    lo_bits = pltpu.bitcast(lo.astype(BF16).astype(F32), jnp.uint32) >> 16
    hi_bits = pltpu.bitcast(hi.astype(BF16).astype(F32), jnp.uint32) & jnp.uint32(0xFFFF0000)
    return lo_bits | hi_bits


def _unpack_bf16_pair(words):
    lo = pltpu.bitcast(words << 16, F32).astype(BF16)
    hi = pltpu.bitcast(words & jnp.uint32(0xFFFF0000), F32).astype(BF16)
    return jnp.concatenate([lo, hi], axis=1)


def _sigmoid(x):
    return 0.5 * jnp.tanh(0.5 * x) + 0.5


def _softplus(x):
    return jnp.maximum(x, 0.0) + jnp.log1p(jnp.exp(-jnp.abs(x)))


def _ln(x):
    mu = jnp.mean(x, -1, keepdims=True)
    xc = x - mu
    var = jnp.mean(xc * xc, -1, keepdims=True)
    return xc * lax.rsqrt(var + LN_EPS)


def _mod_kernel(cond_ref, w_ref, b_ref, o_ref):
    c = cond_ref[...]
    s = (c * _sigmoid(c)).astype(BF16)
    o_ref[...] = jnp.dot(s, w_ref[...].astype(BF16), preferred_element_type=F32) + b_ref[...]


def _modulation(cond, w_mod, b_mod):
    tn = 512
    n = w_mod.shape[1]
    return pl.pallas_call(
        _mod_kernel,
        grid=(n // tn,),
        in_specs=[pl.BlockSpec((8, D_MODEL), lambda j: (0, 0)),
                  pl.BlockSpec((D_MODEL, tn), lambda j: (0, j)),
                  pl.BlockSpec((1, tn), lambda j: (0, j))],
        out_specs=pl.BlockSpec((8, tn), lambda j: (0, j)),
        out_shape=jax.ShapeDtypeStruct((8, n), F32),
        compiler_params=_params(("arbitrary",)),
        name="modulation",
    )(cond, w_mod, b_mod)


def _ln_mod_kernel(xp_ref, xs_ref, shift_ref, scale_ref, h_ref, *, n_p_tiles, tiles_per_seq):
    i = pl.program_id(0)

    def fill(x_ref, row):
        h = _ln(x_ref[...]) * (1.0 + scale_ref[pl.ds(row, 1), :]) + shift_ref[pl.ds(row, 1), :]
        h_ref[...] = h.astype(BF16)

    @pl.when(i < n_p_tiles)
    def _():
        fill(xp_ref, 0)

    @pl.when(i >= n_p_tiles)
    def _():
        fill(xs_ref, 1 + (i - n_p_tiles) // tiles_per_seq)


def _ln_modulate(xp, xs, shift, scale, dec_seq):
    tp, ts = xp.shape[0], xs.shape[0]
    tm = TM_LN
    n_p, n_s = tp // tm, ts // tm
    kern = functools.partial(_ln_mod_kernel, n_p_tiles=n_p, tiles_per_seq=dec_seq // tm)
    return pl.pallas_call(
        kern,
        grid=(n_p + n_s,),
        in_specs=[pl.BlockSpec((tm, D_MODEL), lambda i: (jnp.minimum(i, n_p - 1), 0)),
                  pl.BlockSpec((tm, D_MODEL), lambda i: (jnp.maximum(i - n_p, 0), 0)),
                  pl.BlockSpec((8, D_MODEL), lambda i: (0, 0)),
                  pl.BlockSpec((8, D_MODEL), lambda i: (0, 0))],
        out_specs=pl.BlockSpec((tm, D_MODEL), lambda i: (i, 0)),
        out_shape=jax.ShapeDtypeStruct((tp + ts, D_MODEL), BF16),
        compiler_params=_params(("arbitrary",)),
        name="ln_modulate",
    )(xp, xs, shift, scale)


def _matmul_kernel(h_ref, w_ref, o_ref):
    o_ref[...] = jnp.dot(h_ref[...], w_ref[...], preferred_element_type=F32).astype(o_ref.dtype)


def _in_projection(h, w_in_bf16):
    t = h.shape[0]
    tm, tn = TM_PROJ, TN_IN
    return pl.pallas_call(
        _matmul_kernel,
        grid=(t // tm, IN_COLS // tn),
        in_specs=[pl.BlockSpec((tm, D_MODEL), lambda i, j: (i, 0)),
                  pl.BlockSpec((D_MODEL, tn), lambda i, j: (0, j))],
        out_specs=pl.BlockSpec((tm, tn), lambda i, j: (i, j)),
        out_shape=jax.ShapeDtypeStruct((t, IN_COLS), BF16),
        compiler_params=_params(("arbitrary", "arbitrary")),
        name="in_projection",
    )(h, w_in_bf16)


def _lru_kernel(x_ref, g_ref, cw_ref, cb_ref, wg_ref, pb_ref, h0_ref, y_ref, st_ref,
                xc_scr, af_scr, uf_scr, ab_scr, ub_scr, *, bg, seq, pitch, ncb):
    rows = lax.broadcasted_iota(jnp.int32, (seq, LANES), 0)
    for c in range(ncb):
        lanes = slice(c * LANES, (c + 1) * LANES)
        w = cw_ref[:, lanes]
        bias = cb_ref[:, lanes]
        for b in range(bg):
            x = x_ref[b, :, lanes].astype(F32)
            xm2 = jnp.where(rows >= 2, pltpu.roll(x, 2, 0), 0.0)
            xm1 = jnp.where(rows >= 1, pltpu.roll(x, 1, 0), 0.0)
            xp1 = jnp.where(rows < seq - 1, pltpu.roll(x, seq - 1, 0), 0.0)
            xc_scr[pl.ds(b * seq, seq), :] = bias + xm2 * w[0:1] + xm1 * w[1:2] + x * w[2:3] + xp1 * w[3:4]

        xc = xc_scr[...]
        gates = jnp.dot(xc.astype(BF16), wg_ref[c], preferred_element_type=F32)
        pb = pb_ref[:, lanes]
        half_xc = 0.5 * xc
        for d, (a_scr, u_scr) in enumerate(((af_scr, uf_scr), (ab_scr, ub_scr))):
            t_r = jnp.tanh(gates[:, (2 * d) * LANES:(2 * d + 1) * LANES] + pb[3 * d:3 * d + 1])
            t_i = jnp.tanh(gates[:, (2 * d + 1) * LANES:(2 * d + 2) * LANES] + pb[3 * d + 1:3 * d + 2])
            c4 = (0.5 * LRU_C) * _softplus(-pb[3 * d + 2:3 * d + 3])
            neg_log_a = c4 * t_r + c4
            a = jnp.exp(-neg_log_a)
            y = jnp.tanh(neg_log_a) * (1.0 + a * a)
            root = jnp.where(y > 0.0, y * lax.rsqrt(y), 0.0)
            u = root * (half_xc * (t_i + 1.0))
            for b in range(bg):
                a_scr[c, pl.ds(b * pitch, seq), :] = a[b * seq:(b + 1) * seq]
                u_scr[c, pl.ds(b * pitch, seq), :] = u[b * seq:(b + 1) * seq]

    def step(t, carry):
        tb = seq - 1 - t
        out = []
        for c in range(ncb):
            hf, hb = carry[2 * c], carry[2 * c + 1]
            hf = af_scr[c, pl.ds(t, bg, stride=pitch), :] * hf + uf_scr[c, pl.ds(t, bg, stride=pitch), :]
            uf_scr[c, pl.ds(t, bg, stride=pitch), :] = hf
            hb = ab_scr[c, pl.ds(tb, bg, stride=pitch), :] * hb + ub_scr[c, pl.ds(tb, bg, stride=pitch), :]
            ub_scr[c, pl.ds(tb, bg, stride=pitch), :] = hb
            out += [hf, hb]
        return tuple(out)

    init = []
    for c in range(ncb):
        init += [h0_ref[0, :, c * LANES:(c + 1) * LANES], h0_ref[1, :, c * LANES:(c + 1) * LANES]]
    last = lax.fori_loop(0, seq, step, tuple(init), unroll=8)
    for c in range(ncb):
        lanes = slice(c * LANES, (c + 1) * LANES)
        st_ref[0, :, lanes] = last[2 * c]
        st_ref[1, :, lanes] = last[2 * c + 1]
        for b in range(bg):
            hs = uf_scr[c, pl.ds(b * pitch, seq), :] + ub_scr[c, pl.ds(b * pitch, seq), :]
            y_ref[b, :, lanes] = (jax.nn.gelu(g_ref[b, :, lanes].astype(F32)) * hs).astype(BF16)


def _lru_mixer(proj3, seq0, nseq, bg, ncb, conv_w, conv_b, w_gates, pb, h0):
    seq = proj3.shape[1]
    pitch = seq + SCAN_PAD
    off = seq0 // bg
    cw = ncb * LANES
    ncol = LRU_WIDTH // cw
    kern = functools.partial(_lru_kernel, bg=bg, seq=seq, pitch=pitch, ncb=ncb)
    return pl.pallas_call(
        kern,
        grid=(nseq // bg, ncol),
        in_specs=[pl.BlockSpec((bg, seq, cw), lambda b, c: (b + off, 0, c)),
                  pl.BlockSpec((bg, seq, cw), lambda b, c: (b + off, 0, ncol + c)),
                  pl.BlockSpec((4, cw), lambda b, c: (0, c)),
                  pl.BlockSpec((1, cw), lambda b, c: (0, c)),
                  pl.BlockSpec((ncb, LANES, 4 * LANES), lambda b, c: (c, 0, 0)),
                  pl.BlockSpec((6, cw), lambda b, c: (0, c)),
                  pl.BlockSpec((2, bg, cw), lambda b, c: (0, b, c))],
        out_specs=[pl.BlockSpec((bg, seq, cw), lambda b, c: (b, 0, c)),
                   pl.BlockSpec((2, bg, cw), lambda b, c: (0, b, c))],
        out_shape=[jax.ShapeDtypeStruct((nseq, seq, LRU_WIDTH), BF16),
                   jax.ShapeDtypeStruct((2, nseq, LRU_WIDTH), F32)],
        scratch_shapes=[pltpu.VMEM((bg * seq, LANES), F32)] + [pltpu.VMEM((ncb, bg * pitch, LANES), F32)] * 4,
        compiler_params=_params(("arbitrary", "arbitrary")),
        name="lru_mixer",
    )(proj3, proj3, conv_w, conv_b, w_gates, pb, h0)


def _rope(x, cos, sin_signed, first_half):
    partner = jnp.where(first_half, pltpu.roll(x, LANES - 32, 1), pltpu.roll(x, 32, 1))
    return x * cos + partner * sin_signed


def _ret_kernel(*refs, bg, seq, rope, has_state, emit_state, qb):
    refs = list(refs)
    q_ref, k_ref, v_ref, g_ref, dec_ref = refs[:5]
    pos = 5
    if rope:
        cos_ref, sin_ref = refs[pos:pos + 2]
        pos += 2
    if has_state:
        s0_ref = refs[pos]
        pos += 1
    y_ref = refs[pos]
    pos += 1
    if emit_state:
        st_ref = refs[pos]
        pos += 1
    mask_scr = refs[pos]

    log_g = -_softplus(-dec_ref[0])
    lgf, lgb = log_g[0:1], log_g[1:2]
    reps = seq // LANES
    lgf_row = jnp.concatenate([lgf] * reps, axis=1)
    lgb_row = jnp.concatenate([lgb] * reps, axis=1)

    @pl.when(pl.program_id(1) == 0)
    def _():
        for blk in range(seq // qb):
            ti = lax.broadcasted_iota(jnp.int32, (qb, seq), 0) + blk * qb
            si = lax.broadcasted_iota(jnp.int32, (qb, seq), 1)
            dist = (ti - si).astype(F32)
            e = jnp.where(dist >= 0, dist * lgf_row, (-dist) * lgb_row)
            mask_scr[pl.ds(blk * qb, qb), :] = jnp.where(dist == 0, 2.0, jnp.exp(e))

    lane = lax.broadcasted_iota(jnp.int32, (seq, LANES), 1)
    first_half = (lane % 64) < 32
    trow = lax.broadcasted_iota(jnp.int32, (seq, LANES), 0).astype(F32)
    for b in range(bg):
        q = q_ref[b].astype(F32)
        k = k_ref[b].astype(F32) * (RET_DH ** -0.5)
        v16 = v_ref[b]
        if rope:
            q = _rope(q, cos_ref[...], sin_ref[...], first_half)
            k = _rope(k, cos_ref[...], sin_ref[...], first_half)
        q16 = q.astype(BF16)
        k16 = k.astype(BF16)
        if has_state:
            qf16 = (q * jnp.exp((trow + 1.0) * lgf)).astype(BF16)
            qb16 = (q * jnp.exp((float(seq) - trow) * lgb)).astype(BF16)
            s0f = s0_ref[b, 0].astype(BF16)
            s0b = s0_ref[b, 1].astype(BF16)
        for blk in range(seq // qb):
            sl = slice(blk * qb, (blk + 1) * qb)
            s = lax.dot_general(q16[sl], k16, (((1,), (1,)), ((), ())), preferred_element_type=F32)
            p = (s * mask_scr[pl.ds(blk * qb, qb), :]).astype(BF16)
            o = jnp.dot(p, v16, preferred_element_type=F32)
            if has_state:
                o = o + jnp.dot(qf16[sl], s0f, preferred_element_type=F32)
                o = o + jnp.dot(qb16[sl], s0b, preferred_element_type=F32)
            mu = jnp.mean(o, -1, keepdims=True)
            oc = o - mu
            var = jnp.mean(oc * oc, -1, keepdims=True)
            on = oc * lax.rsqrt(var + GN_EPS)
            gt = g_ref[b, pl.ds(blk * qb, qb), :].astype(F32)
            y_ref[b, pl.ds(blk * qb, qb), :] = (gt * _sigmoid(gt) * on).astype(BF16)
        if emit_state:
            kf16 = (k * jnp.exp((float(seq - 1) - trow) * lgf)).astype(BF16)
            kb16 = (k * jnp.exp(trow * lgb)).astype(BF16)
            sf = lax.dot_general(kf16, v16, (((0,), (0,)), ((), ())), preferred_element_type=F32)
            sb = lax.dot_general(kb16, v16, (((0,), (0,)), ((), ())), preferred_element_type=F32)
            if has_state:
                sf = sf + jnp.exp(float(seq) * lgf) * s0_ref[b, 0]
                sb = sb + jnp.exp(float(seq) * lgb) * s0_ref[b, 1]
            st_ref[b, 0] = sf
            st_ref[b, 1] = sb


def _ret_mixer(proj3, seq0, nseq, bg, decay, rope_tabs=None, s0=None, emit_state=False):
    seq = proj3.shape[1]
    off = seq0 // bg
    qb = min(seq, 256)
    rope = rope_tabs is not None
    has_state = s0 is not None
    kern = functools.partial(_ret_kernel, bg=bg, seq=seq, rope=rope, has_state=has_state,
                             emit_state=emit_state, qb=qb)

    def col(base):
        return pl.BlockSpec((bg, seq, LANES), lambda h, b: (b + off, 0, base + h))

    st_spec = pl.BlockSpec((bg, None, 2, None, RET_DH, RET_DH), lambda h, b: (b, 0, 0, h, 0, 0))
    in_specs = [col(2 * LRU_BLOCKS), col(2 * LRU_BLOCKS + RET_HEADS), col(2 * LRU_BLOCKS + 2 * RET_HEADS),
                col(2 * LRU_BLOCKS + 3 * RET_HEADS), pl.BlockSpec((1, 2, LANES), lambda h, b: (h, 0, 0))]
    args = [proj3, proj3, proj3, proj3, decay]
    if rope:
        in_specs += [pl.BlockSpec((seq, LANES), lambda h, b: (0, 0))] * 2
        args += list(rope_tabs)
    if has_state:
        in_specs.append(st_spec)
        args.append(s0)
    out_specs = [pl.BlockSpec((bg, seq, LANES), lambda h, b: (b, 0, h))]
    out_shape = [jax.ShapeDtypeStruct((nseq, seq, RET_HEADS * RET_DH), BF16)]
    if emit_state:
        out_specs.append(st_spec)
        out_shape.append(jax.ShapeDtypeStruct((nseq, 1, 2, RET_HEADS, RET_DH, RET_DH), F32))
    return pl.pallas_call(
        kern,
        grid=(RET_HEADS, nseq // bg),
        in_specs=in_specs,
        out_specs=out_specs,
        out_shape=out_shape,
        scratch_shapes=[pltpu.VMEM((seq, seq), F32)],
        compiler_params=_params(("arbitrary", "arbitrary")),
        name="ret_mixer",
    )(*args)


def _rope_tables(seq):
    nf = RET_DH // 4
    freqs = (np.float32(ROPE_BASE) ** (-np.arange(nf, dtype=np.float32) / np.float32(nf))).astype(np.float32)
    t = np.arange(seq)
    row = (t // GRID_W).astype(np.float32)[:, None] * freqs[None, :]
    colp = (t % GRID_W).astype(np.float32)[:, None] * freqs[None, :]
    cos = np.concatenate([np.cos(row), np.cos(row), np.cos(colp), np.cos(colp)], -1)
    sin = np.concatenate([-np.sin(row), np.sin(row), -np.sin(colp), np.sin(colp)], -1)
    return jnp.asarray(cos, F32), jnp.asarray(sin, F32)


def _outproj_kernel(xp_ref, xs_ref, ylp_ref, yrp_ref, yls_ref, yrs_ref, wa_ref, wb_ref, gate_ref, z_ref,
                    *, n_p_tiles, tiles_per_seq):
    i = pl.program_id(0)
    is_p = i < n_p_tiles
    row = jnp.where(is_p, 0, 1 + (i - n_p_tiles) // tiles_per_seq)

    def mix(x_ref, yl_ref, yr_ref):
        m = jnp.dot(yl_ref[...], wa_ref[...], preferred_element_type=F32)
        m = m + jnp.dot(yr_ref[...], wb_ref[...], preferred_element_type=F32)
        z_ref[...] = ALPHA * x_ref[...] + gate_ref[pl.ds(row, 1), :] * m

    @pl.when(is_p)
    def _():
        mix(xp_ref, ylp_ref, yrp_ref)

    @pl.when(jnp.logical_not(is_p))
    def _():
        mix(xs_ref, yls_ref, yrs_ref)


def _out_projection(xp, xs, ylp, yrp, yls, yrs, w_out_bf16, gate1, dec_seq):
    tp, ts = xp.shape[0], xs.shape[0]
    tm, tn = TM_OUT, TN_OUT
    n_p, n_s = tp // tm, ts // tm
    half = LRU_WIDTH
    kern = functools.partial(_outproj_kernel, n_p_tiles=n_p, tiles_per_seq=dec_seq // tm)
    p_idx = lambda i, j: (jnp.minimum(i, n_p - 1), 0)
    s_idx = lambda i, j: (jnp.maximum(i - n_p, 0), 0)
    once = pl.Buffered(1)
    return pl.pallas_call(
        kern,
        grid=(n_p + n_s, D_MODEL // tn),
        in_specs=[pl.BlockSpec((tm, tn), lambda i, j: (jnp.minimum(i, n_p - 1), j)),
                  pl.BlockSpec((tm, tn), lambda i, j: (jnp.maximum(i - n_p, 0), j)),
                  pl.BlockSpec((tm, half), p_idx), pl.BlockSpec((tm, half), p_idx),
                  pl.BlockSpec((tm, half), s_idx, pipeline_mode=once),
                  pl.BlockSpec((tm, half), s_idx, pipeline_mode=once),
                  pl.BlockSpec((half, tn), lambda i, j: (0, j)),
                  pl.BlockSpec((half, tn), lambda i, j: (1, j)),
                  pl.BlockSpec((8, tn), lambda i, j: (0, j))],
        out_specs=pl.BlockSpec((tm, tn), lambda i, j: (i, j)),
        out_shape=jax.ShapeDtypeStruct((tp + ts, D_MODEL), F32),
        compiler_params=_params(("arbitrary", "arbitrary")),
        name="out_projection",
    )(xp, xs, ylp, yrp, yls, yrs, w_out_bf16, w_out_bf16, gate1)


def _norm_router_kernel(z_ref, lng_ref, lnb_ref, shift_ref, scale_ref, wr_ref, br_ref, h2_ref, lg_ref, x1_scr,
                        *, n_p_tiles, tiles_per_seq, tn):
    i = pl.program_id(0)
    row = jnp.where(i < n_p_tiles, 0, 1 + (i - n_p_tiles) // tiles_per_seq)
    tm = z_ref.shape[0]
    nj = D_MODEL // tn
    inv_d = 1.0 / D_MODEL
    s1 = jnp.zeros((tm, 1), F32)
    for c in range(nj):
        s1 = s1 + jnp.sum(z_ref[:, c * tn:(c + 1) * tn], -1, keepdims=True)
    mu = s1 * inv_d
    s2 = jnp.zeros((tm, 1), F32)
    for c in range(nj):
        zc = z_ref[:, c * tn:(c + 1) * tn] - mu
        s2 = s2 + jnp.sum(zc * zc, -1, keepdims=True)
    rstd = lax.rsqrt(s2 * inv_d + LN_EPS)
    t1 = jnp.zeros((tm, 1), F32)
    for c in range(nj):
        cs = slice(c * tn, (c + 1) * tn)
        x1 = (z_ref[:, cs] - mu) * rstd * lng_ref[:, cs] + lnb_ref[:, cs]
        x1_scr[c] = x1
        t1 = t1 + jnp.sum(x1, -1, keepdims=True)
    mu2 = t1 * inv_d
    t2 = jnp.zeros((tm, 1), F32)
    for c in range(nj):
        xc = x1_scr[c] - mu2
        t2 = t2 + jnp.sum(xc * xc, -1, keepdims=True)
    rstd2 = lax.rsqrt(t2 * inv_d + LN_EPS)
    logits = jnp.zeros((tm, LANES), F32) + br_ref[...]

    def h2_chunk(c):
        cs = slice(c * tn, (c + 1) * tn)
        return (x1_scr[c] - mu2) * rstd2 * (1.0 + scale_ref[pl.ds(row, 1), cs]) + shift_ref[pl.ds(row, 1), cs]

    for c in range(nj // 2):
        lo, hi = h2_chunk(c), h2_chunk(c + nj // 2)
        h2_ref[:, c * tn:(c + 1) * tn] = _pack_bf16_pair(lo, hi)
        logits = logits + jnp.dot(lo.astype(BF16), wr_ref[c * tn:(c + 1) * tn, :], preferred_element_type=F32)
        logits = logits + jnp.dot(hi.astype(BF16), wr_ref[HALF_D + c * tn:HALF_D + (c + 1) * tn, :],
                                  preferred_element_type=F32)
    lg_ref[...] = _route_lanes(logits)


def _route_lanes(logits):
    lane = lax.broadcasted_iota(jnp.int32, logits.shape, 1)
    neg = -jnp.inf
    big = LANES

    def first_max(vals):
        top = jnp.max(vals, -1, keepdims=True)
        return top, jnp.min(jnp.where(vals == top, lane, big), -1, keepdims=True)

    is_group = lane < N_GROUPS
    g_top, g_sel = first_max(jnp.where(is_group, logits, neg))
    p_sel = 1.0 / jnp.sum(jnp.where(is_group, jnp.exp(logits - g_top), 0.0), -1, keepdims=True)
    first = N_GROUPS + EXPERTS_PER_GROUP * g_sel
    cand = jnp.where(jnp.logical_and(lane >= first, lane < first + EXPERTS_PER_GROUP), logits, neg)
    v1, i1 = first_max(cand)
    v2, i2 = first_max(jnp.where(lane == i1, neg, cand))
    w1 = 1.0 / (1.0 + jnp.exp(v2 - v1))
    out = jnp.where(lane == 0, (i1 - N_GROUPS).astype(F32), 0.0)
    out = jnp.where(lane == 1, (i2 - N_GROUPS).astype(F32), out)
    out = jnp.where(lane == 2, p_sel * w1, out)
    return jnp.where(lane == 3, p_sel * (1.0 - w1), out)


def _norm_router(z, n_prompt, ln_g, ln_b, shift2, scale2, w_router, b_router, dec_seq):
    t = z.shape[0]
    tm, tn = TM_NORM, 512
    full = lambda i: (0, 0)
    kern = functools.partial(_norm_router_kernel, n_p_tiles=n_prompt // tm, tiles_per_seq=dec_seq // tm, tn=tn)
    return pl.pallas_call(
        kern,
        grid=(t // tm,),
        in_specs=[pl.BlockSpec((tm, D_MODEL), lambda i: (i, 0)),
                  pl.BlockSpec((1, D_MODEL), full), pl.BlockSpec((1, D_MODEL), full),
                  pl.BlockSpec((8, D_MODEL), full), pl.BlockSpec((8, D_MODEL), full),
                  pl.BlockSpec((D_MODEL, LANES), full), pl.BlockSpec((1, LANES), full)],
        out_specs=[pl.BlockSpec((tm, HALF_D), lambda i: (i, 0)),
                   pl.BlockSpec((tm, LANES), lambda i: (i, 0))],
        out_shape=[jax.ShapeDtypeStruct((t, HALF_D), jnp.uint32),
                   jax.ShapeDtypeStruct((t, LANES), F32)],
        scratch_shapes=[pltpu.VMEM((D_MODEL // tn, tm, tn), F32)],
        compiler_params=_params(("arbitrary",)),
        name="norm_router",
    )(z, ln_g, ln_b, shift2, scale2, w_router, b_router)


def _dispatch_plan(expert, n_sub, n_super):
    i32 = jnp.int32
    flat_e = expert.reshape(-1)
    n_assign = flat_e.shape[0]
    ids = jnp.arange(N_EXPERTS, dtype=i32)
    onehot = (flat_e[:, None] == ids[None, :]).astype(i32)
    csum = jnp.cumsum(onehot, 0)
    counts = csum[-1]
    rank = jnp.sum(onehot * (csum - 1), 1)
    nb = (counts + SUB_ROWS - 1) // SUB_ROWS
    sub_end = jnp.cumsum(nb)
    sub_start = sub_end - nb
    dest = (jnp.sum(onehot * (sub_start * SUB_ROWS)[None, :], 1) + rank).astype(i32)
    pad_start = (sub_start * SUB_ROWS + counts).astype(i32)
    pad_len = (nb * SUB_ROWS - counts).astype(i32)
    tail = jnp.stack([sub_end[-1], n_sub - sub_end[-1]]).astype(i32)

    nsup = (nb + SUPER - 1) // SUPER
    sup_end = jnp.cumsum(nsup)
    sup_start = sup_end - nsup
    n_used = sup_end[-1]
    s = jnp.arange(n_super, dtype=i32)
    used = s < n_used
    last_exp = jnp.max(jnp.where(counts > 0, ids, 0))
    e_s = jnp.where(used, jnp.minimum(jnp.searchsorted(sup_end, s, side="right").astype(i32), N_EXPERTS - 1), last_exp)
    local = s - sup_start[e_s]
    first_sub = sub_start[e_s] + SUPER * local
    n_comp = jnp.where(used, jnp.clip(nb[e_s] - SUPER * local, 0, SUPER), 0).astype(i32)
    zero_first = sub_end[-1] + SUPER * (s - n_used)
    n_zero = jnp.where(used, 0, jnp.clip(n_sub - zero_first, 0, SUPER)).astype(i32)
    out_sub = jnp.where(used, first_sub, jnp.minimum(zero_first, n_sub - 1)).astype(i32)
    k = jnp.arange(SUPER, dtype=i32)
    x_sub_used = first_sub[:, None] + jnp.minimum(k[None, :], jnp.maximum(n_comp - 1, 0)[:, None])
    x_sub_last = x_sub_used[jnp.maximum(n_used - 1, 0)]
    x_sub = jnp.where(used[:, None], x_sub_used, x_sub_last[None, :]).astype(i32).reshape(-1)
    return dest, (pad_start, pad_len, tail), (e_s.astype(i32), n_comp, n_zero, out_sub, x_sub)


_PAD_PIECES = (128, 64, 32, 16, 8)


def _dispatch_kernel(dest_ref, pad_start_ref, pad_len_ref, tail_ref, h_ref, xs_hbm, zeros, sem, zsem):
    i = pl.program_id(0)
    tt = h_ref.shape[0]
    base = i * tt * 2

    def row_copy(r, k):
        return pltpu.make_async_copy(h_ref.at[pl.ds(r, 1)], xs_hbm.at[pl.ds(dest_ref[base + 2 * r + k], 1)], sem)

    def issue(g, c):
        r0 = pl.multiple_of(g * ISSUE_GROUP, ISSUE_GROUP)
        for k in range(ISSUE_GROUP):
            row_copy(r0 + k, 0).start(priority=0)
            row_copy(r0 + k, 1).start(priority=1)
        return c

    lax.fori_loop(0, tt // ISSUE_GROUP, issue, 0)

    def zero_fill(act):
        def pad(e, c):
            start = pad_start_ref[e]
            length = pad_len_ref[e]
            head = length & 7

            def head_row(r, cc):
                act(pltpu.make_async_copy(zeros.at[pl.ds(0, 1)], xs_hbm.at[pl.ds(start + r, 1)], zsem))
                return cc

            lax.fori_loop(0, head, head_row, 0)
            pos = start + head
            for piece in _PAD_PIECES:
                on = (length & piece) != 0

                @pl.when(on)
                def _():
                    rows = pl.ds(pl.multiple_of(pos, 8), piece)
                    act(pltpu.make_async_copy(zeros.at[pl.ds(0, piece)], xs_hbm.at[rows], zsem))

                pos = pos + jnp.where(on, piece, 0)
            return c

        lax.fori_loop(0, N_EXPERTS, pad, 0)

        def tail(q, c):
            row0 = pl.multiple_of((tail_ref[0] + q) * SUB_ROWS, SUB_ROWS)
            act(pltpu.make_async_copy(zeros, xs_hbm.at[pl.ds(row0, SUB_ROWS)], zsem))
            return c

        lax.fori_loop(0, tail_ref[1], tail, 0)

    @pl.when(i == 0)
    def _():
        zeros[...] = jnp.zeros_like(zeros)
        zero_fill(lambda cp: cp.start())
        zero_fill(lambda cp: cp.wait())

    def drain(r, c):
        row_copy(r, 0).wait()
        row_copy(r, 1).wait()
        return c

    lax.fori_loop(0, tt, drain, 0, unroll=4)


def _dispatch(dest, pad_start, pad_len, tail, h2_packed, n_rows):
    t = h2_packed.shape[0]
    grid_spec = pltpu.PrefetchScalarGridSpec(
        num_scalar_prefetch=4,
        grid=(t // DISPATCH_TOKENS,),
        in_specs=[pl.BlockSpec((DISPATCH_TOKENS, HALF_D), lambda i, d, ps, plen, tl: (i, 0))],
        out_specs=pl.BlockSpec(memory_space=pl.ANY),
        scratch_shapes=[pltpu.VMEM((SUB_ROWS, HALF_D), jnp.uint32),
                        pltpu.SemaphoreType.DMA(()), pltpu.SemaphoreType.DMA(())],
    )
    return pl.pallas_call(
        _dispatch_kernel,
        grid_spec=grid_spec,
        out_shape=jax.ShapeDtypeStruct((n_rows, HALF_D), jnp.uint32),
        compiler_params=_params(("arbitrary",)),
        name="dispatch",
    )(dest, pad_start, pad_len, tail, h2_packed)


def _expert_kernel(exp_ref, nc_ref, nz_ref, osub_ref, xsub_ref, x0_ref, x1_ref, x2_ref, x3_ref,
                   wg_ref, wu_ref, wd_ref, y_hbm, a_scr, ytile, sem, *, nf, nd):
    s = pl.program_id(0)
    t = pl.program_id(1)
    n_comp = nc_ref[s]
    n_out = n_comp + nz_ref[s]
    x_refs = (x0_ref, x1_ref, x2_ref, x3_ref)

    @pl.when(jnp.logical_and(t < nf, n_comp > 0))
    def _():
        wg16 = wg_ref[...].astype(BF16)
        wu16 = wu_ref[...].astype(BF16)

        def up(k):
            x = _unpack_bf16_pair(x_refs[k][...])
            g = jnp.dot(x, wg16, preferred_element_type=F32)
            u = jnp.dot(x, wu16, preferred_element_type=F32)
            a_scr[t, k * SUB_ROWS:(k + 1) * SUB_ROWS, :] = (g * _sigmoid(g) * u).astype(BF16)

        up(0)
        for k in range(1, SUPER):
            pl.when(k < n_comp)(functools.partial(up, k))

    @pl.when(t >= nf)
    def _():
        j = t - nf
        slot = j % 2

        @pl.when(n_comp > 0)
        def _():
            wd16 = wd_ref[...].astype(BF16)

            def down(k):
                rows = slice(k * SUB_ROWS, (k + 1) * SUB_ROWS)
                a = jnp.concatenate([a_scr[f, rows, :] for f in range(nf)], axis=1)
                acc = jnp.dot(a, wd16, preferred_element_type=F32)
                ytile[slot, rows, :] = _pack_bf16_pair(acc[:, :TN_DOWN // 2], acc[:, TN_DOWN // 2:])

            down(0)
            for k in range(1, SUPER):
                pl.when(k < n_comp)(functools.partial(down, k))

        @pl.when(n_comp == 0)
        def _():
            ytile[slot] = jnp.zeros(ytile.shape[1:], jnp.uint32)

        def out_copy(sl, k, jj):
            dst_rows = pl.ds(pl.multiple_of((osub_ref[s] + k) * SUB_ROWS, SUB_ROWS), SUB_ROWS)
            dst_cols = pl.ds(pl.multiple_of(jj * (TN_DOWN // 2), TN_DOWN // 2), TN_DOWN // 2)
            return pltpu.make_async_copy(ytile.at[sl, pl.ds(k * SUB_ROWS, SUB_ROWS), :],
                                         y_hbm.at[dst_rows, dst_cols], sem)

        for k in range(SUPER):
            @pl.when(jnp.logical_and(j > 0, k < n_out))
            def _():
                out_copy(1 - slot, k, j - 1).wait()
        for k in range(SUPER):
            @pl.when(k < n_out)
            def _():
                out_copy(slot, k, j).start()
        for k in range(SUPER):
            @pl.when(jnp.logical_and(j == nd - 1, k < n_out))
            def _():
                out_copy(slot, k, j).wait()


def _experts(xs, sup_exp, n_comp, n_zero, out_sub, x_sub, w_gate, w_up, w_down):
    n_super = sup_exp.shape[0]
    n_rows = xs.shape[0]
    nf = D_EXPERT // F_CHUNK
    nd = D_MODEL // TN_DOWN

    def nxt(s, nc):
        s2 = jnp.minimum(s + 1, n_super - 1)
        return s2, jnp.logical_and(nc[s] > 0, nc[s2] > 0)

    def up_idx(s, t, e, nc, nz, osub, xsub):
        s2, has_next = nxt(s, nc)
        ahead = jnp.logical_and(t >= nf, has_next)
        chunk = jnp.where(nc[s] > 0, jnp.minimum(t, nf - 1), nf - 1)
        return (jnp.where(ahead, e[s2], e[s]), 0, jnp.where(ahead, 0, chunk))

    def down_idx(s, t, e, nc, nz, osub, xsub):
        sp = jnp.maximum(s - 1, 0)
        behind = jnp.logical_and(jnp.logical_and(t == 0, s > 0), nc[s] > 0)
        chunk = jnp.where(nc[s] > 0, jnp.maximum(t - nf, 0), nd - 1)
        return (jnp.where(behind, e[sp], e[s]), 0, jnp.where(behind, nd - 1, chunk))

    def x_spec(k):
        def idx(s, t, e, nc, nz, osub, xsub):
            s2, has_next = nxt(s, nc)
            return (xsub[jnp.where(jnp.logical_and(t >= nf, has_next), s2, s) * SUPER + k], 0)

        return pl.BlockSpec((SUB_ROWS, HALF_D), idx)

    grid_spec = pltpu.PrefetchScalarGridSpec(
        num_scalar_prefetch=5,
        grid=(n_super, nf + nd),
        in_specs=[x_spec(0), x_spec(1), x_spec(2), x_spec(3),
                  pl.BlockSpec((None, D_MODEL, F_CHUNK), up_idx),
                  pl.BlockSpec((None, D_MODEL, F_CHUNK), up_idx),
                  pl.BlockSpec((None, D_EXPERT, TN_DOWN), down_idx)],
        out_specs=pl.BlockSpec(memory_space=pl.ANY),
        scratch_shapes=[pltpu.VMEM((nf, SUPER * SUB_ROWS, F_CHUNK), BF16),
                        pltpu.VMEM((2, SUPER * SUB_ROWS, TN_DOWN // 2), jnp.uint32),
                        pltpu.SemaphoreType.DMA(())],
    )
    return pl.pallas_call(
        functools.partial(_expert_kernel, nf=nf, nd=nd),
        grid_spec=grid_spec,
        out_shape=jax.ShapeDtypeStruct((n_rows, HALF_D), jnp.uint32),
        compiler_params=_params(("arbitrary", "arbitrary"), 60 * 1024 * 1024),
        name="experts",
    )(sup_exp, n_comp, n_zero, out_sub, x_sub, xs, xs, xs, xs, w_gate, w_up, w_down)


def _final_kernel(dest_ref, y_hbm, z_ref, wt_ref, gate_ref, g1_ref, b1_ref, g2_ref, b2_ref,
                  op_ref, os_ref, ybuf, sem, *, n_p_tiles, tiles_per_seq):
    i = pl.program_id(0)
    n_tiles = pl.num_programs(0)
    tm = z_ref.shape[0]
    slot = i % 2

    def row_copy(src_row, sl, k, r):
        return pltpu.make_async_copy(y_hbm.at[pl.ds(src_row, 1)], ybuf.at[sl, k, pl.ds(r, 1)], sem.at[sl])

    def gather(tile, sl):
        base = tile * tm * 2

        def issue(g, c):
            r0 = pl.multiple_of(g * ISSUE_GROUP, ISSUE_GROUP)
            for k in range(ISSUE_GROUP):
                row_copy(dest_ref[base + 2 * (r0 + k)], sl, 0, r0 + k).start(priority=0)
                row_copy(dest_ref[base + 2 * (r0 + k) + 1], sl, 1, r0 + k).start(priority=1)
            return c

        lax.fori_loop(0, tm // ISSUE_GROUP, issue, 0)

    @pl.when(i == 0)
    def _():
        gather(0, 0)

    @pl.when(i + 1 < n_tiles)
    def _():
        gather(i + 1, 1 - slot)

    def drain(r, c):
        row_copy(0, slot, 0, r).wait()
        row_copy(0, slot, 1, r).wait()
        return c

    lax.fori_loop(0, tm, drain, 0, unroll=4)

    is_p = i < n_p_tiles
    row = jnp.where(is_p, 0, 1 + (i - n_p_tiles) // tiles_per_seq)
    wt = wt_ref[...]
    w0, w1 = wt[:, 0:1], wt[:, 1:2]
    half_tile = TN_DOWN // 2
    pieces = []
    for j in range(D_MODEL // TN_DOWN):
        y0 = ybuf[slot, 0, :, j * half_tile:(j + 1) * half_tile]
        y1 = ybuf[slot, 1, :, j * half_tile:(j + 1) * half_tile]
        pieces.append(w0 * pltpu.bitcast(y0 << 16, F32) + w1 * pltpu.bitcast(y1 << 16, F32))
        pieces.append(w0 * pltpu.bitcast(y0 & jnp.uint32(0xFFFF0000), F32)
                      + w1 * pltpu.bitcast(y1 & jnp.uint32(0xFFFF0000), F32))
    f = jnp.concatenate(pieces, axis=1)
    x1 = _ln(z_ref[...]) * g1_ref[...] + b1_ref[...]
    out = _ln(ALPHA * x1 + gate_ref[pl.ds(row, 1), :] * f) * g2_ref[...] + b2_ref[...]

    @pl.when(is_p)
    def _():
        op_ref[...] = out

    @pl.when(jnp.logical_not(is_p))
    def _():
        os_ref[...] = out


def _combine(dest, y_rows, z, weight, gate2, g1, b1, g2, b2, n_prompt, dec_seq):
    t = z.shape[0]
    tm = TM_FIN
    n_p = n_prompt // tm
    n_s = (t - n_prompt) // tm
    full = lambda i, d: (0, 0)
    grid_spec = pltpu.PrefetchScalarGridSpec(
        num_scalar_prefetch=1,
        grid=(n_p + n_s,),
        in_specs=[pl.BlockSpec(memory_space=pl.ANY),
                  pl.BlockSpec((tm, D_MODEL), lambda i, d: (i, 0)),
                  pl.BlockSpec((tm, 2), lambda i, d: (i, 0)),
                  pl.BlockSpec((8, D_MODEL), full),
                  pl.BlockSpec((1, D_MODEL), full), pl.BlockSpec((1, D_MODEL), full),
                  pl.BlockSpec((1, D_MODEL), full), pl.BlockSpec((1, D_MODEL), full)],
        out_specs=[pl.BlockSpec((tm, D_MODEL), lambda i, d: (jnp.minimum(i, n_p - 1), 0)),
                   pl.BlockSpec((tm, D_MODEL), lambda i, d: (jnp.maximum(i - n_p, 0), 0))],
        scratch_shapes=[pltpu.VMEM((2, 2, tm, HALF_D), jnp.uint32), pltpu.SemaphoreType.DMA((2,))],
    )
    return pl.pallas_call(
        functools.partial(_final_kernel, n_p_tiles=n_p, tiles_per_seq=dec_seq // tm),
        grid_spec=grid_spec,
        out_shape=[jax.ShapeDtypeStruct((n_prompt, D_MODEL), F32),
                   jax.ShapeDtypeStruct((t - n_prompt, D_MODEL), F32)],
        compiler_params=_params(("arbitrary",)),
        name="combine",
    )(dest, y_rows, z, weight, gate2, g1, b1, g2, b2)


def kernel(x_prompt, x_sample, state_lru, state_ret, c, c_ctx, w_mod, b_mod, w_in, conv_w, conv_b, lru_wa, lru_ba,
           lru_wx, lru_bx, lru_lam, ret_decay, w_out, ln1_g, ln1_b, router_g, router_g_b, router_e, router_e_b,
           w_gate, w_up, w_down, ln2_g, ln2_b):
    assert w_in.shape[0] == 1, "single trunk layer"
    nb, seq, d = x_prompt.shape
    nbs, dec_seq, _ = x_sample.shape
    tp, ts = nb * seq, nbs * dec_seq
    assert tp % dec_seq == 0 and d == D_MODEL

    cond = jnp.zeros((8, d), F32).at[0].set(c_ctx).at[1:1 + nbs].set(c)
    mod = _modulation(cond, w_mod[0], b_mod[0][None, :])
    shift1, scale1, gate1, shift2, scale2, gate2 = [mod[:, k * d:(k + 1) * d] for k in range(6)]

    xp = x_prompt.reshape(tp, d)
    xs = x_sample.reshape(ts, d)
    proj = _in_projection(_ln_modulate(xp, xs, shift1, scale1, dec_seq), w_in[0].astype(BF16))

    w_gates = (0.5 * jnp.concatenate([lru_wa[0, 0], lru_wx[0, 0], lru_wa[0, 1], lru_wx[0, 1]], -1)).astype(BF16)
    pb = jnp.stack([0.5 * lru_ba[0, 0], 0.5 * lru_bx[0, 0], lru_lam[0, 0],
                    0.5 * lru_ba[0, 1], 0.5 * lru_bx[0, 1], lru_lam[0, 1]], 0)
    proj_p = proj.reshape((tp + ts) // seq, seq, IN_COLS)
    proj_s = proj.reshape((tp + ts) // dec_seq, dec_seq, IN_COLS)
    ylp, st_lru = _lru_mixer(proj_p, 0, nb, 16, 1, conv_w[0], conv_b[0][None, :], w_gates, pb,
                             jnp.zeros((2, nb, LRU_WIDTH), F32))
    yls, _ = _lru_mixer(proj_s, tp // dec_seq, nbs, nbs, 4, conv_w[0], conv_b[0][None, :], w_gates, pb,
                        jnp.swapaxes(state_lru[:, 0], 0, 1))

    decay = jnp.broadcast_to(ret_decay[0].T[:, :, None], (RET_HEADS, 2, LANES))
    yrp, st_ret = _ret_mixer(proj_p, 0, nb, 8, decay, emit_state=True)
    (yrs,) = _ret_mixer(proj_s, tp // dec_seq, nbs, 1, decay, rope_tabs=_rope_tables(dec_seq), s0=state_ret)

    w_router = jnp.concatenate(
        [router_g[0], jnp.transpose(router_e[0], (1, 0, 2)).reshape(d, N_EXPERTS),
         jnp.zeros((d, LANES - N_GROUPS - N_EXPERTS), F32)], -1).astype(BF16)
    b_router = jnp.concatenate([router_g_b[0], router_e_b[0].reshape(-1),
                                jnp.zeros((LANES - N_GROUPS - N_EXPERTS,), F32)])[None, :]
    z = _out_projection(xp, xs, ylp.reshape(tp, -1), yrp.reshape(tp, -1), yls.reshape(ts, -1), yrs.reshape(ts, -1),
                        w_out[0].astype(BF16), gate1, dec_seq)
    h2, route = _norm_router(z, tp, ln1_g, ln1_b, shift2, scale2, w_router, b_router, dec_seq)
    expert = route[:, 0:2].astype(jnp.int32)
    weight = route[:, 2:4]

    n_assign = 2 * (tp + ts)
    n_sub = -(-(n_assign + N_EXPERTS * (SUB_ROWS - 1)) // SUB_ROWS)
    n_super = N_EXPERTS + -(-n_assign // (SUB_ROWS * SUPER))
    assert SUPER * n_super >= n_sub + (SUPER - 1) * N_EXPERTS
    dest, fill_plan, super_plan = _dispatch_plan(expert, n_sub, n_super)
    xs_rows = _dispatch(dest, *fill_plan, h2, n_sub * SUB_ROWS)
    y_rows = _experts(xs_rows, *super_plan, w_gate[0], w_up[0], w_down[0])
    y_p, y_s = _combine(dest, y_rows, z, weight, gate2, ln1_g, ln1_b, ln2_g, ln2_b, tp, dec_seq)

    new_state_lru = jnp.swapaxes(st_lru, 0, 1)[:, None]
    return (y_p.reshape(nb, seq, d), y_s.reshape(nbs, dec_seq, d), new_state_lru, st_ret)
```

```python
import functools

import jax
import jax.numpy as jnp
import numpy as np
from jax import lax
from jax.experimental import pallas as pl
from jax.experimental.pallas import tpu as pltpu

F32 = jnp.float32
BF16 = jnp.bfloat16

D_MODEL = 4096
LRU_WIDTH = 2048
LRU_BLOCKS = 16
LANES = 128
RET_HEADS = 16
RET_DH = 128
IN_COLS = 12288
GRID_W = 64
ROPE_BASE = 10000.0
LRU_C = 8.0
N_GROUPS = 4
EXPERTS_PER_GROUP = 8
N_EXPERTS = 32
D_EXPERT = 1024
LN_EPS = 1e-6
GN_EPS = 1e-5
ALPHA = 2.0 ** 0.25

VMEM_LIMIT = 56 * 1024 * 1024
SCAN_PAD = 8

TN_MOD = 1024
TM_LN = 512
TM_PROJ = 1024
TN_IN = 1536
TN_OUT = 512
TM_OUT = 1024
TM_NORM = 512
SUB_ROWS = 256
SUPER = 4
F_CHUNK = 256
TN_DOWN = 1024
DISPATCH_TOKENS = 512
ISSUE_GROUP = 16
TM_FIN = 256
HALF_D = D_MODEL // 2


def _params(sem, vmem_limit=VMEM_LIMIT):
    return pltpu.CompilerParams(dimension_semantics=sem, vmem_limit_bytes=vmem_limit)


def _pack_bf16_pair(lo, hi):
    lo_bits = pltpu.bitcast(lo.astype(BF16).astype(F32), jnp.uint32) >> 16
    hi_bits = pltpu.bitcast(hi.astype(BF16).astype(F32), jnp.uint32) & jnp.uint32(0xFFFF0000)
    return lo_bits | hi_bits


def _unpack_bf16_pair(words):
    lo = pltpu.bitcast(words << 16, F32).astype(BF16)
    hi = pltpu.bitcast(words & jnp.uint32(0xFFFF0000), F32).astype(BF16)
    return jnp.concatenate([lo, hi], axis=1)


def _sigmoid(x):
    return 0.5 * jnp.tanh(0.5 * x) + 0.5


def _softplus(x):
    return jnp.maximum(x, 0.0) + jnp.log1p(jnp.exp(-jnp.abs(x)))


def _ln(x):
    mu = jnp.mean(x, -1, keepdims=True)
    xc = x - mu
    var = jnp.mean(xc * xc, -1, keepdims=True)
    return xc * lax.rsqrt(var + LN_EPS)


def _mod_kernel(cond_ref, w_ref, b_ref, o_ref):
    c = cond_ref[...]
    s = (c * _sigmoid(c)).astype(BF16)
    o_ref[...] = jnp.dot(s, w_ref[...].astype(BF16), preferred_element_type=F32) + b_ref[...]


def _modulation(cond, w_mod, b_mod):
    tn = TN_MOD
    n = w_mod.shape[1]
    return pl.pallas_call(
        _mod_kernel,
        grid=(n // tn,),
        in_specs=[pl.BlockSpec((8, D_MODEL), lambda j: (0, 0)),
                  pl.BlockSpec((D_MODEL, tn), lambda j: (0, j)),
                  pl.BlockSpec((1, tn), lambda j: (0, j))],
        out_specs=pl.BlockSpec((8, tn), lambda j: (0, j)),
        out_shape=jax.ShapeDtypeStruct((8, n), F32),
        compiler_params=_params(("arbitrary",)),
        name="modulation",
    )(cond, w_mod, b_mod)


def _ln_mod_kernel(xp_ref, xs_ref, shift_ref, scale_ref, h_ref, *, n_p_tiles, tiles_per_seq):
    i = pl.program_id(0)

    def fill(x_ref, row):
        h = _ln(x_ref[...]) * (1.0 + scale_ref[pl.ds(row, 1), :]) + shift_ref[pl.ds(row, 1), :]
        h_ref[...] = h.astype(BF16)

    @pl.when(i < n_p_tiles)
    def _():
        fill(xp_ref, 0)

    @pl.when(i >= n_p_tiles)
    def _():
        fill(xs_ref, 1 + (i - n_p_tiles) // tiles_per_seq)


def _ln_modulate(xp, xs, shift, scale, dec_seq):
    tp, ts = xp.shape[0], xs.shape[0]
    tm = TM_LN
    n_p, n_s = tp // tm, ts // tm
    kern = functools.partial(_ln_mod_kernel, n_p_tiles=n_p, tiles_per_seq=dec_seq // tm)
    return pl.pallas_call(
        kern,
        grid=(n_p + n_s,),
        in_specs=[pl.BlockSpec((tm, D_MODEL), lambda i: (jnp.minimum(i, n_p - 1), 0)),
                  pl.BlockSpec((tm, D_MODEL), lambda i: (jnp.maximum(i - n_p, 0), 0)),
                  pl.BlockSpec((8, D_MODEL), lambda i: (0, 0)),
                  pl.BlockSpec((8, D_MODEL), lambda i: (0, 0))],
        out_specs=pl.BlockSpec((tm, D_MODEL), lambda i: (i, 0)),
        out_shape=jax.ShapeDtypeStruct((tp + ts, D_MODEL), BF16),
        compiler_params=_params(("arbitrary",)),
        name="ln_modulate",
    )(xp, xs, shift, scale)


def _matmul_kernel(h_ref, w_ref, o_ref):
    o_ref[...] = jnp.dot(h_ref[...], w_ref[...], preferred_element_type=F32).astype(o_ref.dtype)


def _in_projection(h, w_in_bf16):
    t = h.shape[0]
    tm, tn = TM_PROJ, TN_IN
    return pl.pallas_call(
        _matmul_kernel,
        grid=(t // tm, IN_COLS // tn),
        in_specs=[pl.BlockSpec((tm, D_MODEL), lambda i, j: (i, 0)),
                  pl.BlockSpec((D_MODEL, tn), lambda i, j: (0, j))],
        out_specs=pl.BlockSpec((tm, tn), lambda i, j: (i, j)),
        out_shape=jax.ShapeDtypeStruct((t, IN_COLS), BF16),
        compiler_params=_params(("arbitrary", "arbitrary")),
        name="in_projection",
    )(h, w_in_bf16)


def _lru_kernel(x_ref, g_ref, cw_ref, cb_ref, wg_ref, pb_ref, h0_ref, y_ref, st_ref,
                xc_scr, af_scr, uf_scr, ab_scr, ub_scr, *, bg, seq, pitch, ncb):
    rows = lax.broadcasted_iota(jnp.int32, (seq, LANES), 0)
    for c in range(ncb):
        lanes = slice(c * LANES, (c + 1) * LANES)
        w = cw_ref[:, lanes]
        bias = cb_ref[:, lanes]
        for b in range(bg):
            x = x_ref[b, :, lanes].astype(F32)
            xm2 = jnp.where(rows >= 2, pltpu.roll(x, 2, 0), 0.0)
            xm1 = jnp.where(rows >= 1, pltpu.roll(x, 1, 0), 0.0)
            xp1 = jnp.where(rows < seq - 1, pltpu.roll(x, seq - 1, 0), 0.0)
            xc_scr[pl.ds(b * seq, seq), :] = bias + xm2 * w[0:1] + xm1 * w[1:2] + x * w[2:3] + xp1 * w[3:4]

        xc = xc_scr[...]
        gates = jnp.dot(xc.astype(BF16), wg_ref[c], preferred_element_type=F32)
        pb = pb_ref[:, lanes]
        half_xc = 0.5 * xc
        for d, (a_scr, u_scr) in enumerate(((af_scr, uf_scr), (ab_scr, ub_scr))):
            t_r = jnp.tanh(gates[:, (2 * d) * LANES:(2 * d + 1) * LANES] + pb[3 * d:3 * d + 1])
            t_i = jnp.tanh(gates[:, (2 * d + 1) * LANES:(2 * d + 2) * LANES] + pb[3 * d + 1:3 * d + 2])
            c4 = (0.5 * LRU_C) * _softplus(-pb[3 * d + 2:3 * d + 3])
            neg_log_a = c4 * t_r + c4
            a = jnp.exp(-neg_log_a)
            y = jnp.tanh(neg_log_a) * (1.0 + a * a)
            root = jnp.where(y > 0.0, y * lax.rsqrt(y), 0.0)
            u = root * (half_xc * (t_i + 1.0))
            for b in range(bg):
                a_scr[c, pl.ds(b * pitch, seq), :] = a[b * seq:(b + 1) * seq]
                u_scr[c, pl.ds(b * pitch, seq), :] = u[b * seq:(b + 1) * seq]

    def step(t, carry):
        tb = seq - 1 - t
        out = []
        for c in range(ncb):
            hf, hb = carry[2 * c], carry[2 * c + 1]
            hf = af_scr[c, pl.ds(t, bg, stride=pitch), :] * hf + uf_scr[c, pl.ds(t, bg, stride=pitch), :]
            uf_scr[c, pl.ds(t, bg, stride=pitch), :] = hf
            hb = ab_scr[c, pl.ds(tb, bg, stride=pitch), :] * hb + ub_scr[c, pl.ds(tb, bg, stride=pitch), :]
            ub_scr[c, pl.ds(tb, bg, stride=pitch), :] = hb
            out += [hf, hb]
        return tuple(out)

    init = []
    for c in range(ncb):
        init += [h0_ref[0, :, c * LANES:(c + 1) * LANES], h0_ref[1, :, c * LANES:(c + 1) * LANES]]
    last = lax.fori_loop(0, seq, step, tuple(init), unroll=8)
    for c in range(ncb):
        lanes = slice(c * LANES, (c + 1) * LANES)
        st_ref[0, :, lanes] = last[2 * c]
        st_ref[1, :, lanes] = last[2 * c + 1]
        for b in range(bg):
            hs = uf_scr[c, pl.ds(b * pitch, seq), :] + ub_scr[c, pl.ds(b * pitch, seq), :]
            y_ref[b, :, lanes] = (jax.nn.gelu(g_ref[b, :, lanes].astype(F32)) * hs).astype(BF16)


def _lru_mixer(proj3, seq0, nseq, bg, ncb, conv_w, conv_b, w_gates, pb, h0):
    seq = proj3.shape[1]
    pitch = seq + SCAN_PAD
    off = seq0 // bg
    cw = ncb * LANES
    ncol = LRU_WIDTH // cw
    kern = functools.partial(_lru_kernel, bg=bg, seq=seq, pitch=pitch, ncb=ncb)
    return pl.pallas_call(
        kern,
        grid=(nseq // bg, ncol),
        in_specs=[pl.BlockSpec((bg, seq, cw), lambda b, c: (b + off, 0, c)),
                  pl.BlockSpec((bg, seq, cw), lambda b, c: (b + off, 0, ncol + c)),
                  pl.BlockSpec((4, cw), lambda b, c: (0, c)),
                  pl.BlockSpec((1, cw), lambda b, c: (0, c)),
                  pl.BlockSpec((ncb, LANES, 4 * LANES), lambda b, c: (c, 0, 0)),
                  pl.BlockSpec((6, cw), lambda b, c: (0, c)),
                  pl.BlockSpec((2, bg, cw), lambda b, c: (0, b, c))],
        out_specs=[pl.BlockSpec((bg, seq, cw), lambda b, c: (b, 0, c)),
                   pl.BlockSpec((2, bg, cw), lambda b, c: (0, b, c))],
        out_shape=[jax.ShapeDtypeStruct((nseq, seq, LRU_WIDTH), BF16),
                   jax.ShapeDtypeStruct((2, nseq, LRU_WIDTH), F32)],
        scratch_shapes=[pltpu.VMEM((bg * seq, LANES), F32)] + [pltpu.VMEM((ncb, bg * pitch, LANES), F32)] * 4,
        compiler_params=_params(("arbitrary", "arbitrary")),
        name="lru_mixer",
    )(proj3, proj3, conv_w, conv_b, w_gates, pb, h0)


def _rope(x, cos, sin_signed, first_half):
    partner = jnp.where(first_half, pltpu.roll(x, LANES - 32, 1), pltpu.roll(x, 32, 1))
    return x * cos + partner * sin_signed


def _ret_kernel(*refs, bg, seq, rope, has_state, emit_state, qb):
    refs = list(refs)
    q_ref, k_ref, v_ref, g_ref, dec_ref = refs[:5]
    pos = 5
    if rope:
        cos_ref, sin_ref = refs[pos:pos + 2]
        pos += 2
    if has_state:
        s0_ref = refs[pos]
        pos += 1
    y_ref = refs[pos]
    pos += 1
    if emit_state:
        st_ref = refs[pos]
        pos += 1
    mask_scr = refs[pos]

    log_g = -_softplus(-dec_ref[0])
    lgf, lgb = log_g[0:1], log_g[1:2]
    reps = seq // LANES
    lgf_row = jnp.concatenate([lgf] * reps, axis=1)
    lgb_row = jnp.concatenate([lgb] * reps, axis=1)

    @pl.when(pl.program_id(1) == 0)
    def _():
        for blk in range(seq // qb):
            ti = lax.broadcasted_iota(jnp.int32, (qb, seq), 0) + blk * qb
            si = lax.broadcasted_iota(jnp.int32, (qb, seq), 1)
            dist = (ti - si).astype(F32)
            e = jnp.where(dist >= 0, dist * lgf_row, (-dist) * lgb_row)
            mask_scr[pl.ds(blk * qb, qb), :] = jnp.where(dist == 0, 2.0, jnp.exp(e))

    lane = lax.broadcasted_iota(jnp.int32, (seq, LANES), 1)
    first_half = (lane % 64) < 32
    trow = lax.broadcasted_iota(jnp.int32, (seq, LANES), 0).astype(F32)
    for b in range(bg):
        q = q_ref[b].astype(F32)
        k = k_ref[b].astype(F32) * (RET_DH ** -0.5)
        v16 = v_ref[b]
        if rope:
            q = _rope(q, cos_ref[...], sin_ref[...], first_half)
            k = _rope(k, cos_ref[...], sin_ref[...], first_half)
        q16 = q.astype(BF16)
        k16 = k.astype(BF16)
        if has_state:
            qf16 = (q * jnp.exp((trow + 1.0) * lgf)).astype(BF16)
            qb16 = (q * jnp.exp((float(seq) - trow) * lgb)).astype(BF16)
            s0f = s0_ref[b, 0].astype(BF16)
            s0b = s0_ref[b, 1].astype(BF16)
        for blk in range(seq // qb):
            sl = slice(blk * qb, (blk + 1) * qb)
            s = lax.dot_general(q16[sl], k16, (((1,), (1,)), ((), ())), preferred_element_type=F32)
            p = (s * mask_scr[pl.ds(blk * qb, qb), :]).astype(BF16)
            o = jnp.dot(p, v16, preferred_element_type=F32)
            if has_state:
                o = o + jnp.dot(qf16[sl], s0f, preferred_element_type=F32)
                o = o + jnp.dot(qb16[sl], s0b, preferred_element_type=F32)
            mu = jnp.mean(o, -1, keepdims=True)
            oc = o - mu
            var = jnp.mean(oc * oc, -1, keepdims=True)
            on = oc * lax.rsqrt(var + GN_EPS)
            gt = g_ref[b, pl.ds(blk * qb, qb), :].astype(F32)
            y_ref[b, pl.ds(blk * qb, qb), :] = (gt * _sigmoid(gt) * on).astype(BF16)
        if emit_state:
            kf16 = (k * jnp.exp((float(seq - 1) - trow) * lgf)).astype(BF16)
            kb16 = (k * jnp.exp(trow * lgb)).astype(BF16)
            sf = lax.dot_general(kf16, v16, (((0,), (0,)), ((), ())), preferred_element_type=F32)
            sb = lax.dot_general(kb16, v16, (((0,), (0,)), ((), ())), preferred_element_type=F32)
            if has_state:
                sf = sf + jnp.exp(float(seq) * lgf) * s0_ref[b, 0]
                sb = sb + jnp.exp(float(seq) * lgb) * s0_ref[b, 1]
            st_ref[b, 0] = sf
            st_ref[b, 1] = sb


def _ret_mixer(proj3, seq0, nseq, bg, decay, rope_tabs=None, s0=None, emit_state=False):
    seq = proj3.shape[1]
    off = seq0 // bg
    qb = min(seq, 256)
    rope = rope_tabs is not None
    has_state = s0 is not None
    kern = functools.partial(_ret_kernel, bg=bg, seq=seq, rope=rope, has_state=has_state,
                             emit_state=emit_state, qb=qb)

    def col(base):
        return pl.BlockSpec((bg, seq, LANES), lambda h, b: (b + off, 0, base + h))

    st_spec = pl.BlockSpec((bg, None, 2, None, RET_DH, RET_DH), lambda h, b: (b, 0, 0, h, 0, 0))
    in_specs = [col(2 * LRU_BLOCKS), col(2 * LRU_BLOCKS + RET_HEADS), col(2 * LRU_BLOCKS + 2 * RET_HEADS),
                col(2 * LRU_BLOCKS + 3 * RET_HEADS), pl.BlockSpec((1, 2, LANES), lambda h, b: (h, 0, 0))]
    args = [proj3, proj3, proj3, proj3, decay]
    if rope:
        in_specs += [pl.BlockSpec((seq, LANES), lambda h, b: (0, 0))] * 2
        args += list(rope_tabs)
    if has_state:
        in_specs.append(st_spec)
        args.append(s0)
    out_specs = [pl.BlockSpec((bg, seq, LANES), lambda h, b: (b, 0, h))]
    out_shape = [jax.ShapeDtypeStruct((nseq, seq, RET_HEADS * RET_DH), BF16)]
    if emit_state:
        out_specs.append(st_spec)
        out_shape.append(jax.ShapeDtypeStruct((nseq, 1, 2, RET_HEADS, RET_DH, RET_DH), F32))
    return pl.pallas_call(
        kern,
        grid=(RET_HEADS, nseq // bg),
        in_specs=in_specs,
        out_specs=out_specs,
        out_shape=out_shape,
        scratch_shapes=[pltpu.VMEM((seq, seq), F32)],
        compiler_params=_params(("arbitrary", "arbitrary")),
        name="ret_mixer",
    )(*args)


def _rope_tables(seq):
    nf = RET_DH // 4
    freqs = (np.float32(ROPE_BASE) ** (-np.arange(nf, dtype=np.float32) / np.float32(nf))).astype(np.float32)
    t = np.arange(seq)
    row = (t // GRID_W).astype(np.float32)[:, None] * freqs[None, :]
    colp = (t % GRID_W).astype(np.float32)[:, None] * freqs[None, :]
    cos = np.concatenate([np.cos(row), np.cos(row), np.cos(colp), np.cos(colp)], -1)
    sin = np.concatenate([-np.sin(row), np.sin(row), -np.sin(colp), np.sin(colp)], -1)
    return jnp.asarray(cos, F32), jnp.asarray(sin, F32)


def _outproj_kernel(xp_ref, xs_ref, ylp_ref, yrp_ref, yls_ref, yrs_ref, wa_ref, wb_ref, gate_ref, z_ref,
                    *, n_p_tiles, tiles_per_seq):
    i = pl.program_id(0)
    is_p = i < n_p_tiles
    row = jnp.where(is_p, 0, 1 + (i - n_p_tiles) // tiles_per_seq)

    def mix(x_ref, yl_ref, yr_ref):
        m = jnp.dot(yl_ref[...], wa_ref[...], preferred_element_type=F32)
        m = m + jnp.dot(yr_ref[...], wb_ref[...], preferred_element_type=F32)
        z_ref[...] = ALPHA * x_ref[...] + gate_ref[pl.ds(row, 1), :] * m

    @pl.when(is_p)
    def _():
        mix(xp_ref, ylp_ref, yrp_ref)

    @pl.when(jnp.logical_not(is_p))
    def _():
        mix(xs_ref, yls_ref, yrs_ref)


def _out_projection(xp, xs, ylp, yrp, yls, yrs, w_out_bf16, gate1, dec_seq):
    tp, ts = xp.shape[0], xs.shape[0]
    tm, tn = TM_OUT, TN_OUT
    n_p, n_s = tp // tm, ts // tm
    half = LRU_WIDTH
    kern = functools.partial(_outproj_kernel, n_p_tiles=n_p, tiles_per_seq=dec_seq // tm)
    p_idx = lambda i, j: (jnp.minimum(i, n_p - 1), 0)
    s_idx = lambda i, j: (jnp.maximum(i - n_p, 0), 0)
    once = pl.Buffered(1)
    return pl.pallas_call(
        kern,
        grid=(n_p + n_s, D_MODEL // tn),
        in_specs=[pl.BlockSpec((tm, tn), lambda i, j: (jnp.minimum(i, n_p - 1), j)),
                  pl.BlockSpec((tm, tn), lambda i, j: (jnp.maximum(i - n_p, 0), j)),
                  pl.BlockSpec((tm, half), p_idx), pl.BlockSpec((tm, half), p_idx),
                  pl.BlockSpec((tm, half), s_idx, pipeline_mode=once),
                  pl.BlockSpec((tm, half), s_idx, pipeline_mode=once),
                  pl.BlockSpec((half, tn), lambda i, j: (0, j)),
                  pl.BlockSpec((half, tn), lambda i, j: (1, j)),
                  pl.BlockSpec((8, tn), lambda i, j: (0, j))],
        out_specs=pl.BlockSpec((tm, tn), lambda i, j: (i, j)),
        out_shape=jax.ShapeDtypeStruct((tp + ts, D_MODEL), F32),
        compiler_params=_params(("arbitrary", "arbitrary")),
        name="out_projection",
    )(xp, xs, ylp, yrp, yls, yrs, w_out_bf16, w_out_bf16, gate1)


def _norm_router_kernel(z_ref, lng_ref, lnb_ref, shift_ref, scale_ref, wr_ref, br_ref, h2_ref, lg_ref, x1_scr,
                        *, n_p_tiles, tiles_per_seq, tn):
    i = pl.program_id(0)
    row = jnp.where(i < n_p_tiles, 0, 1 + (i - n_p_tiles) // tiles_per_seq)
    tm = z_ref.shape[0]
    nj = D_MODEL // tn
    inv_d = 1.0 / D_MODEL
    s1 = jnp.zeros((tm, 1), F32)
    for c in range(nj):
        s1 = s1 + jnp.sum(z_ref[:, c * tn:(c + 1) * tn], -1, keepdims=True)
    mu = s1 * inv_d
    s2 = jnp.zeros((tm, 1), F32)
    for c in range(nj):
        zc = z_ref[:, c * tn:(c + 1) * tn] - mu
        s2 = s2 + jnp.sum(zc * zc, -1, keepdims=True)
    rstd = lax.rsqrt(s2 * inv_d + LN_EPS)
    t1 = jnp.zeros((tm, 1), F32)
    for c in range(nj):
        cs = slice(c * tn, (c + 1) * tn)
        x1 = (z_ref[:, cs] - mu) * rstd * lng_ref[:, cs] + lnb_ref[:, cs]
        x1_scr[c] = x1
        t1 = t1 + jnp.sum(x1, -1, keepdims=True)
    mu2 = t1 * inv_d
    t2 = jnp.zeros((tm, 1), F32)
    for c in range(nj):
        xc = x1_scr[c] - mu2
        t2 = t2 + jnp.sum(xc * xc, -1, keepdims=True)
    rstd2 = lax.rsqrt(t2 * inv_d + LN_EPS)
    logits = jnp.zeros((tm, LANES), F32) + br_ref[...]

    def h2_chunk(c):
        cs = slice(c * tn, (c + 1) * tn)
        return (x1_scr[c] - mu2) * rstd2 * (1.0 + scale_ref[pl.ds(row, 1), cs]) + shift_ref[pl.ds(row, 1), cs]

    for c in range(nj // 2):
        lo, hi = h2_chunk(c), h2_chunk(c + nj // 2)
        h2_ref[:, c * tn:(c + 1) * tn] = _pack_bf16_pair(lo, hi)
        logits = logits + jnp.dot(lo.astype(BF16), wr_ref[c * tn:(c + 1) * tn, :], preferred_element_type=F32)
        logits = logits + jnp.dot(hi.astype(BF16), wr_ref[HALF_D + c * tn:HALF_D + (c + 1) * tn, :],
                                  preferred_element_type=F32)
    lg_ref[...] = _route_lanes(logits)


def _route_lanes(logits):
    lane = lax.broadcasted_iota(jnp.int32, logits.shape, 1)
    neg = -jnp.inf
    big = LANES

    def first_max(vals):
        top = jnp.max(vals, -1, keepdims=True)
        return top, jnp.min(jnp.where(vals == top, lane, big), -1, keepdims=True)

    is_group = lane < N_GROUPS
    g_top, g_sel = first_max(jnp.where(is_group, logits, neg))
    p_sel = 1.0 / jnp.sum(jnp.where(is_group, jnp.exp(logits - g_top), 0.0), -1, keepdims=True)
    first = N_GROUPS + EXPERTS_PER_GROUP * g_sel
    cand = jnp.where(jnp.logical_and(lane >= first, lane < first + EXPERTS_PER_GROUP), logits, neg)
    v1, i1 = first_max(cand)
    v2, i2 = first_max(jnp.where(lane == i1, neg, cand))
    w1 = 1.0 / (1.0 + jnp.exp(v2 - v1))
    out = jnp.where(lane == 0, (i1 - N_GROUPS).astype(F32), 0.0)
    out = jnp.where(lane == 1, (i2 - N_GROUPS).astype(F32), out)
    out = jnp.where(lane == 2, p_sel * w1, out)
    return jnp.where(lane == 3, p_sel * (1.0 - w1), out)


def _norm_router(z, n_prompt, ln_g, ln_b, shift2, scale2, w_router, b_router, dec_seq):
    t = z.shape[0]
    tm, tn = TM_NORM, 512
    full = lambda i: (0, 0)
    kern = functools.partial(_norm_router_kernel, n_p_tiles=n_prompt // tm, tiles_per_seq=dec_seq // tm, tn=tn)
    return pl.pallas_call(
        kern,
        grid=(t // tm,),
        in_specs=[pl.BlockSpec((tm, D_MODEL), lambda i: (i, 0)),
                  pl.BlockSpec((1, D_MODEL), full), pl.BlockSpec((1, D_MODEL), full),
                  pl.BlockSpec((8, D_MODEL), full), pl.BlockSpec((8, D_MODEL), full),
                  pl.BlockSpec((D_MODEL, LANES), full), pl.BlockSpec((1, LANES), full)],
        out_specs=[pl.BlockSpec((tm, HALF_D), lambda i: (i, 0)),
                   pl.BlockSpec((tm, LANES), lambda i: (i, 0))],
        out_shape=[jax.ShapeDtypeStruct((t, HALF_D), jnp.uint32),
                   jax.ShapeDtypeStruct((t, LANES), F32)],
        scratch_shapes=[pltpu.VMEM((D_MODEL // tn, tm, tn), F32)],
        compiler_params=_params(("arbitrary",)),
        name="norm_router",
    )(z, ln_g, ln_b, shift2, scale2, w_router, b_router)


def _dispatch_plan(expert, n_sub, n_super):
    i32 = jnp.int32
    flat_e = expert.reshape(-1)
    n_assign = flat_e.shape[0]
    ids = jnp.arange(N_EXPERTS, dtype=i32)
    onehot = (flat_e[:, None] == ids[None, :]).astype(i32)
    csum = jnp.cumsum(onehot, 0)
    counts = csum[-1]
    rank = jnp.sum(onehot * (csum - 1), 1)
    nb = (counts + SUB_ROWS - 1) // SUB_ROWS
    sub_end = jnp.cumsum(nb)
    sub_start = sub_end - nb
    dest = (jnp.sum(onehot * (sub_start * SUB_ROWS)[None, :], 1) + rank).astype(i32)
    pad_start = (sub_start * SUB_ROWS + counts).astype(i32)
    pad_len = (nb * SUB_ROWS - counts).astype(i32)
    tail = jnp.stack([sub_end[-1], n_sub - sub_end[-1]]).astype(i32)

    nsup = (nb + SUPER - 1) // SUPER
    sup_end = jnp.cumsum(nsup)
    sup_start = sup_end - nsup
    n_used = sup_end[-1]
    s = jnp.arange(n_super, dtype=i32)
    used = s < n_used
    last_exp = jnp.max(jnp.where(counts > 0, ids, 0))
    e_s = jnp.where(used, jnp.minimum(jnp.searchsorted(sup_end, s, side="right").astype(i32), N_EXPERTS - 1), last_exp)
    local = s - sup_start[e_s]
    first_sub = sub_start[e_s] + SUPER * local
    n_comp = jnp.where(used, jnp.clip(nb[e_s] - SUPER * local, 0, SUPER), 0).astype(i32)
    zero_first = sub_end[-1] + SUPER * (s - n_used)
    n_zero = jnp.where(used, 0, jnp.clip(n_sub - zero_first, 0, SUPER)).astype(i32)
    out_sub = jnp.where(used, first_sub, jnp.minimum(zero_first, n_sub - 1)).astype(i32)
    k = jnp.arange(SUPER, dtype=i32)
    x_sub_used = first_sub[:, None] + jnp.minimum(k[None, :], jnp.maximum(n_comp - 1, 0)[:, None])
    x_sub_last = x_sub_used[jnp.maximum(n_used - 1, 0)]
    x_sub = jnp.where(used[:, None], x_sub_used, x_sub_last[None, :]).astype(i32).reshape(-1)
    return dest, (pad_start, pad_len, tail), (e_s.astype(i32), n_comp, n_zero, out_sub, x_sub)


_PAD_PIECES = (128, 64, 32, 16, 8)


def _dispatch_kernel(dest_ref, pad_start_ref, pad_len_ref, tail_ref, h_ref, xs_hbm, zeros, sem, zsem):
    i = pl.program_id(0)
    tt = h_ref.shape[0]
    base = i * tt * 2

    def row_copy(r, k):
        return pltpu.make_async_copy(h_ref.at[pl.ds(r, 1)], xs_hbm.at[pl.ds(dest_ref[base + 2 * r + k], 1)], sem)

    def issue(g, c):
        r0 = pl.multiple_of(g * ISSUE_GROUP, ISSUE_GROUP)
        for k in range(ISSUE_GROUP):
            row_copy(r0 + k, 0).start(priority=0)
            row_copy(r0 + k, 1).start(priority=1)
        return c

    lax.fori_loop(0, tt // ISSUE_GROUP, issue, 0)

    def zero_fill(act):
        def pad(e, c):
            start = pad_start_ref[e]
            length = pad_len_ref[e]
            head = length & 7

            def head_row(r, cc):
                act(pltpu.make_async_copy(zeros.at[pl.ds(0, 1)], xs_hbm.at[pl.ds(start + r, 1)], zsem))
                return cc

            lax.fori_loop(0, head, head_row, 0)
            pos = start + head
            for piece in _PAD_PIECES:
                on = (length & piece) != 0

                @pl.when(on)
                def _():
                    rows = pl.ds(pl.multiple_of(pos, 8), piece)
                    act(pltpu.make_async_copy(zeros.at[pl.ds(0, piece)], xs_hbm.at[rows], zsem))

                pos = pos + jnp.where(on, piece, 0)
            return c

        lax.fori_loop(0, N_EXPERTS, pad, 0)

        def tail(q, c):
            row0 = pl.multiple_of((tail_ref[0] + q) * SUB_ROWS, SUB_ROWS)
            act(pltpu.make_async_copy(zeros, xs_hbm.at[pl.ds(row0, SUB_ROWS)], zsem))
            return c

        lax.fori_loop(0, tail_ref[1], tail, 0)

    @pl.when(i == 0)
    def _():
        zeros[...] = jnp.zeros_like(zeros)
        zero_fill(lambda cp: cp.start())
        zero_fill(lambda cp: cp.wait())

    def drain(r, c):
        row_copy(r, 0).wait()
        row_copy(r, 1).wait()
        return c

    lax.fori_loop(0, tt, drain, 0, unroll=4)


def _dispatch(dest, pad_start, pad_len, tail, h2_packed, n_rows):
    t = h2_packed.shape[0]
    grid_spec = pltpu.PrefetchScalarGridSpec(
        num_scalar_prefetch=4,
        grid=(t // DISPATCH_TOKENS,),
        in_specs=[pl.BlockSpec((DISPATCH_TOKENS, HALF_D), lambda i, d, ps, plen, tl: (i, 0))],
        out_specs=pl.BlockSpec(memory_space=pl.ANY),
        scratch_shapes=[pltpu.VMEM((SUB_ROWS, HALF_D), jnp.uint32),
                        pltpu.SemaphoreType.DMA(()), pltpu.SemaphoreType.DMA(())],
    )
    return pl.pallas_call(
        _dispatch_kernel,
        grid_spec=grid_spec,
        out_shape=jax.ShapeDtypeStruct((n_rows, HALF_D), jnp.uint32),
        compiler_params=_params(("arbitrary",)),
        name="dispatch",
    )(dest, pad_start, pad_len, tail, h2_packed)


def _expert_kernel(exp_ref, nc_ref, nz_ref, osub_ref, xsub_ref, x0_ref, x1_ref, x2_ref, x3_ref,
                   wg_ref, wu_ref, wd_ref, y_hbm, a_scr, ytile, sem, *, nf, nd):
    s = pl.program_id(0)
    t = pl.program_id(1)
    n_comp = nc_ref[s]
    n_out = n_comp + nz_ref[s]
    x_refs = (x0_ref, x1_ref, x2_ref, x3_ref)

    @pl.when(jnp.logical_and(t < nf, n_comp > 0))
    def _():
        wg16 = wg_ref[...].astype(BF16)
        wu16 = wu_ref[...].astype(BF16)

        def up(k):
            x = _unpack_bf16_pair(x_refs[k][...])
            g = jnp.dot(x, wg16, preferred_element_type=F32)
            u = jnp.dot(x, wu16, preferred_element_type=F32)
            a_scr[t, k * SUB_ROWS:(k + 1) * SUB_ROWS, :] = (g * _sigmoid(g) * u).astype(BF16)

        up(0)
        for k in range(1, SUPER):
            pl.when(k < n_comp)(functools.partial(up, k))

    @pl.when(t >= nf)
    def _():
        j = t - nf
        slot = j % 2

        @pl.when(n_comp > 0)
        def _():
            wd16 = wd_ref[...].astype(BF16)

            def down(k):
                rows = slice(k * SUB_ROWS, (k + 1) * SUB_ROWS)
                a = jnp.concatenate([a_scr[f, rows, :] for f in range(nf)], axis=1)
                acc = jnp.dot(a, wd16, preferred_element_type=F32)
                ytile[slot, rows, :] = _pack_bf16_pair(acc[:, :TN_DOWN // 2], acc[:, TN_DOWN // 2:])

            down(0)
            for k in range(1, SUPER):
                pl.when(k < n_comp)(functools.partial(down, k))

        @pl.when(n_comp == 0)
        def _():
            ytile[slot] = jnp.zeros(ytile.shape[1:], jnp.uint32)

        def out_copy(sl, k, jj):
            dst_rows = pl.ds(pl.multiple_of((osub_ref[s] + k) * SUB_ROWS, SUB_ROWS), SUB_ROWS)
            dst_cols = pl.ds(pl.multiple_of(jj * (TN_DOWN // 2), TN_DOWN // 2), TN_DOWN // 2)
            return pltpu.make_async_copy(ytile.at[sl, pl.ds(k * SUB_ROWS, SUB_ROWS), :],
                                         y_hbm.at[dst_rows, dst_cols], sem)

        for k in range(SUPER):
            @pl.when(jnp.logical_and(j > 0, k < n_out))
            def _():
                out_copy(1 - slot, k, j - 1).wait()
        for k in range(SUPER):
            @pl.when(k < n_out)
            def _():
                out_copy(slot, k, j).start()
        for k in range(SUPER):
            @pl.when(jnp.logical_and(j == nd - 1, k < n_out))
            def _():
                out_copy(slot, k, j).wait()


def _experts(xs, sup_exp, n_comp, n_zero, out_sub, x_sub, w_gate, w_up, w_down):
    n_super = sup_exp.shape[0]
    n_rows = xs.shape[0]
    nf = D_EXPERT // F_CHUNK
    nd = D_MODEL // TN_DOWN

    def nxt(s, nc):
        s2 = jnp.minimum(s + 1, n_super - 1)
        return s2, jnp.logical_and(nc[s] > 0, nc[s2] > 0)

    def up_idx(s, t, e, nc, nz, osub, xsub):
        s2, has_next = nxt(s, nc)
        ahead = jnp.logical_and(t >= nf, has_next)
        chunk = jnp.where(nc[s] > 0, jnp.minimum(t, nf - 1), nf - 1)
        return (jnp.where(ahead, e[s2], e[s]), 0, jnp.where(ahead, 0, chunk))

    def down_idx(s, t, e, nc, nz, osub, xsub):
        sp = jnp.maximum(s - 1, 0)
        behind = jnp.logical_and(jnp.logical_and(t == 0, s > 0), nc[s] > 0)
        chunk = jnp.where(nc[s] > 0, jnp.maximum(t - nf, 0), nd - 1)
        return (jnp.where(behind, e[sp], e[s]), 0, jnp.where(behind, nd - 1, chunk))

    def x_spec(k):
        def idx(s, t, e, nc, nz, osub, xsub):
            s2, has_next = nxt(s, nc)
            return (xsub[jnp.where(jnp.logical_and(t >= nf, has_next), s2, s) * SUPER + k], 0)

        return pl.BlockSpec((SUB_ROWS, HALF_D), idx)

    grid_spec = pltpu.PrefetchScalarGridSpec(
        num_scalar_prefetch=5,
        grid=(n_super, nf + nd),
        in_specs=[x_spec(0), x_spec(1), x_spec(2), x_spec(3),
                  pl.BlockSpec((None, D_MODEL, F_CHUNK), up_idx),
                  pl.BlockSpec((None, D_MODEL, F_CHUNK), up_idx),
                  pl.BlockSpec((None, D_EXPERT, TN_DOWN), down_idx)],
        out_specs=pl.BlockSpec(memory_space=pl.ANY),
        scratch_shapes=[pltpu.VMEM((nf, SUPER * SUB_ROWS, F_CHUNK), BF16),
                        pltpu.VMEM((2, SUPER * SUB_ROWS, TN_DOWN // 2), jnp.uint32),
                        pltpu.SemaphoreType.DMA(())],
    )
    return pl.pallas_call(
        functools.partial(_expert_kernel, nf=nf, nd=nd),
        grid_spec=grid_spec,
        out_shape=jax.ShapeDtypeStruct((n_rows, HALF_D), jnp.uint32),
        compiler_params=_params(("arbitrary", "arbitrary"), 60 * 1024 * 1024),
        name="experts",
    )(sup_exp, n_comp, n_zero, out_sub, x_sub, xs, xs, xs, xs, w_gate, w_up, w_down)


def _final_kernel(dest_ref, y_hbm, z_ref, wt_ref, gate_ref, g1_ref, b1_ref, g2_ref, b2_ref,
                  op_ref, os_ref, ybuf, sem, *, n_p_tiles, tiles_per_seq):
    i = pl.program_id(0)
    n_tiles = pl.num_programs(0)
    tm = z_ref.shape[0]
    slot = i % 2

    def row_copy(src_row, sl, k, r):
        return pltpu.make_async_copy(y_hbm.at[pl.ds(src_row, 1)], ybuf.at[sl, k, pl.ds(r, 1)], sem.at[sl])

    def gather(tile, sl):
        base = tile * tm * 2

        def issue(g, c):
            r0 = pl.multiple_of(g * ISSUE_GROUP, ISSUE_GROUP)
            for k in range(ISSUE_GROUP):
                row_copy(dest_ref[base + 2 * (r0 + k)], sl, 0, r0 + k).start(priority=0)
                row_copy(dest_ref[base + 2 * (r0 + k) + 1], sl, 1, r0 + k).start(priority=1)
            return c

        lax.fori_loop(0, tm // ISSUE_GROUP, issue, 0)

    @pl.when(i == 0)
    def _():
        gather(0, 0)

    @pl.when(i + 1 < n_tiles)
    def _():
        gather(i + 1, 1 - slot)

    def drain(r, c):
        row_copy(0, slot, 0, r).wait()
        row_copy(0, slot, 1, r).wait()
        return c

    lax.fori_loop(0, tm, drain, 0, unroll=4)

    is_p = i < n_p_tiles
    row = jnp.where(is_p, 0, 1 + (i - n_p_tiles) // tiles_per_seq)
    wt = wt_ref[...]
    w0, w1 = wt[:, 0:1], wt[:, 1:2]
    half_tile = TN_DOWN // 2
    pieces = []
    for j in range(D_MODEL // TN_DOWN):
        y0 = ybuf[slot, 0, :, j * half_tile:(j + 1) * half_tile]
        y1 = ybuf[slot, 1, :, j * half_tile:(j + 1) * half_tile]
        pieces.append(w0 * pltpu.bitcast(y0 << 16, F32) + w1 * pltpu.bitcast(y1 << 16, F32))
        pieces.append(w0 * pltpu.bitcast(y0 & jnp.uint32(0xFFFF0000), F32)
                      + w1 * pltpu.bitcast(y1 & jnp.uint32(0xFFFF0000), F32))
    f = jnp.concatenate(pieces, axis=1)
    x1 = _ln(z_ref[...]) * g1_ref[...] + b1_ref[...]
    out = _ln(ALPHA * x1 + gate_ref[pl.ds(row, 1), :] * f) * g2_ref[...] + b2_ref[...]

    @pl.when(is_p)
    def _():
        op_ref[...] = out

    @pl.when(jnp.logical_not(is_p))
    def _():
        os_ref[...] = out


def _combine(dest, y_rows, z, weight, gate2, g1, b1, g2, b2, n_prompt, dec_seq):
    t = z.shape[0]
    tm = TM_FIN
    n_p = n_prompt // tm
    n_s = (t - n_prompt) // tm
    full = lambda i, d: (0, 0)
    grid_spec = pltpu.PrefetchScalarGridSpec(
        num_scalar_prefetch=1,
        grid=(n_p + n_s,),
        in_specs=[pl.BlockSpec(memory_space=pl.ANY),
                  pl.BlockSpec((tm, D_MODEL), lambda i, d: (i, 0)),
                  pl.BlockSpec((tm, 2), lambda i, d: (i, 0)),
                  pl.BlockSpec((8, D_MODEL), full),
                  pl.BlockSpec((1, D_MODEL), full), pl.BlockSpec((1, D_MODEL), full),
                  pl.BlockSpec((1, D_MODEL), full), pl.BlockSpec((1, D_MODEL), full)],
        out_specs=[pl.BlockSpec((tm, D_MODEL), lambda i, d: (jnp.minimum(i, n_p - 1), 0)),
                   pl.BlockSpec((tm, D_MODEL), lambda i, d: (jnp.maximum(i - n_p, 0), 0))],
        scratch_shapes=[pltpu.VMEM((2, 2, tm, HALF_D), jnp.uint32), pltpu.SemaphoreType.DMA((2,))],
    )
    return pl.pallas_call(
        functools.partial(_final_kernel, n_p_tiles=n_p, tiles_per_seq=dec_seq // tm),
        grid_spec=grid_spec,
        out_shape=[jax.ShapeDtypeStruct((n_prompt, D_MODEL), F32),
                   jax.ShapeDtypeStruct((t - n_prompt, D_MODEL), F32)],
        compiler_params=_params(("arbitrary",)),
        name="combine",
    )(dest, y_rows, z, weight, gate2, g1, b1, g2, b2)


def kernel(x_prompt, x_sample, state_lru, state_ret, c, c_ctx, w_mod, b_mod, w_in, conv_w, conv_b, lru_wa, lru_ba,
           lru_wx, lru_bx, lru_lam, ret_decay, w_out, ln1_g, ln1_b, router_g, router_g_b, router_e, router_e_b,
           w_gate, w_up, w_down, ln2_g, ln2_b):
    assert w_in.shape[0] == 1, "single trunk layer"
    nb, seq, d = x_prompt.shape
    nbs, dec_seq, _ = x_sample.shape
    tp, ts = nb * seq, nbs * dec_seq
    assert tp % dec_seq == 0 and d == D_MODEL

    cond = jnp.zeros((8, d), F32).at[0].set(c_ctx).at[1:1 + nbs].set(c)
    mod = _modulation(cond, w_mod[0], b_mod[0][None, :])
    shift1, scale1, gate1, shift2, scale2, gate2 = [mod[:, k * d:(k + 1) * d] for k in range(6)]

    xp = x_prompt.reshape(tp, d)
    xs = x_sample.reshape(ts, d)
    proj = _in_projection(_ln_modulate(xp, xs, shift1, scale1, dec_seq), w_in[0].astype(BF16))

    w_gates = (0.5 * jnp.concatenate([lru_wa[0, 0], lru_wx[0, 0], lru_wa[0, 1], lru_wx[0, 1]], -1)).astype(BF16)
    pb = jnp.stack([0.5 * lru_ba[0, 0], 0.5 * lru_bx[0, 0], lru_lam[0, 0],
                    0.5 * lru_ba[0, 1], 0.5 * lru_bx[0, 1], lru_lam[0, 1]], 0)
    proj_p = proj.reshape((tp + ts) // seq, seq, IN_COLS)
    proj_s = proj.reshape((tp + ts) // dec_seq, dec_seq, IN_COLS)
    ylp, st_lru = _lru_mixer(proj_p, 0, nb, 16, 1, conv_w[0], conv_b[0][None, :], w_gates, pb,
                             jnp.zeros((2, nb, LRU_WIDTH), F32))
    yls, _ = _lru_mixer(proj_s, tp // dec_seq, nbs, nbs, 4, conv_w[0], conv_b[0][None, :], w_gates, pb,
                        jnp.swapaxes(state_lru[:, 0], 0, 1))

    decay = jnp.broadcast_to(ret_decay[0].T[:, :, None], (RET_HEADS, 2, LANES))
    yrp, st_ret = _ret_mixer(proj_p, 0, nb, 16, decay, emit_state=True)
    (yrs,) = _ret_mixer(proj_s, tp // dec_seq, nbs, 1, decay, rope_tabs=_rope_tables(dec_seq), s0=state_ret)

    w_router = jnp.concatenate(
        [router_g[0], jnp.transpose(router_e[0], (1, 0, 2)).reshape(d, N_EXPERTS),
         jnp.zeros((d, LANES - N_GROUPS - N_EXPERTS), F32)], -1).astype(BF16)
    b_router = jnp.concatenate([router_g_b[0], router_e_b[0].reshape(-1),
                                jnp.zeros((LANES - N_GROUPS - N_EXPERTS,), F32)])[None, :]
    z = _out_projection(xp, xs, ylp.reshape(tp, -1), yrp.reshape(tp, -1), yls.reshape(ts, -1), yrs.reshape(ts, -1),
                        w_out[0].astype(BF16), gate1, dec_seq)
    h2, route = _norm_router(z, tp, ln1_g, ln1_b, shift2, scale2, w_router, b_router, dec_seq)
    expert = route[:, 0:2].astype(jnp.int32)
    weight = route[:, 2:4]

    n_assign = 2 * (tp + ts)
    n_sub = -(-(n_assign + N_EXPERTS * (SUB_ROWS - 1)) // SUB_ROWS)
    n_super = N_EXPERTS + -(-n_assign // (SUB_ROWS * SUPER))
    assert SUPER * n_super >= n_sub + (SUPER - 1) * N_EXPERTS
    dest, fill_plan, super_plan = _dispatch_plan(expert, n_sub, n_super)
    xs_rows = _dispatch(dest, *fill_plan, h2, n_sub * SUB_ROWS)
    y_rows = _experts(xs_rows, *super_plan, w_gate[0], w_up[0], w_down[0])
    y_p, y_s = _combine(dest, y_rows, z, weight, gate2, ln1_g, ln1_b, ln2_g, ln2_b, tp, dec_seq)

    new_state_lru = jnp.swapaxes(st_lru, 0, 1)[:, None]
    return (y_p.reshape(nb, seq, d), y_s.reshape(nbs, dec_seq, d), new_state_lru, st_ret)
```

```python
import functools

import jax
import jax.numpy as jnp
import numpy as np
from jax import lax
from jax.experimental import pallas as pl
from jax.experimental.pallas import tpu as pltpu

F32 = jnp.float32
BF16 = jnp.bfloat16

D_MODEL = 4096
LRU_WIDTH = 2048
LRU_BLOCKS = 16
LANES = 128
RET_HEADS = 16
RET_DH = 128
IN_COLS = 12288
GRID_W = 64
ROPE_BASE = 10000.0
LRU_C = 8.0
N_GROUPS = 4
EXPERTS_PER_GROUP = 8
N_EXPERTS = 32
D_EXPERT = 1024
LN_EPS = 1e-6
GN_EPS = 1e-5
ALPHA = 2.0 ** 0.25

VMEM_LIMIT = 56 * 1024 * 1024
VMEM_LIMIT_EXPERTS = 60 * 1024 * 1024
SUBLANES = 8
SCAN_PAD = 8
ROT = RET_DH // 4
RET_QBLOCK = 256
TN_NORM = 512

TN_MOD = 1024
TM_LN = 512
TM_PROJ = 1024
TN_IN = 1536
TN_OUT = 512
TM_OUT = 1024
TM_NORM = 512
SUB_ROWS = 256
SUPER = 4
F_CHUNK = 256
TN_DOWN = 1024
DISPATCH_TOKENS = 512
ISSUE_GROUP = 16
TM_FIN = 256
HALF_D = D_MODEL // 2


def _params(sem, vmem_limit=VMEM_LIMIT):
    return pltpu.CompilerParams(dimension_semantics=sem, vmem_limit_bytes=vmem_limit)


def _pack_bf16_pair(lo, hi):
    lo_bits = pltpu.bitcast(lo.astype(BF16).astype(F32), jnp.uint32) >> 16
    hi_bits = pltpu.bitcast(hi.astype(BF16).astype(F32), jnp.uint32) & jnp.uint32(0xFFFF0000)
    return lo_bits | hi_bits


def _unpack_bf16_pair(words):
    lo = pltpu.bitcast(words << 16, F32).astype(BF16)
    hi = pltpu.bitcast(words & jnp.uint32(0xFFFF0000), F32).astype(BF16)
    return jnp.concatenate([lo, hi], axis=1)


def _sigmoid(x):
    return 0.5 * jnp.tanh(0.5 * x) + 0.5


def _softplus(x):
    return jnp.maximum(x, 0.0) + jnp.log1p(jnp.exp(-jnp.abs(x)))


def _ln(x):
    mu = jnp.mean(x, -1, keepdims=True)
    xc = x - mu
    var = jnp.mean(xc * xc, -1, keepdims=True)
    return xc * lax.rsqrt(var + LN_EPS)


def _mod_kernel(cond_ref, w_ref, b_ref, o_ref):
    c = cond_ref[...]
    s = (c * _sigmoid(c)).astype(BF16)
    o_ref[...] = jnp.dot(s, w_ref[...].astype(BF16), preferred_element_type=F32) + b_ref[...]


def _modulation(cond, w_mod, b_mod):
    tn = TN_MOD
    n = w_mod.shape[1]
    return pl.pallas_call(
        _mod_kernel,
        grid=(n // tn,),
        in_specs=[pl.BlockSpec((8, D_MODEL), lambda j: (0, 0)),
                  pl.BlockSpec((D_MODEL, tn), lambda j: (0, j)),
                  pl.BlockSpec((1, tn), lambda j: (0, j))],
        out_specs=pl.BlockSpec((8, tn), lambda j: (0, j)),
        out_shape=jax.ShapeDtypeStruct((8, n), F32),
        compiler_params=_params(("arbitrary",)),
        name="modulation",
    )(cond, w_mod, b_mod)


def _ln_mod_kernel(xp_ref, xs_ref, shift_ref, scale_ref, h_ref, *, n_p_tiles, tiles_per_seq):
    i = pl.program_id(0)

    def fill(x_ref, row):
        h = _ln(x_ref[...]) * (1.0 + scale_ref[pl.ds(row, 1), :]) + shift_ref[pl.ds(row, 1), :]
        h_ref[...] = h.astype(BF16)

    @pl.when(i < n_p_tiles)
    def _():
        fill(xp_ref, 0)

    @pl.when(i >= n_p_tiles)
    def _():
        fill(xs_ref, 1 + (i - n_p_tiles) // tiles_per_seq)


def _ln_modulate(xp, xs, shift, scale, dec_seq):
    tp, ts = xp.shape[0], xs.shape[0]
    tm = TM_LN
    n_p, n_s = tp // tm, ts // tm
    kern = functools.partial(_ln_mod_kernel, n_p_tiles=n_p, tiles_per_seq=dec_seq // tm)
    return pl.pallas_call(
        kern,
        grid=(n_p + n_s,),
        in_specs=[pl.BlockSpec((tm, D_MODEL), lambda i: (jnp.minimum(i, n_p - 1), 0)),
                  pl.BlockSpec((tm, D_MODEL), lambda i: (jnp.maximum(i - n_p, 0), 0)),
                  pl.BlockSpec((8, D_MODEL), lambda i: (0, 0)),
                  pl.BlockSpec((8, D_MODEL), lambda i: (0, 0))],
        out_specs=pl.BlockSpec((tm, D_MODEL), lambda i: (i, 0)),
        out_shape=jax.ShapeDtypeStruct((tp + ts, D_MODEL), BF16),
        compiler_params=_params(("arbitrary",)),
        name="ln_modulate",
    )(xp, xs, shift, scale)


def _matmul_kernel(h_ref, w_ref, o_ref):
    o_ref[...] = jnp.dot(h_ref[...], w_ref[...], preferred_element_type=F32).astype(o_ref.dtype)


def _in_projection(h, w_in_bf16):
    t = h.shape[0]
    tm, tn = TM_PROJ, TN_IN
    return pl.pallas_call(
        _matmul_kernel,
        grid=(t // tm, IN_COLS // tn),
        in_specs=[pl.BlockSpec((tm, D_MODEL), lambda i, j: (i, 0)),
                  pl.BlockSpec((D_MODEL, tn), lambda i, j: (0, j))],
        out_specs=pl.BlockSpec((tm, tn), lambda i, j: (i, j)),
        out_shape=jax.ShapeDtypeStruct((t, IN_COLS), BF16),
        compiler_params=_params(("arbitrary", "arbitrary")),
        name="in_projection",
    )(h, w_in_bf16)


def _lru_kernel(x_ref, g_ref, cw_ref, cb_ref, wg_ref, pb_ref, h0_ref, y_ref, st_ref,
                xc_scr, af_scr, uf_scr, ab_scr, ub_scr, *, bg, seq, pitch, ncb):
    rows = lax.broadcasted_iota(jnp.int32, (seq, LANES), 0)
    for c in range(ncb):
        lanes = slice(c * LANES, (c + 1) * LANES)
        w = cw_ref[:, lanes]
        bias = cb_ref[:, lanes]
        for b in range(bg):
            x = x_ref[b, :, lanes].astype(F32)
            xm2 = jnp.where(rows >= 2, pltpu.roll(x, 2, 0), 0.0)
            xm1 = jnp.where(rows >= 1, pltpu.roll(x, 1, 0), 0.0)
            xp1 = jnp.where(rows < seq - 1, pltpu.roll(x, seq - 1, 0), 0.0)
            xc_scr[pl.ds(b * seq, seq), :] = bias + xm2 * w[0:1] + xm1 * w[1:2] + x * w[2:3] + xp1 * w[3:4]

        xc = xc_scr[...]
        gates = jnp.dot(xc.astype(BF16), wg_ref[c], preferred_element_type=F32)
        pb = pb_ref[:, lanes]
        half_xc = 0.5 * xc
        for d, (a_scr, u_scr) in enumerate(((af_scr, uf_scr), (ab_scr, ub_scr))):
            t_r = jnp.tanh(gates[:, (2 * d) * LANES:(2 * d + 1) * LANES] + pb[3 * d:3 * d + 1])
            t_i = jnp.tanh(gates[:, (2 * d + 1) * LANES:(2 * d + 2) * LANES] + pb[3 * d + 1:3 * d + 2])
            c4 = (0.5 * LRU_C) * _softplus(-pb[3 * d + 2:3 * d + 3])
            neg_log_a = c4 * t_r + c4
            a = jnp.exp(-neg_log_a)
            y = jnp.tanh(neg_log_a) * (1.0 + a * a)
            root = jnp.where(y > 0.0, y * lax.rsqrt(y), 0.0)
            u = root * (half_xc * (t_i + 1.0))
            for b in range(bg):
                a_scr[c, pl.ds(b * pitch, seq), :] = a[b * seq:(b + 1) * seq]
                u_scr[c, pl.ds(b * pitch, seq), :] = u[b * seq:(b + 1) * seq]

    def step(t, carry):
        tb = seq - 1 - t
        out = []
        for c in range(ncb):
            hf, hb = carry[2 * c], carry[2 * c + 1]
            hf = af_scr[c, pl.ds(t, bg, stride=pitch), :] * hf + uf_scr[c, pl.ds(t, bg, stride=pitch), :]
            uf_scr[c, pl.ds(t, bg, stride=pitch), :] = hf
            hb = ab_scr[c, pl.ds(tb, bg, stride=pitch), :] * hb + ub_scr[c, pl.ds(tb, bg, stride=pitch), :]
            ub_scr[c, pl.ds(tb, bg, stride=pitch), :] = hb
            out += [hf, hb]
        return tuple(out)

    init = []
    for c in range(ncb):
        init += [h0_ref[0, :, c * LANES:(c + 1) * LANES], h0_ref[1, :, c * LANES:(c + 1) * LANES]]
    last = lax.fori_loop(0, seq, step, tuple(init), unroll=8)
    for c in range(ncb):
        lanes = slice(c * LANES, (c + 1) * LANES)
        st_ref[0, :, lanes] = last[2 * c]
        st_ref[1, :, lanes] = last[2 * c + 1]
        for b in range(bg):
            hs = uf_scr[c, pl.ds(b * pitch, seq), :] + ub_scr[c, pl.ds(b * pitch, seq), :]
            y_ref[b, :, lanes] = (jax.nn.gelu(g_ref[b, :, lanes].astype(F32)) * hs).astype(BF16)


def _lru_mixer(proj3, seq0, nseq, bg, ncb, conv_w, conv_b, w_gates, pb, h0):
    seq = proj3.shape[1]
    pitch = seq + SCAN_PAD
    off = seq0 // bg
    cw = ncb * LANES
    ncol = LRU_WIDTH // cw
    kern = functools.partial(_lru_kernel, bg=bg, seq=seq, pitch=pitch, ncb=ncb)
    return pl.pallas_call(
        kern,
        grid=(nseq // bg, ncol),
        in_specs=[pl.BlockSpec((bg, seq, cw), lambda b, c: (b + off, 0, c)),
                  pl.BlockSpec((bg, seq, cw), lambda b, c: (b + off, 0, ncol + c)),
                  pl.BlockSpec((4, cw), lambda b, c: (0, c)),
                  pl.BlockSpec((1, cw), lambda b, c: (0, c)),
                  pl.BlockSpec((ncb, LANES, 4 * LANES), lambda b, c: (c, 0, 0)),
                  pl.BlockSpec((6, cw), lambda b, c: (0, c)),
                  pl.BlockSpec((2, bg, cw), lambda b, c: (0, b, c))],
        out_specs=[pl.BlockSpec((bg, seq, cw), lambda b, c: (b, 0, c)),
                   pl.BlockSpec((2, bg, cw), lambda b, c: (0, b, c))],
        out_shape=[jax.ShapeDtypeStruct((nseq, seq, LRU_WIDTH), BF16),
                   jax.ShapeDtypeStruct((2, nseq, LRU_WIDTH), F32)],
        scratch_shapes=[pltpu.VMEM((bg * seq, LANES), F32)] + [pltpu.VMEM((ncb, bg * pitch, LANES), F32)] * 4,
        compiler_params=_params(("arbitrary", "arbitrary")),
        name="lru_mixer",
    )(proj3, proj3, conv_w, conv_b, w_gates, pb, h0)


def _rope(x, cos, sin_signed, first_half):
    partner = jnp.where(first_half, pltpu.roll(x, LANES - ROT, 1), pltpu.roll(x, ROT, 1))
    return x * cos + partner * sin_signed


def _ret_kernel(*refs, bg, seq, rope, has_state, emit_state, qb):
    refs = list(refs)
    q_ref, k_ref, v_ref, g_ref, dec_ref = refs[:5]
    pos = 5
    if rope:
        cos_ref, sin_ref = refs[pos:pos + 2]
        pos += 2
    if has_state:
        s0_ref = refs[pos]
        pos += 1
    y_ref = refs[pos]
    pos += 1
    if emit_state:
        st_ref = refs[pos]
        pos += 1
    mask_scr = refs[pos]

    log_g = -_softplus(-dec_ref[0])
    lgf, lgb = log_g[0:1], log_g[1:2]
    reps = seq // LANES
    lgf_row = jnp.concatenate([lgf] * reps, axis=1)
    lgb_row = jnp.concatenate([lgb] * reps, axis=1)

    @pl.when(pl.program_id(1) == 0)
    def _():
        for blk in range(seq // qb):
            ti = lax.broadcasted_iota(jnp.int32, (qb, seq), 0) + blk * qb
            si = lax.broadcasted_iota(jnp.int32, (qb, seq), 1)
            dist = (ti - si).astype(F32)
            e = jnp.where(dist >= 0, dist * lgf_row, (-dist) * lgb_row)
            mask_scr[pl.ds(blk * qb, qb), :] = jnp.where(dist == 0, 2.0, jnp.exp(e))

    lane = lax.broadcasted_iota(jnp.int32, (seq, LANES), 1)
    first_half = (lane % (2 * ROT)) < ROT
    trow = lax.broadcasted_iota(jnp.int32, (seq, LANES), 0).astype(F32)
    for b in range(bg):
        q = q_ref[b].astype(F32)
        k = k_ref[b].astype(F32) * (RET_DH ** -0.5)
        v16 = v_ref[b]
        if rope:
            q = _rope(q, cos_ref[...], sin_ref[...], first_half)
            k = _rope(k, cos_ref[...], sin_ref[...], first_half)
        q16 = q.astype(BF16)
        k16 = k.astype(BF16)
        if has_state:
            qf16 = (q * jnp.exp((trow + 1.0) * lgf)).astype(BF16)
            qb16 = (q * jnp.exp((float(seq) - trow) * lgb)).astype(BF16)
            s0f = s0_ref[b, 0].astype(BF16)
            s0b = s0_ref[b, 1].astype(BF16)
        for blk in range(seq // qb):
            sl = slice(blk * qb, (blk + 1) * qb)
            s = lax.dot_general(q16[sl], k16, (((1,), (1,)), ((), ())), preferred_element_type=F32)
            p = (s * mask_scr[pl.ds(blk * qb, qb), :]).astype(BF16)
            o = jnp.dot(p, v16, preferred_element_type=F32)
            if has_state:
                o = o + jnp.dot(qf16[sl], s0f, preferred_element_type=F32)
                o = o + jnp.dot(qb16[sl], s0b, preferred_element_type=F32)
            mu = jnp.mean(o, -1, keepdims=True)
            oc = o - mu
            var = jnp.mean(oc * oc, -1, keepdims=True)
            on = oc * lax.rsqrt(var + GN_EPS)
            gt = g_ref[b, pl.ds(blk * qb, qb), :].astype(F32)
            y_ref[b, pl.ds(blk * qb, qb), :] = (gt * _sigmoid(gt) * on).astype(BF16)
        if emit_state:
            kf16 = (k * jnp.exp((float(seq - 1) - trow) * lgf)).astype(BF16)
            kb16 = (k * jnp.exp(trow * lgb)).astype(BF16)
            sf = lax.dot_general(kf16, v16, (((0,), (0,)), ((), ())), preferred_element_type=F32)
            sb = lax.dot_general(kb16, v16, (((0,), (0,)), ((), ())), preferred_element_type=F32)
            if has_state:
                sf = sf + jnp.exp(float(seq) * lgf) * s0_ref[b, 0]
                sb = sb + jnp.exp(float(seq) * lgb) * s0_ref[b, 1]
            st_ref[b, 0] = sf
            st_ref[b, 1] = sb


def _ret_mixer(proj3, seq0, nseq, bg, decay, rope_tabs=None, s0=None, emit_state=False):
    seq = proj3.shape[1]
    off = seq0 // bg
    qb = min(seq, RET_QBLOCK)
    rope = rope_tabs is not None
    has_state = s0 is not None
    kern = functools.partial(_ret_kernel, bg=bg, seq=seq, rope=rope, has_state=has_state,
                             emit_state=emit_state, qb=qb)

    def col(base):
        return pl.BlockSpec((bg, seq, LANES), lambda h, b: (b + off, 0, base + h))

    st_spec = pl.BlockSpec((bg, None, 2, None, RET_DH, RET_DH), lambda h, b: (b, 0, 0, h, 0, 0))
    in_specs = [col(2 * LRU_BLOCKS), col(2 * LRU_BLOCKS + RET_HEADS), col(2 * LRU_BLOCKS + 2 * RET_HEADS),
                col(2 * LRU_BLOCKS + 3 * RET_HEADS), pl.BlockSpec((1, 2, LANES), lambda h, b: (h, 0, 0))]
    args = [proj3, proj3, proj3, proj3, decay]
    if rope:
        in_specs += [pl.BlockSpec((seq, LANES), lambda h, b: (0, 0))] * 2
        args += list(rope_tabs)
    if has_state:
        in_specs.append(st_spec)
        args.append(s0)
    out_specs = [pl.BlockSpec((bg, seq, LANES), lambda h, b: (b, 0, h))]
    out_shape = [jax.ShapeDtypeStruct((nseq, seq, RET_HEADS * RET_DH), BF16)]
    if emit_state:
        out_specs.append(st_spec)
        out_shape.append(jax.ShapeDtypeStruct((nseq, 1, 2, RET_HEADS, RET_DH, RET_DH), F32))
    return pl.pallas_call(
        kern,
        grid=(RET_HEADS, nseq // bg),
        in_specs=in_specs,
        out_specs=out_specs,
        out_shape=out_shape,
        scratch_shapes=[pltpu.VMEM((seq, seq), F32)],
        compiler_params=_params(("arbitrary", "arbitrary")),
        name="ret_mixer",
    )(*args)


def _rope_tables(seq):
    nf = RET_DH // 4
    freqs = (np.float32(ROPE_BASE) ** (-np.arange(nf, dtype=np.float32) / np.float32(nf))).astype(np.float32)
    t = np.arange(seq)
    row = (t // GRID_W).astype(np.float32)[:, None] * freqs[None, :]
    colp = (t % GRID_W).astype(np.float32)[:, None] * freqs[None, :]
    cos = np.concatenate([np.cos(row), np.cos(row), np.cos(colp), np.cos(colp)], -1)
    sin = np.concatenate([-np.sin(row), np.sin(row), -np.sin(colp), np.sin(colp)], -1)
    return jnp.asarray(cos, F32), jnp.asarray(sin, F32)


def _outproj_kernel(xp_ref, xs_ref, ylp_ref, yrp_ref, yls_ref, yrs_ref, wa_ref, wb_ref, gate_ref, z_ref,
                    *, n_p_tiles, tiles_per_seq):
    i = pl.program_id(0)
    is_p = i < n_p_tiles
    row = jnp.where(is_p, 0, 1 + (i - n_p_tiles) // tiles_per_seq)

    def mix(x_ref, yl_ref, yr_ref):
        m = jnp.dot(yl_ref[...], wa_ref[...], preferred_element_type=F32)
        m = m + jnp.dot(yr_ref[...], wb_ref[...], preferred_element_type=F32)
        z_ref[...] = ALPHA * x_ref[...] + gate_ref[pl.ds(row, 1), :] * m

    @pl.when(is_p)
    def _():
        mix(xp_ref, ylp_ref, yrp_ref)

    @pl.when(jnp.logical_not(is_p))
    def _():
        mix(xs_ref, yls_ref, yrs_ref)


def _out_projection(xp, xs, ylp, yrp, yls, yrs, w_out_bf16, gate1, dec_seq):
    tp, ts = xp.shape[0], xs.shape[0]
    tm, tn = TM_OUT, TN_OUT
    n_p, n_s = tp // tm, ts // tm
    half = LRU_WIDTH
    kern = functools.partial(_outproj_kernel, n_p_tiles=n_p, tiles_per_seq=dec_seq // tm)
    p_idx = lambda i, j: (jnp.minimum(i, n_p - 1), 0)
    s_idx = lambda i, j: (jnp.maximum(i - n_p, 0), 0)
    once = pl.Buffered(1)
    return pl.pallas_call(
        kern,
        grid=(n_p + n_s, D_MODEL // tn),
        in_specs=[pl.BlockSpec((tm, tn), lambda i, j: (jnp.minimum(i, n_p - 1), j)),
                  pl.BlockSpec((tm, tn), lambda i, j: (jnp.maximum(i - n_p, 0), j)),
                  pl.BlockSpec((tm, half), p_idx), pl.BlockSpec((tm, half), p_idx),
                  pl.BlockSpec((tm, half), s_idx, pipeline_mode=once),
                  pl.BlockSpec((tm, half), s_idx, pipeline_mode=once),
                  pl.BlockSpec((half, tn), lambda i, j: (0, j)),
                  pl.BlockSpec((half, tn), lambda i, j: (1, j)),
                  pl.BlockSpec((8, tn), lambda i, j: (0, j))],
        out_specs=pl.BlockSpec((tm, tn), lambda i, j: (i, j)),
        out_shape=jax.ShapeDtypeStruct((tp + ts, D_MODEL), F32),
        compiler_params=_params(("arbitrary", "arbitrary")),
        name="out_projection",
    )(xp, xs, ylp, yrp, yls, yrs, w_out_bf16, w_out_bf16, gate1)


def _norm_router_kernel(z_ref, lng_ref, lnb_ref, shift_ref, scale_ref, wr_ref, br_ref, h2_ref, lg_ref, x1_scr,
                        *, n_p_tiles, tiles_per_seq, tn):
    i = pl.program_id(0)
    row = jnp.where(i < n_p_tiles, 0, 1 + (i - n_p_tiles) // tiles_per_seq)
    tm = z_ref.shape[0]
    nj = D_MODEL // tn
    inv_d = 1.0 / D_MODEL
    s1 = jnp.zeros((tm, 1), F32)
    for c in range(nj):
        s1 = s1 + jnp.sum(z_ref[:, c * tn:(c + 1) * tn], -1, keepdims=True)
    mu = s1 * inv_d
    s2 = jnp.zeros((tm, 1), F32)
    for c in range(nj):
        zc = z_ref[:, c * tn:(c + 1) * tn] - mu
        s2 = s2 + jnp.sum(zc * zc, -1, keepdims=True)
    rstd = lax.rsqrt(s2 * inv_d + LN_EPS)
    t1 = jnp.zeros((tm, 1), F32)
    for c in range(nj):
        cs = slice(c * tn, (c + 1) * tn)
        x1 = (z_ref[:, cs] - mu) * rstd * lng_ref[:, cs] + lnb_ref[:, cs]
        x1_scr[c] = x1
        t1 = t1 + jnp.sum(x1, -1, keepdims=True)
    mu2 = t1 * inv_d
    t2 = jnp.zeros((tm, 1), F32)
    for c in range(nj):
        xc = x1_scr[c] - mu2
        t2 = t2 + jnp.sum(xc * xc, -1, keepdims=True)
    rstd2 = lax.rsqrt(t2 * inv_d + LN_EPS)
    logits = jnp.zeros((tm, LANES), F32) + br_ref[...]

    def h2_chunk(c):
        cs = slice(c * tn, (c + 1) * tn)
        return (x1_scr[c] - mu2) * rstd2 * (1.0 + scale_ref[pl.ds(row, 1), cs]) + shift_ref[pl.ds(row, 1), cs]

    for c in range(nj // 2):
        lo, hi = h2_chunk(c), h2_chunk(c + nj // 2)
        h2_ref[:, c * tn:(c + 1) * tn] = _pack_bf16_pair(lo, hi)
        logits = logits + jnp.dot(lo.astype(BF16), wr_ref[c * tn:(c + 1) * tn, :], preferred_element_type=F32)
        logits = logits + jnp.dot(hi.astype(BF16), wr_ref[HALF_D + c * tn:HALF_D + (c + 1) * tn, :],
                                  preferred_element_type=F32)
    lg_ref[...] = _route_lanes(logits)


def _route_lanes(logits):
    lane = lax.broadcasted_iota(jnp.int32, logits.shape, 1)
    neg = -jnp.inf
    big = LANES

    def first_max(vals):
        top = jnp.max(vals, -1, keepdims=True)
        return top, jnp.min(jnp.where(vals == top, lane, big), -1, keepdims=True)

    is_group = lane < N_GROUPS
    g_top, g_sel = first_max(jnp.where(is_group, logits, neg))
    p_sel = 1.0 / jnp.sum(jnp.where(is_group, jnp.exp(logits - g_top), 0.0), -1, keepdims=True)
    first = N_GROUPS + EXPERTS_PER_GROUP * g_sel
    cand = jnp.where(jnp.logical_and(lane >= first, lane < first + EXPERTS_PER_GROUP), logits, neg)
    v1, i1 = first_max(cand)
    v2, i2 = first_max(jnp.where(lane == i1, neg, cand))
    w1 = 1.0 / (1.0 + jnp.exp(v2 - v1))
    out = jnp.where(lane == 0, (i1 - N_GROUPS).astype(F32), 0.0)
    out = jnp.where(lane == 1, (i2 - N_GROUPS).astype(F32), out)
    out = jnp.where(lane == 2, p_sel * w1, out)
    return jnp.where(lane == 3, p_sel * (1.0 - w1), out)


def _norm_router(z, n_prompt, ln_g, ln_b, shift2, scale2, w_router, b_router, dec_seq):
    t = z.shape[0]
    tm, tn = TM_NORM, TN_NORM
    full = lambda i: (0, 0)
    kern = functools.partial(_norm_router_kernel, n_p_tiles=n_prompt // tm, tiles_per_seq=dec_seq // tm, tn=tn)
    return pl.pallas_call(
        kern,
        grid=(t // tm,),
        in_specs=[pl.BlockSpec((tm, D_MODEL), lambda i: (i, 0)),
                  pl.BlockSpec((1, D_MODEL), full), pl.BlockSpec((1, D_MODEL), full),
                  pl.BlockSpec((8, D_MODEL), full), pl.BlockSpec((8, D_MODEL), full),
                  pl.BlockSpec((D_MODEL, LANES), full), pl.BlockSpec((1, LANES), full)],
        out_specs=[pl.BlockSpec((tm, HALF_D), lambda i: (i, 0)),
                   pl.BlockSpec((tm, LANES), lambda i: (i, 0))],
        out_shape=[jax.ShapeDtypeStruct((t, HALF_D), jnp.uint32),
                   jax.ShapeDtypeStruct((t, LANES), F32)],
        scratch_shapes=[pltpu.VMEM((D_MODEL // tn, tm, tn), F32)],
        compiler_params=_params(("arbitrary",)),
        name="norm_router",
    )(z, ln_g, ln_b, shift2, scale2, w_router, b_router)


def _dispatch_plan(expert, n_sub, n_super):
    i32 = jnp.int32
    flat_e = expert.reshape(-1)
    n_assign = flat_e.shape[0]
    ids = jnp.arange(N_EXPERTS, dtype=i32)
    onehot = (flat_e[:, None] == ids[None, :]).astype(i32)
    csum = jnp.cumsum(onehot, 0)
    counts = csum[-1]
    rank = jnp.sum(onehot * (csum - 1), 1)
    nb = (counts + SUB_ROWS - 1) // SUB_ROWS
    sub_end = jnp.cumsum(nb)
    sub_start = sub_end - nb
    dest = (jnp.sum(onehot * (sub_start * SUB_ROWS)[None, :], 1) + rank).astype(i32)
    pad_start = (sub_start * SUB_ROWS + counts).astype(i32)
    pad_len = (nb * SUB_ROWS - counts).astype(i32)
    tail = jnp.stack([sub_end[-1], n_sub - sub_end[-1]]).astype(i32)

    nsup = (nb + SUPER - 1) // SUPER
    sup_end = jnp.cumsum(nsup)
    sup_start = sup_end - nsup
    n_used = sup_end[-1]
    s = jnp.arange(n_super, dtype=i32)
    used = s < n_used
    last_exp = jnp.max(jnp.where(counts > 0, ids, 0))
    e_s = jnp.where(used, jnp.minimum(jnp.searchsorted(sup_end, s, side="right").astype(i32), N_EXPERTS - 1), last_exp)
    local = s - sup_start[e_s]
    first_sub = sub_start[e_s] + SUPER * local
    n_comp = jnp.where(used, jnp.clip(nb[e_s] - SUPER * local, 0, SUPER), 0).astype(i32)
    zero_first = sub_end[-1] + SUPER * (s - n_used)
    n_zero = jnp.where(used, 0, jnp.clip(n_sub - zero_first, 0, SUPER)).astype(i32)
    out_sub = jnp.where(used, first_sub, jnp.minimum(zero_first, n_sub - 1)).astype(i32)
    k = jnp.arange(SUPER, dtype=i32)
    x_sub_used = first_sub[:, None] + jnp.minimum(k[None, :], jnp.maximum(n_comp - 1, 0)[:, None])
    x_sub_last = x_sub_used[jnp.maximum(n_used - 1, 0)]
    x_sub = jnp.where(used[:, None], x_sub_used, x_sub_last[None, :]).astype(i32).reshape(-1)
    return dest, (pad_start, pad_len, tail), (e_s.astype(i32), n_comp, n_zero, out_sub, x_sub)


_PAD_PIECES = (128, 64, 32, 16, 8)


def _dispatch_kernel(dest_ref, pad_start_ref, pad_len_ref, tail_ref, h_ref, xs_hbm, zeros, sem, zsem):
    i = pl.program_id(0)
    tt = h_ref.shape[0]
    base = i * tt * 2

    def row_copy(r, k):
        return pltpu.make_async_copy(h_ref.at[pl.ds(r, 1)], xs_hbm.at[pl.ds(dest_ref[base + 2 * r + k], 1)], sem)

    def issue(g, c):
        r0 = pl.multiple_of(g * ISSUE_GROUP, ISSUE_GROUP)
        for k in range(ISSUE_GROUP):
            row_copy(r0 + k, 0).start(priority=0)
            row_copy(r0 + k, 1).start(priority=1)
        return c

    lax.fori_loop(0, tt // ISSUE_GROUP, issue, 0)

    def zero_fill(act):
        def pad(e, c):
            start = pad_start_ref[e]
            length = pad_len_ref[e]
            head = length & (SUBLANES - 1)

            def head_row(r, cc):
                act(pltpu.make_async_copy(zeros.at[pl.ds(0, 1)], xs_hbm.at[pl.ds(start + r, 1)], zsem))
                return cc

            lax.fori_loop(0, head, head_row, 0)
            pos = start + head
            for piece in _PAD_PIECES:
                on = (length & piece) != 0

                @pl.when(on)
                def _():
                    rows = pl.ds(pl.multiple_of(pos, SUBLANES), piece)
                    act(pltpu.make_async_copy(zeros.at[pl.ds(0, piece)], xs_hbm.at[rows], zsem))

                pos = pos + jnp.where(on, piece, 0)
            return c

        lax.fori_loop(0, N_EXPERTS, pad, 0)

        def tail(q, c):
            row0 = pl.multiple_of((tail_ref[0] + q) * SUB_ROWS, SUB_ROWS)
            act(pltpu.make_async_copy(zeros, xs_hbm.at[pl.ds(row0, SUB_ROWS)], zsem))
            return c

        lax.fori_loop(0, tail_ref[1], tail, 0)

    @pl.when(i == 0)
    def _():
        zeros[...] = jnp.zeros_like(zeros)
        zero_fill(lambda cp: cp.start())
        zero_fill(lambda cp: cp.wait())

    def drain(r, c):
        row_copy(r, 0).wait()
        row_copy(r, 1).wait()
        return c

    lax.fori_loop(0, tt, drain, 0, unroll=4)


def _dispatch(dest, pad_start, pad_len, tail, h2_packed, n_rows):
    t = h2_packed.shape[0]
    grid_spec = pltpu.PrefetchScalarGridSpec(
        num_scalar_prefetch=4,
        grid=(t // DISPATCH_TOKENS,),
        in_specs=[pl.BlockSpec((DISPATCH_TOKENS, HALF_D), lambda i, d, ps, plen, tl: (i, 0))],
        out_specs=pl.BlockSpec(memory_space=pl.ANY),
        scratch_shapes=[pltpu.VMEM((SUB_ROWS, HALF_D), jnp.uint32),
                        pltpu.SemaphoreType.DMA(()), pltpu.SemaphoreType.DMA(())],
    )
    return pl.pallas_call(
        _dispatch_kernel,
        grid_spec=grid_spec,
        out_shape=jax.ShapeDtypeStruct((n_rows, HALF_D), jnp.uint32),
        compiler_params=_params(("arbitrary",)),
        name="dispatch",
    )(dest, pad_start, pad_len, tail, h2_packed)


def _expert_kernel(exp_ref, nc_ref, nz_ref, osub_ref, xsub_ref, x0_ref, x1_ref, x2_ref, x3_ref,
                   wg_ref, wu_ref, wd_ref, y_hbm, a_scr, ytile, sem, *, nf, nd):
    s = pl.program_id(0)
    t = pl.program_id(1)
    n_comp = nc_ref[s]
    n_out = n_comp + nz_ref[s]
    x_refs = (x0_ref, x1_ref, x2_ref, x3_ref)

    @pl.when(jnp.logical_and(t < nf, n_comp > 0))
    def _():
        wg16 = wg_ref[...].astype(BF16)
        wu16 = wu_ref[...].astype(BF16)

        def up(k):
            x = _unpack_bf16_pair(x_refs[k][...])
            g = jnp.dot(x, wg16, preferred_element_type=F32)
            u = jnp.dot(x, wu16, preferred_element_type=F32)
            a_scr[t, k * SUB_ROWS:(k + 1) * SUB_ROWS, :] = (g * _sigmoid(g) * u).astype(BF16)

        up(0)
        for k in range(1, SUPER):
            pl.when(k < n_comp)(functools.partial(up, k))

    @pl.when(t >= nf)
    def _():
        j = t - nf
        slot = j % 2

        @pl.when(n_comp > 0)
        def _():
            wd16 = wd_ref[...].astype(BF16)

            def down(k):
                rows = slice(k * SUB_ROWS, (k + 1) * SUB_ROWS)
                a = jnp.concatenate([a_scr[f, rows, :] for f in range(nf)], axis=1)
                acc = jnp.dot(a, wd16, preferred_element_type=F32)
                ytile[slot, rows, :] = _pack_bf16_pair(acc[:, :TN_DOWN // 2], acc[:, TN_DOWN // 2:])

            down(0)
            for k in range(1, SUPER):
                pl.when(k < n_comp)(functools.partial(down, k))

        @pl.when(n_comp == 0)
        def _():
            ytile[slot] = jnp.zeros(ytile.shape[1:], jnp.uint32)

        def out_copy(sl, k, jj):
            dst_rows = pl.ds(pl.multiple_of((osub_ref[s] + k) * SUB_ROWS, SUB_ROWS), SUB_ROWS)
            dst_cols = pl.ds(pl.multiple_of(jj * (TN_DOWN // 2), TN_DOWN // 2), TN_DOWN // 2)
            return pltpu.make_async_copy(ytile.at[sl, pl.ds(k * SUB_ROWS, SUB_ROWS), :],
                                         y_hbm.at[dst_rows, dst_cols], sem)

        for k in range(SUPER):
            @pl.when(jnp.logical_and(j > 0, k < n_out))
            def _():
                out_copy(1 - slot, k, j - 1).wait()
        for k in range(SUPER):
            @pl.when(k < n_out)
            def _():
                out_copy(slot, k, j).start()
        for k in range(SUPER):
            @pl.when(jnp.logical_and(j == nd - 1, k < n_out))
            def _():
                out_copy(slot, k, j).wait()


def _experts(xs, sup_exp, n_comp, n_zero, out_sub, x_sub, w_gate, w_up, w_down):
    n_super = sup_exp.shape[0]
    n_rows = xs.shape[0]
    nf = D_EXPERT // F_CHUNK
    nd = D_MODEL // TN_DOWN

    def nxt(s, nc):
        s2 = jnp.minimum(s + 1, n_super - 1)
        return s2, jnp.logical_and(nc[s] > 0, nc[s2] > 0)

    def up_idx(s, t, e, nc, nz, osub, xsub):
        s2, has_next = nxt(s, nc)
        ahead = jnp.logical_and(t >= nf, has_next)
        chunk = jnp.where(nc[s] > 0, jnp.minimum(t, nf - 1), nf - 1)
        return (jnp.where(ahead, e[s2], e[s]), 0, jnp.where(ahead, 0, chunk))

    def down_idx(s, t, e, nc, nz, osub, xsub):
        sp = jnp.maximum(s - 1, 0)
        behind = jnp.logical_and(jnp.logical_and(t == 0, s > 0), nc[s] > 0)
        chunk = jnp.where(nc[s] > 0, jnp.maximum(t - nf, 0), nd - 1)
        return (jnp.where(behind, e[sp], e[s]), 0, jnp.where(behind, nd - 1, chunk))

    def x_spec(k):
        def idx(s, t, e, nc, nz, osub, xsub):
            s2, has_next = nxt(s, nc)
            return (xsub[jnp.where(jnp.logical_and(t >= nf, has_next), s2, s) * SUPER + k], 0)

        return pl.BlockSpec((SUB_ROWS, HALF_D), idx)

    grid_spec = pltpu.PrefetchScalarGridSpec(
        num_scalar_prefetch=5,
        grid=(n_super, nf + nd),
        in_specs=[x_spec(0), x_spec(1), x_spec(2), x_spec(3),
                  pl.BlockSpec((None, D_MODEL, F_CHUNK), up_idx),
                  pl.BlockSpec((None, D_MODEL, F_CHUNK), up_idx),
                  pl.BlockSpec((None, D_EXPERT, TN_DOWN), down_idx)],
        out_specs=pl.BlockSpec(memory_space=pl.ANY),
        scratch_shapes=[pltpu.VMEM((nf, SUPER * SUB_ROWS, F_CHUNK), BF16),
                        pltpu.VMEM((2, SUPER * SUB_ROWS, TN_DOWN // 2), jnp.uint32),
                        pltpu.SemaphoreType.DMA(())],
    )
    return pl.pallas_call(
        functools.partial(_expert_kernel, nf=nf, nd=nd),
        grid_spec=grid_spec,
        out_shape=jax.ShapeDtypeStruct((n_rows, HALF_D), jnp.uint32),
        compiler_params=_params(("arbitrary", "arbitrary"), VMEM_LIMIT_EXPERTS),
        name="experts",
    )(sup_exp, n_comp, n_zero, out_sub, x_sub, xs, xs, xs, xs, w_gate, w_up, w_down)


def _final_kernel(dest_ref, y_hbm, z_ref, wt_ref, gate_ref, g1_ref, b1_ref, g2_ref, b2_ref,
                  op_ref, os_ref, ybuf, sem, *, n_p_tiles, tiles_per_seq):
    i = pl.program_id(0)
    n_tiles = pl.num_programs(0)
    tm = z_ref.shape[0]
    slot = i % 2

    def row_copy(src_row, sl, k, r):
        return pltpu.make_async_copy(y_hbm.at[pl.ds(src_row, 1)], ybuf.at[sl, k, pl.ds(r, 1)], sem.at[sl])

    def gather(tile, sl):
        base = tile * tm * 2

        def issue(g, c):
            r0 = pl.multiple_of(g * ISSUE_GROUP, ISSUE_GROUP)
            for k in range(ISSUE_GROUP):
                row_copy(dest_ref[base + 2 * (r0 + k)], sl, 0, r0 + k).start(priority=0)
                row_copy(dest_ref[base + 2 * (r0 + k) + 1], sl, 1, r0 + k).start(priority=1)
            return c

        lax.fori_loop(0, tm // ISSUE_GROUP, issue, 0)

    @pl.when(i == 0)
    def _():
        gather(0, 0)

    @pl.when(i + 1 < n_tiles)
    def _():
        gather(i + 1, 1 - slot)

    def drain(r, c):
        row_copy(0, slot, 0, r).wait()
        row_copy(0, slot, 1, r).wait()
        return c

    lax.fori_loop(0, tm, drain, 0, unroll=4)

    is_p = i < n_p_tiles
    row = jnp.where(is_p, 0, 1 + (i - n_p_tiles) // tiles_per_seq)
    wt = wt_ref[...]
    w0, w1 = wt[:, 0:1], wt[:, 1:2]
    half_tile = TN_DOWN // 2
    pieces = []
    for j in range(D_MODEL // TN_DOWN):
        y0 = ybuf[slot, 0, :, j * half_tile:(j + 1) * half_tile]
        y1 = ybuf[slot, 1, :, j * half_tile:(j + 1) * half_tile]
        pieces.append(w0 * pltpu.bitcast(y0 << 16, F32) + w1 * pltpu.bitcast(y1 << 16, F32))
        pieces.append(w0 * pltpu.bitcast(y0 & jnp.uint32(0xFFFF0000), F32)
                      + w1 * pltpu.bitcast(y1 & jnp.uint32(0xFFFF0000), F32))
    f = jnp.concatenate(pieces, axis=1)
    x1 = _ln(z_ref[...]) * g1_ref[...] + b1_ref[...]
    out = _ln(ALPHA * x1 + gate_ref[pl.ds(row, 1), :] * f) * g2_ref[...] + b2_ref[...]

    @pl.when(is_p)
    def _():
        op_ref[...] = out

    @pl.when(jnp.logical_not(is_p))
    def _():
        os_ref[...] = out


def _combine(dest, y_rows, z, weight, gate2, g1, b1, g2, b2, n_prompt, dec_seq):
    t = z.shape[0]
    tm = TM_FIN
    n_p = n_prompt // tm
    n_s = (t - n_prompt) // tm
    full = lambda i, d: (0, 0)
    grid_spec = pltpu.PrefetchScalarGridSpec(
        num_scalar_prefetch=1,
        grid=(n_p + n_s,),
        in_specs=[pl.BlockSpec(memory_space=pl.ANY),
                  pl.BlockSpec((tm, D_MODEL), lambda i, d: (i, 0)),
                  pl.BlockSpec((tm, 2), lambda i, d: (i, 0)),
                  pl.BlockSpec((8, D_MODEL), full),
                  pl.BlockSpec((1, D_MODEL), full), pl.BlockSpec((1, D_MODEL), full),
                  pl.BlockSpec((1, D_MODEL), full), pl.BlockSpec((1, D_MODEL), full)],
        out_specs=[pl.BlockSpec((tm, D_MODEL), lambda i, d: (jnp.minimum(i, n_p - 1), 0)),
                   pl.BlockSpec((tm, D_MODEL), lambda i, d: (jnp.maximum(i - n_p, 0), 0))],
        scratch_shapes=[pltpu.VMEM((2, 2, tm, HALF_D), jnp.uint32), pltpu.SemaphoreType.DMA((2,))],
    )
    return pl.pallas_call(
        functools.partial(_final_kernel, n_p_tiles=n_p, tiles_per_seq=dec_seq // tm),
        grid_spec=grid_spec,
        out_shape=[jax.ShapeDtypeStruct((n_prompt, D_MODEL), F32),
                   jax.ShapeDtypeStruct((t - n_prompt, D_MODEL), F32)],
        compiler_params=_params(("arbitrary",)),
        name="combine",
    )(dest, y_rows, z, weight, gate2, g1, b1, g2, b2)


def kernel(x_prompt, x_sample, state_lru, state_ret, c, c_ctx, w_mod, b_mod, w_in, conv_w, conv_b, lru_wa, lru_ba,
           lru_wx, lru_bx, lru_lam, ret_decay, w_out, ln1_g, ln1_b, router_g, router_g_b, router_e, router_e_b,
           w_gate, w_up, w_down, ln2_g, ln2_b):
    assert w_in.shape[0] == 1, "single trunk layer"
    nb, seq, d = x_prompt.shape
    nbs, dec_seq, _ = x_sample.shape
    tp, ts = nb * seq, nbs * dec_seq
    assert tp % dec_seq == 0 and d == D_MODEL

    cond = jnp.zeros((8, d), F32).at[0].set(c_ctx).at[1:1 + nbs].set(c)
    mod = _modulation(cond, w_mod[0], b_mod[0][None, :])
    shift1, scale1, gate1, shift2, scale2, gate2 = [mod[:, k * d:(k + 1) * d] for k in range(6)]

    xp = x_prompt.reshape(tp, d)
    xs = x_sample.reshape(ts, d)
    proj = _in_projection(_ln_modulate(xp, xs, shift1, scale1, dec_seq), w_in[0].astype(BF16))

    w_gates = (0.5 * jnp.concatenate([lru_wa[0, 0], lru_wx[0, 0], lru_wa[0, 1], lru_wx[0, 1]], -1)).astype(BF16)
    pb = jnp.stack([0.5 * lru_ba[0, 0], 0.5 * lru_bx[0, 0], lru_lam[0, 0],
                    0.5 * lru_ba[0, 1], 0.5 * lru_bx[0, 1], lru_lam[0, 1]], 0)
    proj_p = proj.reshape((tp + ts) // seq, seq, IN_COLS)
    proj_s = proj.reshape((tp + ts) // dec_seq, dec_seq, IN_COLS)
    ylp, st_lru = _lru_mixer(proj_p, 0, nb, 16, 2, conv_w[0], conv_b[0][None, :], w_gates, pb,
                             jnp.zeros((2, nb, LRU_WIDTH), F32))
    yls, _ = _lru_mixer(proj_s, tp // dec_seq, nbs, nbs, 4, conv_w[0], conv_b[0][None, :], w_gates, pb,
                        jnp.swapaxes(state_lru[:, 0], 0, 1))

    decay = jnp.broadcast_to(ret_decay[0].T[:, :, None], (RET_HEADS, 2, LANES))
    yrp, st_ret = _ret_mixer(proj_p, 0, nb, 32, decay, emit_state=True)
    (yrs,) = _ret_mixer(proj_s, tp // dec_seq, nbs, nbs, decay, rope_tabs=_rope_tables(dec_seq), s0=state_ret)

    w_router = jnp.concatenate(
        [router_g[0], jnp.transpose(router_e[0], (1, 0, 2)).reshape(d, N_EXPERTS),
         jnp.zeros((d, LANES - N_GROUPS - N_EXPERTS), F32)], -1).astype(BF16)
    b_router = jnp.concatenate([router_g_b[0], router_e_b[0].reshape(-1),
                                jnp.zeros((LANES - N_GROUPS - N_EXPERTS,), F32)])[None, :]
    z = _out_projection(xp, xs, ylp.reshape(tp, -1), yrp.reshape(tp, -1), yls.reshape(ts, -1), yrs.reshape(ts, -1),
                        w_out[0].astype(BF16), gate1, dec_seq)
    h2, route = _norm_router(z, tp, ln1_g, ln1_b, shift2, scale2, w_router, b_router, dec_seq)
    expert = route[:, 0:2].astype(jnp.int32)
    weight = route[:, 2:4]

    n_assign = 2 * (tp + ts)
    n_sub = -(-(n_assign + N_EXPERTS * (SUB_ROWS - 1)) // SUB_ROWS)
    n_super = N_EXPERTS + -(-n_assign // (SUB_ROWS * SUPER))
    assert SUPER * n_super >= n_sub + (SUPER - 1) * N_EXPERTS
    dest, fill_plan, super_plan = _dispatch_plan(expert, n_sub, n_super)
    xs_rows = _dispatch(dest, *fill_plan, h2, n_sub * SUB_ROWS)
    y_rows = _experts(xs_rows, *super_plan, w_gate[0], w_up[0], w_down[0])
    y_p, y_s = _combine(dest, y_rows, z, weight, gate2, ln1_g, ln1_b, ln2_g, ln2_b, tp, dec_seq)

    new_state_lru = jnp.swapaxes(st_lru, 0, 1)[:, None]
    return (y_p.reshape(nb, seq, d), y_s.reshape(nbs, dec_seq, d), new_state_lru, st_ret)
```
